```python
import functools
import jax, jax.numpy as jnp
from jax import lax
import numpy as np

D_MODEL = 1024
BATCH = 8
SEQ = 2048
DEPTH = 2
DEC_BATCH = 32
DEC_SEQ = 4
PAST_LEN = 8192
PAGE_SIZE = 128

HEAD_DIM = 64
D_MIX = D_MODEL
RET_HEADS = 6
MOBA_HEADS = 6
POOL_GROUPS = 4
RET_W = RET_HEADS * HEAD_DIM
MOBA_W = MOBA_HEADS * HEAD_DIM
POOL_W = D_MIX - RET_W - MOBA_W
POOL_GC = POOL_W // POOL_GROUPS
POOL_WINDOWS = (2, 4, 8, 16)
POOL_BUF = 15
RET_CHUNK = 128
MOBA_BLOCK = 256
MOBA_TOPK = 3
MOBA_QBLK = 32
ROPE_THETA = 10000.0
EPS = 1e-6
D_IN = 4 * RET_W + 4 * MOBA_W + 2 * POOL_W

kernel_name = "hybrid_retention_moba_pool_decode_step"

F32 = jnp.float32


def _split_points():
    sizes = [RET_W] * 4 + [MOBA_W] * 4 + [POOL_W] * 2
    return [int(s) for s in np.cumsum(sizes)[:-1]]


def _rmsnorm(x, w):
    xf = x.astype(F32)
    return xf * lax.rsqrt(jnp.mean(xf * xf, axis=-1, keepdims=True) + EPS) * w.astype(F32)


def _heads(t):
    return t.reshape(t.shape[:-1] + (-1, HEAD_DIM))


def _rope(x, pos):
    half = HEAD_DIM // 2
    inv = 1.0 / (ROPE_THETA ** (jnp.arange(half, dtype=F32) / half))
    ang = pos.astype(F32)[:, None] * inv[None, :]
    cos = jnp.cos(ang)[:, None, :]
    sin = jnp.sin(ang)[:, None, :]
    x1, x2 = x[..., :half], x[..., half:]
    return jnp.concatenate([x1 * cos - x2 * sin, x2 * cos + x1 * sin], axis=-1)


def _ret_chunk(S, q, k, v, log_g):
    L = q.shape[2]
    i = jnp.arange(L, dtype=F32)
    rel = i[:, None] - i[None, :]
    lg = log_g[:, None, None]
    decay = jnp.where(rel[None] >= 0, jnp.exp(rel[None] * lg), 0.0)
    inner = jnp.einsum('bhid,bhjd->bhij', q, k) * decay
    o = jnp.einsum('bhij,bhjd->bhid', inner, v)
    o = o + jnp.einsum('bhid,bhde->bhie', q, S) * jnp.exp((i + 1.0)[None, :, None] * lg)
    k_dec = k * jnp.exp((L - 1.0 - i)[None, :, None] * lg)
    S_new = S * jnp.exp(L * log_g)[:, None, None] + jnp.einsum('bhjd,bhje->bhde', k_dec, v)
    return S_new, o


def _retention(q, k, v, S0, gn_w):
    B, L, H, d = q.shape
    log_g = jnp.log(1.0 - 2.0 ** (-5.0 - jnp.arange(H, dtype=F32)))
    c = RET_CHUNK if L % RET_CHUNK == 0 else L
    n = L // c

    def chunks(t):
        return t.astype(F32).reshape(B, n, c, H, d).transpose(1, 0, 3, 2, 4)

    S, o = lax.scan(lambda s, xs: _ret_chunk(s, xs[0], xs[1], xs[2], log_g),
                    S0.astype(F32), (chunks(q), chunks(k), chunks(v)))
    o = o.transpose(1, 0, 3, 2, 4).reshape(B, L, H, d)
    mu = jnp.mean(o, axis=-1, keepdims=True)
    var = jnp.mean(jnp.square(o - mu), axis=-1, keepdims=True)
    o = (o - mu) * lax.rsqrt(var + EPS) * gn_w.astype(F32).reshape(H, d)
    return o.reshape(B, L, H * d), S


def _moba_core(q, q_pos, k_blocks, v_blocks, k_mean, n_past, k_own, v_own, own_pos):
    B, H, Q, _ = q.shape
    scale = HEAD_DIM ** -0.5
    s_own = jnp.einsum('bhqd,bhkd->bhqk', q, k_own).astype(F32) * scale
    s_own = jnp.where(own_pos[None, :] <= q_pos[:, None], s_own, -jnp.inf)
    n_c = k_blocks.shape[2]
    k_sel = min(MOBA_TOPK, n_c)
    if k_sel == 0:
        p = jax.nn.softmax(s_own, axis=-1)
        return jnp.einsum('bhqk,bhkd->bhqd', p, v_own.astype(F32))
    gate = jnp.einsum('bhqd,bhnd->bhqn', q.astype(F32), k_mean)
    gate = jnp.where(jnp.arange(n_c) < n_past, gate, -jnp.inf)
    _, idx = lax.top_k(gate, k_sel)
    gather = jax.vmap(jax.vmap(lambda blk, ix: blk[ix]))
    k_g = gather(k_blocks, idx)
    v_g = gather(v_blocks, idx)
    s_sel = jnp.einsum('bhqd,bhqjkd->bhqjk', q, k_g).astype(F32) * scale
    s_sel = jnp.where((jnp.arange(k_sel) < n_past)[:, None], s_sel, -jnp.inf)
    n_sel = k_sel * MOBA_BLOCK
    p = jax.nn.softmax(jnp.concatenate([s_sel.reshape(B, H, Q, n_sel), s_own], axis=-1), axis=-1)
    o = jnp.einsum('bhqjk,bhqjkd->bhqd', p[..., :n_sel].reshape(B, H, Q, k_sel, MOBA_BLOCK), v_g.astype(F32))
    return o + jnp.einsum('bhqk,bhkd->bhqd', p[..., n_sel:], v_own.astype(F32))


def _moba_prompt(q, k, v):
    B, S, H, d = q.shape
    nb = -(-S // MOBA_BLOCK)
    pad = nb * MOBA_BLOCK - S
    qt = q.transpose(0, 2, 1, 3)
    padw = ((0, 0), (0, 0), (0, pad), (0, 0))
    kb = jnp.pad(k.transpose(0, 2, 1, 3), padw).reshape(B, H, nb, MOBA_BLOCK, d)
    vb = jnp.pad(v.transpose(0, 2, 1, 3), padw).reshape(B, H, nb, MOBA_BLOCK, d)
    k_cand, v_cand = kb[:, :, :nb - 1], vb[:, :, :nb - 1]
    k_mean = jnp.mean(k_cand.astype(F32), axis=3)

    def one_block(i):
        q0 = i * MOBA_QBLK
        qb = lax.dynamic_slice_in_dim(qt, q0, MOBA_QBLK, axis=2)
        q_pos = q0 + jnp.arange(MOBA_QBLK)
        own = q0 // MOBA_BLOCK
        k_own = lax.dynamic_index_in_dim(kb, own, axis=2, keepdims=False)
        v_own = lax.dynamic_index_in_dim(vb, own, axis=2, keepdims=False)
        own_pos = own * MOBA_BLOCK + jnp.arange(MOBA_BLOCK)
        return _moba_core(qb, q_pos, k_cand, v_cand, k_mean, own, k_own, v_own, own_pos)

    o = lax.map(one_block, jnp.arange(S // MOBA_QBLK))
    return o.transpose(1, 0, 3, 2, 4).reshape(B, S, H * d)


def _moba_sample(q, k, v, cache_k_l, cache_v_l, page_table, past_len):
    DB, T, H, d = q.shape
    kp = cache_k_l[page_table].reshape(DB, past_len, H, d).transpose(0, 2, 1, 3)
    vp = cache_v_l[page_table].reshape(DB, past_len, H, d).transpose(0, 2, 1, 3)
    n_full = past_len // MOBA_BLOCK
    own_start = n_full * MOBA_BLOCK
    kb = kp[:, :, :own_start].reshape(DB, H, n_full, MOBA_BLOCK, d)
    vb = vp[:, :, :own_start].reshape(DB, H, n_full, MOBA_BLOCK, d)
    k_mean = jnp.mean(kb.astype(F32), axis=3)
    k_own = jnp.concatenate([kp[:, :, own_start:].astype(F32), k.transpose(0, 2, 1, 3).astype(F32)], axis=2)
    v_own = jnp.concatenate([vp[:, :, own_start:].astype(F32), v.transpose(0, 2, 1, 3).astype(F32)], axis=2)
    own_pos = own_start + jnp.arange(k_own.shape[2])
    q_pos = past_len + jnp.arange(T)
    o = _moba_core(q.transpose(0, 2, 1, 3), q_pos, kb, vb, k_mean, n_full, k_own, v_own, own_pos)
    return o.transpose(0, 2, 1, 3).reshape(DB, T, H * d)


def _pool_mixer(u, buf, pos0, w_pool, b_pool, scale):
    L = u.shape[1]
    u_full = jnp.concatenate([buf.astype(F32), u.astype(F32)], axis=1)
    cs = jnp.cumsum(jnp.pad(u_full, ((0, 0), (1, 0), (0, 0))), axis=1)
    pos = pos0 + jnp.arange(L)
    outs = []
    for g, w in enumerate(POOL_WINDOWS):
        sl = slice(g * POOL_GC, (g + 1) * POOL_GC)
        win_sum = cs[:, POOL_BUF + 1:, sl] - cs[:, POOL_BUF + 1 - w:POOL_BUF + 1 - w + L, sl]
        cnt = jnp.minimum(pos + 1, w).astype(F32)[None, :, None]
        pooled = win_sum / cnt - u_full[:, POOL_BUF:, sl]
        outs.append(jnp.einsum('blc,cd->bld', pooled, w_pool[g]) + b_pool[g])
    y = jnp.concatenate(outs, axis=-1) * scale
    return y, u_full[:, -POOL_BUF:]


def _layer(x, pos0, S0, pool_buf, norm_w, w_in, w_out, ret_gn_w, q_norm_w, k_norm_w,
           pool_w, pool_b, pool_scale, moba_attend):
    L = x.shape[1]
    pos = pos0 + jnp.arange(L)
    h = _rmsnorm(x, norm_w)
    z = h @ w_in
    rq, rk, rv, rg, mq, mk, mv, mg, pu, pg = jnp.split(z, _split_points(), axis=-1)
    rq = _rope(_heads(rq), pos)
    rk = _rope(_heads(rk), pos) * (HEAD_DIM ** -0.5)
    r_o, S_new = _retention(rq, rk, _heads(rv), S0, ret_gn_w)
    mq = _rope(_rmsnorm(_heads(mq), q_norm_w), pos)
    mk = _rope(_rmsnorm(_heads(mk), k_norm_w), pos)
    mv = _heads(mv)
    m_o = moba_attend(mq, mk, mv)
    p_o, buf_new = _pool_mixer(pu, pool_buf, pos0, pool_w, pool_b, pool_scale)
    mix = jnp.concatenate([r_o * jax.nn.silu(rg), m_o * jax.nn.silu(mg), p_o * jax.nn.silu(pg)], axis=-1)
    y = x + (mix @ w_out).astype(x.dtype)
    return y, S_new, mk, mv, buf_new


def setup_inputs(seed: int = 0) -> dict:
    key = jax.random.key(seed)
    ks = jax.random.split(key, 16)
    n_pages = PAST_LEN // PAGE_SIZE
    n_pool = (DEC_BATCH * n_pages * 5) // 4
    page_table = jax.random.permutation(ks[0], n_pool)[:DEC_BATCH * n_pages].reshape(DEC_BATCH, n_pages).astype(jnp.int32)
    nrm = jax.random.normal
    return {
        "x_prompt": nrm(ks[1], (BATCH, SEQ, D_MODEL), F32),
        "x_sample": nrm(ks[2], (DEC_BATCH, DEC_SEQ, D_MODEL), F32),
        "cache_k": nrm(ks[3], (DEPTH, n_pool, PAGE_SIZE, MOBA_HEADS, HEAD_DIM), F32),
        "cache_v": nrm(ks[4], (DEPTH, n_pool, PAGE_SIZE, MOBA_HEADS, HEAD_DIM), F32),
        "state_ret": nrm(ks[5], (DEPTH, DEC_BATCH, RET_HEADS, HEAD_DIM, HEAD_DIM), F32),
        "state_pool": nrm(ks[6], (DEPTH, DEC_BATCH, POOL_BUF, POOL_W), F32),
        "page_table": page_table,
        "norm_w": 1.0 + 0.02 * nrm(ks[7], (DEPTH, D_MODEL), F32),
        "w_in": nrm(ks[8], (DEPTH, D_MODEL, D_IN), F32) * D_MODEL ** -0.5,
        "w_out": nrm(ks[9], (DEPTH, D_MIX, D_MODEL), F32) * D_MIX ** -0.5,
        "ret_gn_w": 1.0 + 0.02 * nrm(ks[10], (DEPTH, RET_W), F32),
        "q_norm_w": 1.0 + 0.02 * nrm(ks[11], (DEPTH, HEAD_DIM), F32),
        "k_norm_w": 1.0 + 0.02 * nrm(ks[12], (DEPTH, HEAD_DIM), F32),
        "pool_w": nrm(ks[13], (DEPTH, POOL_GROUPS, POOL_GC, POOL_GC), F32) * POOL_GC ** -0.5,
        "pool_b": 0.02 * nrm(ks[14], (DEPTH, POOL_GROUPS, POOL_GC), F32),
        "pool_scale": 1.0 + 0.1 * nrm(ks[15], (DEPTH, POOL_W), F32),
    }


def reference(x_prompt, x_sample, cache_k, cache_v, state_ret, state_pool, page_table,
              norm_w, w_in, w_out, ret_gn_w, q_norm_w, k_norm_w, pool_w, pool_b, pool_scale):
    past_len = page_table.shape[1] * cache_k.shape[2]
    B = x_prompt.shape[0]
    xp, xs = x_prompt, x_sample
    kps, vps, kss, vss, rps, rss, pps, pss = [], [], [], [], [], [], [], []
    for l in range(DEPTH):
        shared = (norm_w[l], w_in[l], w_out[l], ret_gn_w[l], q_norm_w[l], k_norm_w[l],
                  pool_w[l], pool_b[l], pool_scale[l])
        S0 = jnp.zeros((B, RET_HEADS, HEAD_DIM, HEAD_DIM), F32)
        buf0 = jnp.zeros((B, POOL_BUF, POOL_W), F32)
        xp, S_p, k_p, v_p, b_p = _layer(xp, 0, S0, buf0, *shared, moba_attend=_moba_prompt)
        moba_s = functools.partial(_moba_sample, cache_k_l=cache_k[l], cache_v_l=cache_v[l],
                                   page_table=page_table, past_len=past_len)
        xs, S_s, k_s, v_s, b_s = _layer(xs, past_len, state_ret[l], state_pool[l], *shared, moba_attend=moba_s)
        kps.append(k_p); vps.append(v_p); kss.append(k_s); vss.append(v_s)
        rps.append(S_p); rss.append(S_s); pps.append(b_p); pss.append(b_s)
    return (xp, xs, jnp.stack(kps), jnp.stack(vps), jnp.stack(kss), jnp.stack(vss),
            jnp.stack(rps), jnp.stack(rss), jnp.stack(pps), jnp.stack(pss))
```

```python
import functools

import jax
import jax.numpy as jnp
from jax import lax
from jax.experimental import pallas as pl
from jax.experimental.pallas import tpu as pltpu

F32 = jnp.float32
BF16 = jnp.bfloat16

HEAD_DIM = 64
HALF = HEAD_DIM // 2
LANES = 128
RET_HEADS = 6
MOBA_HEADS = 6
RET_W = RET_HEADS * HEAD_DIM
MOBA_W = MOBA_HEADS * HEAD_DIM
POOL_W = 256
POOL_GC = 64
POOL_WINDOWS = (2, 4, 8, 16)
POOL_BUF = 15
RET_CHUNK = 128
MOBA_BLOCK = 256
MOBA_TOPK = 3
ROPE_THETA = 10000.0
EPS = 1e-6
SUBLANES = 8
NEG_INF = float("-inf")

OFF_RQ, OFF_RV, OFF_MQ, OFF_MV, OFF_PU = 0, 2 * RET_W, 4 * RET_W, 4 * RET_W + 2 * MOBA_W, 4 * RET_W + 4 * MOBA_W
D_IN = OFF_PU + 2 * POOL_W

VMEM_LIMIT = 56 * 1024 * 1024


def _silu(x):
    return x / (1.0 + jnp.exp(-x))


def _cparams(sem):
    return pltpu.CompilerParams(dimension_semantics=sem, vmem_limit_bytes=VMEM_LIMIT)


def _in_proj_kernel(x_ref, nw_ref, w_ref, cos_ref, sin_ref, qnw_ref, knw_ref, bd_ref,
                    zr_ref, mq_ref, mk_ref, mv_ref, mg_ref, zp_ref, h_scr, z_scr):
    tm = x_ref.shape[0]
    x = x_ref[...]
    ms = jnp.mean(x * x, axis=-1, keepdims=True)
    h_scr[...] = (x * lax.rsqrt(ms + EPS) * nw_ref[...]).astype(BF16)
    cos = cos_ref[...]
    sin = sin_ref[...]
    lane = lax.broadcasted_iota(jnp.int32, (tm, LANES), 1)
    first_half = (lane & HALF) == 0

    def rope(z):
        partner = jnp.where(first_half, pltpu.roll(z, LANES - HALF, 1), pltpu.roll(z, HALF, 1))
        return z * cos + partner * sin

    def proj(off, width):
        return jnp.dot(h_scr[...], w_ref[:, off:off + width], preferred_element_type=F32)

    z_scr[...] = proj(OFF_RQ, 2 * RET_W)
    for g in range(2 * RET_W // LANES):
        out = rope(z_scr[:, g * LANES:(g + 1) * LANES])
        if g >= RET_W // LANES:
            out = out * (HEAD_DIM ** -0.5)
        zr_ref[:, g * LANES:(g + 1) * LANES] = out
    zr_ref[:, OFF_RV:OFF_MQ] = proj(OFF_RV, 2 * RET_W)
    z_scr[...] = proj(OFF_MQ, 2 * MOBA_W)
    for t, (nw, dst) in enumerate(((qnw_ref, mq_ref), (knw_ref, mk_ref))):
        z = z_scr[:, t * MOBA_W:(t + 1) * MOBA_W]
        msq = jnp.dot((z * z).astype(BF16), bd_ref[...], preferred_element_type=F32)
        z_scr[:, t * MOBA_W:(t + 1) * MOBA_W] = z * lax.rsqrt(msq + EPS) * nw[...]
        for g in range(MOBA_W // LANES):
            c0 = t * MOBA_W + g * LANES
            dst[:, g * LANES:(g + 1) * LANES] = rope(z_scr[:, c0:c0 + LANES])
    z_scr[...] = proj(OFF_MV, 2 * MOBA_W)
    mv_ref[...] = z_scr[:, :MOBA_W]
    mg_ref[...] = z_scr[:, MOBA_W:]
    zp_ref[...] = proj(OFF_PU, 2 * POOL_W)


def _in_proj(x2d, nw, w_bf, cos, sin, qnw, knw, bd, *, tm):
    m, d = x2d.shape
    n_pos = cos.shape[0] // tm
    row = lambda i: (i, 0)
    const = lambda i: (0, 0)
    pos = lambda i: (i % n_pos, 0)
    outs = [(2 * RET_W + 2 * RET_W), MOBA_W, MOBA_W, MOBA_W, MOBA_W, 2 * POOL_W]
    return pl.pallas_call(
        _in_proj_kernel,
        grid=(m // tm,),
        in_specs=[
            pl.BlockSpec((tm, d), row),
            pl.BlockSpec((1, d), const),
            pl.BlockSpec((d, D_IN), const),
            pl.BlockSpec((tm, LANES), pos),
            pl.BlockSpec((tm, LANES), pos),
            pl.BlockSpec((1, MOBA_W), const),
            pl.BlockSpec((1, MOBA_W), const),
            pl.BlockSpec((MOBA_W, MOBA_W), const),
        ],
        out_specs=[pl.BlockSpec((tm, w), row) for w in outs],
        out_shape=[jax.ShapeDtypeStruct((m, w), F32) for w in outs],
        scratch_shapes=[pltpu.VMEM((tm, d), BF16), pltpu.VMEM((tm, 2 * RET_W), F32)],
        compiler_params=_cparams(("arbitrary",)),
        name="in_proj",
    )(x2d, nw, w_bf, cos, sin, qnw, knw, bd)


def _retention_kernel(q_ref, k_ref, v_ref, g_ref, s0_ref, dm_ref, rs_ref, kd_ref, gc_ref, gnw_ref,
                      o_ref, sout_ref, s_scr, *, chunk):
    length = q_ref.shape[1]
    rows_in = min(length, chunk)
    n_chunks = max(1, length // chunk)
    lane = lax.broadcasted_iota(jnp.int32, (chunk, LANES), 1)
    head0 = lane < HEAD_DIM
    r = lax.broadcasted_iota(jnp.int32, (LANES, LANES), 0)
    c = lax.broadcasted_iota(jnp.int32, (LANES, LANES), 1)
    same_head = (r < HEAD_DIM) == (c < HEAD_DIM)
    d0 = dm_ref[0]
    d1 = dm_ref[1]
    rs = rs_ref[0]
    kd = kd_ref[0]
    gc = gc_ref[0]
    gnw = gnw_ref[0]
    contract_lanes = (((1,), (1,)), ((), ()))

    def head_mean(x):
        m0 = jnp.sum(jnp.where(head0, x, 0.0), axis=1, keepdims=True)
        m1 = jnp.sum(jnp.where(head0, 0.0, x), axis=1, keepdims=True)
        return jnp.where(head0, m0, m1) * (1.0 / HEAD_DIM)

    def load(ref, rows):
        a = ref[0, rows, :]
        if rows_in < chunk:
            a = jnp.concatenate([a, jnp.zeros((chunk - rows_in, LANES), F32)], axis=0)
        return a

    s_scr[...] = s0_ref[0, 0]

    def step(rows):
        q = load(q_ref, rows)
        k = load(k_ref, rows)
        v = load(v_ref, rows)
        qb = q.astype(BF16)
        kb = k.astype(BF16)
        vb = v.astype(BF16)
        q0 = jnp.where(head0, q, 0.0).astype(BF16)
        q1 = jnp.where(head0, 0.0, q).astype(BF16)
        in0 = (lax.dot_general(q0, kb, contract_lanes, preferred_element_type=F32) * d0).astype(BF16)
        in1 = (lax.dot_general(q1, kb, contract_lanes, preferred_element_type=F32) * d1).astype(BF16)
        o_intra = jnp.where(head0,
                            jnp.dot(in0, vb, preferred_element_type=F32),
                            jnp.dot(in1, vb, preferred_element_type=F32))
        s = s_scr[...]
        o = o_intra + jnp.dot(qb, s.astype(BF16), preferred_element_type=F32) * rs
        kdt = (k * kd).T.astype(BF16)
        kv = jnp.dot(kdt, vb, preferred_element_type=F32)
        s_scr[...] = s * gc + jnp.where(same_head, kv, 0.0)
        mu = head_mean(o)
        oc = o - mu
        var = head_mean(oc * oc)
        on = oc * lax.rsqrt(var + EPS) * gnw
        out = on * _silu(load(g_ref, rows))
        o_ref[0, rows, :] = out[:rows_in].astype(o_ref.dtype)

    if n_chunks == 1:
        step(slice(0, rows_in))
    else:
        def body(ci, carry):
            step(pl.ds(pl.multiple_of(ci * chunk, chunk), chunk))
            return carry
        lax.fori_loop(0, n_chunks, body, 0)
    sout_ref[0, 0] = s_scr[...]


def _retention(zr3, s0bd, tabs, gnw, *, chunk):
    b, length, _ = zr3.shape
    n_pairs = RET_W // LANES
    dm, rs, kd, gc = tabs
    col = lambda off: (lambda bi, p: (bi, 0, off + p))
    return pl.pallas_call(
        functools.partial(_retention_kernel, chunk=chunk),
        grid=(b, n_pairs),
        in_specs=[
            pl.BlockSpec((1, length, LANES), col(0)),
            pl.BlockSpec((1, length, LANES), col(n_pairs)),
            pl.BlockSpec((1, length, LANES), col(2 * n_pairs)),
            pl.BlockSpec((1, length, LANES), col(3 * n_pairs)),
            pl.BlockSpec((1, 1, LANES, LANES), lambda bi, p: (bi, p, 0, 0)),
            pl.BlockSpec((2, chunk, chunk), lambda bi, p: (p, 0, 0)),
            pl.BlockSpec((1, chunk, LANES), lambda bi, p: (p, 0, 0)),
            pl.BlockSpec((1, chunk, LANES), lambda bi, p: (p, 0, 0)),
            pl.BlockSpec((1, 1, LANES), lambda bi, p: (p, 0, 0)),
            pl.BlockSpec((1, 1, LANES), lambda bi, p: (p, 0, 0)),
        ],
        out_specs=[
            pl.BlockSpec((1, length, LANES), lambda bi, p: (bi, 0, p)),
            pl.BlockSpec((1, 1, LANES, LANES), lambda bi, p: (bi, p, 0, 0)),
        ],
        out_shape=[
            jax.ShapeDtypeStruct((b, length, RET_W), BF16),
            jax.ShapeDtypeStruct((b, n_pairs, LANES, LANES), F32),
        ],
        scratch_shapes=[pltpu.VMEM((LANES, LANES), F32)],
        compiler_params=_cparams(("arbitrary", "arbitrary")),
        name="retention",
    )(zr3, zr3, zr3, zr3, s0bd, dm, rs, kd, gc, gnw)


def _retention_tables(chunk, n_tokens):
    lg = jnp.log(1.0 - 2.0 ** (-5.0 - jnp.arange(RET_HEADS, dtype=F32)))
    i = jnp.arange(chunk, dtype=F32)
    rel = i[:, None] - i[None, :]
    dm = jnp.where(rel[None] >= 0, jnp.exp(rel[None] * lg[:, None, None]), 0.0)
    lg_lanes = jnp.repeat(lg, HEAD_DIM).reshape(RET_W // LANES, 1, LANES)
    rs = jnp.exp((i + 1.0)[None, :, None] * lg_lanes)
    kd = jnp.where((i < n_tokens)[None, :, None], jnp.exp((n_tokens - 1.0 - i)[None, :, None] * lg_lanes), 0.0)
    gc = jnp.exp(float(n_tokens) * lg_lanes)
    return dm, rs, kd, gc


def _to_pair_state(s):
    b = s.shape[0]
    s = s.reshape(b, RET_HEADS // 2, 2, HEAD_DIM, HEAD_DIM)
    z = jnp.zeros_like(s[:, :, 0])
    top = jnp.concatenate([s[:, :, 0], z], axis=-1)
    bot = jnp.concatenate([z, s[:, :, 1]], axis=-1)
    return jnp.concatenate([top, bot], axis=-2)


def _from_pair_state(sbd):
    b = sbd.shape[0]
    s0 = sbd[:, :, :HEAD_DIM, :HEAD_DIM]
    s1 = sbd[:, :, HEAD_DIM:, HEAD_DIM:]
    return jnp.stack([s0, s1], axis=2).reshape(b, RET_HEADS, HEAD_DIM, HEAD_DIM)


def _moba_prompt_kernel(q_ref, k_ref, v_ref, g_ref, o_ref, kt_ref, vt_ref, kh_scr, vt_scr, km_scr, sel_scr, *,
                        n_blocks):
    i = pl.program_id(2)
    blk = MOBA_BLOCK
    contract_lanes = (((1,), (1,)), ((), ()))

    @pl.when(i == 0)
    def _():
        lane = lax.broadcasted_iota(jnp.int32, (blk, LANES), 1)
        head0 = lane < HEAD_DIM
        for j in range(n_blocks):
            kj = k_ref[0, j * blk:(j + 1) * blk, :]
            kh_scr[0, j] = jnp.where(head0, kj, 0.0).astype(BF16)
            kh_scr[1, j] = jnp.where(head0, 0.0, kj).astype(BF16)
            vt = v_ref[0, j * blk:(j + 1) * blk, :].T
            vt_scr[j] = vt.astype(BF16)
            km_scr[j:j + 1, :] = jnp.sum(kj, axis=0, keepdims=True) * (1.0 / blk)
            kt_ref[0, :, j * blk:(j + 1) * blk] = kj.T
            vt_ref[0, :, j * blk:(j + 1) * blk] = vt

    q = q_ref[0] * (HEAD_DIM ** -0.5)
    qh = q.astype(BF16)
    ql = (q - qh.astype(F32)).astype(BF16)
    km = km_scr[...]
    lane8 = lax.broadcasted_iota(jnp.int32, (n_blocks, LANES), 1)
    row8 = lax.broadcasted_iota(jnp.int32, (n_blocks, blk), 0)
    krow = lax.broadcasted_iota(jnp.int32, (blk, blk), 0)
    qcol = lax.broadcasted_iota(jnp.int32, (blk, blk), 1)
    causal = jnp.where(krow <= qcol, 0.0, NEG_INF)

    for h in range(2):
        kmh = jnp.where((lane8 < HEAD_DIM) == (h == 0), km, 0.0)
        kmh_hi = kmh.astype(BF16)
        kmh_lo = (kmh - kmh_hi.astype(F32)).astype(BF16)
        gate = (lax.dot_general(kmh_hi, qh, contract_lanes, preferred_element_type=F32)
                + lax.dot_general(kmh_lo, qh, contract_lanes, preferred_element_type=F32)
                + lax.dot_general(kmh_hi, ql, contract_lanes, preferred_element_type=F32))
        rank = jnp.zeros((n_blocks, blk), jnp.int32)
        for jp in range(n_blocks):
            gj = gate[jp:jp + 1, :]
            beats = (gj > gate) | ((gj == gate) & (jp < row8))
            rank = rank + jnp.where(beats, 1, 0) * jnp.where(jp < i, 1, 0)
        sel = (rank < MOBA_TOPK) & (row8 < i)
        sel_scr[h] = jnp.where(sel, 0.0, NEG_INF)

    def tile(h, j, mask, m, l, acc):
        s = lax.dot_general(kh_scr[h, j], qh, contract_lanes, preferred_element_type=F32) + mask
        m_new = jnp.maximum(m, jnp.max(s, axis=0, keepdims=True))
        alpha = jnp.exp(m - m_new)
        p = jnp.exp(s - m_new)
        l = alpha * l + jnp.sum(p, axis=0, keepdims=True)
        pv = jnp.dot(vt_scr[j, h * HEAD_DIM:(h + 1) * HEAD_DIM, :], p.astype(BF16), preferred_element_type=F32)
        return m_new, l, alpha * acc + pv

    state = []
    for h in range(2):
        m0 = jnp.full((1, blk), NEG_INF, F32)
        l0 = jnp.zeros((1, blk), F32)
        a0 = jnp.zeros((HEAD_DIM, blk), F32)
        state.extend(tile(h, i, causal, m0, l0, a0))

    def body(j, st):
        out = []
        for h in range(2):
            m, l, acc = st[3 * h:3 * h + 3]
            out.extend(tile(h, j, sel_scr[h, pl.ds(j, 1), :], m, l, acc))
        return tuple(out)

    state = lax.fori_loop(0, i, body, tuple(state))
    ot = jnp.concatenate([state[2] / state[1], state[5] / state[4]], axis=0)
    o_ref[0] = (ot.T * _silu(g_ref[0])).astype(o_ref.dtype)


def _moba_prompt(mq3, mk3, mv3, mg3):
    b, length, _ = mq3.shape
    n_pairs = MOBA_W // LANES
    n_blocks = length // MOBA_BLOCK
    whole = lambda bi, p, i: (bi, 0, p)
    qblk = lambda bi, p, i: (bi, i, p)
    return pl.pallas_call(
        functools.partial(_moba_prompt_kernel, n_blocks=n_blocks),
        grid=(b, n_pairs, n_blocks),
        in_specs=[
            pl.BlockSpec((1, MOBA_BLOCK, LANES), qblk),
            pl.BlockSpec((1, length, LANES), whole),
            pl.BlockSpec((1, length, LANES), whole),
            pl.BlockSpec((1, MOBA_BLOCK, LANES), qblk),
        ],
        out_specs=[
            pl.BlockSpec((1, MOBA_BLOCK, LANES), qblk),
            pl.BlockSpec((1, LANES, length), lambda bi, p, i: (bi, p, 0)),
            pl.BlockSpec((1, LANES, length), lambda bi, p, i: (bi, p, 0)),
        ],
        out_shape=[
            jax.ShapeDtypeStruct((b, length, MOBA_W), BF16),
            jax.ShapeDtypeStruct((b, MOBA_W, length), F32),
            jax.ShapeDtypeStruct((b, MOBA_W, length), F32),
        ],
        scratch_shapes=[
            pltpu.VMEM((2, n_blocks, MOBA_BLOCK, LANES), BF16),
            pltpu.VMEM((n_blocks, LANES, MOBA_BLOCK), BF16),
            pltpu.VMEM((n_blocks, LANES), F32),
            pltpu.VMEM((2, n_blocks, MOBA_BLOCK), F32),
        ],
        compiler_params=_cparams(("arbitrary", "arbitrary", "arbitrary")),
        name="moba_prompt",
    )(mq3, mk3, mv3, mg3)


def _moba_sample_kernel(pt_ref, q_ref, kn_ref, vn_ref, g_ref, *rest, n_pages, pps, page):
    k_pages = rest[:pps]
    v_pages = rest[pps:2 * pps]
    o_ref = rest[2 * pps]
    qs_scr, s_scr, p_scr, acc_scr, l_scr = rest[2 * pps + 1:]
    ph = pl.program_id(1)
    st = pl.program_id(2)
    n_steps = n_pages // pps
    n_blocks = n_pages * page // MOBA_BLOCK
    rows = MOBA_HEADS * SUBLANES
    contract_lanes = (((1,), (1,)), ((), ()))
    lane = lax.broadcasted_iota(jnp.int32, (SUBLANES, MOBA_W), 1)

    @pl.when((ph == 0) & (st == 0))
    def _():
        q = q_ref[0] * (HEAD_DIM ** -0.5)
        for h in range(MOBA_HEADS):
            qs_scr[h * SUBLANES:(h + 1) * SUBLANES, :] = jnp.where(lane // HEAD_DIM == h, q, 0.0)

    @pl.when(ph == 0)
    def _():
        qb = qs_scr[...].astype(BF16)
        for r in range(pps):
            pg = st * pps + r
            s_scr[:, pl.ds(pl.multiple_of(pg * page, page), page)] = jnp.dot(
                qb, k_pages[r][...].astype(BF16), preferred_element_type=F32)

    @pl.when((ph == 0) & (st == n_steps - 1))
    def _():
        qh = qs_scr[...].astype(BF16)
        col = lax.broadcasted_iota(jnp.int32, (rows, LANES), 1)
        gate = jnp.zeros((rows, LANES), F32)
        for n in range(n_blocks):
            c0 = n * MOBA_BLOCK
            gate = gate + jnp.where(col == n, jnp.sum(s_scr[:, c0:c0 + MOBA_BLOCK], axis=1, keepdims=True), 0.0)
        rank = jnp.zeros(gate.shape, jnp.int32)
        for jp in range(n_blocks):
            gj = gate[:, jp:jp + 1]
            beats = (gj > gate) | ((gj == gate) & (jp < col))
            rank = rank + jnp.where(beats, 1, 0)
        sel = jnp.where(rank < min(MOBA_TOPK, n_blocks), 0.0, NEG_INF)
        kn = kn_ref[0]
        own = jnp.concatenate([kn, jnp.zeros((page - SUBLANES, MOBA_W), F32)], axis=0).astype(BF16)
        t_idx = lax.broadcasted_iota(jnp.int32, (rows, page), 0) % SUBLANES
        key_idx = lax.broadcasted_iota(jnp.int32, (rows, page), 1)
        s_own = lax.dot_general(qh, own, contract_lanes, preferred_element_type=F32)
        s_own = s_own + jnp.where(key_idx <= t_idx, 0.0, NEG_INF)
        s_scr[:, n_pages * page:] = s_own
        m = jnp.max(s_own, axis=1, keepdims=True)
        for n in range(n_blocks):
            c0 = n * MOBA_BLOCK
            sm = s_scr[:, c0:c0 + MOBA_BLOCK] + sel[:, n:n + 1]
            s_scr[:, c0:c0 + MOBA_BLOCK] = sm
            m = jnp.maximum(m, jnp.max(sm, axis=1, keepdims=True))
        l = jnp.zeros((rows, 1), F32)
        for n in range(n_pages + 1):
            c0 = n * page
            p = jnp.exp(s_scr[:, c0:c0 + page] - m)
            l = l + jnp.sum(p, axis=1, keepdims=True)
            p_scr[:, c0:c0 + page] = p.astype(BF16)
        l_scr[...] = l
        vn = vn_ref[0]
        vown = jnp.concatenate([vn, jnp.zeros((page - SUBLANES, MOBA_W), F32)], axis=0).astype(BF16)
        acc_scr[...] = jnp.dot(p_scr[:, n_pages * page:], vown, preferred_element_type=F32)

    @pl.when(ph == 1)
    def _():
        acc = acc_scr[...]
        for r in range(pps):
            pg = st * pps + r
            pblk = p_scr[:, pl.ds(pl.multiple_of(pg * page, page), page)]
            acc = acc + lax.dot_general(pblk, v_pages[r][...].astype(BF16), contract_lanes,
                                        preferred_element_type=F32)
        acc_scr[...] = acc

    @pl.when((ph == 1) & (st == n_steps - 1))
    def _():
        o = acc_scr[...] / l_scr[...]
        out = jnp.zeros((SUBLANES, MOBA_W), F32)
        for h in range(MOBA_HEADS):
            out = out + jnp.where(lane // HEAD_DIM == h, o[h * SUBLANES:(h + 1) * SUBLANES, :], 0.0)
        o_ref[0] = (out * _silu(g_ref[0])).astype(o_ref.dtype)


def _moba_sample(page_table, mq3, mk3, mv3, mg3, cache_k4, cache_v4, layer, *, pps):
    db = mq3.shape[0]
    n_pages = page_table.shape[1]
    page = cache_k4.shape[3]
    n_steps = n_pages // pps
    rows = MOBA_HEADS * SUBLANES
    assert n_pages * page // MOBA_BLOCK <= LANES and page == LANES
    per_b = lambda bi, ph, st, pt: (bi, 0, 0)

    def k_map(r):
        return lambda bi, ph, st, pt: (layer, pt[bi, (1 - ph) * (st * pps + r) + ph * ((n_steps - 1) * pps + r)], 0, 0)

    def v_map(r):
        return lambda bi, ph, st, pt: (layer, pt[bi, ph * (st * pps + r) + (1 - ph) * r], 0, 0)

    small = pl.BlockSpec((1, SUBLANES, MOBA_W), per_b)
    in_specs = [small, small, small, small]
    in_specs += [pl.BlockSpec((None, None, MOBA_W, page), k_map(r)) for r in range(pps)]
    in_specs += [pl.BlockSpec((None, None, MOBA_W, page), v_map(r)) for r in range(pps)]
    width = (n_pages + 1) * page
    return pl.pallas_call(
        functools.partial(_moba_sample_kernel, n_pages=n_pages, pps=pps, page=page),
        grid_spec=pltpu.PrefetchScalarGridSpec(
            num_scalar_prefetch=1,
            grid=(db, 2, n_steps),
            in_specs=in_specs,
            out_specs=pl.BlockSpec((1, SUBLANES, MOBA_W), per_b),
            scratch_shapes=[
                pltpu.VMEM((rows, MOBA_W), F32),
                pltpu.VMEM((rows, width), F32),
                pltpu.VMEM((rows, width), BF16),
                pltpu.VMEM((rows, MOBA_W), F32),
                pltpu.VMEM((rows, 1), F32),
            ],
        ),
        out_shape=jax.ShapeDtypeStruct((db, SUBLANES, MOBA_W), BF16),
        compiler_params=_cparams(("arbitrary", "arbitrary", "arbitrary")),
        name="moba_sample",
    )(page_table, mq3, mk3, mv3, mg3, *([cache_k4] * pps), *([cache_v4] * pps))


def _pool_kernel(u_ref, g_ref, buf_ref, cnt_ref, w_ref, b_ref, sc_ref, o_ref, bufo_ref, x_scr, *, n_tokens, rows):
    length = u_ref.shape[1]
    halo = POOL_BUF + 1
    x_scr[0:halo, :] = buf_ref[0]
    x_scr[halo:, :] = u_ref[0]
    lane = lax.broadcasted_iota(jnp.int32, (rows, POOL_W), 1)
    group = lane // POOL_GC

    for r0 in range(0, length, rows):
        u = x_scr[halo + r0:halo + r0 + rows, :]
        acc = u
        win = jnp.zeros_like(u)
        for s in range(1, max(POOL_WINDOWS)):
            acc = acc + x_scr[halo + r0 - s:halo + r0 - s + rows, :]
            if s + 1 in POOL_WINDOWS:
                win = jnp.where(group == POOL_WINDOWS.index(s + 1), acc, win)
        pooled = win / cnt_ref[r0:r0 + rows, :] - u
        y = jnp.dot(pooled.astype(BF16), w_ref[...], preferred_element_type=F32) + b_ref[...]
        y = y * sc_ref[...]
        o_ref[0, r0:r0 + rows, :] = (y * _silu(g_ref[0, r0:r0 + rows, :])).astype(o_ref.dtype)
    bufo_ref[0] = x_scr[n_tokens + 1:n_tokens + 1 + POOL_BUF, :]


def _pool(zp3, buf16, cnt, w_bd, bias, scale, *, n_tokens, rows):
    b, length, _ = zp3.shape
    const = lambda bi: (0, 0)
    return pl.pallas_call(
        functools.partial(_pool_kernel, n_tokens=n_tokens, rows=rows),
        grid=(b,),
        in_specs=[
            pl.BlockSpec((1, length, POOL_W), lambda bi: (bi, 0, 0)),
            pl.BlockSpec((1, length, POOL_W), lambda bi: (bi, 0, 1)),
            pl.BlockSpec((1, POOL_BUF + 1, POOL_W), lambda bi: (bi, 0, 0)),
            pl.BlockSpec((length, POOL_W), const),
            pl.BlockSpec((POOL_W, POOL_W), const),
            pl.BlockSpec((1, POOL_W), const),
            pl.BlockSpec((1, POOL_W), const),
        ],
        out_specs=[
            pl.BlockSpec((1, length, POOL_W), lambda bi: (bi, 0, 0)),
            pl.BlockSpec((1, POOL_BUF, POOL_W), lambda bi: (bi, 0, 0)),
        ],
        out_shape=[
            jax.ShapeDtypeStruct((b, length, POOL_W), BF16),
            jax.ShapeDtypeStruct((b, POOL_BUF, POOL_W), F32),
        ],
        scratch_shapes=[pltpu.VMEM((POOL_BUF + 1 + length, POOL_W), F32)],
        compiler_params=_cparams(("arbitrary",)),
        name="pool",
    )(zp3, zp3, buf16, cnt, w_bd, bias, scale)


def _out_proj_kernel(mr_ref, mm_ref, mp_ref, x_ref, w_ref, y_ref, mix_scr):
    mix_scr[:, :RET_W] = mr_ref[...]
    mix_scr[:, RET_W:RET_W + MOBA_W] = mm_ref[...]
    mix_scr[:, RET_W + MOBA_W:] = mp_ref[...]
    y_ref[...] = x_ref[...] + jnp.dot(mix_scr[...], w_ref[...], preferred_element_type=F32)


def _out_proj(mr, mm, mp, x2d, w_bf, *, tm):
    m, d = x2d.shape
    d_mix = RET_W + MOBA_W + POOL_W
    row = lambda i: (i, 0)
    return pl.pallas_call(
        _out_proj_kernel,
        grid=(m // tm,),
        in_specs=[
            pl.BlockSpec((tm, RET_W), row),
            pl.BlockSpec((tm, MOBA_W), row),
            pl.BlockSpec((tm, POOL_W), row),
            pl.BlockSpec((tm, d), row),
            pl.BlockSpec((d_mix, d), lambda i: (0, 0)),
        ],
        out_specs=pl.BlockSpec((tm, d), row),
        out_shape=jax.ShapeDtypeStruct((m, d), F32),
        scratch_shapes=[pltpu.VMEM((tm, d_mix), BF16)],
        compiler_params=_cparams(("arbitrary",)),
        name="out_proj",
    )(mr, mm, mp, x2d, w_bf)


def _rope_tables(pos):
    inv = 1.0 / (ROPE_THETA ** (jnp.arange(HALF, dtype=F32) / HALF))
    ang = pos.astype(F32)[:, None] * inv[None, :]
    c, s = jnp.cos(ang), jnp.sin(ang)
    return jnp.concatenate([c, c, c, c], axis=-1), jnp.concatenate([-s, s, -s, s], axis=-1)


def _block_diag(blocks):
    g, n, _ = blocks.shape
    eye = jnp.eye(g, dtype=blocks.dtype)
    return (eye[:, None, :, None] * blocks[:, :, None, :]).reshape(g * n, g * n)


def _pool_counts(pos0, length):
    pos = pos0 + jnp.arange(length)
    w = jnp.repeat(jnp.asarray(POOL_WINDOWS, jnp.int32), POOL_GC)
    return jnp.minimum(pos[:, None] + 1, w[None, :]).astype(F32)


def _layer(x3, pos_tabs, ret_state_bd, ret_tabs, ret_chunk, pool_buf16, pool_cnt, n_tokens, moba_fn, params, *, tm):
    b, length, d = x3.shape
    nw, w_in, w_out, gnw, qnw, knw, bd, pw_bd, pbias, pscale = params
    x2d = x3.reshape(b * length, d)
    zr, mq, mk, mv, mg, zp = _in_proj(x2d, nw, w_in, pos_tabs[0], pos_tabs[1], qnw, knw, bd, tm=tm)
    three = lambda a: a.reshape(b, length, a.shape[-1])
    mix_r, s_new = _retention(three(zr), ret_state_bd, ret_tabs, gnw, chunk=ret_chunk)
    mix_m, k_new, v_new = moba_fn(three(mq), three(mk), three(mv), three(mg))
    mix_p, buf_new = _pool(three(zp), pool_buf16, pool_cnt, pw_bd, pbias, pscale,
                           n_tokens=n_tokens, rows=min(length, 128))
    two = lambda a: a.reshape(b * length, a.shape[-1])
    y = _out_proj(two(mix_r), two(mix_m), two(mix_p), x2d, w_out, tm=tm)
    return y.reshape(b, length, d), s_new, k_new, v_new, buf_new


def _moba_prompt_wrap(mq3, mk3, mv3, mg3):
    mix, kt, vt = _moba_prompt(mq3, mk3, mv3, mg3)
    b, _, length = kt.shape
    rows = lambda t: t.reshape(b, MOBA_HEADS, HEAD_DIM, length).transpose(0, 3, 1, 2)
    return mix, rows(kt), rows(vt)


def _moba_sample_wrap(mq3, mk3, mv3, mg3, *, page_table, cache_k4, cache_v4, layer, pps, n_tokens):
    mix = _moba_sample(page_table, mq3, mk3, mv3, mg3, cache_k4, cache_v4, layer, pps=pps)
    rows = lambda t: t[:, :n_tokens].reshape(t.shape[0], n_tokens, MOBA_HEADS, HEAD_DIM)
    return mix, rows(mk3), rows(mv3)


def kernel(x_prompt, x_sample, cache_k, cache_v, state_ret, state_pool, page_table, norm_w, w_in, w_out, ret_gn_w, q_norm_w, k_norm_w, pool_w, pool_b, pool_scale):
    depth = w_in.shape[0]
    b, seq, d = x_prompt.shape
    db, dec_seq, _ = x_sample.shape
    n_pool, page = cache_k.shape[1], cache_k.shape[2]
    past_len = page_table.shape[1] * page
    assert dec_seq <= SUBLANES and seq % MOBA_BLOCK == 0 and seq % RET_CHUNK == 0 and past_len % MOBA_BLOCK == 0

    p_tabs = _rope_tables(jnp.arange(seq))
    tm_s = db * SUBLANES
    s_tabs = tuple(jnp.tile(t, (db, 1)) for t in _rope_tables(past_len + jnp.arange(SUBLANES)))
    p_ret_tabs = _retention_tables(RET_CHUNK, RET_CHUNK)
    s_ret_tabs = _retention_tables(RET_CHUNK, dec_seq)
    p_cnt = _pool_counts(0, seq)
    s_cnt = _pool_counts(past_len, SUBLANES)
    bd = _block_diag(jnp.full((MOBA_HEADS, HEAD_DIM, HEAD_DIM), 1.0 / HEAD_DIM, F32)).astype(BF16)
    zero_state = jnp.zeros((b, RET_HEADS // 2, LANES, LANES), F32)
    zero_buf = jnp.zeros((b, POOL_BUF + 1, POOL_W), F32)

    cache_k4 = cache_k.transpose(0, 1, 3, 4, 2).reshape(depth, n_pool, MOBA_W, page)
    cache_v4 = cache_v.transpose(0, 1, 3, 4, 2).reshape(depth, n_pool, MOBA_W, page)
    xp = x_prompt
    xs = jnp.pad(x_sample, ((0, 0), (0, SUBLANES - dec_seq), (0, 0)))
    pps = min(16, page_table.shape[1])

    outs = [[] for _ in range(8)]
    for l in range(depth):
        params = (norm_w[l].reshape(1, d), w_in[l].astype(BF16), w_out[l].astype(BF16),
                  ret_gn_w[l].reshape(RET_W // LANES, 1, LANES),
                  jnp.tile(q_norm_w[l], MOBA_HEADS).reshape(1, MOBA_W),
                  jnp.tile(k_norm_w[l], MOBA_HEADS).reshape(1, MOBA_W),
                  bd, _block_diag(pool_w[l]).astype(BF16),
                  pool_b[l].reshape(1, POOL_W), pool_scale[l].reshape(1, POOL_W))
        xp, s_p, k_p, v_p, b_p = _layer(xp, p_tabs, zero_state, p_ret_tabs, RET_CHUNK, zero_buf, p_cnt, seq,
                                        _moba_prompt_wrap, params, tm=512)
        moba_s = functools.partial(_moba_sample_wrap, page_table=page_table, cache_k4=cache_k4,
                                   cache_v4=cache_v4, layer=l, pps=pps, n_tokens=dec_seq)
        xs, s_s, k_s, v_s, b_s = _layer(xs, s_tabs, _to_pair_state(state_ret[l]), s_ret_tabs, RET_CHUNK,
                                        jnp.pad(state_pool[l], ((0, 0), (1, 0), (0, 0))), s_cnt, dec_seq,
                                        moba_s, params, tm=tm_s)
        for lst, val in zip(outs, (k_p, v_p, k_s, v_s, _from_pair_state(s_p), _from_pair_state(s_s), b_p, b_s)):
            lst.append(val)
    return (xp, xs[:, :dec_seq]) + tuple(jnp.stack(o) for o in outs)
```

```python
import functools
import math

import jax
import jax.numpy as jnp
from jax import lax
from jax.experimental import pallas as pl
from jax.experimental.pallas import tpu as pltpu

F32 = jnp.float32
BF16 = jnp.bfloat16

HEAD_DIM = 64
HALF = HEAD_DIM // 2
LANES = 128
RET_HEADS = 6
MOBA_HEADS = 6
RET_W = RET_HEADS * HEAD_DIM
MOBA_W = MOBA_HEADS * HEAD_DIM
POOL_W = 256
POOL_GC = 64
POOL_WINDOWS = (2, 4, 8, 16)
POOL_BUF = 15
RET_CHUNK = 128
MOBA_BLOCK = 256
MOBA_TOPK = 3
ROPE_THETA = 10000.0
EPS = 1e-6
SUBLANES = 8
NEG_INF = float("-inf")
LOG2_E = 1.4426950408889634

OFF_RQ, OFF_RV, OFF_MQ, OFF_MV, OFF_PU = 0, 2 * RET_W, 4 * RET_W, 4 * RET_W + 2 * MOBA_W, 4 * RET_W + 4 * MOBA_W
D_IN = OFF_PU + 2 * POOL_W

VMEM_LIMIT = 56 * 1024 * 1024


def _silu(x):
    return x / (1.0 + jnp.exp(-x))


def _cparams(sem):
    return pltpu.CompilerParams(dimension_semantics=sem, vmem_limit_bytes=VMEM_LIMIT)


def _in_proj_kernel(x_ref, nw_ref, w_ref, cos_ref, sin_ref, qnw_ref, knw_ref, bd_ref,
                    zr_ref, mq_ref, mk_ref, mv_ref, mg_ref, zp_ref, h_scr, z_scr):
    tm = x_ref.shape[0]
    x = x_ref[...]
    ms = jnp.mean(x * x, axis=-1, keepdims=True)
    h_scr[...] = (x * lax.rsqrt(ms + EPS) * nw_ref[...]).astype(BF16)
    cos = cos_ref[...]
    sin = sin_ref[...]
    lane = lax.broadcasted_iota(jnp.int32, (tm, LANES), 1)
    first_half = (lane & HALF) == 0

    def rope(z):
        partner = jnp.where(first_half, pltpu.roll(z, LANES - HALF, 1), pltpu.roll(z, HALF, 1))
        return z * cos + partner * sin

    def proj(off, width):
        return jnp.dot(h_scr[...], w_ref[:, off:off + width], preferred_element_type=F32)

    z_scr[...] = proj(OFF_RQ, 2 * RET_W)
    for g in range(2 * RET_W // LANES):
        out = rope(z_scr[:, g * LANES:(g + 1) * LANES])
        if g >= RET_W // LANES:
            out = out * (HEAD_DIM ** -0.5)
        zr_ref[:, g * LANES:(g + 1) * LANES] = out
    zr_ref[:, OFF_RV:OFF_MQ] = proj(OFF_RV, 2 * RET_W)
    z_scr[...] = proj(OFF_MQ, 2 * MOBA_W)
    for t, (nw, dst) in enumerate(((qnw_ref, mq_ref), (knw_ref, mk_ref))):
        z = z_scr[:, t * MOBA_W:(t + 1) * MOBA_W]
        msq = jnp.dot((z * z).astype(BF16), bd_ref[...], preferred_element_type=F32)
        z_scr[:, t * MOBA_W:(t + 1) * MOBA_W] = z * lax.rsqrt(msq + EPS) * nw[...]
        for g in range(MOBA_W // LANES):
            c0 = t * MOBA_W + g * LANES
            dst[:, g * LANES:(g + 1) * LANES] = rope(z_scr[:, c0:c0 + LANES])
    z_scr[...] = proj(OFF_MV, 2 * MOBA_W)
    mv_ref[...] = z_scr[:, :MOBA_W]
    mg_ref[...] = z_scr[:, MOBA_W:]
    zp_ref[...] = proj(OFF_PU, 2 * POOL_W)


def _in_proj(x2d, nw, w_bf, cos, sin, qnw, knw, bd, *, tm):
    m, d = x2d.shape
    n_pos = cos.shape[0] // tm
    row = lambda i: (i, 0)
    const = lambda i: (0, 0)
    pos = lambda i: (i % n_pos, 0)
    outs = [(2 * RET_W + 2 * RET_W), MOBA_W, MOBA_W, MOBA_W, MOBA_W, 2 * POOL_W]
    return pl.pallas_call(
        _in_proj_kernel,
        grid=(m // tm,),
        in_specs=[
            pl.BlockSpec((tm, d), row),
            pl.BlockSpec((1, d), const),
            pl.BlockSpec((d, D_IN), const),
            pl.BlockSpec((tm, LANES), pos),
            pl.BlockSpec((tm, LANES), pos),
            pl.BlockSpec((1, MOBA_W), const),
            pl.BlockSpec((1, MOBA_W), const),
            pl.BlockSpec((MOBA_W, MOBA_W), const),
        ],
        out_specs=[pl.BlockSpec((tm, w), row) for w in outs],
        out_shape=[jax.ShapeDtypeStruct((m, w), F32) for w in outs],
        scratch_shapes=[pltpu.VMEM((tm, d), BF16), pltpu.VMEM((tm, 2 * RET_W), F32)],
        compiler_params=_cparams(("arbitrary",)),
        name="in_proj",
    )(x2d, nw, w_bf, cos, sin, qnw, knw, bd)


def _retention_kernel(q_ref, k_ref, v_ref, g_ref, s0_ref, dm_ref, rs_ref, kd_ref, gc_ref, gnw_ref,
                      o_ref, sout_ref, oi_scr, kv_scr, sb_scr, *, chunk):
    bb, length, width = q_ref.shape
    pairs = width // LANES
    rows_in = min(length, chunk)
    n_chunks = max(1, length // chunk)
    lane = lax.broadcasted_iota(jnp.int32, (chunk, LANES), 1)
    head0 = lane < HEAD_DIM
    r = lax.broadcasted_iota(jnp.int32, (LANES, LANES), 0)
    c = lax.broadcasted_iota(jnp.int32, (LANES, LANES), 1)
    same_head = (r < HEAD_DIM) == (c < HEAD_DIM)
    contract_lanes = (((1,), (1,)), ((), ()))

    def head_mean(x):
        m0 = jnp.sum(jnp.where(head0, x, 0.0), axis=1, keepdims=True)
        m1 = jnp.sum(jnp.where(head0, 0.0, x), axis=1, keepdims=True)
        return jnp.where(head0, m0, m1) * (1.0 / HEAD_DIM)

    for bi in range(bb):
        for p in range(pairs):
            slot = bi * pairs + p
            cols = slice(p * LANES, (p + 1) * LANES)
            d0 = dm_ref[2 * p]
            d1 = dm_ref[2 * p + 1]
            rs = rs_ref[p]
            kd = kd_ref[p]
            gc = gc_ref[p]
            gnw = gnw_ref[p]

            def load(ref, ci):
                a = ref[bi, ci * chunk:ci * chunk + rows_in, cols]
                if rows_in < chunk:
                    a = jnp.concatenate([a, jnp.zeros((chunk - rows_in, LANES), F32)], axis=0)
                return a

            for ci in range(n_chunks):
                q = load(q_ref, ci)
                k = load(k_ref, ci)
                kb = k.astype(BF16)
                vb = load(v_ref, ci).astype(BF16)
                q0 = jnp.where(head0, q, 0.0).astype(BF16)
                q1 = jnp.where(head0, 0.0, q).astype(BF16)
                in0 = (lax.dot_general(q0, kb, contract_lanes, preferred_element_type=F32) * d0).astype(BF16)
                in1 = (lax.dot_general(q1, kb, contract_lanes, preferred_element_type=F32) * d1).astype(BF16)
                oi_scr[slot, ci * chunk:(ci + 1) * chunk, :] = jnp.where(
                    head0, jnp.dot(in0, vb, preferred_element_type=F32), jnp.dot(in1, vb, preferred_element_type=F32))
                kdt = (k * kd).T.astype(BF16)
                kv_scr[slot, ci] = jnp.where(same_head, jnp.dot(kdt, vb, preferred_element_type=F32), 0.0)

            s = s0_ref[bi, p]
            for ci in range(n_chunks):
                sb_scr[slot, ci] = s.astype(BF16)
                s = s * gc + kv_scr[slot, ci]
            sout_ref[bi, p] = s

            for ci in range(n_chunks):
                qb = load(q_ref, ci).astype(BF16)
                o = (oi_scr[slot, ci * chunk:(ci + 1) * chunk, :]
                     + jnp.dot(qb, sb_scr[slot, ci], preferred_element_type=F32) * rs)
                mu = head_mean(o)
                oc = o - mu
                var = head_mean(oc * oc)
                on = oc * lax.rsqrt(var + EPS) * gnw
                out = on * _silu(load(g_ref, ci))
                o_ref[bi, ci * chunk:ci * chunk + rows_in, cols] = out[:rows_in].astype(o_ref.dtype)


def _retention(zr3, s0bd, tabs, gnw, *, chunk, bb, pairs):
    b, length, _ = zr3.shape
    n_pairs = RET_W // LANES
    dm, rs, kd, gc = tabs
    w = pairs * LANES
    n_chunks = max(1, length // chunk)
    col = lambda off: (lambda bi, p: (bi, 0, off + p))
    tab = lambda bi, p: (p, 0, 0)
    return pl.pallas_call(
        functools.partial(_retention_kernel, chunk=chunk),
        grid=(b // bb, n_pairs // pairs),
        in_specs=[
            pl.BlockSpec((bb, length, w), col(0)),
            pl.BlockSpec((bb, length, w), col(n_pairs // pairs)),
            pl.BlockSpec((bb, length, w), col(2 * n_pairs // pairs)),
            pl.BlockSpec((bb, length, w), col(3 * n_pairs // pairs)),
            pl.BlockSpec((bb, pairs, LANES, LANES), lambda bi, p: (bi, p, 0, 0)),
            pl.BlockSpec((2 * pairs, chunk, chunk), tab),
            pl.BlockSpec((pairs, chunk, LANES), tab),
            pl.BlockSpec((pairs, chunk, LANES), tab),
            pl.BlockSpec((pairs, 1, LANES), tab),
            pl.BlockSpec((pairs, 1, LANES), tab),
        ],
        out_specs=[
            pl.BlockSpec((bb, length, w), lambda bi, p: (bi, 0, p)),
            pl.BlockSpec((bb, pairs, LANES, LANES), lambda bi, p: (bi, p, 0, 0)),
        ],
        out_shape=[
            jax.ShapeDtypeStruct((b, length, RET_W), BF16),
            jax.ShapeDtypeStruct((b, n_pairs, LANES, LANES), F32),
        ],
        scratch_shapes=[
            pltpu.VMEM((bb * pairs, n_chunks * chunk, LANES), F32),
            pltpu.VMEM((bb * pairs, n_chunks, LANES, LANES), F32),
            pltpu.VMEM((bb * pairs, n_chunks, LANES, LANES), BF16),
        ],
        compiler_params=_cparams(("arbitrary", "arbitrary")),
        name="retention",
    )(zr3, zr3, zr3, zr3, s0bd, dm, rs, kd, gc, gnw)


def _retention_tables(chunk, n_tokens):
    lg = jnp.log(1.0 - 2.0 ** (-5.0 - jnp.arange(RET_HEADS, dtype=F32)))
    i = jnp.arange(chunk, dtype=F32)
    rel = i[:, None] - i[None, :]
    dm = jnp.where(rel[None] >= 0, jnp.exp(rel[None] * lg[:, None, None]), 0.0)
    lg_lanes = jnp.repeat(lg, HEAD_DIM).reshape(RET_W // LANES, 1, LANES)
    rs = jnp.exp((i + 1.0)[None, :, None] * lg_lanes)
    kd = jnp.where((i < n_tokens)[None, :, None], jnp.exp((n_tokens - 1.0 - i)[None, :, None] * lg_lanes), 0.0)
    gc = jnp.exp(float(n_tokens) * lg_lanes)
    return dm, rs, kd, gc


def _to_pair_state(s):
    b = s.shape[0]
    s = s.reshape(b, RET_HEADS // 2, 2, HEAD_DIM, HEAD_DIM)
    z = jnp.zeros_like(s[:, :, 0])
    top = jnp.concatenate([s[:, :, 0], z], axis=-1)
    bot = jnp.concatenate([z, s[:, :, 1]], axis=-1)
    return jnp.concatenate([top, bot], axis=-2)


def _from_pair_state(sbd):
    b = sbd.shape[0]
    s0 = sbd[:, :, :HEAD_DIM, :HEAD_DIM]
    s1 = sbd[:, :, HEAD_DIM:, HEAD_DIM:]
    return jnp.stack([s0, s1], axis=2).reshape(b, RET_HEADS, HEAD_DIM, HEAD_DIM)


def _moba_prompt_kernel(q_ref, k_ref, v_ref, g_ref, o_ref, kt_ref, vt_ref, kh_scr, vt_scr, km_scr, s_scr, p_scr, *,
                        n_blocks):
    blk = MOBA_BLOCK
    sub = blk // 2
    contract_lanes = (((1,), (1,)), ((), ()))
    lane = lax.broadcasted_iota(jnp.int32, (blk, LANES), 1)
    head0 = lane < HEAD_DIM
    for j in range(n_blocks):
        kj = k_ref[0, j * blk:(j + 1) * blk, :]
        kh_scr[0, j] = jnp.where(head0, kj, 0.0).astype(BF16)
        kh_scr[1, j] = jnp.where(head0, 0.0, kj).astype(BF16)
        vt = v_ref[0, j * blk:(j + 1) * blk, :].T
        vt_scr[:, j * blk:(j + 1) * blk] = vt.astype(BF16)
        km_scr[j:j + 1, :] = jnp.sum(kj, axis=0, keepdims=True) * (1.0 / blk)
        kt_ref[0, :, j * blk:(j + 1) * blk] = kj.T
        vt_ref[0, :, j * blk:(j + 1) * blk] = vt

    km = km_scr[...]
    lane8 = lax.broadcasted_iota(jnp.int32, (n_blocks, LANES), 1)
    row8 = lax.broadcasted_iota(jnp.int32, (n_blocks, blk), 0)
    krow = lax.broadcasted_iota(jnp.int32, (blk, blk), 0)
    qcol = lax.broadcasted_iota(jnp.int32, (blk, blk), 1)
    causal = jnp.where(krow <= qcol, 0.0, NEG_INF)
    fold = lambda t: t.reshape(t.shape[0] // SUBLANES, SUBLANES, blk)
    km_parts = []
    for h in range(2):
        kmh = jnp.where((lane8 < HEAD_DIM) == (h == 0), km, 0.0)
        kmh_hi = kmh.astype(BF16)
        km_parts.append((kmh_hi, (kmh - kmh_hi.astype(F32)).astype(BF16)))

    for i in range(n_blocks):
        q = q_ref[0, i * blk:(i + 1) * blk, :] * (HEAD_DIM ** -0.5 * LOG2_E)
        qh = q.astype(BF16)
        buf = i % 2
        halves = []
        for h in range(2):
            selmask = None
            if i > MOBA_TOPK:
                ql = (q - qh.astype(F32)).astype(BF16)
                kmh_hi, kmh_lo = km_parts[h]
                gate = (lax.dot_general(kmh_hi, qh, contract_lanes, preferred_element_type=F32)
                        + lax.dot_general(kmh_lo, qh, contract_lanes, preferred_element_type=F32)
                        + lax.dot_general(kmh_hi, ql, contract_lanes, preferred_element_type=F32))
                rank = jnp.zeros((n_blocks, blk), jnp.int32)
                for jp in range(i):
                    gj = gate[jp:jp + 1, :]
                    rank = rank + jnp.where((gj > gate) | ((gj == gate) & (jp < row8)), 1, 0)
                selmask = jnp.where((rank < MOBA_TOPK) & (row8 < i), 0.0, NEG_INF)
            m8 = None
            for j in range(i + 1):
                for u in range(blk // sub):
                    rows = slice(u * sub, (u + 1) * sub)
                    s = lax.dot_general(kh_scr[h, j, rows, :], qh, contract_lanes, preferred_element_type=F32)
                    if j == i:
                        s = s + causal[rows]
                    elif selmask is not None:
                        s = s + selmask[j:j + 1, :]
                    s_scr[buf, h, j, rows, :] = s
                    tmax = jnp.max(fold(s), axis=0)
                    m8 = tmax if m8 is None else jnp.maximum(m8, tmax)
            m = jnp.max(m8, axis=0, keepdims=True)
            l8 = jnp.zeros((SUBLANES, blk), F32)
            for j in range(i + 1):
                for u in range(blk // sub):
                    rows = slice(u * sub, (u + 1) * sub)
                    p = jnp.exp2(s_scr[buf, h, j, rows, :] - m)
                    l8 = l8 + jnp.sum(fold(p), axis=0)
                    p_scr[buf, h, j * blk + u * sub:j * blk + (u + 1) * sub, :] = p.astype(BF16)
            l = jnp.sum(l8, axis=0, keepdims=True)
            nk = (i + 1) * blk
            ot = jnp.dot(vt_scr[h * HEAD_DIM:(h + 1) * HEAD_DIM, 0:nk], p_scr[buf, h, 0:nk, :],
                         preferred_element_type=F32)
            halves.append(ot / l)
        ot = jnp.concatenate(halves, axis=0)
        o_ref[0, i * blk:(i + 1) * blk, :] = (ot.T * _silu(g_ref[0, i * blk:(i + 1) * blk, :])).astype(o_ref.dtype)


def _moba_prompt(mq3, mk3, mv3, mg3):
    b, length, _ = mq3.shape
    n_pairs = MOBA_W // LANES
    n_blocks = length // MOBA_BLOCK
    rows = pl.BlockSpec((1, length, LANES), lambda bi, p: (bi, 0, p))
    cols = pl.BlockSpec((1, LANES, length), lambda bi, p: (bi, p, 0))
    return pl.pallas_call(
        functools.partial(_moba_prompt_kernel, n_blocks=n_blocks),
        grid=(b, n_pairs),
        in_specs=[rows, rows, rows, rows],
        out_specs=[rows, cols, cols],
        out_shape=[
            jax.ShapeDtypeStruct((b, length, MOBA_W), BF16),
            jax.ShapeDtypeStruct((b, MOBA_W, length), F32),
            jax.ShapeDtypeStruct((b, MOBA_W, length), F32),
        ],
        scratch_shapes=[
            pltpu.VMEM((2, n_blocks, MOBA_BLOCK, LANES), BF16),
            pltpu.VMEM((LANES, length), BF16),
            pltpu.VMEM((n_blocks, LANES), F32),
            pltpu.VMEM((2, 2, n_blocks, MOBA_BLOCK, MOBA_BLOCK), F32),
            pltpu.VMEM((2, 2, length, MOBA_BLOCK), BF16),
        ],
        compiler_params=_cparams(("arbitrary", "arbitrary")),
        name="moba_prompt",
    )(mq3, mk3, mv3, mg3)


def _moba_sample_kernel(pt_ref, q_ref, kn_ref, vn_ref, g_ref, *rest, n_pages, pps, page):
    k_pages = rest[:pps]
    v_pages = rest[pps:2 * pps]
    o_ref = rest[2 * pps]
    qs_scr, s_scr, p_scr, acc_scr, l_scr = rest[2 * pps + 1:]
    ph = pl.program_id(1)
    st = pl.program_id(2)
    n_steps = n_pages // pps
    n_blocks = n_pages * page // MOBA_BLOCK
    rows = MOBA_HEADS * SUBLANES
    contract_lanes = (((1,), (1,)), ((), ()))
    lane = lax.broadcasted_iota(jnp.int32, (SUBLANES, MOBA_W), 1)

    @pl.when((ph == 0) & (st == 0))
    def _():
        q = q_ref[0] * (HEAD_DIM ** -0.5)
        for h in range(MOBA_HEADS):
            qs_scr[h * SUBLANES:(h + 1) * SUBLANES, :] = jnp.where(lane // HEAD_DIM == h, q, 0.0)

    @pl.when(ph == 0)
    def _():
        qb = qs_scr[...].astype(BF16)
        for r in range(pps):
            pg = st * pps + r
            s_scr[:, pl.ds(pl.multiple_of(pg * page, page), page)] = jnp.dot(
                qb, k_pages[r][...].astype(BF16), preferred_element_type=F32)

    @pl.when((ph == 0) & (st == n_steps - 1))
    def _():
        qh = qs_scr[...].astype(BF16)
        col = lax.broadcasted_iota(jnp.int32, (rows, LANES), 1)
        gate = jnp.zeros((rows, LANES), F32)
        for n in range(n_blocks):
            c0 = n * MOBA_BLOCK
            gate = gate + jnp.where(col == n, jnp.sum(s_scr[:, c0:c0 + MOBA_BLOCK], axis=1, keepdims=True), 0.0)
        rank = jnp.zeros(gate.shape, jnp.int32)
        for jp in range(n_blocks):
            gj = gate[:, jp:jp + 1]
            beats = (gj > gate) | ((gj == gate) & (jp < col))
            rank = rank + jnp.where(beats, 1, 0)
        sel = jnp.where(rank < min(MOBA_TOPK, n_blocks), 0.0, NEG_INF)
        kn = kn_ref[0]
        own = jnp.concatenate([kn, jnp.zeros((page - SUBLANES, MOBA_W), F32)], axis=0).astype(BF16)
        t_idx = lax.broadcasted_iota(jnp.int32, (rows, page), 0) % SUBLANES
        key_idx = lax.broadcasted_iota(jnp.int32, (rows, page), 1)
        s_own = lax.dot_general(qh, own, contract_lanes, preferred_element_type=F32)
        s_own = s_own + jnp.where(key_idx <= t_idx, 0.0, NEG_INF)
        s_scr[:, n_pages * page:] = s_own
        m = jnp.max(s_own, axis=1, keepdims=True)
        for n in range(n_blocks):
            c0 = n * MOBA_BLOCK
            sm = s_scr[:, c0:c0 + MOBA_BLOCK] + sel[:, n:n + 1]
            s_scr[:, c0:c0 + MOBA_BLOCK] = sm
            m = jnp.maximum(m, jnp.max(sm, axis=1, keepdims=True))
        l = jnp.zeros((rows, 1), F32)
        for n in range(n_pages + 1):
            c0 = n * page
            p = jnp.exp(s_scr[:, c0:c0 + page] - m)
            l = l + jnp.sum(p, axis=1, keepdims=True)
            p_scr[:, c0:c0 + page] = p.astype(BF16)
        l_scr[...] = l
        vn = vn_ref[0]
        vown = jnp.concatenate([vn, jnp.zeros((page - SUBLANES, MOBA_W), F32)], axis=0).astype(BF16)
        acc_scr[...] = jnp.dot(p_scr[:, n_pages * page:], vown, preferred_element_type=F32)

    @pl.when(ph == 1)
    def _():
        acc = acc_scr[...]
        for r in range(pps):
            pg = st * pps + r
            pblk = p_scr[:, pl.ds(pl.multiple_of(pg * page, page), page)]
            acc = acc + lax.dot_general(pblk, v_pages[r][...].astype(BF16), contract_lanes,
                                        preferred_element_type=F32)
        acc_scr[...] = acc

    @pl.when((ph == 1) & (st == n_steps - 1))
    def _():
        o = acc_scr[...] / l_scr[...]
        out = jnp.zeros((SUBLANES, MOBA_W), F32)
        for h in range(MOBA_HEADS):
            out = out + jnp.where(lane // HEAD_DIM == h, o[h * SUBLANES:(h + 1) * SUBLANES, :], 0.0)
        o_ref[0] = (out * _silu(g_ref[0])).astype(o_ref.dtype)


def _moba_sample(page_table, mq3, mk3, mv3, mg3, cache_k4, cache_v4, layer, *, pps):
    db = mq3.shape[0]
    n_pages = page_table.shape[1]
    page = cache_k4.shape[3]
    n_steps = n_pages // pps
    rows = MOBA_HEADS * SUBLANES
    assert n_pages * page // MOBA_BLOCK <= LANES and page == LANES
    per_b = lambda bi, ph, st, pt: (bi, 0, 0)

    def k_map(r):
        return lambda bi, ph, st, pt: (layer, pt[bi, (1 - ph) * (st * pps + r) + ph * ((n_steps - 1) * pps + r)], 0, 0)

    def v_map(r):
        return lambda bi, ph, st, pt: (layer, pt[bi, ph * (st * pps + r) + (1 - ph) * r], 0, 0)

    small = pl.BlockSpec((1, SUBLANES, MOBA_W), per_b)
    in_specs = [small, small, small, small]
    in_specs += [pl.BlockSpec((None, None, MOBA_W, page), k_map(r)) for r in range(pps)]
    in_specs += [pl.BlockSpec((None, None, MOBA_W, page), v_map(r)) for r in range(pps)]
    width = (n_pages + 1) * page
    return pl.pallas_call(
        functools.partial(_moba_sample_kernel, n_pages=n_pages, pps=pps, page=page),
        grid_spec=pltpu.PrefetchScalarGridSpec(
            num_scalar_prefetch=1,
            grid=(db, 2, n_steps),
            in_specs=in_specs,
            out_specs=pl.BlockSpec((1, SUBLANES, MOBA_W), per_b),
            scratch_shapes=[
                pltpu.VMEM((rows, MOBA_W), F32),
                pltpu.VMEM((rows, width), F32),
                pltpu.VMEM((rows, width), BF16),
                pltpu.VMEM((rows, MOBA_W), F32),
                pltpu.VMEM((rows, 1), F32),
            ],
        ),
        out_shape=jax.ShapeDtypeStruct((db, SUBLANES, MOBA_W), BF16),
        compiler_params=_cparams(("arbitrary", "arbitrary", "arbitrary")),
        name="moba_sample",
    )(page_table, mq3, mk3, mv3, mg3, *([cache_k4] * pps), *([cache_v4] * pps))


def _pool_kernel(u_ref, g_ref, buf_ref, cnt_ref, w_ref, b_ref, sc_ref, o_ref, bufo_ref, x_scr, *, n_tokens, rows):
    length = u_ref.shape[1]
    halo = POOL_BUF + 1
    x_scr[0:halo, :] = buf_ref[0]
    x_scr[halo:, :] = u_ref[0]
    lane = lax.broadcasted_iota(jnp.int32, (rows, POOL_W), 1)
    group = lane // POOL_GC

    for r0 in range(0, length, rows):
        u = x_scr[halo + r0:halo + r0 + rows, :]
        acc = u
        win = jnp.zeros_like(u)
        for s in range(1, max(POOL_WINDOWS)):
            acc = acc + x_scr[halo + r0 - s:halo + r0 - s + rows, :]
            if s + 1 in POOL_WINDOWS:
                win = jnp.where(group == POOL_WINDOWS.index(s + 1), acc, win)
        pooled = win / cnt_ref[r0:r0 + rows, :] - u
        y = jnp.dot(pooled.astype(BF16), w_ref[...], preferred_element_type=F32) + b_ref[...]
        y = y * sc_ref[...]
        o_ref[0, r0:r0 + rows, :] = (y * _silu(g_ref[0, r0:r0 + rows, :])).astype(o_ref.dtype)
    bufo_ref[0] = x_scr[n_tokens + 1:n_tokens + 1 + POOL_BUF, :]


def _pool(zp3, buf16, cnt, w_bd, bias, scale, *, n_tokens, rows):
    b, length, _ = zp3.shape
    const = lambda bi: (0, 0)
    return pl.pallas_call(
        functools.partial(_pool_kernel, n_tokens=n_tokens, rows=rows),
        grid=(b,),
        in_specs=[
            pl.BlockSpec((1, length, POOL_W), lambda bi: (bi, 0, 0)),
            pl.BlockSpec((1, length, POOL_W), lambda bi: (bi, 0, 1)),
            pl.BlockSpec((1, POOL_BUF + 1, POOL_W), lambda bi: (bi, 0, 0)),
            pl.BlockSpec((length, POOL_W), const),
            pl.BlockSpec((POOL_W, POOL_W), const),
            pl.BlockSpec((1, POOL_W), const),
            pl.BlockSpec((1, POOL_W), const),
        ],
        out_specs=[
            pl.BlockSpec((1, length, POOL_W), lambda bi: (bi, 0, 0)),
            pl.BlockSpec((1, POOL_BUF, POOL_W), lambda bi: (bi, 0, 0)),
        ],
        out_shape=[
            jax.ShapeDtypeStruct((b, length, POOL_W), BF16),
            jax.ShapeDtypeStruct((b, POOL_BUF, POOL_W), F32),
        ],
        scratch_shapes=[pltpu.VMEM((POOL_BUF + 1 + length, POOL_W), F32)],
        compiler_params=_cparams(("arbitrary",)),
        name="pool",
    )(zp3, zp3, buf16, cnt, w_bd, bias, scale)


def _out_proj_kernel(mr_ref, mm_ref, mp_ref, x_ref, w_ref, y_ref, mix_scr):
    mix_scr[:, :RET_W] = mr_ref[...]
    mix_scr[:, RET_W:RET_W + MOBA_W] = mm_ref[...]
    mix_scr[:, RET_W + MOBA_W:] = mp_ref[...]
    y_ref[...] = x_ref[...] + jnp.dot(mix_scr[...], w_ref[...], preferred_element_type=F32)


def _out_proj(mr, mm, mp, x2d, w_bf, *, tm):
    m, d = x2d.shape
    d_mix = RET_W + MOBA_W + POOL_W
    row = lambda i: (i, 0)
    return pl.pallas_call(
        _out_proj_kernel,
        grid=(m // tm,),
        in_specs=[
            pl.BlockSpec((tm, RET_W), row),
            pl.BlockSpec((tm, MOBA_W), row),
            pl.BlockSpec((tm, POOL_W), row),
            pl.BlockSpec((tm, d), row),
            pl.BlockSpec((d_mix, d), lambda i: (0, 0)),
        ],
        out_specs=pl.BlockSpec((tm, d), row),
        out_shape=jax.ShapeDtypeStruct((m, d), F32),
        scratch_shapes=[pltpu.VMEM((tm, d_mix), BF16)],
        compiler_params=_cparams(("arbitrary",)),
        name="out_proj",
    )(mr, mm, mp, x2d, w_bf)


def _rope_tables(pos):
    inv = 1.0 / (ROPE_THETA ** (jnp.arange(HALF, dtype=F32) / HALF))
    ang = pos.astype(F32)[:, None] * inv[None, :]
    c, s = jnp.cos(ang), jnp.sin(ang)
    return jnp.concatenate([c, c, c, c], axis=-1), jnp.concatenate([-s, s, -s, s], axis=-1)


def _block_diag(blocks):
    g, n, _ = blocks.shape
    eye = jnp.eye(g, dtype=blocks.dtype)
    return (eye[:, None, :, None] * blocks[:, :, None, :]).reshape(g * n, g * n)


def _pool_counts(pos0, length):
    pos = pos0 + jnp.arange(length)
    w = jnp.repeat(jnp.asarray(POOL_WINDOWS, jnp.int32), POOL_GC)
    return jnp.minimum(pos[:, None] + 1, w[None, :]).astype(F32)


def _layer(x3, pos_tabs, ret_state_bd, ret_tabs, ret_step, pool_buf16, pool_cnt, n_tokens, moba_fn, params, *, tm):
    b, length, d = x3.shape
    nw, w_in, w_out, gnw, qnw, knw, bd, pw_bd, pbias, pscale = params
    x2d = x3.reshape(b * length, d)
    zr, mq, mk, mv, mg, zp = _in_proj(x2d, nw, w_in, pos_tabs[0], pos_tabs[1], qnw, knw, bd, tm=tm)
    three = lambda a: a.reshape(b, length, a.shape[-1])
    mix_r, s_new = _retention(three(zr), ret_state_bd, ret_tabs, gnw, chunk=RET_CHUNK, bb=ret_step[0],
                              pairs=ret_step[1])
    mix_m, k_new, v_new = moba_fn(three(mq), three(mk), three(mv), three(mg))
    mix_p, buf_new = _pool(three(zp), pool_buf16, pool_cnt, pw_bd, pbias, pscale,
                           n_tokens=n_tokens, rows=min(length, 128))
    two = lambda a: a.reshape(b * length, a.shape[-1])
    y = _out_proj(two(mix_r), two(mix_m), two(mix_p), x2d, w_out, tm=tm)
    return y.reshape(b, length, d), s_new, k_new, v_new, buf_new


def _moba_prompt_wrap(mq3, mk3, mv3, mg3):
    mix, kt, vt = _moba_prompt(mq3, mk3, mv3, mg3)
    b, _, length = kt.shape
    rows = lambda t: t.reshape(b, MOBA_HEADS, HEAD_DIM, length).transpose(0, 3, 1, 2)
    return mix, rows(kt), rows(vt)


def _moba_sample_wrap(mq3, mk3, mv3, mg3, *, page_table, cache_k4, cache_v4, layer, pps, n_tokens):
    mix = _moba_sample(page_table, mq3, mk3, mv3, mg3, cache_k4, cache_v4, layer, pps=pps)
    rows = lambda t: t[:, :n_tokens].reshape(t.shape[0], n_tokens, MOBA_HEADS, HEAD_DIM)
    return mix, rows(mk3), rows(mv3)


def kernel(x_prompt, x_sample, cache_k, cache_v, state_ret, state_pool, page_table, norm_w, w_in, w_out, ret_gn_w, q_norm_w, k_norm_w, pool_w, pool_b, pool_scale):
    depth = w_in.shape[0]
    b, seq, d = x_prompt.shape
    db, dec_seq, _ = x_sample.shape
    n_pool, page = cache_k.shape[1], cache_k.shape[2]
    past_len = page_table.shape[1] * page
    assert dec_seq <= SUBLANES and seq % MOBA_BLOCK == 0 and seq % RET_CHUNK == 0 and past_len % MOBA_BLOCK == 0

    p_tabs = _rope_tables(jnp.arange(seq))
    tm_s = db * SUBLANES
    s_tabs = tuple(jnp.tile(t, (db, 1)) for t in _rope_tables(past_len + jnp.arange(SUBLANES)))
    p_ret_tabs = _retention_tables(RET_CHUNK, RET_CHUNK)
    s_ret_tabs = _retention_tables(RET_CHUNK, dec_seq)
    p_cnt = _pool_counts(0, seq)
    s_cnt = _pool_counts(past_len, SUBLANES)
    bd = _block_diag(jnp.full((MOBA_HEADS, HEAD_DIM, HEAD_DIM), 1.0 / HEAD_DIM, F32)).astype(BF16)
    zero_state = jnp.zeros((b, RET_HEADS // 2, LANES, LANES), F32)
    zero_buf = jnp.zeros((b, POOL_BUF + 1, POOL_W), F32)

    cache_k4 = cache_k.transpose(0, 1, 3, 4, 2).reshape(depth, n_pool, MOBA_W, page)
    cache_v4 = cache_v.transpose(0, 1, 3, 4, 2).reshape(depth, n_pool, MOBA_W, page)
    xp = x_prompt
    xs = jnp.pad(x_sample, ((0, 0), (0, SUBLANES - dec_seq), (0, 0)))
    pps = min(16, page_table.shape[1])

    outs = [[] for _ in range(8)]
    for l in range(depth):
        params = (norm_w[l].reshape(1, d), w_in[l].astype(BF16), w_out[l].astype(BF16),
                  ret_gn_w[l].reshape(RET_W // LANES, 1, LANES),
                  jnp.tile(q_norm_w[l], MOBA_HEADS).reshape(1, MOBA_W),
                  jnp.tile(k_norm_w[l], MOBA_HEADS).reshape(1, MOBA_W),
                  bd, _block_diag(pool_w[l]).astype(BF16),
                  pool_b[l].reshape(1, POOL_W), pool_scale[l].reshape(1, POOL_W))
        xp, s_p, k_p, v_p, b_p = _layer(xp, p_tabs, zero_state, p_ret_tabs, (1, 1), zero_buf, p_cnt, seq,
                                        _moba_prompt_wrap, params, tm=512)
        moba_s = functools.partial(_moba_sample_wrap, page_table=page_table, cache_k4=cache_k4,
                                   cache_v4=cache_v4, layer=l, pps=pps, n_tokens=dec_seq)
        xs, s_s, k_s, v_s, b_s = _layer(xs, s_tabs, _to_pair_state(state_ret[l]), s_ret_tabs,
                                        (math.gcd(db, 8), RET_W // LANES),
                                        jnp.pad(state_pool[l], ((0, 0), (1, 0), (0, 0))), s_cnt, dec_seq,
                                        moba_s, params, tm=tm_s)
        for lst, val in zip(outs, (k_p, v_p, k_s, v_s, _from_pair_state(s_p), _from_pair_state(s_s), b_p, b_s)):
            lst.append(val)
    return (xp, xs[:, :dec_seq]) + tuple(jnp.stack(o) for o in outs)
```

```python
import functools
import math

import jax
import jax.numpy as jnp
from jax import lax
from jax.experimental import pallas as pl
from jax.experimental.pallas import tpu as pltpu

F32 = jnp.float32
BF16 = jnp.bfloat16

HEAD_DIM = 64
HALF = HEAD_DIM // 2
LANES = 128
RET_HEADS = 6
MOBA_HEADS = 6
RET_W = RET_HEADS * HEAD_DIM
MOBA_W = MOBA_HEADS * HEAD_DIM
POOL_W = 256
POOL_GC = 64
POOL_WINDOWS = (2, 4, 8, 16)
POOL_BUF = 15
RET_CHUNK = 128
MOBA_BLOCK = 256
MOBA_TOPK = 3
ROPE_THETA = 10000.0
EPS = 1e-6
SUBLANES = 8
NEG_INF = float("-inf")
LOG2_E = 1.4426950408889634

OFF_RQ, OFF_RV, OFF_MQ, OFF_MV, OFF_PU = 0, 2 * RET_W, 4 * RET_W, 4 * RET_W + 2 * MOBA_W, 4 * RET_W + 4 * MOBA_W
D_IN = OFF_PU + 2 * POOL_W

VMEM_LIMIT = 56 * 1024 * 1024


def _silu(x):
    return x / (1.0 + jnp.exp(-x))


def _cparams(sem):
    return pltpu.CompilerParams(dimension_semantics=sem, vmem_limit_bytes=VMEM_LIMIT)


def _in_proj_kernel(x_ref, nw_ref, w_ref, cos_ref, sin_ref, qnw_ref, knw_ref, bd_ref,
                    zr_ref, mq_ref, mk_ref, mv_ref, mg_ref, zp_ref, h_scr, z_scr):
    tm = x_ref.shape[0]
    x = x_ref[...]
    ms = jnp.mean(x * x, axis=-1, keepdims=True)
    h_scr[...] = (x * lax.rsqrt(ms + EPS) * nw_ref[...]).astype(BF16)
    cos = cos_ref[...]
    sin = sin_ref[...]
    lane = lax.broadcasted_iota(jnp.int32, (tm, LANES), 1)
    first_half = (lane & HALF) == 0

    def rope(z):
        partner = jnp.where(first_half, pltpu.roll(z, LANES - HALF, 1), pltpu.roll(z, HALF, 1))
        return z * cos + partner * sin

    def proj(off, width):
        return jnp.dot(h_scr[...], w_ref[:, off:off + width], preferred_element_type=F32)

    z_scr[...] = proj(OFF_RQ, 2 * RET_W)
    for g in range(2 * RET_W // LANES):
        out = rope(z_scr[:, g * LANES:(g + 1) * LANES])
        if g >= RET_W // LANES:
            out = out * (HEAD_DIM ** -0.5)
        zr_ref[:, g * LANES:(g + 1) * LANES] = out
    zr_ref[:, OFF_RV:OFF_MQ] = proj(OFF_RV, 2 * RET_W)
    z_scr[...] = proj(OFF_MQ, 2 * MOBA_W)
    for t, (nw, dst) in enumerate(((qnw_ref, mq_ref), (knw_ref, mk_ref))):
        z = z_scr[:, t * MOBA_W:(t + 1) * MOBA_W]
        msq = jnp.dot((z * z).astype(BF16), bd_ref[...], preferred_element_type=F32)
        z_scr[:, t * MOBA_W:(t + 1) * MOBA_W] = z * lax.rsqrt(msq + EPS) * nw[...]
        for g in range(MOBA_W // LANES):
            c0 = t * MOBA_W + g * LANES
            dst[:, g * LANES:(g + 1) * LANES] = rope(z_scr[:, c0:c0 + LANES])
    z_scr[...] = proj(OFF_MV, 2 * MOBA_W)
    mv_ref[...] = z_scr[:, :MOBA_W]
    mg_ref[...] = z_scr[:, MOBA_W:]
    zp_ref[...] = proj(OFF_PU, 2 * POOL_W)


def _in_proj(x2d, nw, w_bf, cos, sin, qnw, knw, bd, *, tm):
    m, d = x2d.shape
    n_pos = cos.shape[0] // tm
    row = lambda i: (i, 0)
    const = lambda i: (0, 0)
    pos = lambda i: (i % n_pos, 0)
    outs = [(2 * RET_W + 2 * RET_W), MOBA_W, MOBA_W, MOBA_W, MOBA_W, 2 * POOL_W]
    return pl.pallas_call(
        _in_proj_kernel,
        grid=(m // tm,),
        in_specs=[
            pl.BlockSpec((tm, d), row),
            pl.BlockSpec((1, d), const),
            pl.BlockSpec((d, D_IN), const),
            pl.BlockSpec((tm, LANES), pos),
            pl.BlockSpec((tm, LANES), pos),
            pl.BlockSpec((1, MOBA_W), const),
            pl.BlockSpec((1, MOBA_W), const),
            pl.BlockSpec((MOBA_W, MOBA_W), const),
        ],
        out_specs=[pl.BlockSpec((tm, w), row) for w in outs],
        out_shape=[jax.ShapeDtypeStruct((m, w), F32) for w in outs],
        scratch_shapes=[pltpu.VMEM((tm, d), BF16), pltpu.VMEM((tm, 2 * RET_W), F32)],
        compiler_params=_cparams(("arbitrary",)),
        name="in_proj",
    )(x2d, nw, w_bf, cos, sin, qnw, knw, bd)


def _retention_kernel(q_ref, k_ref, v_ref, g_ref, s0_ref, dm_ref, rs_ref, kd_ref, gc_ref, gnw_ref,
                      o_ref, sout_ref, oi_scr, kv_scr, sb_scr, *, chunk):
    bb, length, width = q_ref.shape
    pairs = width // LANES
    rows_in = min(length, chunk)
    n_chunks = max(1, length // chunk)
    lane = lax.broadcasted_iota(jnp.int32, (chunk, LANES), 1)
    head0 = lane < HEAD_DIM
    r = lax.broadcasted_iota(jnp.int32, (LANES, LANES), 0)
    c = lax.broadcasted_iota(jnp.int32, (LANES, LANES), 1)
    same_head = (r < HEAD_DIM) == (c < HEAD_DIM)
    contract_lanes = (((1,), (1,)), ((), ()))

    def head_mean(x):
        m0 = jnp.sum(jnp.where(head0, x, 0.0), axis=1, keepdims=True)
        m1 = jnp.sum(jnp.where(head0, 0.0, x), axis=1, keepdims=True)
        return jnp.where(head0, m0, m1) * (1.0 / HEAD_DIM)

    for bi in range(bb):
        for p in range(pairs):
            slot = bi * pairs + p
            cols = slice(p * LANES, (p + 1) * LANES)
            d0 = dm_ref[2 * p]
            d1 = dm_ref[2 * p + 1]
            rs = rs_ref[p]
            kd = kd_ref[p]
            gc = gc_ref[p]
            gnw = gnw_ref[p]

            def load(ref, ci):
                a = ref[bi, ci * chunk:ci * chunk + rows_in, cols]
                if rows_in < chunk:
                    a = jnp.concatenate([a, jnp.zeros((chunk - rows_in, LANES), F32)], axis=0)
                return a

            for ci in range(n_chunks):
                q = load(q_ref, ci)
                k = load(k_ref, ci)
                kb = k.astype(BF16)
                vb = load(v_ref, ci).astype(BF16)
                q0 = jnp.where(head0, q, 0.0).astype(BF16)
                q1 = jnp.where(head0, 0.0, q).astype(BF16)
                in0 = (lax.dot_general(q0, kb, contract_lanes, preferred_element_type=F32) * d0).astype(BF16)
                in1 = (lax.dot_general(q1, kb, contract_lanes, preferred_element_type=F32) * d1).astype(BF16)
                oi_scr[slot, ci * chunk:(ci + 1) * chunk, :] = jnp.where(
                    head0, jnp.dot(in0, vb, preferred_element_type=F32), jnp.dot(in1, vb, preferred_element_type=F32))
                kdt = (k * kd).T.astype(BF16)
                kv_scr[slot, ci] = jnp.where(same_head, jnp.dot(kdt, vb, preferred_element_type=F32), 0.0)

            s = s0_ref[bi, p]
            for ci in range(n_chunks):
                sb_scr[slot, ci] = s.astype(BF16)
                s = s * gc + kv_scr[slot, ci]
            sout_ref[bi, p] = s

            for ci in range(n_chunks):
                qb = load(q_ref, ci).astype(BF16)
                o = (oi_scr[slot, ci * chunk:(ci + 1) * chunk, :]
                     + jnp.dot(qb, sb_scr[slot, ci], preferred_element_type=F32) * rs)
                mu = head_mean(o)
                oc = o - mu
                var = head_mean(oc * oc)
                on = oc * lax.rsqrt(var + EPS) * gnw
                out = on * _silu(load(g_ref, ci))
                o_ref[bi, ci * chunk:ci * chunk + rows_in, cols] = out[:rows_in].astype(o_ref.dtype)


def _retention(zr3, s0bd, tabs, gnw, *, chunk, bb, pairs):
    b, length, _ = zr3.shape
    n_pairs = RET_W // LANES
    dm, rs, kd, gc = tabs
    w = pairs * LANES
    n_chunks = max(1, length // chunk)
    col = lambda off: (lambda bi, p: (bi, 0, off + p))
    tab = lambda bi, p: (p, 0, 0)
    return pl.pallas_call(
        functools.partial(_retention_kernel, chunk=chunk),
        grid=(b // bb, n_pairs // pairs),
        in_specs=[
            pl.BlockSpec((bb, length, w), col(0)),
            pl.BlockSpec((bb, length, w), col(n_pairs // pairs)),
            pl.BlockSpec((bb, length, w), col(2 * n_pairs // pairs)),
            pl.BlockSpec((bb, length, w), col(3 * n_pairs // pairs)),
            pl.BlockSpec((bb, pairs, LANES, LANES), lambda bi, p: (bi, p, 0, 0)),
            pl.BlockSpec((2 * pairs, chunk, chunk), tab),
            pl.BlockSpec((pairs, chunk, LANES), tab),
            pl.BlockSpec((pairs, chunk, LANES), tab),
            pl.BlockSpec((pairs, 1, LANES), tab),
            pl.BlockSpec((pairs, 1, LANES), tab),
        ],
        out_specs=[
            pl.BlockSpec((bb, length, w), lambda bi, p: (bi, 0, p)),
            pl.BlockSpec((bb, pairs, LANES, LANES), lambda bi, p: (bi, p, 0, 0)),
        ],
        out_shape=[
            jax.ShapeDtypeStruct((b, length, RET_W), BF16),
            jax.ShapeDtypeStruct((b, n_pairs, LANES, LANES), F32),
        ],
        scratch_shapes=[
            pltpu.VMEM((bb * pairs, n_chunks * chunk, LANES), F32),
            pltpu.VMEM((bb * pairs, n_chunks, LANES, LANES), F32),
            pltpu.VMEM((bb * pairs, n_chunks, LANES, LANES), BF16),
        ],
        compiler_params=_cparams(("arbitrary", "arbitrary")),
        name="retention",
    )(zr3, zr3, zr3, zr3, s0bd, dm, rs, kd, gc, gnw)


def _retention_tables(chunk, n_tokens):
    lg = jnp.log(1.0 - 2.0 ** (-5.0 - jnp.arange(RET_HEADS, dtype=F32)))
    i = jnp.arange(chunk, dtype=F32)
    rel = i[:, None] - i[None, :]
    dm = jnp.where(rel[None] >= 0, jnp.exp(rel[None] * lg[:, None, None]), 0.0)
    lg_lanes = jnp.repeat(lg, HEAD_DIM).reshape(RET_W // LANES, 1, LANES)
    rs = jnp.exp((i + 1.0)[None, :, None] * lg_lanes)
    kd = jnp.where((i < n_tokens)[None, :, None], jnp.exp((n_tokens - 1.0 - i)[None, :, None] * lg_lanes), 0.0)
    gc = jnp.exp(float(n_tokens) * lg_lanes)
    return dm, rs, kd, gc


def _to_pair_state(s):
    b = s.shape[0]
    s = s.reshape(b, RET_HEADS // 2, 2, HEAD_DIM, HEAD_DIM)
    z = jnp.zeros_like(s[:, :, 0])
    top = jnp.concatenate([s[:, :, 0], z], axis=-1)
    bot = jnp.concatenate([z, s[:, :, 1]], axis=-1)
    return jnp.concatenate([top, bot], axis=-2)


def _from_pair_state(sbd):
    b = sbd.shape[0]
    s0 = sbd[:, :, :HEAD_DIM, :HEAD_DIM]
    s1 = sbd[:, :, HEAD_DIM:, HEAD_DIM:]
    return jnp.stack([s0, s1], axis=2).reshape(b, RET_HEADS, HEAD_DIM, HEAD_DIM)


def _moba_prompt_kernel(q_ref, k_ref, v_ref, g_ref, o_ref, kt_ref, vt_ref, kh_scr, vt_scr, km_scr, s_scr, p_scr, *,
                        n_blocks):
    blk = MOBA_BLOCK
    sub = blk // 2
    contract_lanes = (((1,), (1,)), ((), ()))
    lane = lax.broadcasted_iota(jnp.int32, (blk, LANES), 1)
    head0 = lane < HEAD_DIM
    for j in range(n_blocks):
        kj = k_ref[0, j * blk:(j + 1) * blk, :]
        kh_scr[0, j] = jnp.where(head0, kj, 0.0).astype(BF16)
        kh_scr[1, j] = jnp.where(head0, 0.0, kj).astype(BF16)
        vt = v_ref[0, j * blk:(j + 1) * blk, :].T
        vt_scr[:, j * blk:(j + 1) * blk] = vt.astype(BF16)
        km_scr[j:j + 1, :] = jnp.sum(kj, axis=0, keepdims=True) * (1.0 / blk)
        kt_ref[0, :, j * blk:(j + 1) * blk] = kj.T
        vt_ref[0, :, j * blk:(j + 1) * blk] = vt

    km = km_scr[...]
    lane8 = lax.broadcasted_iota(jnp.int32, (n_blocks, LANES), 1)
    row8 = lax.broadcasted_iota(jnp.int32, (n_blocks, blk), 0)
    krow = lax.broadcasted_iota(jnp.int32, (blk, blk), 0)
    qcol = lax.broadcasted_iota(jnp.int32, (blk, blk), 1)
    causal = jnp.where(krow <= qcol, 0.0, NEG_INF)
    fold = lambda t: t.reshape(t.shape[0] // SUBLANES, SUBLANES, blk)
    km_parts = []
    for h in range(2):
        kmh = jnp.where((lane8 < HEAD_DIM) == (h == 0), km, 0.0)
        kmh_hi = kmh.astype(BF16)
        km_parts.append((kmh_hi, (kmh - kmh_hi.astype(F32)).astype(BF16)))

    for i in range(n_blocks):
        q = q_ref[0, i * blk:(i + 1) * blk, :] * (HEAD_DIM ** -0.5 * LOG2_E)
        qh = q.astype(BF16)
        buf = i % 2
        halves = []
        for h in range(2):
            selmask = None
            if i > MOBA_TOPK:
                ql = (q - qh.astype(F32)).astype(BF16)
                kmh_hi, kmh_lo = km_parts[h]
                gate = (lax.dot_general(kmh_hi, qh, contract_lanes, preferred_element_type=F32)
                        + lax.dot_general(kmh_lo, qh, contract_lanes, preferred_element_type=F32)
                        + lax.dot_general(kmh_hi, ql, contract_lanes, preferred_element_type=F32))
                rank = jnp.zeros((n_blocks, blk), jnp.int32)
                for jp in range(i):
                    gj = gate[jp:jp + 1, :]
                    rank = rank + jnp.where((gj > gate) | ((gj == gate) & (jp < row8)), 1, 0)
                selmask = jnp.where((rank < MOBA_TOPK) & (row8 < i), 0.0, NEG_INF)
            m8 = None
            for j in range(i + 1):
                for u in range(blk // sub):
                    rows = slice(u * sub, (u + 1) * sub)
                    s = lax.dot_general(kh_scr[h, j, rows, :], qh, contract_lanes, preferred_element_type=F32)
                    if j == i:
                        s = s + causal[rows]
                    elif selmask is not None:
                        s = s + selmask[j:j + 1, :]
                    s_scr[buf, h, j, rows, :] = s
                    tmax = jnp.max(fold(s), axis=0)
                    m8 = tmax if m8 is None else jnp.maximum(m8, tmax)
            m = jnp.max(m8, axis=0, keepdims=True)
            l8 = jnp.zeros((SUBLANES, blk), F32)
            for j in range(i + 1):
                for u in range(blk // sub):
                    rows = slice(u * sub, (u + 1) * sub)
                    p = jnp.exp2(s_scr[buf, h, j, rows, :] - m)
                    l8 = l8 + jnp.sum(fold(p), axis=0)
                    p_scr[buf, h, j * blk + u * sub:j * blk + (u + 1) * sub, :] = p.astype(BF16)
            l = jnp.sum(l8, axis=0, keepdims=True)
            nk = (i + 1) * blk
            ot = jnp.dot(vt_scr[h * HEAD_DIM:(h + 1) * HEAD_DIM, 0:nk], p_scr[buf, h, 0:nk, :],
                         preferred_element_type=F32)
            halves.append(ot / l)
        ot = jnp.concatenate(halves, axis=0)
        o_ref[0, i * blk:(i + 1) * blk, :] = (ot.T * _silu(g_ref[0, i * blk:(i + 1) * blk, :])).astype(o_ref.dtype)


def _moba_prompt(mq3, mk3, mv3, mg3):
    b, length, _ = mq3.shape
    n_pairs = MOBA_W // LANES
    n_blocks = length // MOBA_BLOCK
    rows = pl.BlockSpec((1, length, LANES), lambda bi, p: (bi, 0, p))
    cols = pl.BlockSpec((1, LANES, length), lambda bi, p: (bi, p, 0))
    return pl.pallas_call(
        functools.partial(_moba_prompt_kernel, n_blocks=n_blocks),
        grid=(b, n_pairs),
        in_specs=[rows, rows, rows, rows],
        out_specs=[rows, cols, cols],
        out_shape=[
            jax.ShapeDtypeStruct((b, length, MOBA_W), BF16),
            jax.ShapeDtypeStruct((b, MOBA_W, length), F32),
            jax.ShapeDtypeStruct((b, MOBA_W, length), F32),
        ],
        scratch_shapes=[
            pltpu.VMEM((2, n_blocks, MOBA_BLOCK, LANES), BF16),
            pltpu.VMEM((LANES, length), BF16),
            pltpu.VMEM((n_blocks, LANES), F32),
            pltpu.VMEM((2, 2, n_blocks, MOBA_BLOCK, MOBA_BLOCK), F32),
            pltpu.VMEM((2, 2, length, MOBA_BLOCK), BF16),
        ],
        compiler_params=_cparams(("arbitrary", "arbitrary")),
        name="moba_prompt",
    )(mq3, mk3, mv3, mg3)


def _moba_sample_kernel(pt_ref, q_ref, kn_ref, vn_ref, g_ref, ck_ref, cv_ref, o_ref,
                        kring, vring, ksem, vsem, qs_scr, s_scr, p_scr, *, layer, n_pages, page):
    b = pl.program_id(0)
    last = pl.num_programs(0) - 1
    n_blocks = n_pages * page // MOBA_BLOCK
    group = math.gcd(n_pages, 16)
    rows = MOBA_HEADS * SUBLANES
    contract_lanes = (((1,), (1,)), ((), ()))
    lane = lax.broadcasted_iota(jnp.int32, (SUBLANES, MOBA_W), 1)

    def k_copy(bi, pg):
        return pltpu.make_async_copy(ck_ref.at[layer, pt_ref[bi, pg]], kring.at[pg], ksem.at[pg])

    def v_copy(bi, pg):
        return pltpu.make_async_copy(cv_ref.at[layer, pt_ref[bi, pg]], vring.at[pg], vsem.at[pg])

    @pl.when(b == 0)
    def _():
        for pg in range(n_pages):
            k_copy(0, pg).start()

    q = q_ref[0] * (HEAD_DIM ** -0.5 * LOG2_E)
    for h in range(MOBA_HEADS):
        qs_scr[h * SUBLANES:(h + 1) * SUBLANES, :] = jnp.where(lane // HEAD_DIM == h, q, 0.0)
    qb = qs_scr[...].astype(BF16)

    for g0 in range(0, n_pages, group):
        for pg in range(g0, g0 + group):
            k_copy(b, pg).wait()
        for pg in range(g0, g0 + group):
            s_scr[:, pg * page:(pg + 1) * page] = jnp.dot(qb, kring[pg].astype(BF16), preferred_element_type=F32)
        for pg in range(g0, g0 + group):
            v_copy(b, pg).start()

    col = lax.broadcasted_iota(jnp.int32, (rows, LANES), 1)
    gate = jnp.zeros((rows, LANES), F32)
    for n in range(n_blocks):
        c0 = n * MOBA_BLOCK
        bsum = s_scr[:, c0:c0 + page]
        for u in range(1, MOBA_BLOCK // page):
            bsum = bsum + s_scr[:, c0 + u * page:c0 + (u + 1) * page]
        gate = jnp.where(col == n, jnp.sum(bsum, axis=1, keepdims=True), gate)
    rank = jnp.zeros(gate.shape, jnp.int32)
    for jp in range(n_blocks):
        gj = gate[:, jp:jp + 1]
        rank = rank + jnp.where((gj > gate) | ((gj == gate) & (jp < col)), 1, 0)
    sel = jnp.where(rank < min(MOBA_TOPK, n_blocks), 0.0, NEG_INF)
    own = jnp.concatenate([kn_ref[0], jnp.zeros((page - SUBLANES, MOBA_W), F32)], axis=0).astype(BF16)
    t_idx = lax.broadcasted_iota(jnp.int32, (rows, page), 0) % SUBLANES
    key_idx = lax.broadcasted_iota(jnp.int32, (rows, page), 1)
    s_own = lax.dot_general(qb, own, contract_lanes, preferred_element_type=F32)
    s_own = s_own + jnp.where(key_idx <= t_idx, 0.0, NEG_INF)
    s_scr[:, n_pages * page:] = s_own
    m_part = s_own
    for n in range(n_blocks):
        add = sel[:, n:n + 1]
        for u in range(MOBA_BLOCK // page):
            c0 = n * MOBA_BLOCK + u * page
            sm = s_scr[:, c0:c0 + page] + add
            s_scr[:, c0:c0 + page] = sm
            m_part = jnp.maximum(m_part, sm)
    m = jnp.max(m_part, axis=1, keepdims=True)
    l_part = jnp.zeros((rows, page), F32)
    for n in range(n_pages + 1):
        c0 = n * page
        p = jnp.exp2(s_scr[:, c0:c0 + page] - m)
        l_part = l_part + p
        p_scr[:, c0:c0 + page] = p.astype(BF16)
    l = jnp.sum(l_part, axis=1, keepdims=True)
    vown = jnp.concatenate([vn_ref[0], jnp.zeros((page - SUBLANES, MOBA_W), F32)], axis=0).astype(BF16)
    acc = jnp.dot(p_scr[:, n_pages * page:], vown, preferred_element_type=F32)

    b_next = jnp.minimum(b + 1, last)
    for g0 in range(0, n_pages, group):
        for pg in range(g0, g0 + group):
            v_copy(b, pg).wait()
        for pg in range(g0, g0 + group):
            acc = acc + lax.dot_general(p_scr[:, pg * page:(pg + 1) * page], vring[pg].astype(BF16),
                                        contract_lanes, preferred_element_type=F32)
        for pg in range(g0, g0 + group):
            k_copy(b_next, pg).start()

    @pl.when(b == last)
    def _():
        for pg in range(n_pages):
            k_copy(b_next, pg).wait()

    o = acc / l
    out = jnp.zeros((SUBLANES, MOBA_W), F32)
    for h in range(MOBA_HEADS):
        out = out + jnp.where(lane // HEAD_DIM == h, o[h * SUBLANES:(h + 1) * SUBLANES, :], 0.0)
    o_ref[0] = (out * _silu(g_ref[0])).astype(o_ref.dtype)


def _moba_sample(page_table, mq3, mk3, mv3, mg3, cache_k4, cache_v4, layer):
    db = mq3.shape[0]
    n_pages = page_table.shape[1]
    page = cache_k4.shape[3]
    rows = MOBA_HEADS * SUBLANES
    assert n_pages * page // MOBA_BLOCK <= LANES and page == LANES
    small = pl.BlockSpec((1, SUBLANES, MOBA_W), lambda bi, pt: (bi, 0, 0))
    hbm = pl.BlockSpec(memory_space=pl.ANY)
    width = (n_pages + 1) * page
    return pl.pallas_call(
        functools.partial(_moba_sample_kernel, layer=layer, n_pages=n_pages, page=page),
        grid_spec=pltpu.PrefetchScalarGridSpec(
            num_scalar_prefetch=1,
            grid=(db,),
            in_specs=[small, small, small, small, hbm, hbm],
            out_specs=small,
            scratch_shapes=[
                pltpu.VMEM((n_pages, MOBA_W, page), F32),
                pltpu.VMEM((n_pages, MOBA_W, page), F32),
                pltpu.SemaphoreType.DMA((n_pages,)),
                pltpu.SemaphoreType.DMA((n_pages,)),
                pltpu.VMEM((rows, MOBA_W), F32),
                pltpu.VMEM((rows, width), F32),
                pltpu.VMEM((rows, width), BF16),
            ],
        ),
        out_shape=jax.ShapeDtypeStruct((db, SUBLANES, MOBA_W), BF16),
        compiler_params=_cparams(("arbitrary",)),
        name="moba_sample",
    )(page_table, mq3, mk3, mv3, mg3, cache_k4, cache_v4)


def _pool_kernel(u_ref, g_ref, buf_ref, cnt_ref, w_ref, b_ref, sc_ref, o_ref, bufo_ref, x_scr, *, n_tokens, rows):
    length = u_ref.shape[1]
    halo = POOL_BUF + 1
    x_scr[0:halo, :] = buf_ref[0]
    x_scr[halo:, :] = u_ref[0]
    lane = lax.broadcasted_iota(jnp.int32, (rows, POOL_W), 1)
    group = lane // POOL_GC

    for r0 in range(0, length, rows):
        u = x_scr[halo + r0:halo + r0 + rows, :]
        acc = u
        win = jnp.zeros_like(u)
        for s in range(1, max(POOL_WINDOWS)):
            acc = acc + x_scr[halo + r0 - s:halo + r0 - s + rows, :]
            if s + 1 in POOL_WINDOWS:
                win = jnp.where(group == POOL_WINDOWS.index(s + 1), acc, win)
        pooled = win / cnt_ref[r0:r0 + rows, :] - u
        y = jnp.dot(pooled.astype(BF16), w_ref[...], preferred_element_type=F32) + b_ref[...]
        y = y * sc_ref[...]
        o_ref[0, r0:r0 + rows, :] = (y * _silu(g_ref[0, r0:r0 + rows, :])).astype(o_ref.dtype)
    bufo_ref[0] = x_scr[n_tokens + 1:n_tokens + 1 + POOL_BUF, :]


def _pool(zp3, buf16, cnt, w_bd, bias, scale, *, n_tokens, rows):
    b, length, _ = zp3.shape
    const = lambda bi: (0, 0)
    return pl.pallas_call(
        functools.partial(_pool_kernel, n_tokens=n_tokens, rows=rows),
        grid=(b,),
        in_specs=[
            pl.BlockSpec((1, length, POOL_W), lambda bi: (bi, 0, 0)),
            pl.BlockSpec((1, length, POOL_W), lambda bi: (bi, 0, 1)),
            pl.BlockSpec((1, POOL_BUF + 1, POOL_W), lambda bi: (bi, 0, 0)),
            pl.BlockSpec((length, POOL_W), const),
            pl.BlockSpec((POOL_W, POOL_W), const),
            pl.BlockSpec((1, POOL_W), const),
            pl.BlockSpec((1, POOL_W), const),
        ],
        out_specs=[
            pl.BlockSpec((1, length, POOL_W), lambda bi: (bi, 0, 0)),
            pl.BlockSpec((1, POOL_BUF, POOL_W), lambda bi: (bi, 0, 0)),
        ],
        out_shape=[
            jax.ShapeDtypeStruct((b, length, POOL_W), BF16),
            jax.ShapeDtypeStruct((b, POOL_BUF, POOL_W), F32),
        ],
        scratch_shapes=[pltpu.VMEM((POOL_BUF + 1 + length, POOL_W), F32)],
        compiler_params=_cparams(("arbitrary",)),
        name="pool",
    )(zp3, zp3, buf16, cnt, w_bd, bias, scale)


def _out_proj_kernel(mr_ref, mm_ref, mp_ref, x_ref, w_ref, y_ref, mix_scr):
    mix_scr[:, :RET_W] = mr_ref[...]
    mix_scr[:, RET_W:RET_W + MOBA_W] = mm_ref[...]
    mix_scr[:, RET_W + MOBA_W:] = mp_ref[...]
    y_ref[...] = x_ref[...] + jnp.dot(mix_scr[...], w_ref[...], preferred_element_type=F32)


def _out_proj(mr, mm, mp, x2d, w_bf, *, tm):
    m, d = x2d.shape
    d_mix = RET_W + MOBA_W + POOL_W
    row = lambda i: (i, 0)
    return pl.pallas_call(
        _out_proj_kernel,
        grid=(m // tm,),
        in_specs=[
            pl.BlockSpec((tm, RET_W), row),
            pl.BlockSpec((tm, MOBA_W), row),
            pl.BlockSpec((tm, POOL_W), row),
            pl.BlockSpec((tm, d), row),
            pl.BlockSpec((d_mix, d), lambda i: (0, 0)),
        ],
        out_specs=pl.BlockSpec((tm, d), row),
        out_shape=jax.ShapeDtypeStruct((m, d), F32),
        scratch_shapes=[pltpu.VMEM((tm, d_mix), BF16)],
        compiler_params=_cparams(("arbitrary",)),
        name="out_proj",
    )(mr, mm, mp, x2d, w_bf)


def _rope_tables(pos):
    inv = 1.0 / (ROPE_THETA ** (jnp.arange(HALF, dtype=F32) / HALF))
    ang = pos.astype(F32)[:, None] * inv[None, :]
    c, s = jnp.cos(ang), jnp.sin(ang)
    return jnp.concatenate([c, c, c, c], axis=-1), jnp.concatenate([-s, s, -s, s], axis=-1)


def _block_diag(blocks):
    g, n, _ = blocks.shape
    eye = jnp.eye(g, dtype=blocks.dtype)
    return (eye[:, None, :, None] * blocks[:, :, None, :]).reshape(g * n, g * n)


def _pool_counts(pos0, length):
    pos = pos0 + jnp.arange(length)
    w = jnp.repeat(jnp.asarray(POOL_WINDOWS, jnp.int32), POOL_GC)
    return jnp.minimum(pos[:, None] + 1, w[None, :]).astype(F32)


def _layer(x3, pos_tabs, ret_state_bd, ret_tabs, ret_step, pool_buf16, pool_cnt, n_tokens, moba_fn, params, *, tm):
    b, length, d = x3.shape
    nw, w_in, w_out, gnw, qnw, knw, bd, pw_bd, pbias, pscale = params
    x2d = x3.reshape(b * length, d)
    zr, mq, mk, mv, mg, zp = _in_proj(x2d, nw, w_in, pos_tabs[0], pos_tabs[1], qnw, knw, bd, tm=tm)
    three = lambda a: a.reshape(b, length, a.shape[-1])
    mix_r, s_new = _retention(three(zr), ret_state_bd, ret_tabs, gnw, chunk=RET_CHUNK, bb=ret_step[0],
                              pairs=ret_step[1])
    mix_m, k_new, v_new = moba_fn(three(mq), three(mk), three(mv), three(mg))
    mix_p, buf_new = _pool(three(zp), pool_buf16, pool_cnt, pw_bd, pbias, pscale,
                           n_tokens=n_tokens, rows=min(length, 128))
    two = lambda a: a.reshape(b * length, a.shape[-1])
    y = _out_proj(two(mix_r), two(mix_m), two(mix_p), x2d, w_out, tm=tm)
    return y.reshape(b, length, d), s_new, k_new, v_new, buf_new


def _moba_prompt_wrap(mq3, mk3, mv3, mg3):
    mix, kt, vt = _moba_prompt(mq3, mk3, mv3, mg3)
    b, _, length = kt.shape
    rows = lambda t: t.reshape(b, MOBA_HEADS, HEAD_DIM, length).transpose(0, 3, 1, 2)
    return mix, rows(kt), rows(vt)


def _moba_sample_wrap(mq3, mk3, mv3, mg3, *, page_table, cache_k4, cache_v4, layer, n_tokens):
    mix = _moba_sample(page_table, mq3, mk3, mv3, mg3, cache_k4, cache_v4, layer)
    rows = lambda t: t[:, :n_tokens].reshape(t.shape[0], n_tokens, MOBA_HEADS, HEAD_DIM)
    return mix, rows(mk3), rows(mv3)


def kernel(x_prompt, x_sample, cache_k, cache_v, state_ret, state_pool, page_table, norm_w, w_in, w_out, ret_gn_w, q_norm_w, k_norm_w, pool_w, pool_b, pool_scale):
    depth = w_in.shape[0]
    b, seq, d = x_prompt.shape
    db, dec_seq, _ = x_sample.shape
    n_pool, page = cache_k.shape[1], cache_k.shape[2]
    past_len = page_table.shape[1] * page
    assert dec_seq <= SUBLANES and seq % MOBA_BLOCK == 0 and seq % RET_CHUNK == 0 and past_len % MOBA_BLOCK == 0

    p_tabs = _rope_tables(jnp.arange(seq))
    tm_s = db * SUBLANES
    s_tabs = tuple(jnp.tile(t, (db, 1)) for t in _rope_tables(past_len + jnp.arange(SUBLANES)))
    p_ret_tabs = _retention_tables(RET_CHUNK, RET_CHUNK)
    s_ret_tabs = _retention_tables(RET_CHUNK, dec_seq)
    p_cnt = _pool_counts(0, seq)
    s_cnt = _pool_counts(past_len, SUBLANES)
    bd = _block_diag(jnp.full((MOBA_HEADS, HEAD_DIM, HEAD_DIM), 1.0 / HEAD_DIM, F32)).astype(BF16)
    zero_state = jnp.zeros((b, RET_HEADS // 2, LANES, LANES), F32)
    zero_buf = jnp.zeros((b, POOL_BUF + 1, POOL_W), F32)

    cache_k4 = cache_k.transpose(0, 1, 3, 4, 2).reshape(depth, n_pool, MOBA_W, page)
    cache_v4 = cache_v.transpose(0, 1, 3, 4, 2).reshape(depth, n_pool, MOBA_W, page)
    xp = x_prompt
    xs = jnp.pad(x_sample, ((0, 0), (0, SUBLANES - dec_seq), (0, 0)))

    outs = [[] for _ in range(8)]
    for l in range(depth):
        params = (norm_w[l].reshape(1, d), w_in[l].astype(BF16), w_out[l].astype(BF16),
                  ret_gn_w[l].reshape(RET_W // LANES, 1, LANES),
                  jnp.tile(q_norm_w[l], MOBA_HEADS).reshape(1, MOBA_W),
                  jnp.tile(k_norm_w[l], MOBA_HEADS).reshape(1, MOBA_W),
                  bd, _block_diag(pool_w[l]).astype(BF16),
                  pool_b[l].reshape(1, POOL_W), pool_scale[l].reshape(1, POOL_W))
        xp, s_p, k_p, v_p, b_p = _layer(xp, p_tabs, zero_state, p_ret_tabs, (1, 1), zero_buf, p_cnt, seq,
                                        _moba_prompt_wrap, params, tm=512)
        moba_s = functools.partial(_moba_sample_wrap, page_table=page_table, cache_k4=cache_k4,
                                   cache_v4=cache_v4, layer=l, n_tokens=dec_seq)
        xs, s_s, k_s, v_s, b_s = _layer(xs, s_tabs, _to_pair_state(state_ret[l]), s_ret_tabs,
                                        (math.gcd(db, 8), RET_W // LANES),
                                        jnp.pad(state_pool[l], ((0, 0), (1, 0), (0, 0))), s_cnt, dec_seq,
                                        moba_s, params, tm=tm_s)
        for lst, val in zip(outs, (k_p, v_p, k_s, v_s, _from_pair_state(s_p), _from_pair_state(s_s), b_p, b_s)):
            lst.append(val)
    return (xp, xs[:, :dec_seq]) + tuple(jnp.stack(o) for o in outs)
```

```python
import functools
import math

import jax
import jax.numpy as jnp
from jax import lax
from jax.experimental import pallas as pl
from jax.experimental.pallas import tpu as pltpu

F32 = jnp.float32
BF16 = jnp.bfloat16

HEAD_DIM = 64
HALF = HEAD_DIM // 2
LANES = 128
RET_HEADS = 6
MOBA_HEADS = 6
RET_W = RET_HEADS * HEAD_DIM
MOBA_W = MOBA_HEADS * HEAD_DIM
POOL_W = 256
POOL_GC = 64
POOL_WINDOWS = (2, 4, 8, 16)
POOL_BUF = 15
RET_CHUNK = 128
MOBA_BLOCK = 256
MOBA_TOPK = 3
ROPE_THETA = 10000.0
EPS = 1e-6
SUBLANES = 8
NEG_INF = float("-inf")
LOG2_E = 1.4426950408889634

OFF_RQ, OFF_RV, OFF_MQ, OFF_MV, OFF_PU = 0, 2 * RET_W, 4 * RET_W, 4 * RET_W + 2 * MOBA_W, 4 * RET_W + 4 * MOBA_W
D_IN = OFF_PU + 2 * POOL_W

VMEM_LIMIT = 56 * 1024 * 1024


def _silu(x):
    return x / (1.0 + jnp.exp(-x))


def _cparams(sem, flags=None):
    return pltpu.CompilerParams(dimension_semantics=sem, vmem_limit_bytes=VMEM_LIMIT, flags=flags)


def _in_proj_kernel(x_ref, nw_ref, w_ref, cos_ref, sin_ref, qnw_ref, knw_ref, bd_ref,
                    zr_ref, mq_ref, mk_ref, mv_ref, mg_ref, zp_ref, h_scr, z_scr):
    tm = x_ref.shape[0]
    x = x_ref[...]
    ms = jnp.mean(x * x, axis=-1, keepdims=True)
    h_scr[...] = (x * lax.rsqrt(ms + EPS) * nw_ref[...]).astype(BF16)
    cos = cos_ref[...]
    sin = sin_ref[...]
    lane = lax.broadcasted_iota(jnp.int32, (tm, LANES), 1)
    first_half = (lane & HALF) == 0

    def rope(z):
        partner = jnp.where(first_half, pltpu.roll(z, LANES - HALF, 1), pltpu.roll(z, HALF, 1))
        return z * cos + partner * sin

    def proj(off, width):
        return jnp.dot(h_scr[...], w_ref[:, off:off + width], preferred_element_type=F32)

    z_scr[...] = proj(OFF_RQ, 2 * RET_W)
    for g in range(2 * RET_W // LANES):
        out = rope(z_scr[:, g * LANES:(g + 1) * LANES])
        if g >= RET_W // LANES:
            out = out * (HEAD_DIM ** -0.5)
        zr_ref[:, g * LANES:(g + 1) * LANES] = out
    zr_ref[:, OFF_RV:OFF_MQ] = proj(OFF_RV, 2 * RET_W)
    z_scr[...] = proj(OFF_MQ, 2 * MOBA_W)
    for t, (nw, dst) in enumerate(((qnw_ref, mq_ref), (knw_ref, mk_ref))):
        z = z_scr[:, t * MOBA_W:(t + 1) * MOBA_W]
        msq = jnp.dot((z * z).astype(BF16), bd_ref[...], preferred_element_type=F32)
        z_scr[:, t * MOBA_W:(t + 1) * MOBA_W] = z * lax.rsqrt(msq + EPS) * nw[...]
        for g in range(MOBA_W // LANES):
            c0 = t * MOBA_W + g * LANES
            dst[:, g * LANES:(g + 1) * LANES] = rope(z_scr[:, c0:c0 + LANES])
    z_scr[...] = proj(OFF_MV, 2 * MOBA_W)
    mv_ref[...] = z_scr[:, :MOBA_W]
    mg_ref[...] = z_scr[:, MOBA_W:]
    zp_ref[...] = proj(OFF_PU, 2 * POOL_W)


def _in_proj(x2d, nw, w_bf, cos, sin, qnw, knw, bd, *, tm):
    m, d = x2d.shape
    n_pos = cos.shape[0] // tm
    row = lambda i: (i, 0)
    const = lambda i: (0, 0)
    pos = lambda i: (i % n_pos, 0)
    outs = [(2 * RET_W + 2 * RET_W), MOBA_W, MOBA_W, MOBA_W, MOBA_W, 2 * POOL_W]
    return pl.pallas_call(
        _in_proj_kernel,
        grid=(m // tm,),
        in_specs=[
            pl.BlockSpec((tm, d), row),
            pl.BlockSpec((1, d), const),
            pl.BlockSpec((d, D_IN), const),
            pl.BlockSpec((tm, LANES), pos),
            pl.BlockSpec((tm, LANES), pos),
            pl.BlockSpec((1, MOBA_W), const),
            pl.BlockSpec((1, MOBA_W), const),
            pl.BlockSpec((MOBA_W, MOBA_W), const),
        ],
        out_specs=[pl.BlockSpec((tm, w), row) for w in outs],
        out_shape=[jax.ShapeDtypeStruct((m, w), F32) for w in outs],
        scratch_shapes=[pltpu.VMEM((tm, d), BF16), pltpu.VMEM((tm, 2 * RET_W), F32)],
        compiler_params=_cparams(("arbitrary",)),
        name="in_proj",
    )(x2d, nw, w_bf, cos, sin, qnw, knw, bd)


def _retention_kernel(q_ref, k_ref, v_ref, g_ref, s0_ref, dm_ref, rs_ref, kd_ref, gc_ref, gnw_ref,
                      o_ref, sout_ref, oi_scr, kv_scr, sb_scr, *, chunk):
    bb, length, width = q_ref.shape
    pairs = width // LANES
    rows_in = min(length, chunk)
    n_chunks = max(1, length // chunk)
    lane = lax.broadcasted_iota(jnp.int32, (chunk, LANES), 1)
    head0 = lane < HEAD_DIM
    r = lax.broadcasted_iota(jnp.int32, (LANES, LANES), 0)
    c = lax.broadcasted_iota(jnp.int32, (LANES, LANES), 1)
    same_head = (r < HEAD_DIM) == (c < HEAD_DIM)
    contract_lanes = (((1,), (1,)), ((), ()))

    def head_mean(x):
        m0 = jnp.sum(jnp.where(head0, x, 0.0), axis=1, keepdims=True)
        m1 = jnp.sum(jnp.where(head0, 0.0, x), axis=1, keepdims=True)
        return jnp.where(head0, m0, m1) * (1.0 / HEAD_DIM)

    def load(ref, bi, p, ci):
        a = ref[bi, ci * chunk:ci * chunk + rows_in, p * LANES:(p + 1) * LANES]
        if rows_in < chunk:
            a = jnp.concatenate([a, jnp.zeros((chunk - rows_in, LANES), F32)], axis=0)
        return a

    items = [(bi, p, ci) for bi in range(bb) for p in range(pairs) for ci in range(n_chunks)]

    def a1(bi, p, ci):
        q = load(q_ref, bi, p, ci)
        k = load(k_ref, bi, p, ci)
        kb = k.astype(BF16)
        vb = load(v_ref, bi, p, ci).astype(BF16)
        q0 = jnp.where(head0, q, 0.0).astype(BF16)
        q1 = jnp.where(head0, 0.0, q).astype(BF16)
        in0 = (lax.dot_general(q0, kb, contract_lanes, preferred_element_type=F32) * dm_ref[2 * p]).astype(BF16)
        in1 = (lax.dot_general(q1, kb, contract_lanes, preferred_element_type=F32) * dm_ref[2 * p + 1]).astype(BF16)
        return bi, p, ci, k, vb, in0, in1

    def a2(st):
        bi, p, ci, k, vb, in0, in1 = st
        slot = bi * pairs + p
        oi_scr[slot, ci * chunk:(ci + 1) * chunk, :] = jnp.where(
            head0, jnp.dot(in0, vb, preferred_element_type=F32), jnp.dot(in1, vb, preferred_element_type=F32))
        kdt = (k * kd_ref[p]).T.astype(BF16)
        kv_scr[slot, ci] = jnp.where(same_head, jnp.dot(kdt, vb, preferred_element_type=F32), 0.0)

    st = None
    for item in items + [None]:
        nxt = a1(*item) if item is not None else None
        if st is not None:
            a2(st)
        st = nxt

    for bi in range(bb):
        for p in range(pairs):
            slot = bi * pairs + p
            s = s0_ref[bi, p]
            for ci in range(n_chunks):
                sb_scr[slot, ci] = s.astype(BF16)
                s = s * gc_ref[p] + kv_scr[slot, ci]
            sout_ref[bi, p] = s

    def c1(bi, p, ci):
        slot = bi * pairs + p
        qb = load(q_ref, bi, p, ci).astype(BF16)
        o = (oi_scr[slot, ci * chunk:(ci + 1) * chunk, :]
             + jnp.dot(qb, sb_scr[slot, ci], preferred_element_type=F32) * rs_ref[p])
        return bi, p, ci, o, head_mean(o)

    def c2(st):
        bi, p, ci, o, mu = st
        oc = o - mu
        return bi, p, ci, oc, head_mean(oc * oc)

    def c3(st):
        bi, p, ci, oc, var = st
        on = oc * lax.rsqrt(var + EPS) * gnw_ref[p]
        out = on * _silu(load(g_ref, bi, p, ci))
        o_ref[bi, ci * chunk:ci * chunk + rows_in, p * LANES:(p + 1) * LANES] = out[:rows_in].astype(o_ref.dtype)

    s1 = s2 = None
    for item in items + [None, None]:
        new1 = c1(*item) if item is not None else None
        new2 = c2(s1) if s1 is not None else None
        if s2 is not None:
            c3(s2)
        s1, s2 = new1, new2


def _retention(zr3, s0bd, tabs, gnw, *, chunk, bb, pairs):
    b, length, _ = zr3.shape
    n_pairs = RET_W // LANES
    dm, rs, kd, gc = tabs
    w = pairs * LANES
    n_chunks = max(1, length // chunk)
    col = lambda off: (lambda bi, p: (bi, 0, off + p))
    tab = lambda bi, p: (p, 0, 0)
    return pl.pallas_call(
        functools.partial(_retention_kernel, chunk=chunk),
        grid=(b // bb, n_pairs // pairs),
        in_specs=[
            pl.BlockSpec((bb, length, w), col(0)),
            pl.BlockSpec((bb, length, w), col(n_pairs // pairs)),
            pl.BlockSpec((bb, length, w), col(2 * n_pairs // pairs)),
            pl.BlockSpec((bb, length, w), col(3 * n_pairs // pairs)),
            pl.BlockSpec((bb, pairs, LANES, LANES), lambda bi, p: (bi, p, 0, 0)),
            pl.BlockSpec((2 * pairs, chunk, chunk), tab),
            pl.BlockSpec((pairs, chunk, LANES), tab),
            pl.BlockSpec((pairs, chunk, LANES), tab),
            pl.BlockSpec((pairs, 1, LANES), tab),
            pl.BlockSpec((pairs, 1, LANES), tab),
        ],
        out_specs=[
            pl.BlockSpec((bb, length, w), lambda bi, p: (bi, 0, p)),
            pl.BlockSpec((bb, pairs, LANES, LANES), lambda bi, p: (bi, p, 0, 0)),
        ],
        out_shape=[
            jax.ShapeDtypeStruct((b, length, RET_W), BF16),
            jax.ShapeDtypeStruct((b, n_pairs, LANES, LANES), F32),
        ],
        scratch_shapes=[
            pltpu.VMEM((bb * pairs, n_chunks * chunk, LANES), F32),
            pltpu.VMEM((bb * pairs, n_chunks, LANES, LANES), F32),
            pltpu.VMEM((bb * pairs, n_chunks, LANES, LANES), BF16),
        ],
        compiler_params=_cparams(("arbitrary", "arbitrary")),
        name="retention",
    )(zr3, zr3, zr3, zr3, s0bd, dm, rs, kd, gc, gnw)


def _retention_tables(chunk, n_tokens):
    lg = jnp.log(1.0 - 2.0 ** (-5.0 - jnp.arange(RET_HEADS, dtype=F32)))
    i = jnp.arange(chunk, dtype=F32)
    rel = i[:, None] - i[None, :]
    dm = jnp.where(rel[None] >= 0, jnp.exp(rel[None] * lg[:, None, None]), 0.0)
    lg_lanes = jnp.repeat(lg, HEAD_DIM).reshape(RET_W // LANES, 1, LANES)
    rs = jnp.exp((i + 1.0)[None, :, None] * lg_lanes)
    kd = jnp.where((i < n_tokens)[None, :, None], jnp.exp((n_tokens - 1.0 - i)[None, :, None] * lg_lanes), 0.0)
    gc = jnp.exp(float(n_tokens) * lg_lanes)
    return dm, rs, kd, gc


def _to_pair_state(s):
    b = s.shape[0]
    s = s.reshape(b, RET_HEADS // 2, 2, HEAD_DIM, HEAD_DIM)
    z = jnp.zeros_like(s[:, :, 0])
    top = jnp.concatenate([s[:, :, 0], z], axis=-1)
    bot = jnp.concatenate([z, s[:, :, 1]], axis=-1)
    return jnp.concatenate([top, bot], axis=-2)


def _from_pair_state(sbd):
    b = sbd.shape[0]
    s0 = sbd[:, :, :HEAD_DIM, :HEAD_DIM]
    s1 = sbd[:, :, HEAD_DIM:, HEAD_DIM:]
    return jnp.stack([s0, s1], axis=2).reshape(b, RET_HEADS, HEAD_DIM, HEAD_DIM)


def _moba_prompt_kernel(q_ref, k_ref, v_ref, g_ref, o_ref, kt_ref, vt_ref, kh_scr, vt_scr, km_scr, s_scr, p_scr, *,
                        n_blocks):
    blk = MOBA_BLOCK
    sub = blk
    contract_lanes = (((1,), (1,)), ((), ()))
    lane = lax.broadcasted_iota(jnp.int32, (blk, LANES), 1)
    head0 = lane < HEAD_DIM
    for j in range(n_blocks):
        kj = k_ref[0, j * blk:(j + 1) * blk, :]
        kh_scr[0, j * blk:(j + 1) * blk, :] = jnp.where(head0, kj, 0.0).astype(BF16)
        kh_scr[1, j * blk:(j + 1) * blk, :] = jnp.where(head0, 0.0, kj).astype(BF16)
        vt = v_ref[0, j * blk:(j + 1) * blk, :].T
        vt_scr[:, j * blk:(j + 1) * blk] = vt.astype(BF16)
        km_scr[j:j + 1, :] = jnp.sum(kj, axis=0, keepdims=True) * (1.0 / blk)
        kt_ref[0, :, j * blk:(j + 1) * blk] = kj.T
        vt_ref[0, :, j * blk:(j + 1) * blk] = vt

    km = km_scr[...]
    lane8 = lax.broadcasted_iota(jnp.int32, (n_blocks, LANES), 1)
    row8 = lax.broadcasted_iota(jnp.int32, (n_blocks, blk), 0)
    krow = lax.broadcasted_iota(jnp.int32, (blk, blk), 0)
    qcol = lax.broadcasted_iota(jnp.int32, (blk, blk), 1)
    causal = jnp.where(krow <= qcol, 0.0, NEG_INF)
    fold = lambda t: t.reshape(t.shape[0] // SUBLANES, SUBLANES, blk)
    km_parts = []
    for h in range(2):
        kmh = jnp.where((lane8 < HEAD_DIM) == (h == 0), km, 0.0)
        kmh_hi = kmh.astype(BF16)
        km_parts.append((kmh_hi, (kmh - kmh_hi.astype(F32)).astype(BF16)))

    heads = range(2)
    tiles = lambda i: [(j, u, h) for j in range(i + 1) for u in range(blk // sub) for h in heads]

    def begin(i):
        q = q_ref[0, i * blk:(i + 1) * blk, :] * (HEAD_DIM ** -0.5 * LOG2_E)
        qh = q.astype(BF16)
        selmask = [None, None]
        if i > MOBA_TOPK:
            ql = (q - qh.astype(F32)).astype(BF16)
            for h in heads:
                kmh_hi, kmh_lo = km_parts[h]
                gate = (lax.dot_general(kmh_hi, qh, contract_lanes, preferred_element_type=F32)
                        + lax.dot_general(kmh_lo, qh, contract_lanes, preferred_element_type=F32)
                        + lax.dot_general(kmh_hi, ql, contract_lanes, preferred_element_type=F32))
                rank = jnp.zeros((n_blocks, blk), jnp.int32)
                for jp in range(i):
                    gj = gate[jp:jp + 1, :]
                    rank = rank + jnp.where((gj > gate) | ((gj == gate) & (jp < row8)), 1, 0)
                selmask[h] = jnp.where((rank < MOBA_TOPK) & (row8 < i), 0.0, NEG_INF)
        return dict(i=i, buf=i % 2, qh=qh, selmask=selmask, m8=[None, None], m=None,
                    l8=[jnp.zeros((SUBLANES, blk), F32) for _ in heads])

    def score_tile(st, j, u, h):
        i = st["i"]
        rows = slice(j * blk + u * sub, j * blk + (u + 1) * sub)
        s = lax.dot_general(kh_scr[h, rows, :], st["qh"], contract_lanes, preferred_element_type=F32)
        if j == i:
            s = s + causal[u * sub:(u + 1) * sub]
        s_scr[st["buf"], h, rows, :] = s
        tmax = jnp.max(fold(s), axis=0)
        if j < i and st["selmask"][h] is not None:
            tmax = tmax + st["selmask"][h][j:j + 1, :]
        st["m8"][h] = tmax if st["m8"][h] is None else jnp.maximum(st["m8"][h], tmax)

    def prob_tile(st, j, u, h):
        i = st["i"]
        if st["m"] is None:
            st["m"] = [jnp.max(st["m8"][hh], axis=0, keepdims=True) for hh in heads]
        rows = slice(j * blk + u * sub, j * blk + (u + 1) * sub)
        sm = st["selmask"][h]
        shift = -st["m"][h] if (j == i or sm is None) else sm[j:j + 1, :] - st["m"][h]
        p = jnp.exp2(s_scr[st["buf"], h, rows, :] + shift)
        st["l8"][h] = st["l8"][h] + jnp.sum(fold(p), axis=0)
        p_scr[st["buf"], h, rows, :] = p.astype(BF16)

    def finish(st):
        i = st["i"]
        nk = (i + 1) * blk
        halves = []
        for h in heads:
            l = jnp.sum(st["l8"][h], axis=0, keepdims=True)
            ot = jnp.dot(vt_scr[h * HEAD_DIM:(h + 1) * HEAD_DIM, 0:nk], p_scr[st["buf"], h, 0:nk, :],
                         preferred_element_type=F32)
            halves.append(ot / l)
        ot = jnp.concatenate(halves, axis=0)
        o_ref[0, i * blk:(i + 1) * blk, :] = (ot.T * _silu(g_ref[0, i * blk:(i + 1) * blk, :])).astype(o_ref.dtype)

    cur = begin(0)
    for t in tiles(0):
        score_tile(cur, *t)
    prev = None
    for i in range(n_blocks):
        nxt = begin(i + 1) if i + 1 < n_blocks else None
        a, b = tiles(i), (tiles(i + 1) if nxt is not None else [])
        for k in range(max(len(a), len(b))):
            if k == 2 and prev is not None:
                finish(prev)
                prev = None
            if k < len(b):
                score_tile(nxt, *b[k])
            if k < len(a):
                prob_tile(cur, *a[k])
        if prev is not None:
            finish(prev)
        prev, cur = cur, nxt
    finish(prev)


def _moba_prompt(mq3, mk3, mv3, mg3):
    b, length, _ = mq3.shape
    n_pairs = MOBA_W // LANES
    n_blocks = length // MOBA_BLOCK
    rows = pl.BlockSpec((1, length, LANES), lambda bi, p: (bi, 0, p))
    cols = pl.BlockSpec((1, LANES, length), lambda bi, p: (bi, p, 0))
    return pl.pallas_call(
        functools.partial(_moba_prompt_kernel, n_blocks=n_blocks),
        grid=(b, n_pairs),
        in_specs=[rows, rows, rows, rows],
        out_specs=[rows, cols, cols],
        out_shape=[
            jax.ShapeDtypeStruct((b, length, MOBA_W), BF16),
            jax.ShapeDtypeStruct((b, MOBA_W, length), F32),
            jax.ShapeDtypeStruct((b, MOBA_W, length), F32),
        ],
        scratch_shapes=[
            pltpu.VMEM((2, length, LANES), BF16),
            pltpu.VMEM((LANES, length), BF16),
            pltpu.VMEM((n_blocks, LANES), F32),
            pltpu.VMEM((2, 2, length, MOBA_BLOCK), F32),
            pltpu.VMEM((2, 2, length, MOBA_BLOCK), BF16),
        ],
        compiler_params=_cparams(("arbitrary", "arbitrary")),
        name="moba_prompt",
    )(mq3, mk3, mv3, mg3)


def _moba_sample_kernel(pt_ref, q_ref, kn_ref, vn_ref, g_ref, ck_ref, cv_ref, o_ref,
                        kring, vring, ksem, vsem, qs_scr, s_scr, p_scr, *, layer, n_pages, page):
    b = pl.program_id(0)
    last = pl.num_programs(0) - 1
    n_blocks = n_pages * page // MOBA_BLOCK
    group = math.gcd(n_pages, 16)
    rows = MOBA_HEADS * SUBLANES
    contract_lanes = (((1,), (1,)), ((), ()))
    lane = lax.broadcasted_iota(jnp.int32, (SUBLANES, MOBA_W), 1)

    def k_copy(bi, pg):
        return pltpu.make_async_copy(ck_ref.at[layer, pt_ref[bi, pg]], kring.at[pg], ksem.at[pg])

    def v_copy(bi, pg):
        return pltpu.make_async_copy(cv_ref.at[layer, pt_ref[bi, pg]], vring.at[pg], vsem.at[pg])

    @pl.when(b == 0)
    def _():
        for pg in range(n_pages):
            k_copy(0, pg).start()

    q = q_ref[0] * (HEAD_DIM ** -0.5 * LOG2_E)
    for h in range(MOBA_HEADS):
        qs_scr[h * SUBLANES:(h + 1) * SUBLANES, :] = jnp.where(lane // HEAD_DIM == h, q, 0.0)
    qb = qs_scr[...].astype(BF16)

    for g0 in range(0, n_pages, group):
        for pg in range(g0, g0 + group):
            k_copy(b, pg).wait()
        for pg in range(g0, g0 + group):
            s_scr[:, pg * page:(pg + 1) * page] = jnp.dot(qb, kring[pg].astype(BF16), preferred_element_type=F32)
        for pg in range(g0, g0 + group):
            v_copy(b, pg).start()

    col = lax.broadcasted_iota(jnp.int32, (rows, LANES), 1)
    gate = jnp.zeros((rows, LANES), F32)
    for n in range(n_blocks):
        c0 = n * MOBA_BLOCK
        bsum = s_scr[:, c0:c0 + page]
        for u in range(1, MOBA_BLOCK // page):
            bsum = bsum + s_scr[:, c0 + u * page:c0 + (u + 1) * page]
        gate = jnp.where(col == n, jnp.sum(bsum, axis=1, keepdims=True), gate)
    rank = jnp.zeros(gate.shape, jnp.int32)
    for jp in range(n_blocks):
        gj = gate[:, jp:jp + 1]
        rank = rank + jnp.where((gj > gate) | ((gj == gate) & (jp < col)), 1, 0)
    sel = jnp.where(rank < min(MOBA_TOPK, n_blocks), 0.0, NEG_INF)
    own = jnp.concatenate([kn_ref[0], jnp.zeros((page - SUBLANES, MOBA_W), F32)], axis=0).astype(BF16)
    t_idx = lax.broadcasted_iota(jnp.int32, (rows, page), 0) % SUBLANES
    key_idx = lax.broadcasted_iota(jnp.int32, (rows, page), 1)
    s_own = lax.dot_general(qb, own, contract_lanes, preferred_element_type=F32)
    s_own = s_own + jnp.where(key_idx <= t_idx, 0.0, NEG_INF)
    s_scr[:, n_pages * page:] = s_own
    m_part = s_own
    for n in range(n_blocks):
        add = sel[:, n:n + 1]
        for u in range(MOBA_BLOCK // page):
            c0 = n * MOBA_BLOCK + u * page
            sm = s_scr[:, c0:c0 + page] + add
            s_scr[:, c0:c0 + page] = sm
            m_part = jnp.maximum(m_part, sm)
    m = jnp.max(m_part, axis=1, keepdims=True)
    l_part = jnp.zeros((rows, page), F32)
    for n in range(n_pages + 1):
        c0 = n * page
        p = jnp.exp2(s_scr[:, c0:c0 + page] - m)
        l_part = l_part + p
        p_scr[:, c0:c0 + page] = p.astype(BF16)
    l = jnp.sum(l_part, axis=1, keepdims=True)
    vown = jnp.concatenate([vn_ref[0], jnp.zeros((page - SUBLANES, MOBA_W), F32)], axis=0).astype(BF16)
    acc = jnp.dot(p_scr[:, n_pages * page:], vown, preferred_element_type=F32)

    b_next = jnp.minimum(b + 1, last)
    for g0 in range(0, n_pages, group):
        for pg in range(g0, g0 + group):
            v_copy(b, pg).wait()
        for pg in range(g0, g0 + group):
            acc = acc + lax.dot_general(p_scr[:, pg * page:(pg + 1) * page], vring[pg].astype(BF16),
                                        contract_lanes, preferred_element_type=F32)
        for pg in range(g0, g0 + group):
            k_copy(b_next, pg).start()

    @pl.when(b == last)
    def _():
        for pg in range(n_pages):
            k_copy(b_next, pg).wait()

    o = acc / l
    out = jnp.zeros((SUBLANES, MOBA_W), F32)
    for h in range(MOBA_HEADS):
        out = out + jnp.where(lane // HEAD_DIM == h, o[h * SUBLANES:(h + 1) * SUBLANES, :], 0.0)
    o_ref[0] = (out * _silu(g_ref[0])).astype(o_ref.dtype)


def _moba_sample(page_table, mq3, mk3, mv3, mg3, cache_k4, cache_v4, layer):
    db = mq3.shape[0]
    n_pages = page_table.shape[1]
    page = cache_k4.shape[3]
    rows = MOBA_HEADS * SUBLANES
    assert n_pages * page // MOBA_BLOCK <= LANES and page == LANES
    small = pl.BlockSpec((1, SUBLANES, MOBA_W), lambda bi, pt: (bi, 0, 0))
    hbm = pl.BlockSpec(memory_space=pl.ANY)
    width = (n_pages + 1) * page
    return pl.pallas_call(
        functools.partial(_moba_sample_kernel, layer=layer, n_pages=n_pages, page=page),
        grid_spec=pltpu.PrefetchScalarGridSpec(
            num_scalar_prefetch=1,
            grid=(db,),
            in_specs=[small, small, small, small, hbm, hbm],
            out_specs=small,
            scratch_shapes=[
                pltpu.VMEM((n_pages, MOBA_W, page), F32),
                pltpu.VMEM((n_pages, MOBA_W, page), F32),
                pltpu.SemaphoreType.DMA((n_pages,)),
                pltpu.SemaphoreType.DMA((n_pages,)),
                pltpu.VMEM((rows, MOBA_W), F32),
                pltpu.VMEM((rows, width), F32),
                pltpu.VMEM((rows, width), BF16),
            ],
        ),
        out_shape=jax.ShapeDtypeStruct((db, SUBLANES, MOBA_W), BF16),
        compiler_params=_cparams(("arbitrary",)),
        name="moba_sample",
    )(page_table, mq3, mk3, mv3, mg3, cache_k4, cache_v4)


def _pool_kernel(u_ref, g_ref, buf_ref, cnt_ref, w_ref, b_ref, sc_ref, o_ref, bufo_ref, x_scr, *, n_tokens, rows):
    length = u_ref.shape[1]
    halo = POOL_BUF + 1
    x_scr[0:halo, :] = buf_ref[0]
    x_scr[halo:, :] = u_ref[0]
    lane = lax.broadcasted_iota(jnp.int32, (rows, POOL_W), 1)
    group = lane // POOL_GC

    for r0 in range(0, length, rows):
        u = x_scr[halo + r0:halo + r0 + rows, :]
        acc = u
        win = jnp.zeros_like(u)
        for s in range(1, max(POOL_WINDOWS)):
            acc = acc + x_scr[halo + r0 - s:halo + r0 - s + rows, :]
            if s + 1 in POOL_WINDOWS:
                win = jnp.where(group == POOL_WINDOWS.index(s + 1), acc, win)
        pooled = win / cnt_ref[r0:r0 + rows, :] - u
        y = jnp.dot(pooled.astype(BF16), w_ref[...], preferred_element_type=F32) + b_ref[...]
        y = y * sc_ref[...]
        o_ref[0, r0:r0 + rows, :] = (y * _silu(g_ref[0, r0:r0 + rows, :])).astype(o_ref.dtype)
    bufo_ref[0] = x_scr[n_tokens + 1:n_tokens + 1 + POOL_BUF, :]


def _pool(zp3, buf16, cnt, w_bd, bias, scale, *, n_tokens, rows):
    b, length, _ = zp3.shape
    const = lambda bi: (0, 0)
    return pl.pallas_call(
        functools.partial(_pool_kernel, n_tokens=n_tokens, rows=rows),
        grid=(b,),
        in_specs=[
            pl.BlockSpec((1, length, POOL_W), lambda bi: (bi, 0, 0)),
            pl.BlockSpec((1, length, POOL_W), lambda bi: (bi, 0, 1)),
            pl.BlockSpec((1, POOL_BUF + 1, POOL_W), lambda bi: (bi, 0, 0)),
            pl.BlockSpec((length, POOL_W), const),
            pl.BlockSpec((POOL_W, POOL_W), const),
            pl.BlockSpec((1, POOL_W), const),
            pl.BlockSpec((1, POOL_W), const),
        ],
        out_specs=[
            pl.BlockSpec((1, length, POOL_W), lambda bi: (bi, 0, 0)),
            pl.BlockSpec((1, POOL_BUF, POOL_W), lambda bi: (bi, 0, 0)),
        ],
        out_shape=[
            jax.ShapeDtypeStruct((b, length, POOL_W), BF16),
            jax.ShapeDtypeStruct((b, POOL_BUF, POOL_W), F32),
        ],
        scratch_shapes=[pltpu.VMEM((POOL_BUF + 1 + length, POOL_W), F32)],
        compiler_params=_cparams(("arbitrary",)),
        name="pool",
    )(zp3, zp3, buf16, cnt, w_bd, bias, scale)


def _out_proj_kernel(mr_ref, mm_ref, mp_ref, x_ref, w_ref, y_ref, mix_scr):
    mix_scr[:, :RET_W] = mr_ref[...]
    mix_scr[:, RET_W:RET_W + MOBA_W] = mm_ref[...]
    mix_scr[:, RET_W + MOBA_W:] = mp_ref[...]
    y_ref[...] = x_ref[...] + jnp.dot(mix_scr[...], w_ref[...], preferred_element_type=F32)


def _out_proj(mr, mm, mp, x2d, w_bf, *, tm):
    m, d = x2d.shape
    d_mix = RET_W + MOBA_W + POOL_W
    row = lambda i: (i, 0)
    return pl.pallas_call(
        _out_proj_kernel,
        grid=(m // tm,),
        in_specs=[
            pl.BlockSpec((tm, RET_W), row),
            pl.BlockSpec((tm, MOBA_W), row),
            pl.BlockSpec((tm, POOL_W), row),
            pl.BlockSpec((tm, d), row),
            pl.BlockSpec((d_mix, d), lambda i: (0, 0)),
        ],
        out_specs=pl.BlockSpec((tm, d), row),
        out_shape=jax.ShapeDtypeStruct((m, d), F32),
        scratch_shapes=[pltpu.VMEM((tm, d_mix), BF16)],
        compiler_params=_cparams(("arbitrary",)),
        name="out_proj",
    )(mr, mm, mp, x2d, w_bf)


def _rope_tables(pos):
    inv = 1.0 / (ROPE_THETA ** (jnp.arange(HALF, dtype=F32) / HALF))
    ang = pos.astype(F32)[:, None] * inv[None, :]
    c, s = jnp.cos(ang), jnp.sin(ang)
    return jnp.concatenate([c, c, c, c], axis=-1), jnp.concatenate([-s, s, -s, s], axis=-1)


def _block_diag(blocks):
    g, n, _ = blocks.shape
    eye = jnp.eye(g, dtype=blocks.dtype)
    return (eye[:, None, :, None] * blocks[:, :, None, :]).reshape(g * n, g * n)


def _pool_counts(pos0, length):
    pos = pos0 + jnp.arange(length)
    w = jnp.repeat(jnp.asarray(POOL_WINDOWS, jnp.int32), POOL_GC)
    return jnp.minimum(pos[:, None] + 1, w[None, :]).astype(F32)


def _layer(x3, pos_tabs, ret_state_bd, ret_tabs, ret_step, pool_buf16, pool_cnt, n_tokens, moba_fn, params, *, tm):
    b, length, d = x3.shape
    nw, w_in, w_out, gnw, qnw, knw, bd, pw_bd, pbias, pscale = params
    x2d = x3.reshape(b * length, d)
    zr, mq, mk, mv, mg, zp = _in_proj(x2d, nw, w_in, pos_tabs[0], pos_tabs[1], qnw, knw, bd, tm=tm)
    three = lambda a: a.reshape(b, length, a.shape[-1])
    mix_r, s_new = _retention(three(zr), ret_state_bd, ret_tabs, gnw, chunk=RET_CHUNK, bb=ret_step[0],
                              pairs=ret_step[1])
    mix_m, k_new, v_new = moba_fn(three(mq), three(mk), three(mv), three(mg))
    mix_p, buf_new = _pool(three(zp), pool_buf16, pool_cnt, pw_bd, pbias, pscale,
                           n_tokens=n_tokens, rows=min(length, 128))
    two = lambda a: a.reshape(b * length, a.shape[-1])
    y = _out_proj(two(mix_r), two(mix_m), two(mix_p), x2d, w_out, tm=tm)
    return y.reshape(b, length, d), s_new, k_new, v_new, buf_new


def _moba_prompt_wrap(mq3, mk3, mv3, mg3):
    mix, kt, vt = _moba_prompt(mq3, mk3, mv3, mg3)
    b, _, length = kt.shape
    rows = lambda t: t.reshape(b, MOBA_HEADS, HEAD_DIM, length).transpose(0, 3, 1, 2)
    return mix, rows(kt), rows(vt)


def _moba_sample_wrap(mq3, mk3, mv3, mg3, *, page_table, cache_k4, cache_v4, layer, n_tokens):
    mix = _moba_sample(page_table, mq3, mk3, mv3, mg3, cache_k4, cache_v4, layer)
    rows = lambda t: t[:, :n_tokens].reshape(t.shape[0], n_tokens, MOBA_HEADS, HEAD_DIM)
    return mix, rows(mk3), rows(mv3)


def kernel(x_prompt, x_sample, cache_k, cache_v, state_ret, state_pool, page_table, norm_w, w_in, w_out, ret_gn_w, q_norm_w, k_norm_w, pool_w, pool_b, pool_scale):
    depth = w_in.shape[0]
    b, seq, d = x_prompt.shape
    db, dec_seq, _ = x_sample.shape
    n_pool, page = cache_k.shape[1], cache_k.shape[2]
    past_len = page_table.shape[1] * page
    assert dec_seq <= SUBLANES and seq % MOBA_BLOCK == 0 and seq % RET_CHUNK == 0 and past_len % MOBA_BLOCK == 0

    p_tabs = _rope_tables(jnp.arange(seq))
    tm_s = db * SUBLANES
    s_tabs = tuple(jnp.tile(t, (db, 1)) for t in _rope_tables(past_len + jnp.arange(SUBLANES)))
    p_ret_tabs = _retention_tables(RET_CHUNK, RET_CHUNK)
    s_ret_tabs = _retention_tables(RET_CHUNK, dec_seq)
    p_cnt = _pool_counts(0, seq)
    s_cnt = _pool_counts(past_len, SUBLANES)
    bd = _block_diag(jnp.full((MOBA_HEADS, HEAD_DIM, HEAD_DIM), 1.0 / HEAD_DIM, F32)).astype(BF16)
    zero_state = jnp.zeros((b, RET_HEADS // 2, LANES, LANES), F32)
    zero_buf = jnp.zeros((b, POOL_BUF + 1, POOL_W), F32)

    cache_k4 = cache_k.transpose(0, 1, 3, 4, 2).reshape(depth, n_pool, MOBA_W, page)
    cache_v4 = cache_v.transpose(0, 1, 3, 4, 2).reshape(depth, n_pool, MOBA_W, page)
    xp = x_prompt
    xs = jnp.pad(x_sample, ((0, 0), (0, SUBLANES - dec_seq), (0, 0)))

    outs = [[] for _ in range(8)]
    for l in range(depth):
        params = (norm_w[l].reshape(1, d), w_in[l].astype(BF16), w_out[l].astype(BF16),
                  ret_gn_w[l].reshape(RET_W // LANES, 1, LANES),
                  jnp.tile(q_norm_w[l], MOBA_HEADS).reshape(1, MOBA_W),
                  jnp.tile(k_norm_w[l], MOBA_HEADS).reshape(1, MOBA_W),
                  bd, _block_diag(pool_w[l]).astype(BF16),
                  pool_b[l].reshape(1, POOL_W), pool_scale[l].reshape(1, POOL_W))
        xp, s_p, k_p, v_p, b_p = _layer(xp, p_tabs, zero_state, p_ret_tabs, (1, 1), zero_buf, p_cnt, seq,
                                        _moba_prompt_wrap, params, tm=512)
        moba_s = functools.partial(_moba_sample_wrap, page_table=page_table, cache_k4=cache_k4,
                                   cache_v4=cache_v4, layer=l, n_tokens=dec_seq)
        xs, s_s, k_s, v_s, b_s = _layer(xs, s_tabs, _to_pair_state(state_ret[l]), s_ret_tabs,
                                        (math.gcd(db, 8), RET_W // LANES),
                                        jnp.pad(state_pool[l], ((0, 0), (1, 0), (0, 0))), s_cnt, dec_seq,
                                        moba_s, params, tm=tm_s)
        for lst, val in zip(outs, (k_p, v_p, k_s, v_s, _from_pair_state(s_p), _from_pair_state(s_s), b_p, b_s)):
            lst.append(val)
    return (xp, xs[:, :dec_seq]) + tuple(jnp.stack(o) for o in outs)
```

```python
import functools
import math

import jax
import jax.numpy as jnp
import numpy as np
from jax import lax
from jax.experimental import pallas as pl
from jax.experimental.pallas import tpu as pltpu

F32 = jnp.float32
BF16 = jnp.bfloat16

HEAD_DIM = 64
HALF = HEAD_DIM // 2
LANES = 128
RET_HEADS = 6
MOBA_HEADS = 6
RET_W = RET_HEADS * HEAD_DIM
MOBA_W = MOBA_HEADS * HEAD_DIM
POOL_W = 256
POOL_GC = 64
POOL_WINDOWS = (2, 4, 8, 16)
POOL_BUF = 15
RET_CHUNK = 128
MOBA_BLOCK = 256
MOBA_TOPK = 3
ROPE_THETA = 10000.0
EPS = 1e-6
SUBLANES = 8
NEG_INF = float("-inf")
LOG2_E = 1.4426950408889634

OFF_RQ, OFF_RV, OFF_MQ, OFF_MV, OFF_PU = 0, 2 * RET_W, 4 * RET_W, 4 * RET_W + 2 * MOBA_W, 4 * RET_W + 4 * MOBA_W
D_IN = OFF_PU + 2 * POOL_W

VMEM_LIMIT = 56 * 1024 * 1024


def _silu(x):
    return x / (1.0 + jnp.exp(-x))


def _cparams(sem, flags=None):
    return pltpu.CompilerParams(dimension_semantics=sem, vmem_limit_bytes=VMEM_LIMIT, flags=flags)


def _in_proj_kernel(x_ref, nw_ref, w_ref, cos_ref, sin_ref, qnw_ref, knw_ref, bd_ref,
                    zr_ref, mq_ref, mk_ref, mv_ref, mg_ref, zp_ref, h_scr, z_scr):
    tm = x_ref.shape[0]
    x = x_ref[...]
    ms = jnp.mean(x * x, axis=-1, keepdims=True)
    h_scr[...] = (x * lax.rsqrt(ms + EPS) * nw_ref[...]).astype(BF16)
    cos = cos_ref[...]
    sin = sin_ref[...]
    lane = lax.broadcasted_iota(jnp.int32, (tm, LANES), 1)
    first_half = (lane & HALF) == 0

    def rope(z):
        partner = jnp.where(first_half, pltpu.roll(z, LANES - HALF, 1), pltpu.roll(z, HALF, 1))
        return z * cos + partner * sin

    def proj(off, width):
        return jnp.dot(h_scr[...], w_ref[:, off:off + width], preferred_element_type=F32)

    z_scr[...] = proj(OFF_RQ, 2 * RET_W)
    for g in range(2 * RET_W // LANES):
        out = rope(z_scr[:, g * LANES:(g + 1) * LANES])
        if g >= RET_W // LANES:
            out = out * (HEAD_DIM ** -0.5)
        zr_ref[:, g * LANES:(g + 1) * LANES] = out
    zr_ref[:, OFF_RV:OFF_MQ] = proj(OFF_RV, 2 * RET_W)
    z_scr[...] = proj(OFF_MQ, 2 * MOBA_W)
    for t, (nw, dst) in enumerate(((qnw_ref, mq_ref), (knw_ref, mk_ref))):
        z = z_scr[:, t * MOBA_W:(t + 1) * MOBA_W]
        msq = jnp.dot((z * z).astype(BF16), bd_ref[...], preferred_element_type=F32)
        z_scr[:, t * MOBA_W:(t + 1) * MOBA_W] = z * lax.rsqrt(msq + EPS) * nw[...]
        for g in range(MOBA_W // LANES):
            c0 = t * MOBA_W + g * LANES
            dst[:, g * LANES:(g + 1) * LANES] = rope(z_scr[:, c0:c0 + LANES])
    z_scr[...] = proj(OFF_MV, 2 * MOBA_W)
    mv_ref[...] = z_scr[:, :MOBA_W]
    mg_ref[...] = z_scr[:, MOBA_W:]
    zp_ref[...] = proj(OFF_PU, 2 * POOL_W)


def _in_proj(x2d, nw, w_bf, cos, sin, qnw, knw, bd, *, tm):
    m, d = x2d.shape
    n_pos = cos.shape[0] // tm
    row = lambda i: (i, 0)
    const = lambda i: (0, 0)
    pos = lambda i: (i % n_pos, 0)
    outs = [(2 * RET_W + 2 * RET_W), MOBA_W, MOBA_W, MOBA_W, MOBA_W, 2 * POOL_W]
    return pl.pallas_call(
        _in_proj_kernel,
        grid=(m // tm,),
        in_specs=[
            pl.BlockSpec((tm, d), row),
            pl.BlockSpec((1, d), const),
            pl.BlockSpec((d, D_IN), const),
            pl.BlockSpec((tm, LANES), pos),
            pl.BlockSpec((tm, LANES), pos),
            pl.BlockSpec((1, MOBA_W), const),
            pl.BlockSpec((1, MOBA_W), const),
            pl.BlockSpec((MOBA_W, MOBA_W), const),
        ],
        out_specs=[pl.BlockSpec((tm, w), row) for w in outs],
        out_shape=[jax.ShapeDtypeStruct((m, w), F32) for w in outs],
        scratch_shapes=[pltpu.VMEM((tm, d), BF16), pltpu.VMEM((tm, 2 * RET_W), F32)],
        compiler_params=_cparams(("arbitrary",)),
        name="in_proj",
    )(x2d, nw, w_bf, cos, sin, qnw, knw, bd)


def _retention_kernel(*refs, chunk, has_state):
    q_ref, k_ref, v_ref, g_ref = refs[:4]
    s0_ref = refs[4] if has_state else None
    dm_ref, rs_ref, kd_ref, gc_ref, gnw_ref, o_ref, sout_ref, oi_scr, kv_scr, sb_scr = refs[4 + has_state:]
    bb, length, width = q_ref.shape
    pairs = width // LANES
    rows_in = min(length, chunk)
    n_chunks = max(1, length // chunk)
    lane = lax.broadcasted_iota(jnp.int32, (chunk, LANES), 1)
    head0 = lane < HEAD_DIM
    r = lax.broadcasted_iota(jnp.int32, (LANES, LANES), 0)
    c = lax.broadcasted_iota(jnp.int32, (LANES, LANES), 1)
    same_head = (r < HEAD_DIM) == (c < HEAD_DIM)
    contract_lanes = (((1,), (1,)), ((), ()))

    def head_mean(x):
        m0 = jnp.sum(jnp.where(head0, x, 0.0), axis=1, keepdims=True)
        m1 = jnp.sum(jnp.where(head0, 0.0, x), axis=1, keepdims=True)
        return jnp.where(head0, m0, m1) * (1.0 / HEAD_DIM)

    def load(ref, bi, p, ci):
        a = ref[bi, ci * chunk:ci * chunk + rows_in, p * LANES:(p + 1) * LANES]
        if rows_in < chunk:
            a = jnp.concatenate([a, jnp.zeros((chunk - rows_in, LANES), F32)], axis=0)
        return a

    items = [(bi, p, ci) for bi in range(bb) for p in range(pairs) for ci in range(n_chunks)]

    def a1(bi, p, ci):
        q = load(q_ref, bi, p, ci)
        k = load(k_ref, bi, p, ci)
        kb = k.astype(BF16)
        vb = load(v_ref, bi, p, ci).astype(BF16)
        q0 = jnp.where(head0, q, 0.0).astype(BF16)
        q1 = jnp.where(head0, 0.0, q).astype(BF16)
        in0 = (lax.dot_general(q0, kb, contract_lanes, preferred_element_type=F32) * dm_ref[2 * p]).astype(BF16)
        in1 = (lax.dot_general(q1, kb, contract_lanes, preferred_element_type=F32) * dm_ref[2 * p + 1]).astype(BF16)
        return bi, p, ci, k, vb, in0, in1

    def a2(st):
        bi, p, ci, k, vb, in0, in1 = st
        slot = bi * pairs + p
        oi_scr[slot, ci * chunk:(ci + 1) * chunk, :] = jnp.where(
            head0, jnp.dot(in0, vb, preferred_element_type=F32), jnp.dot(in1, vb, preferred_element_type=F32))
        kdt = (k * kd_ref[p]).T.astype(BF16)
        kv_scr[slot, ci] = jnp.where(same_head, jnp.dot(kdt, vb, preferred_element_type=F32), 0.0)

    st = None
    for item in items + [None]:
        nxt = a1(*item) if item is not None else None
        if st is not None:
            a2(st)
        st = nxt

    zero = jnp.zeros((HEAD_DIM, HEAD_DIM), F32)
    for bi in range(bb):
        for p in range(pairs):
            slot = bi * pairs + p
            if s0_ref is None:
                s = jnp.zeros((LANES, LANES), F32)
            else:
                s = jnp.concatenate([jnp.concatenate([s0_ref[bi, 2 * p], zero], axis=1),
                                     jnp.concatenate([zero, s0_ref[bi, 2 * p + 1]], axis=1)], axis=0)
            for ci in range(n_chunks):
                sb_scr[slot, ci] = s.astype(BF16)
                s = s * gc_ref[p] + kv_scr[slot, ci]
            sout_ref[bi, 2 * p] = s[:HEAD_DIM, :HEAD_DIM]
            sout_ref[bi, 2 * p + 1] = pltpu.roll(s[HEAD_DIM:], HEAD_DIM, 1)[:, :HEAD_DIM]

    def c1(bi, p, ci):
        slot = bi * pairs + p
        qb = load(q_ref, bi, p, ci).astype(BF16)
        o = (oi_scr[slot, ci * chunk:(ci + 1) * chunk, :]
             + jnp.dot(qb, sb_scr[slot, ci], preferred_element_type=F32) * rs_ref[p])
        return bi, p, ci, o, head_mean(o)

    def c2(st):
        bi, p, ci, o, mu = st
        oc = o - mu
        return bi, p, ci, oc, head_mean(oc * oc)

    def c3(st):
        bi, p, ci, oc, var = st
        on = oc * lax.rsqrt(var + EPS) * gnw_ref[p]
        out = on * _silu(load(g_ref, bi, p, ci))
        o_ref[bi, ci * chunk:ci * chunk + rows_in, p * LANES:(p + 1) * LANES] = out[:rows_in].astype(o_ref.dtype)

    s1 = s2 = None
    for item in items + [None, None]:
        new1 = c1(*item) if item is not None else None
        new2 = c2(s1) if s1 is not None else None
        if s2 is not None:
            c3(s2)
        s1, s2 = new1, new2


def _retention(zr3, state, tabs, gnw, *, chunk, bb, pairs):
    b, length, _ = zr3.shape
    n_pairs = RET_W // LANES
    dm, rs, kd, gc = tabs
    w = pairs * LANES
    n_chunks = max(1, length // chunk)
    col = lambda off: (lambda bi, p: (bi, 0, off + p))
    tab = lambda bi, p: (p, 0, 0)
    state_spec = pl.BlockSpec((bb, 2 * pairs, HEAD_DIM, HEAD_DIM), lambda bi, p: (bi, p, 0, 0))
    return pl.pallas_call(
        functools.partial(_retention_kernel, chunk=chunk, has_state=state is not None),
        grid=(b // bb, n_pairs // pairs),
        in_specs=[
            pl.BlockSpec((bb, length, w), col(0)),
            pl.BlockSpec((bb, length, w), col(n_pairs // pairs)),
            pl.BlockSpec((bb, length, w), col(2 * n_pairs // pairs)),
            pl.BlockSpec((bb, length, w), col(3 * n_pairs // pairs)),
        ] + ([state_spec] if state is not None else []) + [
            pl.BlockSpec((2 * pairs, chunk, chunk), tab),
            pl.BlockSpec((pairs, chunk, LANES), tab),
            pl.BlockSpec((pairs, chunk, LANES), tab),
            pl.BlockSpec((pairs, 1, LANES), tab),
            pl.BlockSpec((pairs, 1, LANES), tab),
        ],
        out_specs=[pl.BlockSpec((bb, length, w), lambda bi, p: (bi, 0, p)), state_spec],
        out_shape=[
            jax.ShapeDtypeStruct((b, length, RET_W), BF16),
            jax.ShapeDtypeStruct((b, RET_HEADS, HEAD_DIM, HEAD_DIM), F32),
        ],
        scratch_shapes=[
            pltpu.VMEM((bb * pairs, n_chunks * chunk, LANES), F32),
            pltpu.VMEM((bb * pairs, n_chunks, LANES, LANES), F32),
            pltpu.VMEM((bb * pairs, n_chunks, LANES, LANES), BF16),
        ],
        compiler_params=_cparams(("arbitrary", "arbitrary")),
        name="retention",
    )(zr3, zr3, zr3, zr3, *(() if state is None else (state,)), dm, rs, kd, gc, gnw)


def _retention_tables(chunk, n_tokens):
    lg = np.log(1.0 - 2.0 ** (-5.0 - np.arange(RET_HEADS, dtype=np.float64)))
    i = np.arange(chunk, dtype=np.float64)
    rel = i[:, None] - i[None, :]
    dm = np.where(rel[None] >= 0, np.exp(rel[None] * lg[:, None, None]), 0.0)
    lg_lanes = np.repeat(lg, HEAD_DIM).reshape(RET_W // LANES, 1, LANES)
    rs = np.exp((i + 1.0)[None, :, None] * lg_lanes)
    kd = np.where((i < n_tokens)[None, :, None], np.exp((n_tokens - 1.0 - i)[None, :, None] * lg_lanes), 0.0)
    gc = np.exp(float(n_tokens) * lg_lanes)
    return tuple(jnp.asarray(t, F32) for t in (dm, rs, kd, gc))


def _moba_prompt_kernel(*refs, n_blocks, n_prev):
    q_ref, k_ref, v_ref, g_ref = refs[:4]
    prev = refs[4:6] if n_prev else ()
    o_ref, kt_ref, vt_ref, kh_scr, vt_scr, km_scr, s_scr, p_scr = refs[4 + len(prev):]
    blk = MOBA_BLOCK
    sub = blk
    contract_lanes = (((1,), (1,)), ((), ()))
    lane = lax.broadcasted_iota(jnp.int32, (blk, LANES), 1)
    head0 = lane < HEAD_DIM
    for j in range(n_blocks):
        kj = k_ref[0, j * blk:(j + 1) * blk, :]
        kh_scr[0, j * blk:(j + 1) * blk, :] = jnp.where(head0, kj, 0.0).astype(BF16)
        kh_scr[1, j * blk:(j + 1) * blk, :] = jnp.where(head0, 0.0, kj).astype(BF16)
        vt = v_ref[0, j * blk:(j + 1) * blk, :].T
        vt_scr[:, j * blk:(j + 1) * blk] = vt.astype(BF16)
        km_scr[j:j + 1, :] = jnp.sum(kj, axis=0, keepdims=True) * (1.0 / blk)
        kt_ref[n_prev, 0, :, j * blk:(j + 1) * blk] = kj.T
        vt_ref[n_prev, 0, :, j * blk:(j + 1) * blk] = vt
    if n_prev:
        kt_ref[0:n_prev] = prev[0][...]
        vt_ref[0:n_prev] = prev[1][...]

    km = km_scr[...]
    lane8 = lax.broadcasted_iota(jnp.int32, (n_blocks, LANES), 1)
    row8 = lax.broadcasted_iota(jnp.int32, (n_blocks, blk), 0)
    krow = lax.broadcasted_iota(jnp.int32, (blk, blk), 0)
    qcol = lax.broadcasted_iota(jnp.int32, (blk, blk), 1)
    causal = jnp.where(krow <= qcol, 0.0, NEG_INF)
    fold = lambda t: t.reshape(t.shape[0] // SUBLANES, SUBLANES, blk)
    km_parts = []
    for h in range(2):
        kmh = jnp.where((lane8 < HEAD_DIM) == (h == 0), km, 0.0)
        kmh_hi = kmh.astype(BF16)
        km_parts.append((kmh_hi, (kmh - kmh_hi.astype(F32)).astype(BF16)))

    heads = range(2)
    tiles = lambda i: [(j, u, h) for j in range(i + 1) for u in range(blk // sub) for h in heads]

    def begin(i):
        q = q_ref[0, i * blk:(i + 1) * blk, :] * (HEAD_DIM ** -0.5 * LOG2_E)
        qh = q.astype(BF16)
        selmask = [None, None]
        if i > MOBA_TOPK:
            ql = (q - qh.astype(F32)).astype(BF16)
            for h in heads:
                kmh_hi, kmh_lo = km_parts[h]
                gate = (lax.dot_general(kmh_hi, qh, contract_lanes, preferred_element_type=F32)
                        + lax.dot_general(kmh_lo, qh, contract_lanes, preferred_element_type=F32)
                        + lax.dot_general(kmh_hi, ql, contract_lanes, preferred_element_type=F32))
                rank = jnp.zeros((n_blocks, blk), jnp.int32)
                for jp in range(i):
                    gj = gate[jp:jp + 1, :]
                    rank = rank + jnp.where((gj > gate) | ((gj == gate) & (jp < row8)), 1, 0)
                selmask[h] = jnp.where((rank < MOBA_TOPK) & (row8 < i), 0.0, NEG_INF)
        return dict(i=i, buf=i % 2, qh=qh, selmask=selmask, m8=[None, None], m=None,
                    l8=[jnp.zeros((SUBLANES, blk), F32) for _ in heads])

    def score_tile(st, j, u, h):
        i = st["i"]
        rows = slice(j * blk + u * sub, j * blk + (u + 1) * sub)
        s = lax.dot_general(kh_scr[h, rows, :], st["qh"], contract_lanes, preferred_element_type=F32)
        if j == i:
            s = s + causal[u * sub:(u + 1) * sub]
        s_scr[st["buf"], h, rows, :] = s
        tmax = jnp.max(fold(s), axis=0)
        if j < i and st["selmask"][h] is not None:
            tmax = tmax + st["selmask"][h][j:j + 1, :]
        st["m8"][h] = tmax if st["m8"][h] is None else jnp.maximum(st["m8"][h], tmax)

    def prob_tile(st, j, u, h):
        i = st["i"]
        if st["m"] is None:
            st["m"] = [jnp.max(st["m8"][hh], axis=0, keepdims=True) for hh in heads]
        rows = slice(j * blk + u * sub, j * blk + (u + 1) * sub)
        sm = st["selmask"][h]
        shift = -st["m"][h] if (j == i or sm is None) else sm[j:j + 1, :] - st["m"][h]
        p = jnp.exp2(s_scr[st["buf"], h, rows, :] + shift)
        st["l8"][h] = st["l8"][h] + jnp.sum(fold(p), axis=0)
        p_scr[st["buf"], h, rows, :] = p.astype(BF16)

    def finish(st):
        i = st["i"]
        nk = (i + 1) * blk
        halves = []
        for h in heads:
            l = jnp.sum(st["l8"][h], axis=0, keepdims=True)
            ot = jnp.dot(vt_scr[h * HEAD_DIM:(h + 1) * HEAD_DIM, 0:nk], p_scr[st["buf"], h, 0:nk, :],
                         preferred_element_type=F32)
            halves.append(ot / l)
        ot = jnp.concatenate(halves, axis=0)
        o_ref[0, i * blk:(i + 1) * blk, :] = (ot.T * _silu(g_ref[0, i * blk:(i + 1) * blk, :])).astype(o_ref.dtype)

    cur = begin(0)
    for t in tiles(0):
        score_tile(cur, *t)
    prev = None
    for i in range(n_blocks):
        nxt = begin(i + 1) if i + 1 < n_blocks else None
        a, b = tiles(i), (tiles(i + 1) if nxt is not None else [])
        for k in range(max(len(a), len(b))):
            if k == 2 and prev is not None:
                finish(prev)
                prev = None
            if k < len(b):
                score_tile(nxt, *b[k])
            if k < len(a):
                prob_tile(cur, *a[k])
        if prev is not None:
            finish(prev)
        prev, cur = cur, nxt
    finish(prev)


def _moba_prompt(mq3, mk3, mv3, mg3, prev_kv):
    b, length, _ = mq3.shape
    n_pairs = MOBA_W // LANES
    n_blocks = length // MOBA_BLOCK
    n_prev = 0 if prev_kv is None else prev_kv[0].shape[0]
    rows = pl.BlockSpec((1, length, LANES), lambda bi, p: (bi, 0, p))
    cols = lambda n: pl.BlockSpec((n, 1, LANES, length), lambda bi, p: (0, bi, p, 0))
    stacked = jax.ShapeDtypeStruct((n_prev + 1, b, MOBA_W, length), F32)
    return pl.pallas_call(
        functools.partial(_moba_prompt_kernel, n_blocks=n_blocks, n_prev=n_prev),
        grid=(b, n_pairs),
        in_specs=[rows, rows, rows, rows] + ([cols(n_prev), cols(n_prev)] if n_prev else []),
        out_specs=[rows, cols(n_prev + 1), cols(n_prev + 1)],
        out_shape=[jax.ShapeDtypeStruct((b, length, MOBA_W), BF16), stacked, stacked],
        scratch_shapes=[
            pltpu.VMEM((2, length, LANES), BF16),
            pltpu.VMEM((LANES, length), BF16),
            pltpu.VMEM((n_blocks, LANES), F32),
            pltpu.VMEM((2, 2, length, MOBA_BLOCK), F32),
            pltpu.VMEM((2, 2, length, MOBA_BLOCK), BF16),
        ],
        compiler_params=_cparams(("arbitrary", "arbitrary")),
        name="moba_prompt",
    )(mq3, mk3, mv3, mg3, *(prev_kv or ()))


def _moba_sample_kernel(pt_ref, q_ref, kn_ref, vn_ref, g_ref, ck_ref, cv_ref, o_ref,
                        kring, vring, ksem, vsem, qs_scr, s_scr, p_scr, *, layer, n_pages, page):
    b = pl.program_id(0)
    last = pl.num_programs(0) - 1
    n_blocks = n_pages * page // MOBA_BLOCK
    group = math.gcd(n_pages, 16)
    rows = MOBA_HEADS * SUBLANES
    contract_lanes = (((1,), (1,)), ((), ()))
    lane = lax.broadcasted_iota(jnp.int32, (SUBLANES, MOBA_W), 1)

    def k_copy(bi, pg):
        return pltpu.make_async_copy(ck_ref.at[layer, pt_ref[bi, pg]], kring.at[pg], ksem.at[pg])

    def v_copy(bi, pg):
        return pltpu.make_async_copy(cv_ref.at[layer, pt_ref[bi, pg]], vring.at[pg], vsem.at[pg])

    @pl.when(b == 0)
    def _():
        for pg in range(n_pages):
            k_copy(0, pg).start()

    q = q_ref[0] * (HEAD_DIM ** -0.5 * LOG2_E)
    for h in range(MOBA_HEADS):
        qs_scr[h * SUBLANES:(h + 1) * SUBLANES, :] = jnp.where(lane // HEAD_DIM == h, q, 0.0)
    qb = qs_scr[...].astype(BF16)

    for g0 in range(0, n_pages, group):
        for pg in range(g0, g0 + group):
            k_copy(b, pg).wait()
        for pg in range(g0, g0 + group):
            s_scr[:, pg * page:(pg + 1) * page] = jnp.dot(qb, kring[pg].astype(BF16), preferred_element_type=F32)
        for pg in range(g0, g0 + group):
            v_copy(b, pg).start()

    col = lax.broadcasted_iota(jnp.int32, (rows, LANES), 1)
    gate = jnp.zeros((rows, LANES), F32)
    for n in range(n_blocks):
        c0 = n * MOBA_BLOCK
        bsum = s_scr[:, c0:c0 + page]
        for u in range(1, MOBA_BLOCK // page):
            bsum = bsum + s_scr[:, c0 + u * page:c0 + (u + 1) * page]
        gate = jnp.where(col == n, jnp.sum(bsum, axis=1, keepdims=True), gate)
    rank = jnp.zeros(gate.shape, jnp.int32)
    for jp in range(n_blocks):
        gj = gate[:, jp:jp + 1]
        rank = rank + jnp.where((gj > gate) | ((gj == gate) & (jp < col)), 1, 0)
    sel = jnp.where(rank < min(MOBA_TOPK, n_blocks), 0.0, NEG_INF)
    own = jnp.concatenate([kn_ref[0], jnp.zeros((page - SUBLANES, MOBA_W), F32)], axis=0).astype(BF16)
    t_idx = lax.broadcasted_iota(jnp.int32, (rows, page), 0) % SUBLANES
    key_idx = lax.broadcasted_iota(jnp.int32, (rows, page), 1)
    s_own = lax.dot_general(qb, own, contract_lanes, preferred_element_type=F32)
    s_own = s_own + jnp.where(key_idx <= t_idx, 0.0, NEG_INF)
    s_scr[:, n_pages * page:] = s_own
    m_part = s_own
    for n in range(n_blocks):
        add = sel[:, n:n + 1]
        for u in range(MOBA_BLOCK // page):
            c0 = n * MOBA_BLOCK + u * page
            sm = s_scr[:, c0:c0 + page] + add
            s_scr[:, c0:c0 + page] = sm
            m_part = jnp.maximum(m_part, sm)
    m = jnp.max(m_part, axis=1, keepdims=True)
    l_part = jnp.zeros((rows, page), F32)
    for n in range(n_pages + 1):
        c0 = n * page
        p = jnp.exp2(s_scr[:, c0:c0 + page] - m)
        l_part = l_part + p
        p_scr[:, c0:c0 + page] = p.astype(BF16)
    l = jnp.sum(l_part, axis=1, keepdims=True)
    vown = jnp.concatenate([vn_ref[0], jnp.zeros((page - SUBLANES, MOBA_W), F32)], axis=0).astype(BF16)
    acc = jnp.dot(p_scr[:, n_pages * page:], vown, preferred_element_type=F32)

    b_next = jnp.minimum(b + 1, last)
    for g0 in range(0, n_pages, group):
        for pg in range(g0, g0 + group):
            v_copy(b, pg).wait()
        for pg in range(g0, g0 + group):
            acc = acc + lax.dot_general(p_scr[:, pg * page:(pg + 1) * page], vring[pg].astype(BF16),
                                        contract_lanes, preferred_element_type=F32)
        for pg in range(g0, g0 + group):
            k_copy(b_next, pg).start()

    @pl.when(b == last)
    def _():
        for pg in range(n_pages):
            k_copy(b_next, pg).wait()

    o = acc / l
    out = jnp.zeros((SUBLANES, MOBA_W), F32)
    for h in range(MOBA_HEADS):
        out = out + jnp.where(lane // HEAD_DIM == h, o[h * SUBLANES:(h + 1) * SUBLANES, :], 0.0)
    o_ref[0] = (out * _silu(g_ref[0])).astype(o_ref.dtype)


def _moba_sample(page_table, mq3, mk3, mv3, mg3, cache_k4, cache_v4, layer):
    db = mq3.shape[0]
    n_pages = page_table.shape[1]
    page = cache_k4.shape[3]
    rows = MOBA_HEADS * SUBLANES
    assert n_pages * page // MOBA_BLOCK <= LANES and page == LANES
    small = pl.BlockSpec((1, SUBLANES, MOBA_W), lambda bi, pt: (bi, 0, 0))
    hbm = pl.BlockSpec(memory_space=pl.ANY)
    width = (n_pages + 1) * page
    return pl.pallas_call(
        functools.partial(_moba_sample_kernel, layer=layer, n_pages=n_pages, page=page),
        grid_spec=pltpu.PrefetchScalarGridSpec(
            num_scalar_prefetch=1,
            grid=(db,),
            in_specs=[small, small, small, small, hbm, hbm],
            out_specs=small,
            scratch_shapes=[
                pltpu.VMEM((n_pages, MOBA_W, page), F32),
                pltpu.VMEM((n_pages, MOBA_W, page), F32),
                pltpu.SemaphoreType.DMA((n_pages,)),
                pltpu.SemaphoreType.DMA((n_pages,)),
                pltpu.VMEM((rows, MOBA_W), F32),
                pltpu.VMEM((rows, width), F32),
                pltpu.VMEM((rows, width), BF16),
            ],
        ),
        out_shape=jax.ShapeDtypeStruct((db, SUBLANES, MOBA_W), BF16),
        compiler_params=_cparams(("arbitrary",)),
        name="moba_sample",
    )(page_table, mq3, mk3, mv3, mg3, cache_k4, cache_v4)


def _pool_kernel(u_ref, g_ref, buf_ref, cnt_ref, w_ref, b_ref, sc_ref, o_ref, bufo_ref, x_scr, w_scr, *,
                 n_tokens, rows):
    bb, length, _ = u_ref.shape
    lead = SUBLANES
    halo = lead + POOL_BUF + 1
    assert POOL_WINDOWS == (2, 4, 8, 16) and halo % SUBLANES == 0
    tiles = [(lead, halo - lead)] + [(halo + r0, rows) for r0 in range(0, length, rows)]
    for bi in range(bb):
        x_scr[bi, 0:lead, :] = jnp.zeros((lead, POOL_W), F32)
        x_scr[bi, lead:halo, :] = buf_ref[bi]
        x_scr[bi, halo:, :] = u_ref[bi]
        w_scr[bi, :, 0:lead, :] = jnp.zeros((3, lead, POOL_W), F32)
        for i0, n in tiles:
            group = lax.broadcasted_iota(jnp.int32, (n, POOL_W), 1) // POOL_GC
            u = x_scr[bi, i0:i0 + n, :]
            w2 = u + x_scr[bi, i0 - 1:i0 - 1 + n, :]
            w_scr[bi, 0, i0:i0 + n, :] = w2
            w4 = w2 + w_scr[bi, 0, i0 - 2:i0 - 2 + n, :]
            w_scr[bi, 1, i0:i0 + n, :] = w4
            w8 = w4 + w_scr[bi, 1, i0 - 4:i0 - 4 + n, :]
            w_scr[bi, 2, i0:i0 + n, :] = w8
            if i0 < halo:
                continue
            w16 = w8 + w_scr[bi, 2, i0 - 8:i0 - 8 + n, :]
            r0 = i0 - halo
            win = jnp.where(group == 0, w2, jnp.where(group == 1, w4, jnp.where(group == 2, w8, w16)))
            pooled = win / cnt_ref[r0:r0 + n, :] - u
            y = jnp.dot(pooled.astype(BF16), w_ref[...], preferred_element_type=F32) + b_ref[...]
            y = y * sc_ref[...]
            o_ref[bi, r0:r0 + n, :] = (y * _silu(g_ref[bi, r0:r0 + n, :])).astype(o_ref.dtype)
        bufo_ref[bi] = x_scr[bi, lead + n_tokens + 1:lead + n_tokens + 1 + POOL_BUF, :]


def _pool(zp3, buf16, cnt, w_bd, bias, scale, *, n_tokens, rows, bb):
    b, length, _ = zp3.shape
    const = lambda bi: (0, 0)
    n_rows = SUBLANES + POOL_BUF + 1 + length
    return pl.pallas_call(
        functools.partial(_pool_kernel, n_tokens=n_tokens, rows=rows),
        grid=(b // bb,),
        in_specs=[
            pl.BlockSpec((bb, length, POOL_W), lambda bi: (bi, 0, 0)),
            pl.BlockSpec((bb, length, POOL_W), lambda bi: (bi, 0, 1)),
            pl.BlockSpec((bb, POOL_BUF + 1, POOL_W), lambda bi: (bi, 0, 0)),
            pl.BlockSpec((length, POOL_W), const),
            pl.BlockSpec((POOL_W, POOL_W), const),
            pl.BlockSpec((1, POOL_W), const),
            pl.BlockSpec((1, POOL_W), const),
        ],
        out_specs=[
            pl.BlockSpec((bb, length, POOL_W), lambda bi: (bi, 0, 0)),
            pl.BlockSpec((bb, POOL_BUF, POOL_W), lambda bi: (bi, 0, 0)),
        ],
        out_shape=[
            jax.ShapeDtypeStruct((b, length, POOL_W), BF16),
            jax.ShapeDtypeStruct((b, POOL_BUF, POOL_W), F32),
        ],
        scratch_shapes=[pltpu.VMEM((bb, n_rows, POOL_W), F32), pltpu.VMEM((bb, 3, n_rows, POOL_W), F32)],
        compiler_params=_cparams(("arbitrary",)),
        name="pool",
    )(zp3, zp3, buf16, cnt, w_bd, bias, scale)


def _out_proj_kernel(mr_ref, mm_ref, mp_ref, x_ref, w_ref, y_ref, mix_scr):
    mix_scr[:, :RET_W] = mr_ref[...]
    mix_scr[:, RET_W:RET_W + MOBA_W] = mm_ref[...]
    mix_scr[:, RET_W + MOBA_W:] = mp_ref[...]
    y_ref[...] = x_ref[...] + jnp.dot(mix_scr[...], w_ref[...], preferred_element_type=F32)


def _out_proj(mr, mm, mp, x2d, w_bf, *, tm):
    m, d = x2d.shape
    d_mix = RET_W + MOBA_W + POOL_W
    row = lambda i: (i, 0)
    return pl.pallas_call(
        _out_proj_kernel,
        grid=(m // tm,),
        in_specs=[
            pl.BlockSpec((tm, RET_W), row),
            pl.BlockSpec((tm, MOBA_W), row),
            pl.BlockSpec((tm, POOL_W), row),
            pl.BlockSpec((tm, d), row),
            pl.BlockSpec((d_mix, d), lambda i: (0, 0)),
        ],
        out_specs=pl.BlockSpec((tm, d), row),
        out_shape=jax.ShapeDtypeStruct((m, d), F32),
        scratch_shapes=[pltpu.VMEM((tm, d_mix), BF16)],
        compiler_params=_cparams(("arbitrary",)),
        name="out_proj",
    )(mr, mm, mp, x2d, w_bf)


def _rope_tables(pos):
    inv = 1.0 / (ROPE_THETA ** (np.arange(HALF, dtype=np.float64) / HALF))
    ang = np.asarray(pos, np.float64)[:, None] * inv[None, :]
    c, s = np.cos(ang), np.sin(ang)
    return (jnp.asarray(np.concatenate([c, c, c, c], axis=-1), F32),
            jnp.asarray(np.concatenate([-s, s, -s, s], axis=-1), F32))


def _block_diag(blocks):
    g, n, _ = blocks.shape
    eye = jnp.eye(g, dtype=blocks.dtype)
    return (eye[:, None, :, None] * blocks[:, :, None, :]).reshape(g * n, g * n)


def _pool_counts(pos0, length):
    pos = pos0 + np.arange(length)
    w = np.repeat(np.asarray(POOL_WINDOWS), POOL_GC)
    return jnp.asarray(np.minimum(pos[:, None] + 1, w[None, :]), F32)


def _layer(x3, pos_tabs, ret_state_bd, ret_tabs, ret_step, pool_buf16, pool_cnt, n_tokens, moba_fn, params, *, tm):
    b, length, d = x3.shape
    nw, w_in, w_out, gnw, qnw, knw, bd, pw_bd, pbias, pscale = params
    x2d = x3.reshape(b * length, d)
    zr, mq, mk, mv, mg, zp = _in_proj(x2d, nw, w_in, pos_tabs[0], pos_tabs[1], qnw, knw, bd, tm=tm)
    three = lambda a: a.reshape(b, length, a.shape[-1])
    mix_r, s_new = _retention(three(zr), ret_state_bd, ret_tabs, gnw, chunk=RET_CHUNK, bb=ret_step[0],
                              pairs=ret_step[1])
    mix_m, k_new, v_new = moba_fn(three(mq), three(mk), three(mv), three(mg))
    mix_p, buf_new = _pool(three(zp), pool_buf16, pool_cnt, pw_bd, pbias, pscale,
                           n_tokens=n_tokens, rows=min(length, 128), bb=ret_step[0])
    two = lambda a: a.reshape(b * length, a.shape[-1])
    y = _out_proj(two(mix_r), two(mix_m), two(mix_p), x2d, w_out, tm=tm)
    return y.reshape(b, length, d), s_new, k_new, v_new, buf_new


def _moba_prompt_wrap(mq3, mk3, mv3, mg3, *, prev_kv):
    return _moba_prompt(mq3, mk3, mv3, mg3, prev_kv)


def _moba_sample_wrap(mq3, mk3, mv3, mg3, *, page_table, cache_k4, cache_v4, layer, n_tokens):
    mix = _moba_sample(page_table, mq3, mk3, mv3, mg3, cache_k4, cache_v4, layer)
    rows = lambda t: t[:, :n_tokens].reshape(t.shape[0], n_tokens, MOBA_HEADS, HEAD_DIM)
    return mix, rows(mk3), rows(mv3)


def kernel(x_prompt, x_sample, cache_k, cache_v, state_ret, state_pool, page_table, norm_w, w_in, w_out, ret_gn_w, q_norm_w, k_norm_w, pool_w, pool_b, pool_scale):
    depth = w_in.shape[0]
    b, seq, d = x_prompt.shape
    db, dec_seq, _ = x_sample.shape
    n_pool, page = cache_k.shape[1], cache_k.shape[2]
    past_len = page_table.shape[1] * page
    assert dec_seq <= SUBLANES and seq % MOBA_BLOCK == 0 and seq % RET_CHUNK == 0 and past_len % MOBA_BLOCK == 0

    p_tabs = _rope_tables(np.arange(seq))
    tm_s = db * SUBLANES
    s_tabs = _rope_tables(np.tile(past_len + np.arange(SUBLANES), db))
    p_ret_tabs = _retention_tables(RET_CHUNK, RET_CHUNK)
    s_ret_tabs = _retention_tables(RET_CHUNK, dec_seq)
    p_cnt = _pool_counts(0, seq)
    s_cnt = _pool_counts(past_len, SUBLANES)
    head_of = np.arange(MOBA_W) // HEAD_DIM
    bd = jnp.asarray((head_of[:, None] == head_of[None, :]) / HEAD_DIM, BF16)
    zero_buf = jnp.zeros((b, POOL_BUF + 1, POOL_W), F32)

    cache_k4 = cache_k.transpose(0, 1, 3, 4, 2).reshape(depth, n_pool, MOBA_W, page)
    cache_v4 = cache_v.transpose(0, 1, 3, 4, 2).reshape(depth, n_pool, MOBA_W, page)
    xp = x_prompt
    xs = jnp.pad(x_sample, ((0, 0), (0, SUBLANES - dec_seq), (0, 0)))

    outs = [[] for _ in range(6)]
    prev_kv = None
    for l in range(depth):
        params = (norm_w[l].reshape(1, d), w_in[l].astype(BF16), w_out[l].astype(BF16),
                  ret_gn_w[l].reshape(RET_W // LANES, 1, LANES),
                  jnp.tile(q_norm_w[l], MOBA_HEADS).reshape(1, MOBA_W),
                  jnp.tile(k_norm_w[l], MOBA_HEADS).reshape(1, MOBA_W),
                  bd, _block_diag(pool_w[l]).astype(BF16),
                  pool_b[l].reshape(1, POOL_W), pool_scale[l].reshape(1, POOL_W))
        xp, s_p, kt, vt, b_p = _layer(xp, p_tabs, None, p_ret_tabs, (1, 1), zero_buf, p_cnt, seq,
                                      functools.partial(_moba_prompt_wrap, prev_kv=prev_kv), params, tm=512)
        prev_kv = (kt, vt)
        moba_s = functools.partial(_moba_sample_wrap, page_table=page_table, cache_k4=cache_k4,
                                   cache_v4=cache_v4, layer=l, n_tokens=dec_seq)
        xs, s_s, k_s, v_s, b_s = _layer(xs, s_tabs, state_ret[l], s_ret_tabs,
                                        (math.gcd(db, 8), RET_W // LANES),
                                        jnp.pad(state_pool[l], ((0, 0), (1, 0), (0, 0))), s_cnt, dec_seq,
                                        moba_s, params, tm=tm_s)
        for lst, val in zip(outs, (k_s, v_s, s_p, s_s, b_p, b_s)):
            lst.append(val)
    rows = lambda t: t.reshape(depth, b, MOBA_HEADS, HEAD_DIM, seq).transpose(0, 1, 4, 2, 3)
    return (xp, xs[:, :dec_seq], rows(prev_kv[0]), rows(prev_kv[1])) + tuple(jnp.stack(o) for o in outs)
```

```python
import functools
import math

import jax
import jax.numpy as jnp
import numpy as np
from jax import lax
from jax.experimental import pallas as pl
from jax.experimental.pallas import tpu as pltpu

F32 = jnp.float32
BF16 = jnp.bfloat16

HEAD_DIM = 64
HALF = HEAD_DIM // 2
LANES = 128
RET_HEADS = 6
MOBA_HEADS = 6
RET_W = RET_HEADS * HEAD_DIM
MOBA_W = MOBA_HEADS * HEAD_DIM
POOL_W = 256
POOL_GC = 64
POOL_WINDOWS = (2, 4, 8, 16)
POOL_BUF = 15
RET_CHUNK = 128
MOBA_BLOCK = 256
MOBA_TOPK = 3
ROPE_THETA = 10000.0
EPS = 1e-6
SUBLANES = 8
NEG_INF = float("-inf")
LOG2_E = 1.4426950408889634

OFF_RQ, OFF_RV, OFF_MQ, OFF_MV, OFF_PU = 0, 2 * RET_W, 4 * RET_W, 4 * RET_W + 2 * MOBA_W, 4 * RET_W + 4 * MOBA_W
D_IN = OFF_PU + 2 * POOL_W

VMEM_LIMIT = 56 * 1024 * 1024


def _silu(x):
    return x / (1.0 + jnp.exp(-x))


def _cparams(sem, flags=None):
    return pltpu.CompilerParams(dimension_semantics=sem, vmem_limit_bytes=VMEM_LIMIT, flags=flags)


def _in_proj_kernel(x_ref, nw_ref, w_ref, cos_ref, sin_ref, qnw_ref, knw_ref, bd_ref,
                    zr_ref, mq_ref, mk_ref, mv_ref, mg_ref, zp_ref, h_scr, z_scr):
    tm = x_ref.shape[0]
    x = x_ref[...]
    ms = jnp.mean(x * x, axis=-1, keepdims=True)
    h_scr[...] = (x * lax.rsqrt(ms + EPS) * nw_ref[...]).astype(BF16)
    cos = cos_ref[...]
    sin = sin_ref[...]
    lane = lax.broadcasted_iota(jnp.int32, (tm, LANES), 1)
    first_half = (lane & HALF) == 0

    def rope(z):
        partner = jnp.where(first_half, pltpu.roll(z, LANES - HALF, 1), pltpu.roll(z, HALF, 1))
        return z * cos + partner * sin

    def proj(off, width):
        return jnp.dot(h_scr[...], w_ref[:, off:off + width], preferred_element_type=F32)

    z_scr[...] = proj(OFF_RQ, 2 * RET_W)
    for g in range(2 * RET_W // LANES):
        out = rope(z_scr[:, g * LANES:(g + 1) * LANES])
        if g >= RET_W // LANES:
            out = out * (HEAD_DIM ** -0.5)
        zr_ref[:, g * LANES:(g + 1) * LANES] = out
    zr_ref[:, OFF_RV:OFF_MQ] = proj(OFF_RV, 2 * RET_W)
    z_scr[...] = proj(OFF_MQ, 2 * MOBA_W)
    for t, (nw, dst) in enumerate(((qnw_ref, mq_ref), (knw_ref, mk_ref))):
        z = z_scr[:, t * MOBA_W:(t + 1) * MOBA_W]
        msq = jnp.dot((z * z).astype(BF16), bd_ref[...], preferred_element_type=F32)
        z_scr[:, t * MOBA_W:(t + 1) * MOBA_W] = z * lax.rsqrt(msq + EPS) * nw[...]
        for g in range(MOBA_W // LANES):
            c0 = t * MOBA_W + g * LANES
            dst[:, g * LANES:(g + 1) * LANES] = rope(z_scr[:, c0:c0 + LANES])
    z_scr[...] = proj(OFF_MV, 2 * MOBA_W)
    mv_ref[...] = z_scr[:, :MOBA_W]
    mg_ref[...] = z_scr[:, MOBA_W:]
    zp_ref[...] = proj(OFF_PU, 2 * POOL_W)


def _in_proj(x2d, nw, w_bf, cos, sin, qnw, knw, bd, *, tm, layer):
    m, d = x2d.shape
    n_pos = cos.shape[0] // tm
    row = lambda i: (i, 0)
    const = lambda i: (0, 0)
    pos = lambda i: (i % n_pos, 0)
    outs = [(2 * RET_W + 2 * RET_W), MOBA_W, MOBA_W, MOBA_W, MOBA_W, 2 * POOL_W]
    return pl.pallas_call(
        _in_proj_kernel,
        grid=(m // tm,),
        in_specs=[
            pl.BlockSpec((tm, d), row),
            pl.BlockSpec((1, d), const),
            pl.BlockSpec((None, d, D_IN), lambda i: (layer, 0, 0)),
            pl.BlockSpec((tm, LANES), pos),
            pl.BlockSpec((tm, LANES), pos),
            pl.BlockSpec((1, MOBA_W), const),
            pl.BlockSpec((1, MOBA_W), const),
            pl.BlockSpec((MOBA_W, MOBA_W), const),
        ],
        out_specs=[pl.BlockSpec((tm, w), row) for w in outs],
        out_shape=[jax.ShapeDtypeStruct((m, w), F32) for w in outs],
        scratch_shapes=[pltpu.VMEM((tm, d), BF16), pltpu.VMEM((tm, 2 * RET_W), F32)],
        compiler_params=_cparams(("arbitrary",)),
        name="in_proj",
    )(x2d, nw, w_bf, cos, sin, qnw, knw, bd)


def _retention_kernel(*refs, chunk, has_state, n_prev):
    q_ref, k_ref, v_ref, g_ref = refs[:4]
    s0_ref = refs[4] if has_state else None
    prev_ref = refs[4 + has_state] if n_prev else None
    dm_ref, rs_ref, kd_ref, gc_ref, gnw_ref, o_ref, sout_ref, oi_scr, kv_scr, sb_scr = refs[4 + has_state + (n_prev > 0):]
    if n_prev:
        sout_ref[0:n_prev] = prev_ref[...]
    bb, length, width = q_ref.shape
    pairs = width // LANES
    rows_in = min(length, chunk)
    n_chunks = max(1, length // chunk)
    lane = lax.broadcasted_iota(jnp.int32, (chunk, LANES), 1)
    head0 = lane < HEAD_DIM
    r = lax.broadcasted_iota(jnp.int32, (LANES, LANES), 0)
    c = lax.broadcasted_iota(jnp.int32, (LANES, LANES), 1)
    same_head = (r < HEAD_DIM) == (c < HEAD_DIM)
    contract_lanes = (((1,), (1,)), ((), ()))

    def head_mean(x):
        m0 = jnp.sum(jnp.where(head0, x, 0.0), axis=1, keepdims=True)
        m1 = jnp.sum(jnp.where(head0, 0.0, x), axis=1, keepdims=True)
        return jnp.where(head0, m0, m1) * (1.0 / HEAD_DIM)

    def load(ref, bi, p, ci):
        a = ref[bi, ci * chunk:ci * chunk + rows_in, p * LANES:(p + 1) * LANES]
        if rows_in < chunk:
            a = jnp.concatenate([a, jnp.zeros((chunk - rows_in, LANES), F32)], axis=0)
        return a

    items = [(bi, p, ci) for bi in range(bb) for p in range(pairs) for ci in range(n_chunks)]

    def a1(bi, p, ci):
        q = load(q_ref, bi, p, ci)
        k = load(k_ref, bi, p, ci)
        kb = k.astype(BF16)
        vb = load(v_ref, bi, p, ci).astype(BF16)
        q0 = jnp.where(head0, q, 0.0).astype(BF16)
        q1 = jnp.where(head0, 0.0, q).astype(BF16)
        in0 = (lax.dot_general(q0, kb, contract_lanes, preferred_element_type=F32) * dm_ref[2 * p]).astype(BF16)
        in1 = (lax.dot_general(q1, kb, contract_lanes, preferred_element_type=F32) * dm_ref[2 * p + 1]).astype(BF16)
        return bi, p, ci, k, vb, in0, in1

    def a2(st):
        bi, p, ci, k, vb, in0, in1 = st
        slot = bi * pairs + p
        oi_scr[slot, ci * chunk:(ci + 1) * chunk, :] = jnp.where(
            head0, jnp.dot(in0, vb, preferred_element_type=F32), jnp.dot(in1, vb, preferred_element_type=F32))
        kdt = (k * kd_ref[p]).T.astype(BF16)
        kv_scr[slot, ci] = jnp.where(same_head, jnp.dot(kdt, vb, preferred_element_type=F32), 0.0)

    st = None
    for item in items + [None]:
        nxt = a1(*item) if item is not None else None
        if st is not None:
            a2(st)
        st = nxt

    zero = jnp.zeros((HEAD_DIM, HEAD_DIM), F32)
    for bi in range(bb):
        for p in range(pairs):
            slot = bi * pairs + p
            if s0_ref is None:
                s = jnp.zeros((LANES, LANES), F32)
            else:
                s = jnp.concatenate([jnp.concatenate([s0_ref[bi, 2 * p], zero], axis=1),
                                     jnp.concatenate([zero, s0_ref[bi, 2 * p + 1]], axis=1)], axis=0)
            for ci in range(n_chunks):
                sb_scr[slot, ci] = s.astype(BF16)
                s = s * gc_ref[p] + kv_scr[slot, ci]
            sout_ref[n_prev, bi, 2 * p] = s[:HEAD_DIM, :HEAD_DIM]
            sout_ref[n_prev, bi, 2 * p + 1] = pltpu.roll(s[HEAD_DIM:], HEAD_DIM, 1)[:, :HEAD_DIM]

    def c1(bi, p, ci):
        slot = bi * pairs + p
        qb = load(q_ref, bi, p, ci).astype(BF16)
        o = (oi_scr[slot, ci * chunk:(ci + 1) * chunk, :]
             + jnp.dot(qb, sb_scr[slot, ci], preferred_element_type=F32) * rs_ref[p])
        return bi, p, ci, o, head_mean(o)

    def c2(st):
        bi, p, ci, o, mu = st
        oc = o - mu
        return bi, p, ci, oc, head_mean(oc * oc)

    def c3(st):
        bi, p, ci, oc, var = st
        on = oc * lax.rsqrt(var + EPS) * gnw_ref[p]
        out = on * _silu(load(g_ref, bi, p, ci))
        o_ref[bi, ci * chunk:ci * chunk + rows_in, p * LANES:(p + 1) * LANES] = out[:rows_in].astype(o_ref.dtype)

    s1 = s2 = None
    for item in items + [None, None]:
        new1 = c1(*item) if item is not None else None
        new2 = c2(s1) if s1 is not None else None
        if s2 is not None:
            c3(s2)
        s1, s2 = new1, new2


def _retention(zr3, state, prev_states, tabs, gnw, *, chunk, bb, pairs, layer):
    b, length, _ = zr3.shape
    n_pairs = RET_W // LANES
    dm, rs, kd, gc = tabs
    w = pairs * LANES
    n_chunks = max(1, length // chunk)
    n_prev = 0 if prev_states is None else prev_states.shape[0]
    col = lambda off: (lambda bi, p: (bi, 0, off + p))
    tab = lambda bi, p: (p, 0, 0)
    heads_blk = (bb, 2 * pairs, HEAD_DIM, HEAD_DIM)
    stacked = lambda n: pl.BlockSpec((n,) + heads_blk, lambda bi, p: (0, bi, p, 0, 0))
    extra_specs, extra_args = [], []
    if state is not None:
        extra_specs.append(pl.BlockSpec((None,) + heads_blk, lambda bi, p: (layer, bi, p, 0, 0)))
        extra_args.append(state)
    if n_prev:
        extra_specs.append(stacked(n_prev))
        extra_args.append(prev_states)
    return pl.pallas_call(
        functools.partial(_retention_kernel, chunk=chunk, has_state=state is not None, n_prev=n_prev),
        grid=(b // bb, n_pairs // pairs),
        in_specs=[
            pl.BlockSpec((bb, length, w), col(0)),
            pl.BlockSpec((bb, length, w), col(n_pairs // pairs)),
            pl.BlockSpec((bb, length, w), col(2 * n_pairs // pairs)),
            pl.BlockSpec((bb, length, w), col(3 * n_pairs // pairs)),
        ] + extra_specs + [
            pl.BlockSpec((2 * pairs, chunk, chunk), tab),
            pl.BlockSpec((pairs, chunk, LANES), tab),
            pl.BlockSpec((pairs, chunk, LANES), tab),
            pl.BlockSpec((pairs, 1, LANES), tab),
            pl.BlockSpec((pairs, 1, LANES), tab),
        ],
        out_specs=[pl.BlockSpec((bb, length, w), lambda bi, p: (bi, 0, p)), stacked(n_prev + 1)],
        out_shape=[
            jax.ShapeDtypeStruct((b, length, RET_W), BF16),
            jax.ShapeDtypeStruct((n_prev + 1, b, RET_HEADS, HEAD_DIM, HEAD_DIM), F32),
        ],
        scratch_shapes=[
            pltpu.VMEM((bb * pairs, n_chunks * chunk, LANES), F32),
            pltpu.VMEM((bb * pairs, n_chunks, LANES, LANES), F32),
            pltpu.VMEM((bb * pairs, n_chunks, LANES, LANES), BF16),
        ],
        compiler_params=_cparams(("arbitrary", "arbitrary")),
        name="retention",
    )(zr3, zr3, zr3, zr3, *extra_args, dm, rs, kd, gc, gnw)


def _retention_tables(chunk, n_tokens):
    lg = np.log(1.0 - 2.0 ** (-5.0 - np.arange(RET_HEADS, dtype=np.float64)))
    i = np.arange(chunk, dtype=np.float64)
    rel = i[:, None] - i[None, :]
    dm = np.where(rel[None] >= 0, np.exp(rel[None] * lg[:, None, None]), 0.0)
    lg_lanes = np.repeat(lg, HEAD_DIM).reshape(RET_W // LANES, 1, LANES)
    rs = np.exp((i + 1.0)[None, :, None] * lg_lanes)
    kd = np.where((i < n_tokens)[None, :, None], np.exp((n_tokens - 1.0 - i)[None, :, None] * lg_lanes), 0.0)
    gc = np.exp(float(n_tokens) * lg_lanes)
    return tuple(jnp.asarray(t, F32) for t in (dm, rs, kd, gc))


def _moba_prompt_kernel(*refs, n_blocks, n_prev):
    q_ref, k_ref, v_ref, g_ref = refs[:4]
    prev = refs[4:6] if n_prev else ()
    o_ref, kt_ref, vt_ref, kh_scr, vt_scr, km_scr, s_scr, p_scr = refs[4 + len(prev):]
    blk = MOBA_BLOCK
    sub = blk
    contract_lanes = (((1,), (1,)), ((), ()))
    lane = lax.broadcasted_iota(jnp.int32, (blk, LANES), 1)
    head0 = lane < HEAD_DIM
    for j in range(n_blocks):
        kj = k_ref[0, j * blk:(j + 1) * blk, :]
        kh_scr[0, j * blk:(j + 1) * blk, :] = jnp.where(head0, kj, 0.0).astype(BF16)
        kh_scr[1, j * blk:(j + 1) * blk, :] = jnp.where(head0, 0.0, kj).astype(BF16)
        vt = v_ref[0, j * blk:(j + 1) * blk, :].T
        vt_scr[:, j * blk:(j + 1) * blk] = vt.astype(BF16)
        km_scr[j:j + 1, :] = jnp.sum(kj, axis=0, keepdims=True) * (1.0 / blk)
        kt_ref[n_prev, 0, :, j * blk:(j + 1) * blk] = kj.T
        vt_ref[n_prev, 0, :, j * blk:(j + 1) * blk] = vt
    if n_prev:
        kt_ref[0:n_prev] = prev[0][...]
        vt_ref[0:n_prev] = prev[1][...]

    km = km_scr[...]
    lane8 = lax.broadcasted_iota(jnp.int32, (n_blocks, LANES), 1)
    row8 = lax.broadcasted_iota(jnp.int32, (n_blocks, blk), 0)
    krow = lax.broadcasted_iota(jnp.int32, (blk, blk), 0)
    qcol = lax.broadcasted_iota(jnp.int32, (blk, blk), 1)
    causal = jnp.where(krow <= qcol, 0.0, NEG_INF)
    fold = lambda t: t.reshape(t.shape[0] // SUBLANES, SUBLANES, blk)
    km_parts = []
    for h in range(2):
        kmh = jnp.where((lane8 < HEAD_DIM) == (h == 0), km, 0.0)
        kmh_hi = kmh.astype(BF16)
        km_parts.append((kmh_hi, (kmh - kmh_hi.astype(F32)).astype(BF16)))

    heads = range(2)
    tiles = lambda i: [(j, u, h) for j in range(i + 1) for u in range(blk // sub) for h in heads]

    def begin(i):
        q = q_ref[0, i * blk:(i + 1) * blk, :] * (HEAD_DIM ** -0.5 * LOG2_E)
        qh = q.astype(BF16)
        selmask = [None, None]
        if i > MOBA_TOPK:
            ql = (q - qh.astype(F32)).astype(BF16)
            for h in heads:
                kmh_hi, kmh_lo = km_parts[h]
                gate = (lax.dot_general(kmh_hi, qh, contract_lanes, preferred_element_type=F32)
                        + lax.dot_general(kmh_lo, qh, contract_lanes, preferred_element_type=F32)
                        + lax.dot_general(kmh_hi, ql, contract_lanes, preferred_element_type=F32))
                rank = jnp.zeros((n_blocks, blk), jnp.int32)
                for jp in range(i):
                    gj = gate[jp:jp + 1, :]
                    rank = rank + jnp.where((gj > gate) | ((gj == gate) & (jp < row8)), 1, 0)
                selmask[h] = jnp.where((rank < MOBA_TOPK) & (row8 < i), 0.0, NEG_INF)
        return dict(i=i, buf=i % 2, qh=qh, selmask=selmask, m8=[None, None], m=None,
                    l8=[jnp.zeros((SUBLANES, blk), F32) for _ in heads])

    def score_tile(st, j, u, h):
        i = st["i"]
        rows = slice(j * blk + u * sub, j * blk + (u + 1) * sub)
        s = lax.dot_general(kh_scr[h, rows, :], st["qh"], contract_lanes, preferred_element_type=F32)
        if j == i:
            s = s + causal[u * sub:(u + 1) * sub]
        s_scr[st["buf"], h, rows, :] = s
        tmax = jnp.max(fold(s), axis=0)
        if j < i and st["selmask"][h] is not None:
            tmax = tmax + st["selmask"][h][j:j + 1, :]
        st["m8"][h] = tmax if st["m8"][h] is None else jnp.maximum(st["m8"][h], tmax)

    def prob_tile(st, j, u, h):
        i = st["i"]
        if st["m"] is None:
            st["m"] = [jnp.max(st["m8"][hh], axis=0, keepdims=True) for hh in heads]
        rows = slice(j * blk + u * sub, j * blk + (u + 1) * sub)
        sm = st["selmask"][h]
        shift = -st["m"][h] if (j == i or sm is None) else sm[j:j + 1, :] - st["m"][h]
        p = jnp.exp2(s_scr[st["buf"], h, rows, :] + shift)
        st["l8"][h] = st["l8"][h] + jnp.sum(fold(p), axis=0)
        p_scr[st["buf"], h, rows, :] = p.astype(BF16)

    def finish(st):
        i = st["i"]
        nk = (i + 1) * blk
        halves = []
        for h in heads:
            l = jnp.sum(st["l8"][h], axis=0, keepdims=True)
            ot = jnp.dot(vt_scr[h * HEAD_DIM:(h + 1) * HEAD_DIM, 0:nk], p_scr[st["buf"], h, 0:nk, :],
                         preferred_element_type=F32)
            halves.append(ot / l)
        ot = jnp.concatenate(halves, axis=0)
        o_ref[0, i * blk:(i + 1) * blk, :] = (ot.T * _silu(g_ref[0, i * blk:(i + 1) * blk, :])).astype(o_ref.dtype)

    cur = begin(0)
    for t in tiles(0):
        score_tile(cur, *t)
    prev = None
    for i in range(n_blocks):
        nxt = begin(i + 1) if i + 1 < n_blocks else None
        a, b = tiles(i), (tiles(i + 1) if nxt is not None else [])
        for k in range(max(len(a), len(b))):
            if k == 2 and prev is not None:
                finish(prev)
                prev = None
            if k < len(b):
                score_tile(nxt, *b[k])
            if k < len(a):
                prob_tile(cur, *a[k])
        if prev is not None:
            finish(prev)
        prev, cur = cur, nxt
    finish(prev)


def _moba_prompt(mq3, mk3, mv3, mg3, prev_kv):
    b, length, _ = mq3.shape
    n_pairs = MOBA_W // LANES
    n_blocks = length // MOBA_BLOCK
    n_prev = 0 if prev_kv is None else prev_kv[0].shape[0]
    rows = pl.BlockSpec((1, length, LANES), lambda bi, p: (bi, 0, p))
    cols = lambda n: pl.BlockSpec((n, 1, LANES, length), lambda bi, p: (0, bi, p, 0))
    stacked = jax.ShapeDtypeStruct((n_prev + 1, b, MOBA_W, length), F32)
    return pl.pallas_call(
        functools.partial(_moba_prompt_kernel, n_blocks=n_blocks, n_prev=n_prev),
        grid=(b, n_pairs),
        in_specs=[rows, rows, rows, rows] + ([cols(n_prev), cols(n_prev)] if n_prev else []),
        out_specs=[rows, cols(n_prev + 1), cols(n_prev + 1)],
        out_shape=[jax.ShapeDtypeStruct((b, length, MOBA_W), BF16), stacked, stacked],
        scratch_shapes=[
            pltpu.VMEM((2, length, LANES), BF16),
            pltpu.VMEM((LANES, length), BF16),
            pltpu.VMEM((n_blocks, LANES), F32),
            pltpu.VMEM((2, 2, length, MOBA_BLOCK), F32),
            pltpu.VMEM((2, 2, length, MOBA_BLOCK), BF16),
        ],
        compiler_params=_cparams(("arbitrary", "arbitrary")),
        name="moba_prompt",
    )(mq3, mk3, mv3, mg3, *(prev_kv or ()))


def _moba_sample_kernel(pt_ref, q_ref, kn_ref, vn_ref, g_ref, ck_ref, cv_ref, o_ref,
                        kring, ksem, vsel, vsem, idx_v, idx_s, isem, qs_scr, s_scr, p_scr, pc_scr, *,
                        layer, n_pages, page, n_tokens):
    b = pl.program_id(0)
    last = pl.num_programs(0) - 1
    n_blocks = n_pages * page // MOBA_BLOCK
    ppb = MOBA_BLOCK // page
    slots = n_tokens * min(MOBA_TOPK, n_blocks)
    group = math.gcd(n_pages, 16)
    rows = MOBA_HEADS * SUBLANES
    contract_lanes = (((1,), (1,)), ((), ()))
    lane = lax.broadcasted_iota(jnp.int32, (SUBLANES, MOBA_W), 1)

    def k_copy(bi, pg):
        return pltpu.make_async_copy(ck_ref.at[layer, pt_ref[bi, pg]], kring.at[pg], ksem.at[pg])

    def v_copy(h, slot, u, blk):
        src = cv_ref.at[layer, pt_ref[b, blk * ppb + u], pl.ds(h * HEAD_DIM, HEAD_DIM), :]
        dst = vsel.at[h, :, pl.ds((slot * ppb + u) * page, page)]
        return pltpu.make_async_copy(src, dst, vsem.at[0])

    @pl.when(b == 0)
    def _():
        for pg in range(n_pages):
            k_copy(0, pg).start()

    q = q_ref[0] * (HEAD_DIM ** -0.5 * LOG2_E)
    for h in range(MOBA_HEADS):
        qs_scr[h * SUBLANES:(h + 1) * SUBLANES, :] = jnp.where(lane // HEAD_DIM == h, q, 0.0)
    qb = qs_scr[...].astype(BF16)

    for g0 in range(0, n_pages, group):
        for pg in range(g0, g0 + group):
            k_copy(b, pg).wait()
        for pg in range(g0, g0 + group):
            s_scr[:, pg * page:(pg + 1) * page] = jnp.dot(qb, kring[pg].astype(BF16), preferred_element_type=F32)
    b_next = jnp.minimum(b + 1, last)
    for pg in range(n_pages):
        k_copy(b_next, pg).start()

    col = lax.broadcasted_iota(jnp.int32, (rows, LANES), 1)
    gate = jnp.zeros((rows, LANES), F32)
    for n in range(n_blocks):
        c0 = n * MOBA_BLOCK
        bsum = s_scr[:, c0:c0 + page]
        for u in range(1, MOBA_BLOCK // page):
            bsum = bsum + s_scr[:, c0 + u * page:c0 + (u + 1) * page]
        gate = jnp.where(col == n, jnp.sum(bsum, axis=1, keepdims=True), gate)
    rank = jnp.zeros(gate.shape, jnp.int32)
    for jp in range(n_blocks):
        gj = gate[:, jp:jp + 1]
        rank = rank + jnp.where((gj > gate) | ((gj == gate) & (jp < col)), 1, 0)
    sel = jnp.where(rank < min(MOBA_TOPK, n_blocks), 0.0, NEG_INF)
    own = jnp.concatenate([kn_ref[0], jnp.zeros((page - SUBLANES, MOBA_W), F32)], axis=0).astype(BF16)
    t_idx = lax.broadcasted_iota(jnp.int32, (rows, page), 0) % SUBLANES
    key_idx = lax.broadcasted_iota(jnp.int32, (rows, page), 1)
    s_own = lax.dot_general(qb, own, contract_lanes, preferred_element_type=F32)
    s_own = s_own + jnp.where(key_idx <= t_idx, 0.0, NEG_INF)
    s_scr[:, n_pages * page:] = s_own
    m_part = s_own
    for n in range(n_blocks):
        add = sel[:, n:n + 1]
        for u in range(MOBA_BLOCK // page):
            c0 = n * MOBA_BLOCK + u * page
            sm = s_scr[:, c0:c0 + page] + add
            s_scr[:, c0:c0 + page] = sm
            m_part = jnp.maximum(m_part, sm)
    m = jnp.max(m_part, axis=1, keepdims=True)
    l_part = jnp.zeros((rows, page), F32)
    for n in range(n_pages + 1):
        c0 = n * page
        p = jnp.exp2(s_scr[:, c0:c0 + page] - m)
        l_part = l_part + p
        p_scr[:, c0:c0 + page] = p
    l = jnp.sum(l_part, axis=1, keepdims=True)

    row8 = lax.broadcasted_iota(jnp.int32, (SUBLANES, LANES), 0)
    col8 = lax.broadcasted_iota(jnp.int32, (SUBLANES, LANES), 1)
    sel_real = jnp.where(t_idx < n_tokens, sel, NEG_INF)
    chosen = jnp.full((SUBLANES, LANES), NEG_INF, F32)
    for h in range(MOBA_HEADS):
        any_row = jnp.max(sel_real[h * SUBLANES:(h + 1) * SUBLANES], axis=0, keepdims=True)
        chosen = jnp.where(row8 == h, any_row, chosen)
    chosen = chosen == 0.0
    before = (lax.broadcasted_iota(jnp.int32, (LANES, LANES), 0)
              < lax.broadcasted_iota(jnp.int32, (LANES, LANES), 1))
    pos = jnp.dot(jnp.where(chosen, 1.0, 0.0).astype(BF16), jnp.where(before, 1.0, 0.0).astype(BF16),
                  preferred_element_type=F32)
    count = jnp.sum(jnp.where(chosen, 1.0, 0.0), axis=1, keepdims=True)
    blk_id = col8.astype(F32)
    pick = lambda j: jnp.sum(jnp.where(chosen & (pos == j), blk_id, 0.0), axis=1, keepdims=True)
    first = pick(0)
    table = jnp.where(col8 == slots, count, 0.0)
    for j in range(slots):
        table = jnp.where(col8 == j, jnp.where(count > j, pick(j), first), table)
    idx_v[...] = table.astype(jnp.int32)
    to_smem = pltpu.make_async_copy(idx_v, idx_s, isem.at[0])
    to_smem.start()
    to_smem.wait()

    for h in range(MOBA_HEADS):
        for j in range(slots):
            for u in range(ppb):
                v_copy(h, j, u, idx_s[h, j]).start(priority=1)
    for h in range(MOBA_HEADS):
        for j in range(slots):
            weight = jnp.where(j < idx_s[h, slots], 1.0, 0.0)
            for u in range(ppb):
                c0 = pl.multiple_of((idx_s[h, j] * ppb + u) * page, page)
                piece = p_scr[h * SUBLANES:(h + 1) * SUBLANES, pl.ds(c0, page)] * weight
                pc_scr[h, :, (j * ppb + u) * page:(j * ppb + u + 1) * page] = piece.astype(BF16)
    vown = jnp.concatenate([vn_ref[0], jnp.zeros((page - SUBLANES, MOBA_W), F32)], axis=0).astype(BF16)
    o_own = jnp.dot(p_scr[:, n_pages * page:].astype(BF16), vown, preferred_element_type=F32)
    out = jnp.zeros((SUBLANES, MOBA_W), F32)
    for h in range(MOBA_HEADS):
        out = out + jnp.where(lane // HEAD_DIM == h, o_own[h * SUBLANES:(h + 1) * SUBLANES, :], 0.0)

    for h in range(MOBA_HEADS):
        for j in range(slots):
            for u in range(ppb):
                v_copy(h, j, u, 0).wait()
    o_sel = [lax.dot_general(pc_scr[h], vsel[h].astype(BF16), contract_lanes, preferred_element_type=F32)
             for h in range(MOBA_HEADS)]
    out = out + jnp.concatenate(o_sel, axis=1)

    @pl.when(b == last)
    def _():
        for pg in range(n_pages):
            k_copy(b_next, pg).wait()

    l_rows = jnp.zeros((SUBLANES, MOBA_W), F32)
    for h in range(MOBA_HEADS):
        l_rows = jnp.where(lane // HEAD_DIM == h, l[h * SUBLANES:(h + 1) * SUBLANES, :], l_rows)
    o_ref[0] = (out / l_rows * _silu(g_ref[0])).astype(o_ref.dtype)


def _moba_sample(page_table, mq3, mk3, mv3, mg3, cache_k4, cache_v4, layer, n_tokens):
    db = mq3.shape[0]
    n_pages = page_table.shape[1]
    page = cache_k4.shape[3]
    rows = MOBA_HEADS * SUBLANES
    n_blocks = n_pages * page // MOBA_BLOCK
    slots = n_tokens * min(MOBA_TOPK, n_blocks)
    assert n_blocks <= LANES and page == LANES and slots < LANES and MOBA_HEADS <= SUBLANES
    small = pl.BlockSpec((1, SUBLANES, MOBA_W), lambda bi, pt: (bi, 0, 0))
    hbm = pl.BlockSpec(memory_space=pl.ANY)
    width = (n_pages + 1) * page
    return pl.pallas_call(
        functools.partial(_moba_sample_kernel, layer=layer, n_pages=n_pages, page=page, n_tokens=n_tokens),
        grid_spec=pltpu.PrefetchScalarGridSpec(
            num_scalar_prefetch=1,
            grid=(db,),
            in_specs=[small, small, small, small, hbm, hbm],
            out_specs=small,
            scratch_shapes=[
                pltpu.VMEM((n_pages, MOBA_W, page), F32),
                pltpu.SemaphoreType.DMA((n_pages,)),
                pltpu.VMEM((MOBA_HEADS, HEAD_DIM, slots * MOBA_BLOCK), F32),
                pltpu.SemaphoreType.DMA((1,)),
                pltpu.VMEM((SUBLANES, LANES), jnp.int32),
                pltpu.SMEM((SUBLANES, LANES), jnp.int32),
                pltpu.SemaphoreType.DMA((1,)),
                pltpu.VMEM((rows, MOBA_W), F32),
                pltpu.VMEM((rows, width), F32),
                pltpu.VMEM((rows, width), F32),
                pltpu.VMEM((MOBA_HEADS, SUBLANES, slots * MOBA_BLOCK), BF16),
            ],
        ),
        out_shape=jax.ShapeDtypeStruct((db, SUBLANES, MOBA_W), BF16),
        compiler_params=_cparams(("arbitrary",)),
        name="moba_sample",
    )(page_table, mq3, mk3, mv3, mg3, cache_k4, cache_v4)


def _pool_kernel(u_ref, g_ref, buf_ref, cnt_ref, w_ref, b_ref, sc_ref, o_ref, bufo_ref, x_scr, w_scr, *,
                 n_tokens, rows):
    bb, length, _ = u_ref.shape
    lead = SUBLANES
    halo = lead + POOL_BUF + 1
    assert POOL_WINDOWS == (2, 4, 8, 16) and halo % SUBLANES == 0
    tiles = [(lead, halo - lead)] + [(halo + r0, rows) for r0 in range(0, length, rows)]
    for bi in range(bb):
        x_scr[bi, 0:lead, :] = jnp.zeros((lead, POOL_W), F32)
        x_scr[bi, lead:halo, :] = buf_ref[bi]
        x_scr[bi, halo:, :] = u_ref[bi]
        w_scr[bi, :, 0:lead, :] = jnp.zeros((3, lead, POOL_W), F32)
        for i0, n in tiles:
            group = lax.broadcasted_iota(jnp.int32, (n, POOL_W), 1) // POOL_GC
            u = x_scr[bi, i0:i0 + n, :]
            w2 = u + x_scr[bi, i0 - 1:i0 - 1 + n, :]
            w_scr[bi, 0, i0:i0 + n, :] = w2
            w4 = w2 + w_scr[bi, 0, i0 - 2:i0 - 2 + n, :]
            w_scr[bi, 1, i0:i0 + n, :] = w4
            w8 = w4 + w_scr[bi, 1, i0 - 4:i0 - 4 + n, :]
            w_scr[bi, 2, i0:i0 + n, :] = w8
            if i0 < halo:
                continue
            w16 = w8 + w_scr[bi, 2, i0 - 8:i0 - 8 + n, :]
            r0 = i0 - halo
            win = jnp.where(group == 0, w2, jnp.where(group == 1, w4, jnp.where(group == 2, w8, w16)))
            pooled = win / cnt_ref[r0:r0 + n, :] - u
            y = jnp.dot(pooled.astype(BF16), w_ref[...], preferred_element_type=F32) + b_ref[...]
            y = y * sc_ref[...]
            o_ref[bi, r0:r0 + n, :] = (y * _silu(g_ref[bi, r0:r0 + n, :])).astype(o_ref.dtype)
        bufo_ref[bi] = x_scr[bi, lead + n_tokens + 1:lead + n_tokens + 1 + POOL_BUF, :]


def _pool(zp3, buf16, cnt, w_bd, bias, scale, *, n_tokens, rows, bb):
    b, length, _ = zp3.shape
    const = lambda bi: (0, 0)
    n_rows = SUBLANES + POOL_BUF + 1 + length
    return pl.pallas_call(
        functools.partial(_pool_kernel, n_tokens=n_tokens, rows=rows),
        grid=(b // bb,),
        in_specs=[
            pl.BlockSpec((bb, length, POOL_W), lambda bi: (bi, 0, 0)),
            pl.BlockSpec((bb, length, POOL_W), lambda bi: (bi, 0, 1)),
            pl.BlockSpec((bb, POOL_BUF + 1, POOL_W), lambda bi: (bi, 0, 0)),
            pl.BlockSpec((length, POOL_W), const),
            pl.BlockSpec((POOL_W, POOL_W), const),
            pl.BlockSpec((1, POOL_W), const),
            pl.BlockSpec((1, POOL_W), const),
        ],
        out_specs=[
            pl.BlockSpec((bb, length, POOL_W), lambda bi: (bi, 0, 0)),
            pl.BlockSpec((bb, POOL_BUF, POOL_W), lambda bi: (bi, 0, 0)),
        ],
        out_shape=[
            jax.ShapeDtypeStruct((b, length, POOL_W), BF16),
            jax.ShapeDtypeStruct((b, POOL_BUF, POOL_W), F32),
        ],
        scratch_shapes=[pltpu.VMEM((bb, n_rows, POOL_W), F32), pltpu.VMEM((bb, 3, n_rows, POOL_W), F32)],
        compiler_params=_cparams(("arbitrary",)),
        name="pool",
    )(zp3, zp3, buf16, cnt, w_bd, bias, scale)


def _out_proj_kernel(mr_ref, mm_ref, mp_ref, x_ref, w_ref, y_ref, mix_scr):
    mix_scr[:, :RET_W] = mr_ref[...]
    mix_scr[:, RET_W:RET_W + MOBA_W] = mm_ref[...]
    mix_scr[:, RET_W + MOBA_W:] = mp_ref[...]
    y_ref[...] = x_ref[...] + jnp.dot(mix_scr[...], w_ref[...], preferred_element_type=F32)


def _out_proj(mr, mm, mp, x2d, w_bf, *, tm, layer):
    m, d = x2d.shape
    d_mix = RET_W + MOBA_W + POOL_W
    row = lambda i: (i, 0)
    return pl.pallas_call(
        _out_proj_kernel,
        grid=(m // tm,),
        in_specs=[
            pl.BlockSpec((tm, RET_W), row),
            pl.BlockSpec((tm, MOBA_W), row),
            pl.BlockSpec((tm, POOL_W), row),
            pl.BlockSpec((tm, d), row),
            pl.BlockSpec((None, d_mix, d), lambda i: (layer, 0, 0)),
        ],
        out_specs=pl.BlockSpec((tm, d), row),
        out_shape=jax.ShapeDtypeStruct((m, d), F32),
        scratch_shapes=[pltpu.VMEM((tm, d_mix), BF16)],
        compiler_params=_cparams(("arbitrary",)),
        name="out_proj",
    )(mr, mm, mp, x2d, w_bf)


def _rope_tables(pos):
    inv = 1.0 / (ROPE_THETA ** (np.arange(HALF, dtype=np.float64) / HALF))
    ang = np.asarray(pos, np.float64)[:, None] * inv[None, :]
    c, s = np.cos(ang), np.sin(ang)
    return (jnp.asarray(np.concatenate([c, c, c, c], axis=-1), F32),
            jnp.asarray(np.concatenate([-s, s, -s, s], axis=-1), F32))


def _block_diag(blocks):
    g, n, _ = blocks.shape
    eye = jnp.eye(g, dtype=blocks.dtype)
    return (eye[:, None, :, None] * blocks[:, :, None, :]).reshape(g * n, g * n)


def _pool_counts(pos0, length):
    pos = pos0 + np.arange(length)
    w = np.repeat(np.asarray(POOL_WINDOWS), POOL_GC)
    return jnp.asarray(np.minimum(pos[:, None] + 1, w[None, :]), F32)


def _layer(x3, pos_tabs, ret_state, ret_prev, ret_tabs, ret_step, pool_buf16, pool_cnt, n_tokens, moba_fn, params, *,
           tm, layer):
    b, length, d = x3.shape
    nw, w_in, w_out, gnw, qnw, knw, bd, pw_bd, pbias, pscale = params
    x2d = x3.reshape(b * length, d)
    zr, mq, mk, mv, mg, zp = _in_proj(x2d, nw, w_in, pos_tabs[0], pos_tabs[1], qnw, knw, bd, tm=tm, layer=layer)
    three = lambda a: a.reshape(b, length, a.shape[-1])
    mix_r, states = _retention(three(zr), ret_state, ret_prev, ret_tabs, gnw, chunk=RET_CHUNK, bb=ret_step[0],
                               pairs=ret_step[1], layer=layer)
    mix_m, k_new, v_new = moba_fn(three(mq), three(mk), three(mv), three(mg))
    mix_p, buf_new = _pool(three(zp), pool_buf16, pool_cnt, pw_bd, pbias, pscale,
                           n_tokens=n_tokens, rows=min(length, 128), bb=ret_step[0])
    two = lambda a: a.reshape(b * length, a.shape[-1])
    y = _out_proj(two(mix_r), two(mix_m), two(mix_p), x2d, w_out, tm=tm, layer=layer)
    return y.reshape(b, length, d), states, k_new, v_new, buf_new


def _moba_prompt_wrap(mq3, mk3, mv3, mg3, *, prev_kv):
    return _moba_prompt(mq3, mk3, mv3, mg3, prev_kv)


def _moba_sample_wrap(mq3, mk3, mv3, mg3, *, page_table, cache_k4, cache_v4, layer, n_tokens):
    mix = _moba_sample(page_table, mq3, mk3, mv3, mg3, cache_k4, cache_v4, layer, n_tokens)
    rows = lambda t: t[:, :n_tokens].reshape(t.shape[0], n_tokens, MOBA_HEADS, HEAD_DIM)
    return mix, rows(mk3), rows(mv3)


def kernel(x_prompt, x_sample, cache_k, cache_v, state_ret, state_pool, page_table, norm_w, w_in, w_out, ret_gn_w, q_norm_w, k_norm_w, pool_w, pool_b, pool_scale):
    depth = w_in.shape[0]
    b, seq, d = x_prompt.shape
    db, dec_seq, _ = x_sample.shape
    n_pool, page = cache_k.shape[1], cache_k.shape[2]
    past_len = page_table.shape[1] * page
    assert dec_seq <= SUBLANES and seq % MOBA_BLOCK == 0 and seq % RET_CHUNK == 0 and past_len % MOBA_BLOCK == 0

    p_tabs = _rope_tables(np.arange(seq))
    tm_s = db * SUBLANES
    s_tabs = _rope_tables(np.tile(past_len + np.arange(SUBLANES), db))
    p_ret_tabs = _retention_tables(RET_CHUNK, RET_CHUNK)
    s_ret_tabs = _retention_tables(RET_CHUNK, dec_seq)
    p_cnt = _pool_counts(0, seq)
    s_cnt = _pool_counts(past_len, SUBLANES)
    head_of = np.arange(MOBA_W) // HEAD_DIM
    bd = jnp.asarray((head_of[:, None] == head_of[None, :]) / HEAD_DIM, BF16)
    zero_buf = jnp.zeros((b, POOL_BUF + 1, POOL_W), F32)

    cache_k4 = cache_k.transpose(0, 1, 3, 4, 2).reshape(depth, n_pool, MOBA_W, page)
    cache_v4 = cache_v.transpose(0, 1, 3, 4, 2).reshape(depth, n_pool, MOBA_W, page)
    xp = x_prompt
    xs = jnp.pad(x_sample, ((0, 0), (0, SUBLANES - dec_seq), (0, 0)))

    w_in_bf = w_in.astype(BF16)
    w_out_bf = w_out.astype(BF16)
    outs = [[] for _ in range(4)]
    prev_kv = s_p = s_s = None
    for l in range(depth):
        params = (norm_w[l].reshape(1, d), w_in_bf, w_out_bf,
                  ret_gn_w[l].reshape(RET_W // LANES, 1, LANES),
                  jnp.tile(q_norm_w[l], MOBA_HEADS).reshape(1, MOBA_W),
                  jnp.tile(k_norm_w[l], MOBA_HEADS).reshape(1, MOBA_W),
                  bd, _block_diag(pool_w[l]).astype(BF16),
                  pool_b[l].reshape(1, POOL_W), pool_scale[l].reshape(1, POOL_W))
        xp, s_p, kt, vt, b_p = _layer(xp, p_tabs, None, s_p, p_ret_tabs, (1, 1), zero_buf, p_cnt, seq,
                                      functools.partial(_moba_prompt_wrap, prev_kv=prev_kv), params,
                                      tm=512, layer=l)
        prev_kv = (kt, vt)
        moba_s = functools.partial(_moba_sample_wrap, page_table=page_table, cache_k4=cache_k4,
                                   cache_v4=cache_v4, layer=l, n_tokens=dec_seq)
        xs, s_s, k_s, v_s, b_s = _layer(xs, s_tabs, state_ret, s_s, s_ret_tabs,
                                        (math.gcd(db, 8), RET_W // LANES),
                                        jnp.pad(state_pool[l], ((0, 0), (1, 0), (0, 0))), s_cnt, dec_seq,
                                        moba_s, params, tm=tm_s, layer=l)
        for lst, val in zip(outs, (k_s, v_s, b_p, b_s)):
            lst.append(val)
    rows = lambda t: t.reshape(depth, b, MOBA_HEADS, HEAD_DIM, seq).transpose(0, 1, 4, 2, 3)
    k_s, v_s, b_p, b_s = (jnp.stack(o) for o in outs)
    return xp, xs[:, :dec_seq], rows(prev_kv[0]), rows(prev_kv[1]), k_s, v_s, s_p, s_s, b_p, b_s
```

```python
import functools
import math

import jax
import jax.numpy as jnp
import numpy as np
from jax import lax
from jax.experimental import pallas as pl
from jax.experimental.pallas import tpu as pltpu

F32 = jnp.float32
BF16 = jnp.bfloat16

HEAD_DIM = 64
HALF = HEAD_DIM // 2
LANES = 128
RET_HEADS = 6
MOBA_HEADS = 6
RET_W = RET_HEADS * HEAD_DIM
MOBA_W = MOBA_HEADS * HEAD_DIM
POOL_W = 256
POOL_GC = 64
POOL_WINDOWS = (2, 4, 8, 16)
POOL_BUF = 15
RET_CHUNK = 128
MOBA_BLOCK = 256
MOBA_TOPK = 3
ROPE_THETA = 10000.0
EPS = 1e-6
SUBLANES = 8
NEG_INF = float("-inf")
LOG2_E = 1.4426950408889634

OFF_RQ, OFF_RV, OFF_MQ, OFF_MV, OFF_PU = 0, 2 * RET_W, 4 * RET_W, 4 * RET_W + 2 * MOBA_W, 4 * RET_W + 4 * MOBA_W
D_IN = OFF_PU + 2 * POOL_W

VMEM_LIMIT = 56 * 1024 * 1024


def _silu(x):
    return x / (1.0 + jnp.exp(-x))


def _cparams(sem, flags=None):
    return pltpu.CompilerParams(dimension_semantics=sem, vmem_limit_bytes=VMEM_LIMIT, flags=flags)


def _in_proj_kernel(x_ref, nw_ref, w_ref, cos_ref, sin_ref, qnw_ref, knw_ref, bd_ref,
                    zr_ref, mq_ref, mk_ref, mv_ref, mg_ref, zp_ref, h_scr, z_scr):
    tm = x_ref.shape[0]
    x = x_ref[...]
    ms = jnp.mean(x * x, axis=-1, keepdims=True)
    h_scr[...] = (x * lax.rsqrt(ms + EPS) * nw_ref[...]).astype(BF16)
    cos = cos_ref[...]
    sin = sin_ref[...]
    lane = lax.broadcasted_iota(jnp.int32, (tm, LANES), 1)
    first_half = (lane & HALF) == 0

    def rope(z):
        partner = jnp.where(first_half, pltpu.roll(z, LANES - HALF, 1), pltpu.roll(z, HALF, 1))
        return z * cos + partner * sin

    def proj(off, width):
        return jnp.dot(h_scr[...], w_ref[:, off:off + width], preferred_element_type=F32)

    z_scr[...] = proj(OFF_RQ, 2 * RET_W)
    for g in range(2 * RET_W // LANES):
        out = rope(z_scr[:, g * LANES:(g + 1) * LANES])
        if g >= RET_W // LANES:
            out = out * (HEAD_DIM ** -0.5)
        zr_ref[:, g * LANES:(g + 1) * LANES] = out
    zr_ref[:, OFF_RV:OFF_MQ] = proj(OFF_RV, 2 * RET_W)
    z_scr[...] = proj(OFF_MQ, 2 * MOBA_W)
    for t, (nw, dst) in enumerate(((qnw_ref, mq_ref), (knw_ref, mk_ref))):
        z = z_scr[:, t * MOBA_W:(t + 1) * MOBA_W]
        msq = jnp.dot((z * z).astype(BF16), bd_ref[...], preferred_element_type=F32)
        z_scr[:, t * MOBA_W:(t + 1) * MOBA_W] = z * lax.rsqrt(msq + EPS) * nw[...]
        for g in range(MOBA_W // LANES):
            c0 = t * MOBA_W + g * LANES
            dst[:, g * LANES:(g + 1) * LANES] = rope(z_scr[:, c0:c0 + LANES])
    z_scr[...] = proj(OFF_MV, 2 * MOBA_W)
    mv_ref[...] = z_scr[:, :MOBA_W]
    mg_ref[...] = z_scr[:, MOBA_W:]
    zp_ref[...] = proj(OFF_PU, 2 * POOL_W)


def _in_proj(x2d, nw, w_bf, cos, sin, qnw, knw, bd, *, tm, layer):
    m, d = x2d.shape
    n_pos = cos.shape[0] // tm
    row = lambda i: (i, 0)
    const = lambda i: (0, 0)
    pos = lambda i: (i % n_pos, 0)
    outs = [(2 * RET_W + 2 * RET_W), MOBA_W, MOBA_W, MOBA_W, MOBA_W, 2 * POOL_W]
    return pl.pallas_call(
        _in_proj_kernel,
        grid=(m // tm,),
        in_specs=[
            pl.BlockSpec((tm, d), row),
            pl.BlockSpec((1, d), const),
            pl.BlockSpec((None, d, D_IN), lambda i: (layer, 0, 0)),
            pl.BlockSpec((tm, LANES), pos),
            pl.BlockSpec((tm, LANES), pos),
            pl.BlockSpec((1, MOBA_W), const),
            pl.BlockSpec((1, MOBA_W), const),
            pl.BlockSpec((MOBA_W, MOBA_W), const),
        ],
        out_specs=[pl.BlockSpec((tm, w), row) for w in outs],
        out_shape=[jax.ShapeDtypeStruct((m, w), F32) for w in outs],
        scratch_shapes=[pltpu.VMEM((tm, d), BF16), pltpu.VMEM((tm, 2 * RET_W), F32)],
        compiler_params=_cparams(("arbitrary",)),
        name="in_proj",
    )(x2d, nw, w_bf, cos, sin, qnw, knw, bd)


def _retention_kernel(*refs, chunk, has_state, n_prev):
    q_ref, k_ref, v_ref, g_ref = refs[:4]
    s0_ref = refs[4] if has_state else None
    prev_ref = refs[4 + has_state] if n_prev else None
    dm_ref, rs_ref, kd_ref, gc_ref, gnw_ref, o_ref, sout_ref, oi_scr, kv_scr, sb_scr = refs[4 + has_state + (n_prev > 0):]
    if n_prev:
        sout_ref[0:n_prev] = prev_ref[...]
    bb, length, width = q_ref.shape
    pairs = width // LANES
    rows_in = min(length, chunk)
    n_chunks = max(1, length // chunk)
    lane = lax.broadcasted_iota(jnp.int32, (chunk, LANES), 1)
    head0 = lane < HEAD_DIM
    r = lax.broadcasted_iota(jnp.int32, (LANES, LANES), 0)
    c = lax.broadcasted_iota(jnp.int32, (LANES, LANES), 1)
    same_head = (r < HEAD_DIM) == (c < HEAD_DIM)
    contract_lanes = (((1,), (1,)), ((), ()))

    def head_mean(x):
        m0 = jnp.sum(jnp.where(head0, x, 0.0), axis=1, keepdims=True)
        m1 = jnp.sum(jnp.where(head0, 0.0, x), axis=1, keepdims=True)
        return jnp.where(head0, m0, m1) * (1.0 / HEAD_DIM)

    def load(ref, bi, p, ci):
        a = ref[bi, ci * chunk:ci * chunk + rows_in, p * LANES:(p + 1) * LANES]
        if rows_in < chunk:
            a = jnp.concatenate([a, jnp.zeros((chunk - rows_in, LANES), F32)], axis=0)
        return a

    items = [(bi, p, ci) for bi in range(bb) for p in range(pairs) for ci in range(n_chunks)]

    def a1(bi, p, ci):
        q = load(q_ref, bi, p, ci)
        k = load(k_ref, bi, p, ci)
        kb = k.astype(BF16)
        vb = load(v_ref, bi, p, ci).astype(BF16)
        q0 = jnp.where(head0, q, 0.0).astype(BF16)
        q1 = jnp.where(head0, 0.0, q).astype(BF16)
        in0 = (lax.dot_general(q0, kb, contract_lanes, preferred_element_type=F32) * dm_ref[2 * p]).astype(BF16)
        in1 = (lax.dot_general(q1, kb, contract_lanes, preferred_element_type=F32) * dm_ref[2 * p + 1]).astype(BF16)
        return bi, p, ci, k, vb, in0, in1

    def a2(st):
        bi, p, ci, k, vb, in0, in1 = st
        slot = bi * pairs + p
        oi_scr[slot, ci * chunk:(ci + 1) * chunk, :] = jnp.where(
            head0, jnp.dot(in0, vb, preferred_element_type=F32), jnp.dot(in1, vb, preferred_element_type=F32))
        kdt = (k * kd_ref[p]).T.astype(BF16)
        kv_scr[slot, ci] = jnp.where(same_head, jnp.dot(kdt, vb, preferred_element_type=F32), 0.0)

    st = None
    for item in items + [None]:
        nxt = a1(*item) if item is not None else None
        if st is not None:
            a2(st)
        st = nxt

    zero = jnp.zeros((HEAD_DIM, HEAD_DIM), F32)
    for bi in range(bb):
        for p in range(pairs):
            slot = bi * pairs + p
            if s0_ref is None:
                s = jnp.zeros((LANES, LANES), F32)
            else:
                s = jnp.concatenate([jnp.concatenate([s0_ref[bi, 2 * p], zero], axis=1),
                                     jnp.concatenate([zero, s0_ref[bi, 2 * p + 1]], axis=1)], axis=0)
            for ci in range(n_chunks):
                sb_scr[slot, ci] = s.astype(BF16)
                s = s * gc_ref[p] + kv_scr[slot, ci]
            sout_ref[n_prev, bi, 2 * p] = s[:HEAD_DIM, :HEAD_DIM]
            sout_ref[n_prev, bi, 2 * p + 1] = pltpu.roll(s[HEAD_DIM:], HEAD_DIM, 1)[:, :HEAD_DIM]

    def c1(bi, p, ci):
        slot = bi * pairs + p
        qb = load(q_ref, bi, p, ci).astype(BF16)
        o = (oi_scr[slot, ci * chunk:(ci + 1) * chunk, :]
             + jnp.dot(qb, sb_scr[slot, ci], preferred_element_type=F32) * rs_ref[p])
        return bi, p, ci, o, head_mean(o)

    def c2(st):
        bi, p, ci, o, mu = st
        oc = o - mu
        return bi, p, ci, oc, head_mean(oc * oc)

    def c3(st):
        bi, p, ci, oc, var = st
        on = oc * lax.rsqrt(var + EPS) * gnw_ref[p]
        out = on * _silu(load(g_ref, bi, p, ci))
        o_ref[bi, ci * chunk:ci * chunk + rows_in, p * LANES:(p + 1) * LANES] = out[:rows_in].astype(o_ref.dtype)

    s1 = s2 = None
    for item in items + [None, None]:
        new1 = c1(*item) if item is not None else None
        new2 = c2(s1) if s1 is not None else None
        if s2 is not None:
            c3(s2)
        s1, s2 = new1, new2


def _retention(zr3, state, prev_states, tabs, gnw, *, chunk, bb, pairs, layer):
    b, length, _ = zr3.shape
    n_pairs = RET_W // LANES
    dm, rs, kd, gc = tabs
    w = pairs * LANES
    n_chunks = max(1, length // chunk)
    n_prev = 0 if prev_states is None else prev_states.shape[0]
    col = lambda off: (lambda bi, p: (bi, 0, off + p))
    tab = lambda bi, p: (p, 0, 0)
    heads_blk = (bb, 2 * pairs, HEAD_DIM, HEAD_DIM)
    stacked = lambda n: pl.BlockSpec((n,) + heads_blk, lambda bi, p: (0, bi, p, 0, 0))
    extra_specs, extra_args = [], []
    if state is not None:
        extra_specs.append(pl.BlockSpec((None,) + heads_blk, lambda bi, p: (layer, bi, p, 0, 0)))
        extra_args.append(state)
    if n_prev:
        extra_specs.append(stacked(n_prev))
        extra_args.append(prev_states)
    return pl.pallas_call(
        functools.partial(_retention_kernel, chunk=chunk, has_state=state is not None, n_prev=n_prev),
        grid=(b // bb, n_pairs // pairs),
        in_specs=[
            pl.BlockSpec((bb, length, w), col(0)),
            pl.BlockSpec((bb, length, w), col(n_pairs // pairs)),
            pl.BlockSpec((bb, length, w), col(2 * n_pairs // pairs)),
            pl.BlockSpec((bb, length, w), col(3 * n_pairs // pairs)),
        ] + extra_specs + [
            pl.BlockSpec((2 * pairs, chunk, chunk), tab),
            pl.BlockSpec((pairs, chunk, LANES), tab),
            pl.BlockSpec((pairs, chunk, LANES), tab),
            pl.BlockSpec((pairs, 1, LANES), tab),
            pl.BlockSpec((pairs, 1, LANES), tab),
        ],
        out_specs=[pl.BlockSpec((bb, length, w), lambda bi, p: (bi, 0, p)), stacked(n_prev + 1)],
        out_shape=[
            jax.ShapeDtypeStruct((b, length, RET_W), BF16),
            jax.ShapeDtypeStruct((n_prev + 1, b, RET_HEADS, HEAD_DIM, HEAD_DIM), F32),
        ],
        scratch_shapes=[
            pltpu.VMEM((bb * pairs, n_chunks * chunk, LANES), F32),
            pltpu.VMEM((bb * pairs, n_chunks, LANES, LANES), F32),
            pltpu.VMEM((bb * pairs, n_chunks, LANES, LANES), BF16),
        ],
        compiler_params=_cparams(("arbitrary", "arbitrary")),
        name="retention",
    )(zr3, zr3, zr3, zr3, *extra_args, dm, rs, kd, gc, gnw)


def _retention_tables(chunk, n_tokens):
    lg = np.log(1.0 - 2.0 ** (-5.0 - np.arange(RET_HEADS, dtype=np.float64)))
    i = np.arange(chunk, dtype=np.float64)
    rel = i[:, None] - i[None, :]
    dm = np.where(rel[None] >= 0, np.exp(rel[None] * lg[:, None, None]), 0.0)
    lg_lanes = np.repeat(lg, HEAD_DIM).reshape(RET_W // LANES, 1, LANES)
    rs = np.exp((i + 1.0)[None, :, None] * lg_lanes)
    kd = np.where((i < n_tokens)[None, :, None], np.exp((n_tokens - 1.0 - i)[None, :, None] * lg_lanes), 0.0)
    gc = np.exp(float(n_tokens) * lg_lanes)
    return tuple(jnp.asarray(t, F32) for t in (dm, rs, kd, gc))


def _moba_prompt_kernel(*refs, n_blocks, n_prev):
    q_ref, k_ref, v_ref, g_ref = refs[:4]
    prev = refs[4:6] if n_prev else ()
    o_ref, kt_ref, vt_ref, kh_scr, vt_scr, km_scr, s_scr, p_scr = refs[4 + len(prev):]
    blk = MOBA_BLOCK
    sub = blk
    contract_lanes = (((1,), (1,)), ((), ()))
    lane = lax.broadcasted_iota(jnp.int32, (blk, LANES), 1)
    head0 = lane < HEAD_DIM
    for j in range(n_blocks):
        kj = k_ref[0, j * blk:(j + 1) * blk, :]
        kh_scr[0, j * blk:(j + 1) * blk, :] = jnp.where(head0, kj, 0.0).astype(BF16)
        kh_scr[1, j * blk:(j + 1) * blk, :] = jnp.where(head0, 0.0, kj).astype(BF16)
        vt = v_ref[0, j * blk:(j + 1) * blk, :].T
        vt_scr[:, j * blk:(j + 1) * blk] = vt.astype(BF16)
        km_scr[j:j + 1, :] = jnp.sum(kj, axis=0, keepdims=True) * (1.0 / blk)
        kt_ref[n_prev, 0, :, j * blk:(j + 1) * blk] = kj.T
        vt_ref[n_prev, 0, :, j * blk:(j + 1) * blk] = vt
    if n_prev:
        kt_ref[0:n_prev] = prev[0][...]
        vt_ref[0:n_prev] = prev[1][...]

    km = km_scr[...]
    lane8 = lax.broadcasted_iota(jnp.int32, (n_blocks, LANES), 1)
    row8 = lax.broadcasted_iota(jnp.int32, (n_blocks, blk), 0)
    krow = lax.broadcasted_iota(jnp.int32, (blk, blk), 0)
    qcol = lax.broadcasted_iota(jnp.int32, (blk, blk), 1)
    causal = jnp.where(krow <= qcol, 0.0, NEG_INF)
    fold = lambda t: t.reshape(t.shape[0] // SUBLANES, SUBLANES, blk)
    km_parts = []
    for h in range(2):
        kmh = jnp.where((lane8 < HEAD_DIM) == (h == 0), km, 0.0)
        kmh_hi = kmh.astype(BF16)
        km_parts.append((kmh_hi, (kmh - kmh_hi.astype(F32)).astype(BF16)))

    heads = range(2)
    tiles = lambda i: [(j, u, h) for j in range(i + 1) for u in range(blk // sub) for h in heads]

    def begin(i):
        q = q_ref[0, i * blk:(i + 1) * blk, :] * (HEAD_DIM ** -0.5 * LOG2_E)
        qh = q.astype(BF16)
        selmask = [None, None]
        if i > MOBA_TOPK:
            ql = (q - qh.astype(F32)).astype(BF16)
            for h in heads:
                kmh_hi, kmh_lo = km_parts[h]
                gate = (lax.dot_general(kmh_hi, qh, contract_lanes, preferred_element_type=F32)
                        + lax.dot_general(kmh_lo, qh, contract_lanes, preferred_element_type=F32)
                        + lax.dot_general(kmh_hi, ql, contract_lanes, preferred_element_type=F32))
                rank = jnp.zeros((n_blocks, blk), jnp.int32)
                for jp in range(i):
                    gj = gate[jp:jp + 1, :]
                    rank = rank + jnp.where((gj > gate) | ((gj == gate) & (jp < row8)), 1, 0)
                selmask[h] = jnp.where((rank < MOBA_TOPK) & (row8 < i), 0.0, NEG_INF)
        return dict(i=i, buf=i % 2, qh=qh, selmask=selmask, m8=[None, None], m=None,
                    l8=[jnp.zeros((SUBLANES, blk), F32) for _ in heads])

    def score_tile(st, j, u, h):
        i = st["i"]
        rows = slice(j * blk + u * sub, j * blk + (u + 1) * sub)
        s = lax.dot_general(kh_scr[h, rows, :], st["qh"], contract_lanes, preferred_element_type=F32)
        if j == i:
            s = s + causal[u * sub:(u + 1) * sub]
        s_scr[st["buf"], h, rows, :] = s
        tmax = jnp.max(fold(s), axis=0)
        if j < i and st["selmask"][h] is not None:
            tmax = tmax + st["selmask"][h][j:j + 1, :]
        st["m8"][h] = tmax if st["m8"][h] is None else jnp.maximum(st["m8"][h], tmax)

    def prob_tile(st, j, u, h):
        i = st["i"]
        if st["m"] is None:
            st["m"] = [jnp.max(st["m8"][hh], axis=0, keepdims=True) for hh in heads]
        rows = slice(j * blk + u * sub, j * blk + (u + 1) * sub)
        sm = st["selmask"][h]
        shift = -st["m"][h] if (j == i or sm is None) else sm[j:j + 1, :] - st["m"][h]
        p = jnp.exp2(s_scr[st["buf"], h, rows, :] + shift)
        st["l8"][h] = st["l8"][h] + jnp.sum(fold(p), axis=0)
        p_scr[st["buf"], h, rows, :] = p.astype(BF16)

    def finish(st):
        i = st["i"]
        nk = (i + 1) * blk
        halves = []
        for h in heads:
            l = jnp.sum(st["l8"][h], axis=0, keepdims=True)
            ot = jnp.dot(vt_scr[h * HEAD_DIM:(h + 1) * HEAD_DIM, 0:nk], p_scr[st["buf"], h, 0:nk, :],
                         preferred_element_type=F32)
            halves.append(ot / l)
        ot = jnp.concatenate(halves, axis=0)
        o_ref[0, i * blk:(i + 1) * blk, :] = (ot.T * _silu(g_ref[0, i * blk:(i + 1) * blk, :])).astype(o_ref.dtype)

    cur = begin(0)
    for t in tiles(0):
        score_tile(cur, *t)
    prev = None
    for i in range(n_blocks):
        nxt = begin(i + 1) if i + 1 < n_blocks else None
        a, b = tiles(i), (tiles(i + 1) if nxt is not None else [])
        for k in range(max(len(a), len(b))):
            if k == 2 and prev is not None:
                finish(prev)
                prev = None
            if k < len(b):
                score_tile(nxt, *b[k])
            if k < len(a):
                prob_tile(cur, *a[k])
        if prev is not None:
            finish(prev)
        prev, cur = cur, nxt
    finish(prev)


def _moba_prompt(mq3, mk3, mv3, mg3, prev_kv):
    b, length, _ = mq3.shape
    n_pairs = MOBA_W // LANES
    n_blocks = length // MOBA_BLOCK
    n_prev = 0 if prev_kv is None else prev_kv[0].shape[0]
    rows = pl.BlockSpec((1, length, LANES), lambda bi, p: (bi, 0, p))
    cols = lambda n: pl.BlockSpec((n, 1, LANES, length), lambda bi, p: (0, bi, p, 0))
    stacked = jax.ShapeDtypeStruct((n_prev + 1, b, MOBA_W, length), F32)
    return pl.pallas_call(
        functools.partial(_moba_prompt_kernel, n_blocks=n_blocks, n_prev=n_prev),
        grid=(b, n_pairs),
        in_specs=[rows, rows, rows, rows] + ([cols(n_prev), cols(n_prev)] if n_prev else []),
        out_specs=[rows, cols(n_prev + 1), cols(n_prev + 1)],
        out_shape=[jax.ShapeDtypeStruct((b, length, MOBA_W), BF16), stacked, stacked],
        scratch_shapes=[
            pltpu.VMEM((2, length, LANES), BF16),
            pltpu.VMEM((LANES, length), BF16),
            pltpu.VMEM((n_blocks, LANES), F32),
            pltpu.VMEM((2, 2, length, MOBA_BLOCK), F32),
            pltpu.VMEM((2, 2, length, MOBA_BLOCK), BF16),
        ],
        compiler_params=_cparams(("arbitrary", "arbitrary")),
        name="moba_prompt",
    )(mq3, mk3, mv3, mg3, *(prev_kv or ()))


def _moba_sample_kernel(pt_ref, q_ref, kn_ref, vn_ref, g_ref, ck_ref, cv_ref, o_ref,
                        kring, ksem, vsel, vsem, idx_v, idx_s, isem, qs_scr, s_scr, p_scr, pc_scr, *,
                        layer, n_pages, page, n_tokens):
    b = pl.program_id(0)
    last = pl.num_programs(0) - 1
    n_blocks = n_pages * page // MOBA_BLOCK
    ppb = MOBA_BLOCK // page
    slots = n_tokens * min(MOBA_TOPK, n_blocks)
    group = math.gcd(n_pages, 16)
    rows = MOBA_HEADS * SUBLANES
    contract_lanes = (((1,), (1,)), ((), ()))
    lane = lax.broadcasted_iota(jnp.int32, (SUBLANES, MOBA_W), 1)

    def k_copy(bi, pg):
        return pltpu.make_async_copy(ck_ref.at[layer, pt_ref[bi, pg]], kring.at[pg], ksem.at[pg])

    def v_copy(h, slot, u, blk):
        src = cv_ref.at[layer, pt_ref[b, blk * ppb + u], pl.ds(h * HEAD_DIM, HEAD_DIM), :]
        dst = vsel.at[h, slot * ppb + u]
        return pltpu.make_async_copy(src, dst, vsem.at[0])

    @pl.when(b == 0)
    def _():
        for pg in range(n_pages):
            k_copy(0, pg).start()

    q = q_ref[0] * (HEAD_DIM ** -0.5 * LOG2_E)
    for h in range(MOBA_HEADS):
        qs_scr[h * SUBLANES:(h + 1) * SUBLANES, :] = jnp.where(lane // HEAD_DIM == h, q, 0.0)
    qb = qs_scr[...].astype(BF16)

    for g0 in range(0, n_pages, group):
        for pg in range(g0, g0 + group):
            k_copy(b, pg).wait()
        for pg in range(g0, g0 + group):
            s_scr[:, pg * page:(pg + 1) * page] = jnp.dot(qb, kring[pg].astype(BF16), preferred_element_type=F32)
    b_next = jnp.minimum(b + 1, last)
    for pg in range(n_pages):
        k_copy(b_next, pg).start()

    col = lax.broadcasted_iota(jnp.int32, (rows, LANES), 1)
    gate = jnp.zeros((rows, LANES), F32)
    for n in range(n_blocks):
        c0 = n * MOBA_BLOCK
        bsum = s_scr[:, c0:c0 + page]
        for u in range(1, MOBA_BLOCK // page):
            bsum = bsum + s_scr[:, c0 + u * page:c0 + (u + 1) * page]
        gate = jnp.where(col == n, jnp.sum(bsum, axis=1, keepdims=True), gate)
    rank = jnp.zeros(gate.shape, jnp.int32)
    for jp in range(n_blocks):
        gj = gate[:, jp:jp + 1]
        rank = rank + jnp.where((gj > gate) | ((gj == gate) & (jp < col)), 1, 0)
    sel = jnp.where(rank < min(MOBA_TOPK, n_blocks), 0.0, NEG_INF)
    own = jnp.concatenate([kn_ref[0], jnp.zeros((page - SUBLANES, MOBA_W), F32)], axis=0).astype(BF16)
    t_idx = lax.broadcasted_iota(jnp.int32, (rows, page), 0) % SUBLANES
    key_idx = lax.broadcasted_iota(jnp.int32, (rows, page), 1)
    s_own = lax.dot_general(qb, own, contract_lanes, preferred_element_type=F32)
    s_own = s_own + jnp.where(key_idx <= t_idx, 0.0, NEG_INF)
    s_scr[:, n_pages * page:] = s_own
    m_part = s_own
    for n in range(n_blocks):
        add = sel[:, n:n + 1]
        for u in range(MOBA_BLOCK // page):
            c0 = n * MOBA_BLOCK + u * page
            sm = s_scr[:, c0:c0 + page] + add
            s_scr[:, c0:c0 + page] = sm
            m_part = jnp.maximum(m_part, sm)
    m = jnp.max(m_part, axis=1, keepdims=True)
    l_part = jnp.zeros((rows, page), F32)
    for n in range(n_pages + 1):
        c0 = n * page
        p = jnp.exp2(s_scr[:, c0:c0 + page] - m)
        l_part = l_part + p
        p_scr[:, c0:c0 + page] = p
    l = jnp.sum(l_part, axis=1, keepdims=True)

    row8 = lax.broadcasted_iota(jnp.int32, (SUBLANES, LANES), 0)
    col8 = lax.broadcasted_iota(jnp.int32, (SUBLANES, LANES), 1)
    sel_real = jnp.where(t_idx < n_tokens, sel, NEG_INF)
    chosen = jnp.full((SUBLANES, LANES), NEG_INF, F32)
    for h in range(MOBA_HEADS):
        any_row = jnp.max(sel_real[h * SUBLANES:(h + 1) * SUBLANES], axis=0, keepdims=True)
        chosen = jnp.where(row8 == h, any_row, chosen)
    chosen = chosen == 0.0
    before = (lax.broadcasted_iota(jnp.int32, (LANES, LANES), 0)
              < lax.broadcasted_iota(jnp.int32, (LANES, LANES), 1))
    pos = jnp.dot(jnp.where(chosen, 1.0, 0.0).astype(BF16), jnp.where(before, 1.0, 0.0).astype(BF16),
                  preferred_element_type=F32)
    count = jnp.sum(jnp.where(chosen, 1.0, 0.0), axis=1, keepdims=True)
    blk_id = col8.astype(F32)
    pick = lambda j: jnp.sum(jnp.where(chosen & (pos == j), blk_id, 0.0), axis=1, keepdims=True)
    first = pick(0)
    table = jnp.where(col8 == slots, count, 0.0)
    for j in range(slots):
        table = jnp.where(col8 == j, jnp.where(count > j, pick(j), first), table)
    idx_v[...] = table.astype(jnp.int32)
    to_smem = pltpu.make_async_copy(idx_v, idx_s, isem.at[0])
    to_smem.start()
    to_smem.wait()

    n_copy = 0
    for h in range(MOBA_HEADS):
        for j in range(slots):
            for u in range(ppb):
                v_copy(h, j, u, idx_s[h, j]).start(priority=n_copy % 2)
                n_copy += 1
    for h in range(MOBA_HEADS):
        for j in range(slots):
            weight = jnp.where(j < idx_s[h, slots], 1.0, 0.0)
            for u in range(ppb):
                c0 = pl.multiple_of((idx_s[h, j] * ppb + u) * page, page)
                pc_scr[h, j * ppb + u] = p_scr[h * SUBLANES:(h + 1) * SUBLANES, pl.ds(c0, page)] * weight
    vown = jnp.concatenate([vn_ref[0], jnp.zeros((page - SUBLANES, MOBA_W), F32)], axis=0).astype(BF16)
    o_own = jnp.dot(p_scr[:, n_pages * page:].astype(BF16), vown, preferred_element_type=F32)
    out = jnp.zeros((SUBLANES, MOBA_W), F32)
    for h in range(MOBA_HEADS):
        out = out + jnp.where(lane // HEAD_DIM == h, o_own[h * SUBLANES:(h + 1) * SUBLANES, :], 0.0)

    for h in range(MOBA_HEADS):
        for j in range(slots):
            for u in range(ppb):
                v_copy(h, j, u, 0).wait()
    o_sel = []
    for h in range(MOBA_HEADS):
        parts = [jnp.zeros((SUBLANES, HEAD_DIM), F32) for _ in range(4)]
        for piece in range(slots * ppb):
            parts[piece % 4] = parts[piece % 4] + lax.dot_general(
                pc_scr[h, piece].astype(BF16), vsel[h, piece].astype(BF16), contract_lanes,
                preferred_element_type=F32)
        o_sel.append((parts[0] + parts[1]) + (parts[2] + parts[3]))
    out = out + jnp.concatenate(o_sel, axis=1)

    @pl.when(b == last)
    def _():
        for pg in range(n_pages):
            k_copy(b_next, pg).wait()

    l_rows = jnp.zeros((SUBLANES, MOBA_W), F32)
    for h in range(MOBA_HEADS):
        l_rows = jnp.where(lane // HEAD_DIM == h, l[h * SUBLANES:(h + 1) * SUBLANES, :], l_rows)
    o_ref[0] = (out / l_rows * _silu(g_ref[0])).astype(o_ref.dtype)


def _moba_sample(page_table, mq3, mk3, mv3, mg3, cache_k4, cache_v4, layer, n_tokens):
    db = mq3.shape[0]
    n_pages = page_table.shape[1]
    page = cache_k4.shape[3]
    rows = MOBA_HEADS * SUBLANES
    n_blocks = n_pages * page // MOBA_BLOCK
    slots = n_tokens * min(MOBA_TOPK, n_blocks)
    assert n_blocks <= LANES and page == LANES and slots < LANES and MOBA_HEADS <= SUBLANES
    small = pl.BlockSpec((1, SUBLANES, MOBA_W), lambda bi, pt: (bi, 0, 0))
    hbm = pl.BlockSpec(memory_space=pl.ANY)
    width = (n_pages + 1) * page
    return pl.pallas_call(
        functools.partial(_moba_sample_kernel, layer=layer, n_pages=n_pages, page=page, n_tokens=n_tokens),
        grid_spec=pltpu.PrefetchScalarGridSpec(
            num_scalar_prefetch=1,
            grid=(db,),
            in_specs=[small, small, small, small, hbm, hbm],
            out_specs=small,
            scratch_shapes=[
                pltpu.VMEM((n_pages, MOBA_W, page), F32),
                pltpu.SemaphoreType.DMA((n_pages,)),
                pltpu.VMEM((MOBA_HEADS, slots * MOBA_BLOCK // page, HEAD_DIM, page), F32),
                pltpu.SemaphoreType.DMA((1,)),
                pltpu.VMEM((SUBLANES, LANES), jnp.int32),
                pltpu.SMEM((SUBLANES, LANES), jnp.int32),
                pltpu.SemaphoreType.DMA((1,)),
                pltpu.VMEM((rows, MOBA_W), F32),
                pltpu.VMEM((rows, width), F32),
                pltpu.VMEM((rows, width), F32),
                pltpu.VMEM((MOBA_HEADS, slots * MOBA_BLOCK // page, SUBLANES, page), F32),
            ],
        ),
        out_shape=jax.ShapeDtypeStruct((db, SUBLANES, MOBA_W), BF16),
        compiler_params=_cparams(("arbitrary",)),
        name="moba_sample",
    )(page_table, mq3, mk3, mv3, mg3, cache_k4, cache_v4)


def _pool_kernel(u_ref, g_ref, buf_ref, cnt_ref, w_ref, b_ref, sc_ref, o_ref, bufo_ref, x_scr, w_scr, *,
                 n_tokens, rows):
    bb, length, _ = u_ref.shape
    lead = SUBLANES
    halo = lead + POOL_BUF + 1
    assert POOL_WINDOWS == (2, 4, 8, 16) and halo % SUBLANES == 0
    tiles = [(lead, halo - lead)] + [(halo + r0, rows) for r0 in range(0, length, rows)]
    for bi in range(bb):
        x_scr[bi, 0:lead, :] = jnp.zeros((lead, POOL_W), F32)
        x_scr[bi, lead:halo, :] = buf_ref[bi]
        x_scr[bi, halo:, :] = u_ref[bi]
        w_scr[bi, :, 0:lead, :] = jnp.zeros((3, lead, POOL_W), F32)
        for i0, n in tiles:
            group = lax.broadcasted_iota(jnp.int32, (n, POOL_W), 1) // POOL_GC
            u = x_scr[bi, i0:i0 + n, :]
            w2 = u + x_scr[bi, i0 - 1:i0 - 1 + n, :]
            w_scr[bi, 0, i0:i0 + n, :] = w2
            w4 = w2 + w_scr[bi, 0, i0 - 2:i0 - 2 + n, :]
            w_scr[bi, 1, i0:i0 + n, :] = w4
            w8 = w4 + w_scr[bi, 1, i0 - 4:i0 - 4 + n, :]
            w_scr[bi, 2, i0:i0 + n, :] = w8
            if i0 < halo:
                continue
            w16 = w8 + w_scr[bi, 2, i0 - 8:i0 - 8 + n, :]
            r0 = i0 - halo
            win = jnp.where(group == 0, w2, jnp.where(group == 1, w4, jnp.where(group == 2, w8, w16)))
            pooled = win / cnt_ref[r0:r0 + n, :] - u
            y = jnp.dot(pooled.astype(BF16), w_ref[...], preferred_element_type=F32) + b_ref[...]
            y = y * sc_ref[...]
            o_ref[bi, r0:r0 + n, :] = (y * _silu(g_ref[bi, r0:r0 + n, :])).astype(o_ref.dtype)
        bufo_ref[bi] = x_scr[bi, lead + n_tokens + 1:lead + n_tokens + 1 + POOL_BUF, :]


def _pool(zp3, buf16, cnt, w_bd, bias, scale, *, n_tokens, rows, bb):
    b, length, _ = zp3.shape
    const = lambda bi: (0, 0)
    n_rows = SUBLANES + POOL_BUF + 1 + length
    return pl.pallas_call(
        functools.partial(_pool_kernel, n_tokens=n_tokens, rows=rows),
        grid=(b // bb,),
        in_specs=[
            pl.BlockSpec((bb, length, POOL_W), lambda bi: (bi, 0, 0)),
            pl.BlockSpec((bb, length, POOL_W), lambda bi: (bi, 0, 1)),
            pl.BlockSpec((bb, POOL_BUF + 1, POOL_W), lambda bi: (bi, 0, 0)),
            pl.BlockSpec((length, POOL_W), const),
            pl.BlockSpec((POOL_W, POOL_W), const),
            pl.BlockSpec((1, POOL_W), const),
            pl.BlockSpec((1, POOL_W), const),
        ],
        out_specs=[
            pl.BlockSpec((bb, length, POOL_W), lambda bi: (bi, 0, 0)),
            pl.BlockSpec((bb, POOL_BUF, POOL_W), lambda bi: (bi, 0, 0)),
        ],
        out_shape=[
            jax.ShapeDtypeStruct((b, length, POOL_W), BF16),
            jax.ShapeDtypeStruct((b, POOL_BUF, POOL_W), F32),
        ],
        scratch_shapes=[pltpu.VMEM((bb, n_rows, POOL_W), F32), pltpu.VMEM((bb, 3, n_rows, POOL_W), F32)],
        compiler_params=_cparams(("arbitrary",)),
        name="pool",
    )(zp3, zp3, buf16, cnt, w_bd, bias, scale)


def _out_proj_kernel(mr_ref, mm_ref, mp_ref, x_ref, w_ref, y_ref, mix_scr):
    mix_scr[:, :RET_W] = mr_ref[...]
    mix_scr[:, RET_W:RET_W + MOBA_W] = mm_ref[...]
    mix_scr[:, RET_W + MOBA_W:] = mp_ref[...]
    y_ref[...] = x_ref[...] + jnp.dot(mix_scr[...], w_ref[...], preferred_element_type=F32)


def _out_proj(mr, mm, mp, x2d, w_bf, *, tm, layer):
    m, d = x2d.shape
    d_mix = RET_W + MOBA_W + POOL_W
    row = lambda i: (i, 0)
    return pl.pallas_call(
        _out_proj_kernel,
        grid=(m // tm,),
        in_specs=[
            pl.BlockSpec((tm, RET_W), row),
            pl.BlockSpec((tm, MOBA_W), row),
            pl.BlockSpec((tm, POOL_W), row),
            pl.BlockSpec((tm, d), row),
            pl.BlockSpec((None, d_mix, d), lambda i: (layer, 0, 0)),
        ],
        out_specs=pl.BlockSpec((tm, d), row),
        out_shape=jax.ShapeDtypeStruct((m, d), F32),
        scratch_shapes=[pltpu.VMEM((tm, d_mix), BF16)],
        compiler_params=_cparams(("arbitrary",)),
        name="out_proj",
    )(mr, mm, mp, x2d, w_bf)


def _rope_tables(pos):
    inv = 1.0 / (ROPE_THETA ** (np.arange(HALF, dtype=np.float64) / HALF))
    ang = np.asarray(pos, np.float64)[:, None] * inv[None, :]
    c, s = np.cos(ang), np.sin(ang)
    return (jnp.asarray(np.concatenate([c, c, c, c], axis=-1), F32),
            jnp.asarray(np.concatenate([-s, s, -s, s], axis=-1), F32))


def _block_diag(blocks):
    g, n, _ = blocks.shape
    eye = jnp.eye(g, dtype=blocks.dtype)
    return (eye[:, None, :, None] * blocks[:, :, None, :]).reshape(g * n, g * n)


def _pool_counts(pos0, length):
    pos = pos0 + np.arange(length)
    w = np.repeat(np.asarray(POOL_WINDOWS), POOL_GC)
    return jnp.asarray(np.minimum(pos[:, None] + 1, w[None, :]), F32)


def _layer(x3, pos_tabs, ret_state, ret_prev, ret_tabs, ret_step, pool_buf16, pool_cnt, n_tokens, moba_fn, params, *,
           tm, layer):
    b, length, d = x3.shape
    nw, w_in, w_out, gnw, qnw, knw, bd, pw_bd, pbias, pscale = params
    x2d = x3.reshape(b * length, d)
    zr, mq, mk, mv, mg, zp = _in_proj(x2d, nw, w_in, pos_tabs[0], pos_tabs[1], qnw, knw, bd, tm=tm, layer=layer)
    three = lambda a: a.reshape(b, length, a.shape[-1])
    mix_r, states = _retention(three(zr), ret_state, ret_prev, ret_tabs, gnw, chunk=RET_CHUNK, bb=ret_step[0],
                               pairs=ret_step[1], layer=layer)
    mix_m, k_new, v_new = moba_fn(three(mq), three(mk), three(mv), three(mg))
    mix_p, buf_new = _pool(three(zp), pool_buf16, pool_cnt, pw_bd, pbias, pscale,
                           n_tokens=n_tokens, rows=min(length, 128), bb=ret_step[0])
    two = lambda a: a.reshape(b * length, a.shape[-1])
    y = _out_proj(two(mix_r), two(mix_m), two(mix_p), x2d, w_out, tm=tm, layer=layer)
    return y.reshape(b, length, d), states, k_new, v_new, buf_new


def _moba_prompt_wrap(mq3, mk3, mv3, mg3, *, prev_kv):
    return _moba_prompt(mq3, mk3, mv3, mg3, prev_kv)


def _moba_sample_wrap(mq3, mk3, mv3, mg3, *, page_table, cache_k4, cache_v4, layer, n_tokens):
    mix = _moba_sample(page_table, mq3, mk3, mv3, mg3, cache_k4, cache_v4, layer, n_tokens)
    rows = lambda t: t[:, :n_tokens].reshape(t.shape[0], n_tokens, MOBA_HEADS, HEAD_DIM)
    return mix, rows(mk3), rows(mv3)


def kernel(x_prompt, x_sample, cache_k, cache_v, state_ret, state_pool, page_table, norm_w, w_in, w_out, ret_gn_w, q_norm_w, k_norm_w, pool_w, pool_b, pool_scale):
    depth = w_in.shape[0]
    b, seq, d = x_prompt.shape
    db, dec_seq, _ = x_sample.shape
    n_pool, page = cache_k.shape[1], cache_k.shape[2]
    past_len = page_table.shape[1] * page
    assert dec_seq <= SUBLANES and seq % MOBA_BLOCK == 0 and seq % RET_CHUNK == 0 and past_len % MOBA_BLOCK == 0

    p_tabs = _rope_tables(np.arange(seq))
    tm_s = db * SUBLANES
    s_tabs = _rope_tables(np.tile(past_len + np.arange(SUBLANES), db))
    p_ret_tabs = _retention_tables(RET_CHUNK, RET_CHUNK)
    s_ret_tabs = _retention_tables(RET_CHUNK, dec_seq)
    p_cnt = _pool_counts(0, seq)
    s_cnt = _pool_counts(past_len, SUBLANES)
    head_of = np.arange(MOBA_W) // HEAD_DIM
    bd = jnp.asarray((head_of[:, None] == head_of[None, :]) / HEAD_DIM, BF16)
    zero_buf = jnp.zeros((b, POOL_BUF + 1, POOL_W), F32)

    cache_k4 = cache_k.transpose(0, 1, 3, 4, 2).reshape(depth, n_pool, MOBA_W, page)
    cache_v4 = cache_v.transpose(0, 1, 3, 4, 2).reshape(depth, n_pool, MOBA_W, page)
    xp = x_prompt
    xs = jnp.pad(x_sample, ((0, 0), (0, SUBLANES - dec_seq), (0, 0)))

    w_in_bf = w_in.astype(BF16)
    w_out_bf = w_out.astype(BF16)
    outs = [[] for _ in range(4)]
    prev_kv = s_p = s_s = None
    for l in range(depth):
        params = (norm_w[l].reshape(1, d), w_in_bf, w_out_bf,
                  ret_gn_w[l].reshape(RET_W // LANES, 1, LANES),
                  jnp.tile(q_norm_w[l], MOBA_HEADS).reshape(1, MOBA_W),
                  jnp.tile(k_norm_w[l], MOBA_HEADS).reshape(1, MOBA_W),
                  bd, _block_diag(pool_w[l]).astype(BF16),
                  pool_b[l].reshape(1, POOL_W), pool_scale[l].reshape(1, POOL_W))
        xp, s_p, kt, vt, b_p = _layer(xp, p_tabs, None, s_p, p_ret_tabs, (1, 1), zero_buf, p_cnt, seq,
                                      functools.partial(_moba_prompt_wrap, prev_kv=prev_kv), params,
                                      tm=512, layer=l)
        prev_kv = (kt, vt)
        moba_s = functools.partial(_moba_sample_wrap, page_table=page_table, cache_k4=cache_k4,
                                   cache_v4=cache_v4, layer=l, n_tokens=dec_seq)
        xs, s_s, k_s, v_s, b_s = _layer(xs, s_tabs, state_ret, s_s, s_ret_tabs,
                                        (math.gcd(db, 8), RET_W // LANES),
                                        jnp.pad(state_pool[l], ((0, 0), (1, 0), (0, 0))), s_cnt, dec_seq,
                                        moba_s, params, tm=tm_s, layer=l)
        for lst, val in zip(outs, (k_s, v_s, b_p, b_s)):
            lst.append(val)
    rows = lambda t: t.reshape(depth, b, MOBA_HEADS, HEAD_DIM, seq).transpose(0, 1, 4, 2, 3)
    k_s, v_s, b_p, b_s = (jnp.stack(o) for o in outs)
    return xp, xs[:, :dec_seq], rows(prev_kv[0]), rows(prev_kv[1]), k_s, v_s, s_p, s_s, b_p, b_s
```

```python
import functools
import math

import jax
import jax.numpy as jnp
import numpy as np
from jax import lax
from jax.experimental import pallas as pl
from jax.experimental.pallas import tpu as pltpu

F32 = jnp.float32
BF16 = jnp.bfloat16

HEAD_DIM = 64
HALF = HEAD_DIM // 2
LANES = 128
RET_HEADS = 6
MOBA_HEADS = 6
RET_W = RET_HEADS * HEAD_DIM
MOBA_W = MOBA_HEADS * HEAD_DIM
POOL_W = 256
POOL_GC = 64
POOL_WINDOWS = (2, 4, 8, 16)
POOL_BUF = 15
RET_CHUNK = 128
MOBA_BLOCK = 256
MOBA_TOPK = 3
ROPE_THETA = 10000.0
EPS = 1e-6
SUBLANES = 8
NEG_INF = float("-inf")
LOG2_E = 1.4426950408889634

OFF_RQ, OFF_RV, OFF_MQ, OFF_MV, OFF_PU = 0, 2 * RET_W, 4 * RET_W, 4 * RET_W + 2 * MOBA_W, 4 * RET_W + 4 * MOBA_W
D_IN = OFF_PU + 2 * POOL_W

VMEM_LIMIT = 56 * 1024 * 1024


def _silu(x):
    return x / (1.0 + jnp.exp(-x))


def _cparams(sem, flags=None):
    return pltpu.CompilerParams(dimension_semantics=sem, vmem_limit_bytes=VMEM_LIMIT, flags=flags)


def _in_proj_kernel(x_ref, nw_ref, w_ref, cos_ref, sin_ref, qnw_ref, knw_ref, bd_ref,
                    zr_ref, mq_ref, mk_ref, mv_ref, mg_ref, zp_ref, h_scr, z_scr):
    tm = x_ref.shape[0]
    x = x_ref[...]
    ms = jnp.mean(x * x, axis=-1, keepdims=True)
    h_scr[...] = (x * lax.rsqrt(ms + EPS) * nw_ref[...]).astype(BF16)
    cos = cos_ref[...]
    sin = sin_ref[...]
    lane = lax.broadcasted_iota(jnp.int32, (tm, LANES), 1)
    first_half = (lane & HALF) == 0

    def rope(z):
        partner = jnp.where(first_half, pltpu.roll(z, LANES - HALF, 1), pltpu.roll(z, HALF, 1))
        return z * cos + partner * sin

    def proj(off, width):
        return jnp.dot(h_scr[...], w_ref[:, off:off + width], preferred_element_type=F32)

    z_scr[...] = proj(OFF_RQ, 2 * RET_W)
    for g in range(2 * RET_W // LANES):
        out = rope(z_scr[:, g * LANES:(g + 1) * LANES])
        if g >= RET_W // LANES:
            out = out * (HEAD_DIM ** -0.5)
        zr_ref[:, g * LANES:(g + 1) * LANES] = out
    zr_ref[:, OFF_RV:OFF_MQ] = proj(OFF_RV, 2 * RET_W)
    z_scr[...] = proj(OFF_MQ, 2 * MOBA_W)
    for t, (nw, dst) in enumerate(((qnw_ref, mq_ref), (knw_ref, mk_ref))):
        z = z_scr[:, t * MOBA_W:(t + 1) * MOBA_W]
        msq = jnp.dot((z * z).astype(BF16), bd_ref[...], preferred_element_type=F32)
        z_scr[:, t * MOBA_W:(t + 1) * MOBA_W] = z * lax.rsqrt(msq + EPS) * nw[...]
        for g in range(MOBA_W // LANES):
            c0 = t * MOBA_W + g * LANES
            dst[:, g * LANES:(g + 1) * LANES] = rope(z_scr[:, c0:c0 + LANES])
    z_scr[...] = proj(OFF_MV, 2 * MOBA_W)
    mv_ref[...] = z_scr[:, :MOBA_W]
    mg_ref[...] = z_scr[:, MOBA_W:]
    zp_ref[...] = proj(OFF_PU, 2 * POOL_W)


def _in_proj(x2d, nw, w_bf, cos, sin, qnw, knw, bd, *, tm, layer):
    m, d = x2d.shape
    n_pos = cos.shape[0] // tm
    row = lambda i: (i, 0)
    const = lambda i: (0, 0)
    pos = lambda i: (i % n_pos, 0)
    outs = [(2 * RET_W + 2 * RET_W), MOBA_W, MOBA_W, MOBA_W, MOBA_W, 2 * POOL_W]
    return pl.pallas_call(
        _in_proj_kernel,
        grid=(m // tm,),
        in_specs=[
            pl.BlockSpec((tm, d), row),
            pl.BlockSpec((1, d), const),
            pl.BlockSpec((None, d, D_IN), lambda i: (layer, 0, 0)),
            pl.BlockSpec((tm, LANES), pos),
            pl.BlockSpec((tm, LANES), pos),
            pl.BlockSpec((1, MOBA_W), const),
            pl.BlockSpec((1, MOBA_W), const),
            pl.BlockSpec((MOBA_W, MOBA_W), const),
        ],
        out_specs=[pl.BlockSpec((tm, w), row) for w in outs],
        out_shape=[jax.ShapeDtypeStruct((m, w), F32) for w in outs],
        scratch_shapes=[pltpu.VMEM((tm, d), BF16), pltpu.VMEM((tm, 2 * RET_W), F32)],
        compiler_params=_cparams(("arbitrary",)),
        name="in_proj",
    )(x2d, nw, w_bf, cos, sin, qnw, knw, bd)


def _retention_kernel(*refs, chunk, has_state, n_prev):
    q_ref, k_ref, v_ref, g_ref = refs[:4]
    s0_ref = refs[4] if has_state else None
    prev_ref = refs[4 + has_state] if n_prev else None
    dm_ref, rs_ref, kd_ref, gc_ref, gnw_ref, o_ref, sout_ref, oi_scr, kv_scr, sb_scr = refs[4 + has_state + (n_prev > 0):]
    if n_prev:
        sout_ref[0:n_prev] = prev_ref[...]
    bb, length, width = q_ref.shape
    pairs = width // LANES
    rows_in = min(length, chunk)
    n_chunks = max(1, length // chunk)
    lane = lax.broadcasted_iota(jnp.int32, (chunk, LANES), 1)
    head0 = lane < HEAD_DIM
    r = lax.broadcasted_iota(jnp.int32, (LANES, LANES), 0)
    c = lax.broadcasted_iota(jnp.int32, (LANES, LANES), 1)
    same_head = (r < HEAD_DIM) == (c < HEAD_DIM)
    contract_lanes = (((1,), (1,)), ((), ()))

    def head_mean(x):
        m0 = jnp.sum(jnp.where(head0, x, 0.0), axis=1, keepdims=True)
        m1 = jnp.sum(jnp.where(head0, 0.0, x), axis=1, keepdims=True)
        return jnp.where(head0, m0, m1) * (1.0 / HEAD_DIM)

    def load(ref, bi, p, ci):
        a = ref[bi, ci * chunk:ci * chunk + rows_in, p * LANES:(p + 1) * LANES]
        if rows_in < chunk:
            a = jnp.concatenate([a, jnp.zeros((chunk - rows_in, LANES), F32)], axis=0)
        return a

    items = [(bi, p, ci) for bi in range(bb) for p in range(pairs) for ci in range(n_chunks)]

    def a1(bi, p, ci):
        q = load(q_ref, bi, p, ci)
        k = load(k_ref, bi, p, ci)
        kb = k.astype(BF16)
        vb = load(v_ref, bi, p, ci).astype(BF16)
        q0 = jnp.where(head0, q, 0.0).astype(BF16)
        q1 = jnp.where(head0, 0.0, q).astype(BF16)
        in0 = (lax.dot_general(q0, kb, contract_lanes, preferred_element_type=F32) * dm_ref[2 * p]).astype(BF16)
        in1 = (lax.dot_general(q1, kb, contract_lanes, preferred_element_type=F32) * dm_ref[2 * p + 1]).astype(BF16)
        return bi, p, ci, k, vb, in0, in1

    def a2(st):
        bi, p, ci, k, vb, in0, in1 = st
        slot = bi * pairs + p
        oi_scr[slot, ci * chunk:(ci + 1) * chunk, :] = jnp.where(
            head0, jnp.dot(in0, vb, preferred_element_type=F32), jnp.dot(in1, vb, preferred_element_type=F32))
        kdt = (k * kd_ref[p]).T.astype(BF16)
        kv_scr[slot, ci] = jnp.where(same_head, jnp.dot(kdt, vb, preferred_element_type=F32), 0.0)

    st = None
    for item in items + [None]:
        nxt = a1(*item) if item is not None else None
        if st is not None:
            a2(st)
        st = nxt

    zero = jnp.zeros((HEAD_DIM, HEAD_DIM), F32)
    for bi in range(bb):
        for p in range(pairs):
            slot = bi * pairs + p
            if s0_ref is None:
                s = jnp.zeros((LANES, LANES), F32)
            else:
                s = jnp.concatenate([jnp.concatenate([s0_ref[bi, 2 * p], zero], axis=1),
                                     jnp.concatenate([zero, s0_ref[bi, 2 * p + 1]], axis=1)], axis=0)
            for ci in range(n_chunks):
                sb_scr[slot, ci] = s.astype(BF16)
                s = s * gc_ref[p] + kv_scr[slot, ci]
            sout_ref[n_prev, bi, 2 * p] = s[:HEAD_DIM, :HEAD_DIM]
            sout_ref[n_prev, bi, 2 * p + 1] = pltpu.roll(s[HEAD_DIM:], HEAD_DIM, 1)[:, :HEAD_DIM]

    def c1(bi, p, ci):
        slot = bi * pairs + p
        qb = load(q_ref, bi, p, ci).astype(BF16)
        o = (oi_scr[slot, ci * chunk:(ci + 1) * chunk, :]
             + jnp.dot(qb, sb_scr[slot, ci], preferred_element_type=F32) * rs_ref[p])
        return bi, p, ci, o, head_mean(o)

    def c2(st):
        bi, p, ci, o, mu = st
        oc = o - mu
        return bi, p, ci, oc, head_mean(oc * oc)

    def c3(st):
        bi, p, ci, oc, var = st
        on = oc * lax.rsqrt(var + EPS) * gnw_ref[p]
        out = on * _silu(load(g_ref, bi, p, ci))
        o_ref[bi, ci * chunk:ci * chunk + rows_in, p * LANES:(p + 1) * LANES] = out[:rows_in].astype(o_ref.dtype)

    s1 = s2 = None
    for item in items + [None, None]:
        new1 = c1(*item) if item is not None else None
        new2 = c2(s1) if s1 is not None else None
        if s2 is not None:
            c3(s2)
        s1, s2 = new1, new2


def _retention(zr3, state, prev_states, tabs, gnw, *, chunk, bb, pairs, layer):
    b, length, _ = zr3.shape
    n_pairs = RET_W // LANES
    dm, rs, kd, gc = tabs
    w = pairs * LANES
    n_chunks = max(1, length // chunk)
    n_prev = 0 if prev_states is None else prev_states.shape[0]
    col = lambda off: (lambda bi, p: (bi, 0, off + p))
    tab = lambda bi, p: (p, 0, 0)
    heads_blk = (bb, 2 * pairs, HEAD_DIM, HEAD_DIM)
    stacked = lambda n: pl.BlockSpec((n,) + heads_blk, lambda bi, p: (0, bi, p, 0, 0))
    extra_specs, extra_args = [], []
    if state is not None:
        extra_specs.append(pl.BlockSpec((None,) + heads_blk, lambda bi, p: (layer, bi, p, 0, 0)))
        extra_args.append(state)
    if n_prev:
        extra_specs.append(stacked(n_prev))
        extra_args.append(prev_states)
    return pl.pallas_call(
        functools.partial(_retention_kernel, chunk=chunk, has_state=state is not None, n_prev=n_prev),
        grid=(b // bb, n_pairs // pairs),
        in_specs=[
            pl.BlockSpec((bb, length, w), col(0)),
            pl.BlockSpec((bb, length, w), col(n_pairs // pairs)),
            pl.BlockSpec((bb, length, w), col(2 * n_pairs // pairs)),
            pl.BlockSpec((bb, length, w), col(3 * n_pairs // pairs)),
        ] + extra_specs + [
            pl.BlockSpec((2 * pairs, chunk, chunk), tab),
            pl.BlockSpec((pairs, chunk, LANES), tab),
            pl.BlockSpec((pairs, chunk, LANES), tab),
            pl.BlockSpec((pairs, 1, LANES), tab),
            pl.BlockSpec((pairs, 1, LANES), tab),
        ],
        out_specs=[pl.BlockSpec((bb, length, w), lambda bi, p: (bi, 0, p)), stacked(n_prev + 1)],
        out_shape=[
            jax.ShapeDtypeStruct((b, length, RET_W), BF16),
            jax.ShapeDtypeStruct((n_prev + 1, b, RET_HEADS, HEAD_DIM, HEAD_DIM), F32),
        ],
        scratch_shapes=[
            pltpu.VMEM((bb * pairs, n_chunks * chunk, LANES), F32),
            pltpu.VMEM((bb * pairs, n_chunks, LANES, LANES), F32),
            pltpu.VMEM((bb * pairs, n_chunks, LANES, LANES), BF16),
        ],
        compiler_params=_cparams(("arbitrary", "arbitrary")),
        name="retention",
    )(zr3, zr3, zr3, zr3, *extra_args, dm, rs, kd, gc, gnw)


def _retention_tables(chunk, n_tokens):
    lg = np.log(1.0 - 2.0 ** (-5.0 - np.arange(RET_HEADS, dtype=np.float64)))
    i = np.arange(chunk, dtype=np.float64)
    rel = i[:, None] - i[None, :]
    dm = np.where(rel[None] >= 0, np.exp(rel[None] * lg[:, None, None]), 0.0)
    lg_lanes = np.repeat(lg, HEAD_DIM).reshape(RET_W // LANES, 1, LANES)
    rs = np.exp((i + 1.0)[None, :, None] * lg_lanes)
    kd = np.where((i < n_tokens)[None, :, None], np.exp((n_tokens - 1.0 - i)[None, :, None] * lg_lanes), 0.0)
    gc = np.exp(float(n_tokens) * lg_lanes)
    return tuple(jnp.asarray(t, F32) for t in (dm, rs, kd, gc))


def _moba_prompt_kernel(*refs, n_blocks, n_prev):
    q_ref, k_ref, v_ref, g_ref = refs[:4]
    prev = refs[4:6] if n_prev else ()
    o_ref, kt_ref, vt_ref, kh_scr, vt_scr, km_scr, s_scr, p_scr = refs[4 + len(prev):]
    blk = MOBA_BLOCK
    sub = blk
    contract_lanes = (((1,), (1,)), ((), ()))
    lane = lax.broadcasted_iota(jnp.int32, (blk, LANES), 1)
    head0 = lane < HEAD_DIM
    for j in range(n_blocks):
        kj = k_ref[0, j * blk:(j + 1) * blk, :]
        kh_scr[0, j * blk:(j + 1) * blk, :] = jnp.where(head0, kj, 0.0).astype(BF16)
        kh_scr[1, j * blk:(j + 1) * blk, :] = jnp.where(head0, 0.0, kj).astype(BF16)
        vt = v_ref[0, j * blk:(j + 1) * blk, :].T
        vt_scr[:, j * blk:(j + 1) * blk] = vt.astype(BF16)
        km_scr[j:j + 1, :] = jnp.sum(kj, axis=0, keepdims=True) * (1.0 / blk)
        kt_ref[n_prev, 0, :, j * blk:(j + 1) * blk] = kj.T
        vt_ref[n_prev, 0, :, j * blk:(j + 1) * blk] = vt
    if n_prev:
        kt_ref[0:n_prev] = prev[0][...]
        vt_ref[0:n_prev] = prev[1][...]

    km = km_scr[...]
    lane8 = lax.broadcasted_iota(jnp.int32, (n_blocks, LANES), 1)
    row8 = lax.broadcasted_iota(jnp.int32, (n_blocks, blk), 0)
    krow = lax.broadcasted_iota(jnp.int32, (blk, blk), 0)
    qcol = lax.broadcasted_iota(jnp.int32, (blk, blk), 1)
    causal = jnp.where(krow <= qcol, 0.0, NEG_INF)
    fold = lambda t: t.reshape(t.shape[0] // SUBLANES, SUBLANES, blk)
    km_parts = []
    for h in range(2):
        kmh = jnp.where((lane8 < HEAD_DIM) == (h == 0), km, 0.0)
        kmh_hi = kmh.astype(BF16)
        km_parts.append((kmh_hi, (kmh - kmh_hi.astype(F32)).astype(BF16)))

    heads = range(2)
    tiles = lambda i: [(j, u, h) for j in range(i + 1) for u in range(blk // sub) for h in heads]

    def begin(i):
        q = q_ref[0, i * blk:(i + 1) * blk, :] * (HEAD_DIM ** -0.5 * LOG2_E)
        qh = q.astype(BF16)
        selmask = [None, None]
        if i > MOBA_TOPK:
            ql = (q - qh.astype(F32)).astype(BF16)
            for h in heads:
                kmh_hi, kmh_lo = km_parts[h]
                gate = (lax.dot_general(kmh_hi, qh, contract_lanes, preferred_element_type=F32)
                        + lax.dot_general(kmh_lo, qh, contract_lanes, preferred_element_type=F32)
                        + lax.dot_general(kmh_hi, ql, contract_lanes, preferred_element_type=F32))
                rank = jnp.zeros((n_blocks, blk), jnp.int32)
                for jp in range(i):
                    gj = gate[jp:jp + 1, :]
                    rank = rank + jnp.where((gj > gate) | ((gj == gate) & (jp < row8)), 1, 0)
                selmask[h] = jnp.where((rank < MOBA_TOPK) & (row8 < i), 0.0, NEG_INF)
        return dict(i=i, buf=i % 2, qh=qh, selmask=selmask, m8=[None, None], m=None,
                    l8=[jnp.zeros((SUBLANES, blk), F32) for _ in heads])

    def score_tile(st, j, u, h):
        i = st["i"]
        rows = slice(j * blk + u * sub, j * blk + (u + 1) * sub)
        s = lax.dot_general(kh_scr[h, rows, :], st["qh"], contract_lanes, preferred_element_type=F32)
        if j == i:
            s = s + causal[u * sub:(u + 1) * sub]
        s_scr[st["buf"], h, rows, :] = s
        tmax = jnp.max(fold(s), axis=0)
        if j < i and st["selmask"][h] is not None:
            tmax = tmax + st["selmask"][h][j:j + 1, :]
        st["m8"][h] = tmax if st["m8"][h] is None else jnp.maximum(st["m8"][h], tmax)

    def prob_tile(st, j, u, h):
        i = st["i"]
        if st["m"] is None:
            st["m"] = [jnp.max(st["m8"][hh], axis=0, keepdims=True) for hh in heads]
        rows = slice(j * blk + u * sub, j * blk + (u + 1) * sub)
        sm = st["selmask"][h]
        shift = -st["m"][h] if (j == i or sm is None) else sm[j:j + 1, :] - st["m"][h]
        p = jnp.exp2(s_scr[st["buf"], h, rows, :] + shift)
        st["l8"][h] = st["l8"][h] + jnp.sum(fold(p), axis=0)
        p_scr[st["buf"], h, rows, :] = p.astype(BF16)

    def finish(st):
        i = st["i"]
        nk = (i + 1) * blk
        halves = []
        for h in heads:
            l = jnp.sum(st["l8"][h], axis=0, keepdims=True)
            ot = jnp.dot(vt_scr[h * HEAD_DIM:(h + 1) * HEAD_DIM, 0:nk], p_scr[st["buf"], h, 0:nk, :],
                         preferred_element_type=F32)
            halves.append(ot / l)
        ot = jnp.concatenate(halves, axis=0)
        o_ref[0, i * blk:(i + 1) * blk, :] = (ot.T * _silu(g_ref[0, i * blk:(i + 1) * blk, :])).astype(o_ref.dtype)

    cur = begin(0)
    for t in tiles(0):
        score_tile(cur, *t)
    prev = None
    for i in range(n_blocks):
        nxt = begin(i + 1) if i + 1 < n_blocks else None
        a, b = tiles(i), (tiles(i + 1) if nxt is not None else [])
        for k in range(max(len(a), len(b))):
            if k == 2 and prev is not None:
                finish(prev)
                prev = None
            if k < len(b):
                score_tile(nxt, *b[k])
            if k < len(a):
                prob_tile(cur, *a[k])
        if prev is not None:
            finish(prev)
        prev, cur = cur, nxt
    finish(prev)


def _moba_prompt(mq3, mk3, mv3, mg3, prev_kv):
    b, length, _ = mq3.shape
    n_pairs = MOBA_W // LANES
    n_blocks = length // MOBA_BLOCK
    n_prev = 0 if prev_kv is None else prev_kv[0].shape[0]
    rows = pl.BlockSpec((1, length, LANES), lambda bi, p: (bi, 0, p))
    cols = lambda n: pl.BlockSpec((n, 1, LANES, length), lambda bi, p: (0, bi, p, 0))
    stacked = jax.ShapeDtypeStruct((n_prev + 1, b, MOBA_W, length), F32)
    return pl.pallas_call(
        functools.partial(_moba_prompt_kernel, n_blocks=n_blocks, n_prev=n_prev),
        grid=(b, n_pairs),
        in_specs=[rows, rows, rows, rows] + ([cols(n_prev), cols(n_prev)] if n_prev else []),
        out_specs=[rows, cols(n_prev + 1), cols(n_prev + 1)],
        out_shape=[jax.ShapeDtypeStruct((b, length, MOBA_W), BF16), stacked, stacked],
        scratch_shapes=[
            pltpu.VMEM((2, length, LANES), BF16),
            pltpu.VMEM((LANES, length), BF16),
            pltpu.VMEM((n_blocks, LANES), F32),
            pltpu.VMEM((2, 2, length, MOBA_BLOCK), F32),
            pltpu.VMEM((2, 2, length, MOBA_BLOCK), BF16),
        ],
        compiler_params=_cparams(("arbitrary", "arbitrary")),
        name="moba_prompt",
    )(mq3, mk3, mv3, mg3, *(prev_kv or ()))


def _moba_sample_kernel(pt_ref, q_ref, kn_ref, vn_ref, g_ref, ck_ref, cv_ref, o_ref,
                        kring, ksem, vsel, vsem, idx_v, idx_s, isem, qs_scr, s_scr, p_scr, pc_scr, *,
                        layer, n_pages, page, n_tokens):
    b = pl.program_id(0)
    last = pl.num_programs(0) - 1
    n_blocks = n_pages * page // MOBA_BLOCK
    ppb = MOBA_BLOCK // page
    slots = n_tokens * min(MOBA_TOPK, n_blocks)
    group = math.gcd(n_pages, 16)
    rows = MOBA_HEADS * SUBLANES
    contract_lanes = (((1,), (1,)), ((), ()))
    lane = lax.broadcasted_iota(jnp.int32, (SUBLANES, MOBA_W), 1)

    def k_copy(bi, pg):
        buf = bi % 2
        return pltpu.make_async_copy(ck_ref.at[layer, pt_ref[bi, pg]], kring.at[buf, pg], ksem.at[buf, pg])

    def v_copy(h, slot, u, blk):
        src = cv_ref.at[layer, pt_ref[b, blk * ppb + u], pl.ds(h * HEAD_DIM, HEAD_DIM), :]
        dst = vsel.at[h, slot * ppb + u]
        return pltpu.make_async_copy(src, dst, vsem.at[0])

    @pl.when(b == 0)
    def _():
        for pg in range(n_pages):
            k_copy(0, pg).start()

    @pl.when((b == 0) & (last >= 1))
    def _():
        for pg in range(n_pages):
            k_copy(1, pg).start()

    q = q_ref[0] * (HEAD_DIM ** -0.5 * LOG2_E)
    for h in range(MOBA_HEADS):
        qs_scr[h * SUBLANES:(h + 1) * SUBLANES, :] = jnp.where(lane // HEAD_DIM == h, q, 0.0)
    qb = qs_scr[...].astype(BF16)

    for g0 in range(0, n_pages, group):
        for pg in range(g0, g0 + group):
            k_copy(b, pg).wait()
        for pg in range(g0, g0 + group):
            s_scr[:, pg * page:(pg + 1) * page] = jnp.dot(qb, kring[b % 2, pg].astype(BF16),
                                                          preferred_element_type=F32)

    @pl.when(b + 2 <= last)
    def _():
        for pg in range(n_pages):
            k_copy(b + 2, pg).start()

    col = lax.broadcasted_iota(jnp.int32, (rows, LANES), 1)
    gate = jnp.zeros((rows, LANES), F32)
    for n in range(n_blocks):
        c0 = n * MOBA_BLOCK
        bsum = s_scr[:, c0:c0 + page]
        for u in range(1, MOBA_BLOCK // page):
            bsum = bsum + s_scr[:, c0 + u * page:c0 + (u + 1) * page]
        gate = jnp.where(col == n, jnp.sum(bsum, axis=1, keepdims=True), gate)
    rank = jnp.zeros(gate.shape, jnp.int32)
    for jp in range(n_blocks):
        gj = gate[:, jp:jp + 1]
        rank = rank + jnp.where((gj > gate) | ((gj == gate) & (jp < col)), 1, 0)
    sel = jnp.where(rank < min(MOBA_TOPK, n_blocks), 0.0, NEG_INF)
    own = jnp.concatenate([kn_ref[0], jnp.zeros((page - SUBLANES, MOBA_W), F32)], axis=0).astype(BF16)
    t_idx = lax.broadcasted_iota(jnp.int32, (rows, page), 0) % SUBLANES
    key_idx = lax.broadcasted_iota(jnp.int32, (rows, page), 1)
    s_own = lax.dot_general(qb, own, contract_lanes, preferred_element_type=F32)
    s_own = s_own + jnp.where(key_idx <= t_idx, 0.0, NEG_INF)
    s_scr[:, n_pages * page:] = s_own
    m_part = s_own
    for n in range(n_blocks):
        add = sel[:, n:n + 1]
        for u in range(MOBA_BLOCK // page):
            c0 = n * MOBA_BLOCK + u * page
            sm = s_scr[:, c0:c0 + page] + add
            s_scr[:, c0:c0 + page] = sm
            m_part = jnp.maximum(m_part, sm)
    m = jnp.max(m_part, axis=1, keepdims=True)
    l_part = jnp.zeros((rows, page), F32)
    for n in range(n_pages + 1):
        c0 = n * page
        p = jnp.exp2(s_scr[:, c0:c0 + page] - m)
        l_part = l_part + p
        p_scr[:, c0:c0 + page] = p
    l = jnp.sum(l_part, axis=1, keepdims=True)

    row8 = lax.broadcasted_iota(jnp.int32, (SUBLANES, LANES), 0)
    col8 = lax.broadcasted_iota(jnp.int32, (SUBLANES, LANES), 1)
    sel_real = jnp.where(t_idx < n_tokens, sel, NEG_INF)
    chosen = jnp.full((SUBLANES, LANES), NEG_INF, F32)
    for h in range(MOBA_HEADS):
        any_row = jnp.max(sel_real[h * SUBLANES:(h + 1) * SUBLANES], axis=0, keepdims=True)
        chosen = jnp.where(row8 == h, any_row, chosen)
    chosen = chosen == 0.0
    before = (lax.broadcasted_iota(jnp.int32, (LANES, LANES), 0)
              < lax.broadcasted_iota(jnp.int32, (LANES, LANES), 1))
    pos = jnp.dot(jnp.where(chosen, 1.0, 0.0).astype(BF16), jnp.where(before, 1.0, 0.0).astype(BF16),
                  preferred_element_type=F32)
    count = jnp.sum(jnp.where(chosen, 1.0, 0.0), axis=1, keepdims=True)
    blk_id = col8.astype(F32)
    pick = lambda j: jnp.sum(jnp.where(chosen & (pos == j), blk_id, 0.0), axis=1, keepdims=True)
    first = pick(0)
    table = jnp.where(col8 == slots, count, 0.0)
    for j in range(slots):
        table = jnp.where(col8 == j, jnp.where(count > j, pick(j), first), table)
    idx_v[...] = table.astype(jnp.int32)
    to_smem = pltpu.make_async_copy(idx_v, idx_s, isem.at[0])
    to_smem.start()
    to_smem.wait()

    for h in range(MOBA_HEADS):
        for j in range(slots):
            for u in range(ppb):
                v_copy(h, j, u, idx_s[h, j]).start(priority=1)
    for h in range(MOBA_HEADS):
        for j in range(slots):
            weight = jnp.where(j < idx_s[h, slots], 1.0, 0.0)
            for u in range(ppb):
                c0 = pl.multiple_of((idx_s[h, j] * ppb + u) * page, page)
                pc_scr[h, j * ppb + u] = p_scr[h * SUBLANES:(h + 1) * SUBLANES, pl.ds(c0, page)] * weight
    vown = jnp.concatenate([vn_ref[0], jnp.zeros((page - SUBLANES, MOBA_W), F32)], axis=0).astype(BF16)
    o_own = jnp.dot(p_scr[:, n_pages * page:].astype(BF16), vown, preferred_element_type=F32)
    out = jnp.zeros((SUBLANES, MOBA_W), F32)
    for h in range(MOBA_HEADS):
        out = out + jnp.where(lane // HEAD_DIM == h, o_own[h * SUBLANES:(h + 1) * SUBLANES, :], 0.0)

    for h in range(MOBA_HEADS):
        for j in range(slots):
            for u in range(ppb):
                v_copy(h, j, u, 0).wait()
    o_sel = []
    for h in range(MOBA_HEADS):
        parts = [jnp.zeros((SUBLANES, HEAD_DIM), F32) for _ in range(4)]
        for piece in range(slots * ppb):
            parts[piece % 4] = parts[piece % 4] + lax.dot_general(
                pc_scr[h, piece].astype(BF16), vsel[h, piece].astype(BF16), contract_lanes,
                preferred_element_type=F32)
        o_sel.append((parts[0] + parts[1]) + (parts[2] + parts[3]))
    out = out + jnp.concatenate(o_sel, axis=1)

    l_rows = jnp.zeros((SUBLANES, MOBA_W), F32)
    for h in range(MOBA_HEADS):
        l_rows = jnp.where(lane // HEAD_DIM == h, l[h * SUBLANES:(h + 1) * SUBLANES, :], l_rows)
    o_ref[0] = (out / l_rows * _silu(g_ref[0])).astype(o_ref.dtype)


def _moba_sample(page_table, mq3, mk3, mv3, mg3, cache_k4, cache_v4, layer, n_tokens):
    db = mq3.shape[0]
    n_pages = page_table.shape[1]
    page = cache_k4.shape[3]
    rows = MOBA_HEADS * SUBLANES
    n_blocks = n_pages * page // MOBA_BLOCK
    slots = n_tokens * min(MOBA_TOPK, n_blocks)
    assert n_blocks <= LANES and page == LANES and slots < LANES and MOBA_HEADS <= SUBLANES
    small = pl.BlockSpec((1, SUBLANES, MOBA_W), lambda bi, pt: (bi, 0, 0))
    hbm = pl.BlockSpec(memory_space=pl.ANY)
    width = (n_pages + 1) * page
    return pl.pallas_call(
        functools.partial(_moba_sample_kernel, layer=layer, n_pages=n_pages, page=page, n_tokens=n_tokens),
        grid_spec=pltpu.PrefetchScalarGridSpec(
            num_scalar_prefetch=1,
            grid=(db,),
            in_specs=[small, small, small, small, hbm, hbm],
            out_specs=small,
            scratch_shapes=[
                pltpu.VMEM((2, n_pages, MOBA_W, page), F32),
                pltpu.SemaphoreType.DMA((2, n_pages)),
                pltpu.VMEM((MOBA_HEADS, slots * MOBA_BLOCK // page, HEAD_DIM, page), F32),
                pltpu.SemaphoreType.DMA((1,)),
                pltpu.VMEM((SUBLANES, LANES), jnp.int32),
                pltpu.SMEM((SUBLANES, LANES), jnp.int32),
                pltpu.SemaphoreType.DMA((1,)),
                pltpu.VMEM((rows, MOBA_W), F32),
                pltpu.VMEM((rows, width), F32),
                pltpu.VMEM((rows, width), F32),
                pltpu.VMEM((MOBA_HEADS, slots * MOBA_BLOCK // page, SUBLANES, page), F32),
            ],
        ),
        out_shape=jax.ShapeDtypeStruct((db, SUBLANES, MOBA_W), BF16),
        compiler_params=_cparams(("arbitrary",)),
        name="moba_sample",
    )(page_table, mq3, mk3, mv3, mg3, cache_k4, cache_v4)


def _pool_kernel(u_ref, g_ref, buf_ref, cnt_ref, w_ref, b_ref, sc_ref, o_ref, bufo_ref, x_scr, w_scr, *,
                 n_tokens, rows):
    bb, length, _ = u_ref.shape
    lead = SUBLANES
    halo = lead + POOL_BUF + 1
    assert POOL_WINDOWS == (2, 4, 8, 16) and halo % SUBLANES == 0
    tiles = [(lead, halo - lead)] + [(halo + r0, rows) for r0 in range(0, length, rows)]
    for bi in range(bb):
        x_scr[bi, 0:lead, :] = jnp.zeros((lead, POOL_W), F32)
        x_scr[bi, lead:halo, :] = buf_ref[bi]
        x_scr[bi, halo:, :] = u_ref[bi]
        w_scr[bi, :, 0:lead, :] = jnp.zeros((3, lead, POOL_W), F32)
        for i0, n in tiles:
            group = lax.broadcasted_iota(jnp.int32, (n, POOL_W), 1) // POOL_GC
            u = x_scr[bi, i0:i0 + n, :]
            w2 = u + x_scr[bi, i0 - 1:i0 - 1 + n, :]
            w_scr[bi, 0, i0:i0 + n, :] = w2
            w4 = w2 + w_scr[bi, 0, i0 - 2:i0 - 2 + n, :]
            w_scr[bi, 1, i0:i0 + n, :] = w4
            w8 = w4 + w_scr[bi, 1, i0 - 4:i0 - 4 + n, :]
            w_scr[bi, 2, i0:i0 + n, :] = w8
            if i0 < halo:
                continue
            w16 = w8 + w_scr[bi, 2, i0 - 8:i0 - 8 + n, :]
            r0 = i0 - halo
            win = jnp.where(group == 0, w2, jnp.where(group == 1, w4, jnp.where(group == 2, w8, w16)))
            pooled = win / cnt_ref[r0:r0 + n, :] - u
            y = jnp.dot(pooled.astype(BF16), w_ref[...], preferred_element_type=F32) + b_ref[...]
            y = y * sc_ref[...]
            o_ref[bi, r0:r0 + n, :] = (y * _silu(g_ref[bi, r0:r0 + n, :])).astype(o_ref.dtype)
        bufo_ref[bi] = x_scr[bi, lead + n_tokens + 1:lead + n_tokens + 1 + POOL_BUF, :]


def _pool(zp3, buf16, cnt, w_bd, bias, scale, *, n_tokens, rows, bb):
    b, length, _ = zp3.shape
    const = lambda bi: (0, 0)
    n_rows = SUBLANES + POOL_BUF + 1 + length
    return pl.pallas_call(
        functools.partial(_pool_kernel, n_tokens=n_tokens, rows=rows),
        grid=(b // bb,),
        in_specs=[
            pl.BlockSpec((bb, length, POOL_W), lambda bi: (bi, 0, 0)),
            pl.BlockSpec((bb, length, POOL_W), lambda bi: (bi, 0, 1)),
            pl.BlockSpec((bb, POOL_BUF + 1, POOL_W), lambda bi: (bi, 0, 0)),
            pl.BlockSpec((length, POOL_W), const),
            pl.BlockSpec((POOL_W, POOL_W), const),
            pl.BlockSpec((1, POOL_W), const),
            pl.BlockSpec((1, POOL_W), const),
        ],
        out_specs=[
            pl.BlockSpec((bb, length, POOL_W), lambda bi: (bi, 0, 0)),
            pl.BlockSpec((bb, POOL_BUF, POOL_W), lambda bi: (bi, 0, 0)),
        ],
        out_shape=[
            jax.ShapeDtypeStruct((b, length, POOL_W), BF16),
            jax.ShapeDtypeStruct((b, POOL_BUF, POOL_W), F32),
        ],
        scratch_shapes=[pltpu.VMEM((bb, n_rows, POOL_W), F32), pltpu.VMEM((bb, 3, n_rows, POOL_W), F32)],
        compiler_params=_cparams(("arbitrary",)),
        name="pool",
    )(zp3, zp3, buf16, cnt, w_bd, bias, scale)


def _out_proj_kernel(mr_ref, mm_ref, mp_ref, x_ref, w_ref, y_ref, mix_scr):
    mix_scr[:, :RET_W] = mr_ref[...]
    mix_scr[:, RET_W:RET_W + MOBA_W] = mm_ref[...]
    mix_scr[:, RET_W + MOBA_W:] = mp_ref[...]
    y_ref[...] = x_ref[...] + jnp.dot(mix_scr[...], w_ref[...], preferred_element_type=F32)


def _out_proj(mr, mm, mp, x2d, w_bf, *, tm, layer):
    m, d = x2d.shape
    d_mix = RET_W + MOBA_W + POOL_W
    row = lambda i: (i, 0)
    return pl.pallas_call(
        _out_proj_kernel,
        grid=(m // tm,),
        in_specs=[
            pl.BlockSpec((tm, RET_W), row),
            pl.BlockSpec((tm, MOBA_W), row),
            pl.BlockSpec((tm, POOL_W), row),
            pl.BlockSpec((tm, d), row),
            pl.BlockSpec((None, d_mix, d), lambda i: (layer, 0, 0)),
        ],
        out_specs=pl.BlockSpec((tm, d), row),
        out_shape=jax.ShapeDtypeStruct((m, d), F32),
        scratch_shapes=[pltpu.VMEM((tm, d_mix), BF16)],
        compiler_params=_cparams(("arbitrary",)),
        name="out_proj",
    )(mr, mm, mp, x2d, w_bf)


def _rope_tables(pos):
    inv = 1.0 / (ROPE_THETA ** (np.arange(HALF, dtype=np.float64) / HALF))
    ang = np.asarray(pos, np.float64)[:, None] * inv[None, :]
    c, s = np.cos(ang), np.sin(ang)
    return (jnp.asarray(np.concatenate([c, c, c, c], axis=-1), F32),
            jnp.asarray(np.concatenate([-s, s, -s, s], axis=-1), F32))


def _block_diag(blocks):
    g, n, _ = blocks.shape
    eye = jnp.eye(g, dtype=blocks.dtype)
    return (eye[:, None, :, None] * blocks[:, :, None, :]).reshape(g * n, g * n)


def _pool_counts(pos0, length):
    pos = pos0 + np.arange(length)
    w = np.repeat(np.asarray(POOL_WINDOWS), POOL_GC)
    return jnp.asarray(np.minimum(pos[:, None] + 1, w[None, :]), F32)


def _layer(x3, pos_tabs, ret_state, ret_prev, ret_tabs, ret_step, pool_buf16, pool_cnt, n_tokens, moba_fn, params, *,
           tm, layer):
    b, length, d = x3.shape
    nw, w_in, w_out, gnw, qnw, knw, bd, pw_bd, pbias, pscale = params
    x2d = x3.reshape(b * length, d)
    zr, mq, mk, mv, mg, zp = _in_proj(x2d, nw, w_in, pos_tabs[0], pos_tabs[1], qnw, knw, bd, tm=tm, layer=layer)
    three = lambda a: a.reshape(b, length, a.shape[-1])
    mix_r, states = _retention(three(zr), ret_state, ret_prev, ret_tabs, gnw, chunk=RET_CHUNK, bb=ret_step[0],
                               pairs=ret_step[1], layer=layer)
    mix_m, k_new, v_new = moba_fn(three(mq), three(mk), three(mv), three(mg))
    mix_p, buf_new = _pool(three(zp), pool_buf16, pool_cnt, pw_bd, pbias, pscale,
                           n_tokens=n_tokens, rows=min(length, 128), bb=ret_step[0])
    two = lambda a: a.reshape(b * length, a.shape[-1])
    y = _out_proj(two(mix_r), two(mix_m), two(mix_p), x2d, w_out, tm=tm, layer=layer)
    return y.reshape(b, length, d), states, k_new, v_new, buf_new


def _moba_prompt_wrap(mq3, mk3, mv3, mg3, *, prev_kv):
    return _moba_prompt(mq3, mk3, mv3, mg3, prev_kv)


def _moba_sample_wrap(mq3, mk3, mv3, mg3, *, page_table, cache_k4, cache_v4, layer, n_tokens):
    mix = _moba_sample(page_table, mq3, mk3, mv3, mg3, cache_k4, cache_v4, layer, n_tokens)
    rows = lambda t: t[:, :n_tokens].reshape(t.shape[0], n_tokens, MOBA_HEADS, HEAD_DIM)
    return mix, rows(mk3), rows(mv3)


def kernel(x_prompt, x_sample, cache_k, cache_v, state_ret, state_pool, page_table, norm_w, w_in, w_out, ret_gn_w, q_norm_w, k_norm_w, pool_w, pool_b, pool_scale):
    depth = w_in.shape[0]
    b, seq, d = x_prompt.shape
    db, dec_seq, _ = x_sample.shape
    n_pool, page = cache_k.shape[1], cache_k.shape[2]
    past_len = page_table.shape[1] * page
    assert dec_seq <= SUBLANES and seq % MOBA_BLOCK == 0 and seq % RET_CHUNK == 0 and past_len % MOBA_BLOCK == 0

    p_tabs = _rope_tables(np.arange(seq))
    tm_s = db * SUBLANES
    s_tabs = _rope_tables(np.tile(past_len + np.arange(SUBLANES), db))
    p_ret_tabs = _retention_tables(RET_CHUNK, RET_CHUNK)
    s_ret_tabs = _retention_tables(RET_CHUNK, dec_seq)
    p_cnt = _pool_counts(0, seq)
    s_cnt = _pool_counts(past_len, SUBLANES)
    head_of = np.arange(MOBA_W) // HEAD_DIM
    bd = jnp.asarray((head_of[:, None] == head_of[None, :]) / HEAD_DIM, BF16)
    zero_buf = jnp.zeros((b, POOL_BUF + 1, POOL_W), F32)

    cache_k4 = cache_k.transpose(0, 1, 3, 4, 2).reshape(depth, n_pool, MOBA_W, page)
    cache_v4 = cache_v.transpose(0, 1, 3, 4, 2).reshape(depth, n_pool, MOBA_W, page)
    xp = x_prompt
    xs = jnp.pad(x_sample, ((0, 0), (0, SUBLANES - dec_seq), (0, 0)))

    w_in_bf = w_in.astype(BF16)
    w_out_bf = w_out.astype(BF16)
    outs = [[] for _ in range(4)]
    prev_kv = s_p = s_s = None
    for l in range(depth):
        params = (norm_w[l].reshape(1, d), w_in_bf, w_out_bf,
                  ret_gn_w[l].reshape(RET_W // LANES, 1, LANES),
                  jnp.tile(q_norm_w[l], MOBA_HEADS).reshape(1, MOBA_W),
                  jnp.tile(k_norm_w[l], MOBA_HEADS).reshape(1, MOBA_W),
                  bd, _block_diag(pool_w[l]).astype(BF16),
                  pool_b[l].reshape(1, POOL_W), pool_scale[l].reshape(1, POOL_W))
        xp, s_p, kt, vt, b_p = _layer(xp, p_tabs, None, s_p, p_ret_tabs, (1, 1), zero_buf, p_cnt, seq,
                                      functools.partial(_moba_prompt_wrap, prev_kv=prev_kv), params,
                                      tm=512, layer=l)
        prev_kv = (kt, vt)
        moba_s = functools.partial(_moba_sample_wrap, page_table=page_table, cache_k4=cache_k4,
                                   cache_v4=cache_v4, layer=l, n_tokens=dec_seq)
        xs, s_s, k_s, v_s, b_s = _layer(xs, s_tabs, state_ret, s_s, s_ret_tabs,
                                        (math.gcd(db, 8), RET_W // LANES),
                                        jnp.pad(state_pool[l], ((0, 0), (1, 0), (0, 0))), s_cnt, dec_seq,
                                        moba_s, params, tm=tm_s, layer=l)
        for lst, val in zip(outs, (k_s, v_s, b_p, b_s)):
            lst.append(val)
    rows = lambda t: t.reshape(depth, b, MOBA_HEADS, HEAD_DIM, seq).transpose(0, 1, 4, 2, 3)
    k_s, v_s, b_p, b_s = (jnp.stack(o) for o in outs)
    return xp, xs[:, :dec_seq], rows(prev_kv[0]), rows(prev_kv[1]), k_s, v_s, s_p, s_s, b_p, b_s
```

```python
import functools
import math

import jax
import jax.numpy as jnp
import numpy as np
from jax import lax
from jax.experimental import pallas as pl
from jax.experimental.pallas import tpu as pltpu

F32 = jnp.float32
BF16 = jnp.bfloat16

HEAD_DIM = 64
HALF = HEAD_DIM // 2
LANES = 128
RET_HEADS = 6
MOBA_HEADS = 6
RET_W = RET_HEADS * HEAD_DIM
MOBA_W = MOBA_HEADS * HEAD_DIM
POOL_W = 256
POOL_GC = 64
POOL_WINDOWS = (2, 4, 8, 16)
POOL_BUF = 15
RET_CHUNK = 128
MOBA_BLOCK = 256
MOBA_TOPK = 3
ROPE_THETA = 10000.0
EPS = 1e-6
SUBLANES = 8
NEG_INF = float("-inf")
LOG2_E = 1.4426950408889634

OFF_RQ, OFF_RV, OFF_MQ, OFF_MV, OFF_PU = 0, 2 * RET_W, 4 * RET_W, 4 * RET_W + 2 * MOBA_W, 4 * RET_W + 4 * MOBA_W
D_IN = OFF_PU + 2 * POOL_W

VMEM_LIMIT = 56 * 1024 * 1024


def _silu(x):
    return x / (1.0 + jnp.exp(-x))


def _cparams(sem):
    return pltpu.CompilerParams(dimension_semantics=sem, vmem_limit_bytes=VMEM_LIMIT)


def _in_proj_kernel(x_ref, nw_ref, w_ref, cos_ref, sin_ref, qnw_ref, knw_ref, bd_ref,
                    zr_ref, mq_ref, mk_ref, mv_ref, mg_ref, zp_ref, h_scr, z_scr):
    tm = x_ref.shape[0]
    x = x_ref[...]
    ms = jnp.mean(x * x, axis=-1, keepdims=True)
    h_scr[...] = (x * lax.rsqrt(ms + EPS) * nw_ref[...]).astype(BF16)
    cos = cos_ref[...]
    sin = sin_ref[...]
    lane = lax.broadcasted_iota(jnp.int32, (tm, LANES), 1)
    first_half = (lane & HALF) == 0

    def rope(z):
        partner = jnp.where(first_half, pltpu.roll(z, LANES - HALF, 1), pltpu.roll(z, HALF, 1))
        return z * cos + partner * sin

    def proj(off, width):
        return jnp.dot(h_scr[...], w_ref[:, off:off + width], preferred_element_type=F32)

    z_scr[...] = proj(OFF_RQ, 2 * RET_W)
    for g in range(2 * RET_W // LANES):
        out = rope(z_scr[:, g * LANES:(g + 1) * LANES])
        if g >= RET_W // LANES:
            out = out * (HEAD_DIM ** -0.5)
        zr_ref[:, g * LANES:(g + 1) * LANES] = out
    zr_ref[:, OFF_RV:OFF_MQ] = proj(OFF_RV, 2 * RET_W)
    z_scr[...] = proj(OFF_MQ, 2 * MOBA_W)
    for t, (nw, dst) in enumerate(((qnw_ref, mq_ref), (knw_ref, mk_ref))):
        z = z_scr[:, t * MOBA_W:(t + 1) * MOBA_W]
        msq = jnp.dot((z * z).astype(BF16), bd_ref[...], preferred_element_type=F32)
        z_scr[:, t * MOBA_W:(t + 1) * MOBA_W] = z * lax.rsqrt(msq + EPS) * nw[...]
        for g in range(MOBA_W // LANES):
            c0 = t * MOBA_W + g * LANES
            dst[:, g * LANES:(g + 1) * LANES] = rope(z_scr[:, c0:c0 + LANES])
    z_scr[...] = proj(OFF_MV, 2 * MOBA_W)
    mv_ref[...] = z_scr[:, :MOBA_W]
    mg_ref[...] = z_scr[:, MOBA_W:]
    zp_ref[...] = proj(OFF_PU, 2 * POOL_W)


def _in_proj(x2d, nw, w_bf, cos, sin, qnw, knw, bd, *, tm, layer):
    m, d = x2d.shape
    n_pos = cos.shape[0] // tm
    row = lambda i: (i, 0)
    const = lambda i: (0, 0)
    pos = lambda i: (i % n_pos, 0)
    outs = [(2 * RET_W + 2 * RET_W), MOBA_W, MOBA_W, MOBA_W, MOBA_W, 2 * POOL_W]
    return pl.pallas_call(
        _in_proj_kernel,
        grid=(m // tm,),
        in_specs=[
            pl.BlockSpec((tm, d), row),
            pl.BlockSpec((1, d), const),
            pl.BlockSpec((None, d, D_IN), lambda i: (layer, 0, 0)),
            pl.BlockSpec((tm, LANES), pos),
            pl.BlockSpec((tm, LANES), pos),
            pl.BlockSpec((1, MOBA_W), const),
            pl.BlockSpec((1, MOBA_W), const),
            pl.BlockSpec((MOBA_W, MOBA_W), const),
        ],
        out_specs=[pl.BlockSpec((tm, w), row) for w in outs],
        out_shape=[jax.ShapeDtypeStruct((m, w), F32) for w in outs],
        scratch_shapes=[pltpu.VMEM((tm, d), BF16), pltpu.VMEM((tm, 2 * RET_W), F32)],
        compiler_params=_cparams(("arbitrary",)),
        name="in_proj",
    )(x2d, nw, w_bf, cos, sin, qnw, knw, bd)


def _retention_kernel(*refs, chunk, has_state, n_prev):
    q_ref, k_ref, v_ref, g_ref = refs[:4]
    s0_ref = refs[4] if has_state else None
    prev_ref = refs[4 + has_state] if n_prev else None
    dm_ref, rs_ref, kd_ref, gc_ref, gnw_ref, o_ref, sout_ref, oi_scr, kv_scr, sb_scr = refs[4 + has_state + (n_prev > 0):]
    if n_prev:
        sout_ref[0:n_prev] = prev_ref[...]
    bb, length, width = q_ref.shape
    pairs = width // LANES
    rows_in = min(length, chunk)
    n_chunks = max(1, length // chunk)
    lane = lax.broadcasted_iota(jnp.int32, (chunk, LANES), 1)
    head0 = lane < HEAD_DIM
    r = lax.broadcasted_iota(jnp.int32, (LANES, LANES), 0)
    c = lax.broadcasted_iota(jnp.int32, (LANES, LANES), 1)
    same_head = (r < HEAD_DIM) == (c < HEAD_DIM)
    contract_lanes = (((1,), (1,)), ((), ()))

    def head_mean(x):
        m0 = jnp.sum(jnp.where(head0, x, 0.0), axis=1, keepdims=True)
        m1 = jnp.sum(jnp.where(head0, 0.0, x), axis=1, keepdims=True)
        return jnp.where(head0, m0, m1) * (1.0 / HEAD_DIM)

    def load(ref, bi, p, ci):
        a = ref[bi, ci * chunk:ci * chunk + rows_in, p * LANES:(p + 1) * LANES]
        if rows_in < chunk:
            a = jnp.concatenate([a, jnp.zeros((chunk - rows_in, LANES), F32)], axis=0)
        return a

    items = [(bi, p, ci) for bi in range(bb) for p in range(pairs) for ci in range(n_chunks)]

    def a1(bi, p, ci):
        q = load(q_ref, bi, p, ci)
        k = load(k_ref, bi, p, ci)
        kb = k.astype(BF16)
        vb = load(v_ref, bi, p, ci).astype(BF16)
        q0 = jnp.where(head0, q, 0.0).astype(BF16)
        q1 = jnp.where(head0, 0.0, q).astype(BF16)
        in0 = (lax.dot_general(q0, kb, contract_lanes, preferred_element_type=F32) * dm_ref[2 * p]).astype(BF16)
        in1 = (lax.dot_general(q1, kb, contract_lanes, preferred_element_type=F32) * dm_ref[2 * p + 1]).astype(BF16)
        return bi, p, ci, k, vb, in0, in1

    def a2(st):
        bi, p, ci, k, vb, in0, in1 = st
        slot = bi * pairs + p
        oi_scr[slot, ci * chunk:(ci + 1) * chunk, :] = jnp.where(
            head0, jnp.dot(in0, vb, preferred_element_type=F32), jnp.dot(in1, vb, preferred_element_type=F32))
        kdt = (k * kd_ref[p]).T.astype(BF16)
        kv_scr[slot, ci] = jnp.where(same_head, jnp.dot(kdt, vb, preferred_element_type=F32), 0.0)

    st = None
    for item in items + [None]:
        nxt = a1(*item) if item is not None else None
        if st is not None:
            a2(st)
        st = nxt

    zero = jnp.zeros((HEAD_DIM, HEAD_DIM), F32)
    for bi in range(bb):
        for p in range(pairs):
            slot = bi * pairs + p
            if s0_ref is None:
                s = jnp.zeros((LANES, LANES), F32)
            else:
                s = jnp.concatenate([jnp.concatenate([s0_ref[bi, 2 * p], zero], axis=1),
                                     jnp.concatenate([zero, s0_ref[bi, 2 * p + 1]], axis=1)], axis=0)
            for ci in range(n_chunks):
                sb_scr[slot, ci] = s.astype(BF16)
                s = s * gc_ref[p] + kv_scr[slot, ci]
            sout_ref[n_prev, bi, 2 * p] = s[:HEAD_DIM, :HEAD_DIM]
            sout_ref[n_prev, bi, 2 * p + 1] = pltpu.roll(s[HEAD_DIM:], HEAD_DIM, 1)[:, :HEAD_DIM]

    def c1(bi, p, ci):
        slot = bi * pairs + p
        qb = load(q_ref, bi, p, ci).astype(BF16)
        o = (oi_scr[slot, ci * chunk:(ci + 1) * chunk, :]
             + jnp.dot(qb, sb_scr[slot, ci], preferred_element_type=F32) * rs_ref[p])
        return bi, p, ci, o, head_mean(o)

    def c2(st):
        bi, p, ci, o, mu = st
        oc = o - mu
        return bi, p, ci, oc, head_mean(oc * oc)

    def c3(st):
        bi, p, ci, oc, var = st
        on = oc * lax.rsqrt(var + EPS) * gnw_ref[p]
        out = on * _silu(load(g_ref, bi, p, ci))
        o_ref[bi, ci * chunk:ci * chunk + rows_in, p * LANES:(p + 1) * LANES] = out[:rows_in].astype(o_ref.dtype)

    s1 = s2 = None
    for item in items + [None, None]:
        new1 = c1(*item) if item is not None else None
        new2 = c2(s1) if s1 is not None else None
        if s2 is not None:
            c3(s2)
        s1, s2 = new1, new2


def _retention(zr3, state, prev_states, tabs, gnw, *, chunk, bb, pairs, layer):
    b, length, _ = zr3.shape
    n_pairs = RET_W // LANES
    dm, rs, kd, gc = tabs
    w = pairs * LANES
    n_chunks = max(1, length // chunk)
    n_prev = 0 if prev_states is None else prev_states.shape[0]
    col = lambda off: (lambda bi, p: (bi, 0, off + p))
    tab = lambda bi, p: (p, 0, 0)
    heads_blk = (bb, 2 * pairs, HEAD_DIM, HEAD_DIM)
    stacked = lambda n: pl.BlockSpec((n,) + heads_blk, lambda bi, p: (0, bi, p, 0, 0))
    extra_specs, extra_args = [], []
    if state is not None:
        extra_specs.append(pl.BlockSpec((None,) + heads_blk, lambda bi, p: (layer, bi, p, 0, 0)))
        extra_args.append(state)
    if n_prev:
        extra_specs.append(stacked(n_prev))
        extra_args.append(prev_states)
    return pl.pallas_call(
        functools.partial(_retention_kernel, chunk=chunk, has_state=state is not None, n_prev=n_prev),
        grid=(b // bb, n_pairs // pairs),
        in_specs=[
            pl.BlockSpec((bb, length, w), col(0)),
            pl.BlockSpec((bb, length, w), col(n_pairs // pairs)),
            pl.BlockSpec((bb, length, w), col(2 * n_pairs // pairs)),
            pl.BlockSpec((bb, length, w), col(3 * n_pairs // pairs)),
        ] + extra_specs + [
            pl.BlockSpec((2 * pairs, chunk, chunk), tab),
            pl.BlockSpec((pairs, chunk, LANES), tab),
            pl.BlockSpec((pairs, chunk, LANES), tab),
            pl.BlockSpec((pairs, 1, LANES), tab),
            pl.BlockSpec((pairs, 1, LANES), tab),
        ],
        out_specs=[pl.BlockSpec((bb, length, w), lambda bi, p: (bi, 0, p)), stacked(n_prev + 1)],
        out_shape=[
            jax.ShapeDtypeStruct((b, length, RET_W), BF16),
            jax.ShapeDtypeStruct((n_prev + 1, b, RET_HEADS, HEAD_DIM, HEAD_DIM), F32),
        ],
        scratch_shapes=[
            pltpu.VMEM((bb * pairs, n_chunks * chunk, LANES), F32),
            pltpu.VMEM((bb * pairs, n_chunks, LANES, LANES), F32),
            pltpu.VMEM((bb * pairs, n_chunks, LANES, LANES), BF16),
        ],
        compiler_params=_cparams(("arbitrary", "arbitrary")),
        name="retention",
    )(zr3, zr3, zr3, zr3, *extra_args, dm, rs, kd, gc, gnw)


def _retention_tables(chunk, n_tokens):
    lg = np.log(1.0 - 2.0 ** (-5.0 - np.arange(RET_HEADS, dtype=np.float64)))
    i = np.arange(chunk, dtype=np.float64)
    rel = i[:, None] - i[None, :]
    dm = np.where(rel[None] >= 0, np.exp(rel[None] * lg[:, None, None]), 0.0)
    lg_lanes = np.repeat(lg, HEAD_DIM).reshape(RET_W // LANES, 1, LANES)
    rs = np.exp((i + 1.0)[None, :, None] * lg_lanes)
    kd = np.where((i < n_tokens)[None, :, None], np.exp((n_tokens - 1.0 - i)[None, :, None] * lg_lanes), 0.0)
    gc = np.exp(float(n_tokens) * lg_lanes)
    return tuple(jnp.asarray(t, F32) for t in (dm, rs, kd, gc))


def _moba_prompt_kernel(*refs, n_blocks, n_prev):
    q_ref, k_ref, v_ref, g_ref = refs[:4]
    prev = refs[4:6] if n_prev else ()
    o_ref, kt_ref, vt_ref, kh_scr, vt_scr, km_scr, s_scr, p_scr = refs[4 + len(prev):]
    blk = MOBA_BLOCK
    sub = blk
    contract_lanes = (((1,), (1,)), ((), ()))
    lane = lax.broadcasted_iota(jnp.int32, (blk, LANES), 1)
    head0 = lane < HEAD_DIM
    for j in range(n_blocks):
        kj = k_ref[0, j * blk:(j + 1) * blk, :]
        kh_scr[0, j * blk:(j + 1) * blk, :] = jnp.where(head0, kj, 0.0).astype(BF16)
        kh_scr[1, j * blk:(j + 1) * blk, :] = jnp.where(head0, 0.0, kj).astype(BF16)
        vt = v_ref[0, j * blk:(j + 1) * blk, :].T
        vt_scr[:, j * blk:(j + 1) * blk] = vt.astype(BF16)
        km_scr[j:j + 1, :] = jnp.sum(kj, axis=0, keepdims=True) * (1.0 / blk)
        kt_ref[n_prev, 0, :, j * blk:(j + 1) * blk] = kj.T
        vt_ref[n_prev, 0, :, j * blk:(j + 1) * blk] = vt
    if n_prev:
        kt_ref[0:n_prev] = prev[0][...]
        vt_ref[0:n_prev] = prev[1][...]

    km = km_scr[...]
    lane8 = lax.broadcasted_iota(jnp.int32, (n_blocks, LANES), 1)
    row8 = lax.broadcasted_iota(jnp.int32, (n_blocks, blk), 0)
    krow = lax.broadcasted_iota(jnp.int32, (blk, blk), 0)
    qcol = lax.broadcasted_iota(jnp.int32, (blk, blk), 1)
    causal = jnp.where(krow <= qcol, 0.0, NEG_INF)
    fold = lambda t: t.reshape(t.shape[0] // SUBLANES, SUBLANES, blk)
    km_parts = []
    for h in range(2):
        kmh = jnp.where((lane8 < HEAD_DIM) == (h == 0), km, 0.0)
        kmh_hi = kmh.astype(BF16)
        km_parts.append((kmh_hi, (kmh - kmh_hi.astype(F32)).astype(BF16)))

    heads = range(2)
    tiles = lambda i: [(j, u, h) for j in range(i + 1) for u in range(blk // sub) for h in heads]

    def begin(i):
        q = q_ref[0, i * blk:(i + 1) * blk, :] * (HEAD_DIM ** -0.5 * LOG2_E)
        qh = q.astype(BF16)
        selmask = [None, None]
        if i > MOBA_TOPK:
            ql = (q - qh.astype(F32)).astype(BF16)
            for h in heads:
                kmh_hi, kmh_lo = km_parts[h]
                gate = (lax.dot_general(kmh_hi, qh, contract_lanes, preferred_element_type=F32)
                        + lax.dot_general(kmh_lo, qh, contract_lanes, preferred_element_type=F32)
                        + lax.dot_general(kmh_hi, ql, contract_lanes, preferred_element_type=F32))
                rank = jnp.zeros((n_blocks, blk), jnp.int32)
                for jp in range(i):
                    gj = gate[jp:jp + 1, :]
                    rank = rank + jnp.where((gj > gate) | ((gj == gate) & (jp < row8)), 1, 0)
                selmask[h] = jnp.where((rank < MOBA_TOPK) & (row8 < i), 0.0, NEG_INF)
        return dict(i=i, buf=i % 2, qh=qh, selmask=selmask, m8=[None, None], m=None,
                    l8=[jnp.zeros((SUBLANES, blk), F32) for _ in heads])

    def score_tile(st, j, u, h):
        i = st["i"]
        rows = slice(j * blk + u * sub, j * blk + (u + 1) * sub)
        s = lax.dot_general(kh_scr[h, rows, :], st["qh"], contract_lanes, preferred_element_type=F32)
        if j == i:
            s = s + causal[u * sub:(u + 1) * sub]
        s_scr[st["buf"], h, rows, :] = s
        tmax = jnp.max(fold(s), axis=0)
        if j < i and st["selmask"][h] is not None:
            tmax = tmax + st["selmask"][h][j:j + 1, :]
        st["m8"][h] = tmax if st["m8"][h] is None else jnp.maximum(st["m8"][h], tmax)

    def prob_tile(st, j, u, h):
        i = st["i"]
        if st["m"] is None:
            st["m"] = [jnp.max(st["m8"][hh], axis=0, keepdims=True) for hh in heads]
        rows = slice(j * blk + u * sub, j * blk + (u + 1) * sub)
        sm = st["selmask"][h]
        shift = -st["m"][h] if (j == i or sm is None) else sm[j:j + 1, :] - st["m"][h]
        p = jnp.exp2(s_scr[st["buf"], h, rows, :] + shift)
        st["l8"][h] = st["l8"][h] + jnp.sum(fold(p), axis=0)
        p_scr[st["buf"], h, rows, :] = p.astype(BF16)

    def finish(st):
        i = st["i"]
        nk = (i + 1) * blk
        halves = []
        for h in heads:
            l = jnp.sum(st["l8"][h], axis=0, keepdims=True)
            ot = jnp.dot(vt_scr[h * HEAD_DIM:(h + 1) * HEAD_DIM, 0:nk], p_scr[st["buf"], h, 0:nk, :],
                         preferred_element_type=F32)
            halves.append(ot / l)
        ot = jnp.concatenate(halves, axis=0)
        o_ref[0, i * blk:(i + 1) * blk, :] = (ot.T * _silu(g_ref[0, i * blk:(i + 1) * blk, :])).astype(o_ref.dtype)

    cur = begin(0)
    for t in tiles(0):
        score_tile(cur, *t)
    prev = None
    for i in range(n_blocks):
        nxt = begin(i + 1) if i + 1 < n_blocks else None
        a, b = tiles(i), (tiles(i + 1) if nxt is not None else [])
        for k in range(max(len(a), len(b))):
            if k == 2 and prev is not None:
                finish(prev)
                prev = None
            if k < len(b):
                score_tile(nxt, *b[k])
            if k < len(a):
                prob_tile(cur, *a[k])
        if prev is not None:
            finish(prev)
        prev, cur = cur, nxt
    finish(prev)


def _moba_prompt(mq3, mk3, mv3, mg3, prev_kv):
    b, length, _ = mq3.shape
    n_pairs = MOBA_W // LANES
    n_blocks = length // MOBA_BLOCK
    n_prev = 0 if prev_kv is None else prev_kv[0].shape[0]
    rows = pl.BlockSpec((1, length, LANES), lambda bi, p: (bi, 0, p))
    cols = lambda n: pl.BlockSpec((n, 1, LANES, length), lambda bi, p: (0, bi, p, 0))
    stacked = jax.ShapeDtypeStruct((n_prev + 1, b, MOBA_W, length), F32)
    return pl.pallas_call(
        functools.partial(_moba_prompt_kernel, n_blocks=n_blocks, n_prev=n_prev),
        grid=(b, n_pairs),
        in_specs=[rows, rows, rows, rows] + ([cols(n_prev), cols(n_prev)] if n_prev else []),
        out_specs=[rows, cols(n_prev + 1), cols(n_prev + 1)],
        out_shape=[jax.ShapeDtypeStruct((b, length, MOBA_W), BF16), stacked, stacked],
        scratch_shapes=[
            pltpu.VMEM((2, length, LANES), BF16),
            pltpu.VMEM((LANES, length), BF16),
            pltpu.VMEM((n_blocks, LANES), F32),
            pltpu.VMEM((2, 2, length, MOBA_BLOCK), F32),
            pltpu.VMEM((2, 2, length, MOBA_BLOCK), BF16),
        ],
        compiler_params=_cparams(("arbitrary", "arbitrary")),
        name="moba_prompt",
    )(mq3, mk3, mv3, mg3, *(prev_kv or ()))


def _moba_sample_kernel(pt_ref, q_ref, kn_ref, vn_ref, g_ref, ck_ref, cv_ref, o_ref,
                        kring, vring, ksem, vsem, qs_scr, s_scr, p_scr, *, layer, n_pages, page):
    b = pl.program_id(0)
    last = pl.num_programs(0) - 1
    n_blocks = n_pages * page // MOBA_BLOCK
    group = math.gcd(n_pages, 16)
    rows = MOBA_HEADS * SUBLANES
    contract_lanes = (((1,), (1,)), ((), ()))
    lane = lax.broadcasted_iota(jnp.int32, (SUBLANES, MOBA_W), 1)

    def k_copy(bi, pg):
        return pltpu.make_async_copy(ck_ref.at[layer, pt_ref[bi, pg]], kring.at[pg], ksem.at[pg])

    def v_copy(bi, pg):
        return pltpu.make_async_copy(cv_ref.at[layer, pt_ref[bi, pg]], vring.at[pg], vsem.at[pg])

    @pl.when(b == 0)
    def _():
        for pg in range(n_pages):
            k_copy(0, pg).start()

    q = q_ref[0] * (HEAD_DIM ** -0.5 * LOG2_E)
    for h in range(MOBA_HEADS):
        qs_scr[h * SUBLANES:(h + 1) * SUBLANES, :] = jnp.where(lane // HEAD_DIM == h, q, 0.0)
    qb = qs_scr[...].astype(BF16)

    for g0 in range(0, n_pages, group):
        for pg in range(g0, g0 + group):
            k_copy(b, pg).wait()
        for pg in range(g0, g0 + group):
            s_scr[:, pg * page:(pg + 1) * page] = jnp.dot(qb, kring[pg].astype(BF16), preferred_element_type=F32)
        for pg in range(g0, g0 + group):
            v_copy(b, pg).start()

    col = lax.broadcasted_iota(jnp.int32, (rows, LANES), 1)
    gate = jnp.zeros((rows, LANES), F32)
    for n in range(n_blocks):
        c0 = n * MOBA_BLOCK
        bsum = s_scr[:, c0:c0 + page]
        for u in range(1, MOBA_BLOCK // page):
            bsum = bsum + s_scr[:, c0 + u * page:c0 + (u + 1) * page]
        gate = jnp.where(col == n, jnp.sum(bsum, axis=1, keepdims=True), gate)
    rank = jnp.zeros(gate.shape, jnp.int32)
    for jp in range(n_blocks):
        gj = gate[:, jp:jp + 1]
        rank = rank + jnp.where((gj > gate) | ((gj == gate) & (jp < col)), 1, 0)
    sel = jnp.where(rank < min(MOBA_TOPK, n_blocks), 0.0, NEG_INF)
    own = jnp.concatenate([kn_ref[0], jnp.zeros((page - SUBLANES, MOBA_W), F32)], axis=0).astype(BF16)
    t_idx = lax.broadcasted_iota(jnp.int32, (rows, page), 0) % SUBLANES
    key_idx = lax.broadcasted_iota(jnp.int32, (rows, page), 1)
    s_own = lax.dot_general(qb, own, contract_lanes, preferred_element_type=F32)
    s_own = s_own + jnp.where(key_idx <= t_idx, 0.0, NEG_INF)
    s_scr[:, n_pages * page:] = s_own
    m_part = s_own
    for n in range(n_blocks):
        add = sel[:, n:n + 1]
        for u in range(MOBA_BLOCK // page):
            c0 = n * MOBA_BLOCK + u * page
            sm = s_scr[:, c0:c0 + page] + add
            s_scr[:, c0:c0 + page] = sm
            m_part = jnp.maximum(m_part, sm)
    m = jnp.max(m_part, axis=1, keepdims=True)
    l_part = jnp.zeros((rows, page), F32)
    for n in range(n_pages + 1):
        c0 = n * page
        p = jnp.exp2(s_scr[:, c0:c0 + page] - m)
        l_part = l_part + p
        p_scr[:, c0:c0 + page] = p.astype(BF16)
    l = jnp.sum(l_part, axis=1, keepdims=True)
    vown = jnp.concatenate([vn_ref[0], jnp.zeros((page - SUBLANES, MOBA_W), F32)], axis=0).astype(BF16)
    acc = jnp.dot(p_scr[:, n_pages * page:], vown, preferred_element_type=F32)

    b_next = jnp.minimum(b + 1, last)
    for g0 in range(0, n_pages, group):
        for pg in range(g0, g0 + group):
            v_copy(b, pg).wait()
        for pg in range(g0, g0 + group):
            acc = acc + lax.dot_general(p_scr[:, pg * page:(pg + 1) * page], vring[pg].astype(BF16),
                                        contract_lanes, preferred_element_type=F32)
        for pg in range(g0, g0 + group):
            k_copy(b_next, pg).start()

    @pl.when(b == last)
    def _():
        for pg in range(n_pages):
            k_copy(b_next, pg).wait()

    o = acc / l
    out = jnp.zeros((SUBLANES, MOBA_W), F32)
    for h in range(MOBA_HEADS):
        out = out + jnp.where(lane // HEAD_DIM == h, o[h * SUBLANES:(h + 1) * SUBLANES, :], 0.0)
    o_ref[0] = (out * _silu(g_ref[0])).astype(o_ref.dtype)


def _moba_sample(page_table, mq3, mk3, mv3, mg3, cache_k4, cache_v4, layer):
    db = mq3.shape[0]
    n_pages = page_table.shape[1]
    page = cache_k4.shape[3]
    rows = MOBA_HEADS * SUBLANES
    assert n_pages * page // MOBA_BLOCK <= LANES and page == LANES
    small = pl.BlockSpec((1, SUBLANES, MOBA_W), lambda bi, pt: (bi, 0, 0))
    hbm = pl.BlockSpec(memory_space=pl.ANY)
    width = (n_pages + 1) * page
    return pl.pallas_call(
        functools.partial(_moba_sample_kernel, layer=layer, n_pages=n_pages, page=page),
        grid_spec=pltpu.PrefetchScalarGridSpec(
            num_scalar_prefetch=1,
            grid=(db,),
            in_specs=[small, small, small, small, hbm, hbm],
            out_specs=small,
            scratch_shapes=[
                pltpu.VMEM((n_pages, MOBA_W, page), F32),
                pltpu.VMEM((n_pages, MOBA_W, page), F32),
                pltpu.SemaphoreType.DMA((n_pages,)),
                pltpu.SemaphoreType.DMA((n_pages,)),
                pltpu.VMEM((rows, MOBA_W), F32),
                pltpu.VMEM((rows, width), F32),
                pltpu.VMEM((rows, width), BF16),
            ],
        ),
        out_shape=jax.ShapeDtypeStruct((db, SUBLANES, MOBA_W), BF16),
        compiler_params=_cparams(("arbitrary",)),
        name="moba_sample",
    )(page_table, mq3, mk3, mv3, mg3, cache_k4, cache_v4)


def _pool_kernel(u_ref, g_ref, buf_ref, cnt_ref, w_ref, b_ref, sc_ref, o_ref, bufo_ref, x_scr, w_scr, *,
                 n_tokens, rows):
    bb, length, _ = u_ref.shape
    lead = SUBLANES
    halo = lead + POOL_BUF + 1
    assert POOL_WINDOWS == (2, 4, 8, 16) and halo % SUBLANES == 0
    tiles = [(lead, halo - lead)] + [(halo + r0, rows) for r0 in range(0, length, rows)]
    for bi in range(bb):
        x_scr[bi, 0:lead, :] = jnp.zeros((lead, POOL_W), F32)
        x_scr[bi, lead:halo, :] = buf_ref[bi]
        x_scr[bi, halo:, :] = u_ref[bi]
        w_scr[bi, :, 0:lead, :] = jnp.zeros((3, lead, POOL_W), F32)
        for i0, n in tiles:
            group = lax.broadcasted_iota(jnp.int32, (n, POOL_W), 1) // POOL_GC
            u = x_scr[bi, i0:i0 + n, :]
            w2 = u + x_scr[bi, i0 - 1:i0 - 1 + n, :]
            w_scr[bi, 0, i0:i0 + n, :] = w2
            w4 = w2 + w_scr[bi, 0, i0 - 2:i0 - 2 + n, :]
            w_scr[bi, 1, i0:i0 + n, :] = w4
            w8 = w4 + w_scr[bi, 1, i0 - 4:i0 - 4 + n, :]
            w_scr[bi, 2, i0:i0 + n, :] = w8
            if i0 < halo:
                continue
            w16 = w8 + w_scr[bi, 2, i0 - 8:i0 - 8 + n, :]
            r0 = i0 - halo
            win = jnp.where(group == 0, w2, jnp.where(group == 1, w4, jnp.where(group == 2, w8, w16)))
            pooled = win / cnt_ref[r0:r0 + n, :] - u
            y = jnp.dot(pooled.astype(BF16), w_ref[...], preferred_element_type=F32) + b_ref[...]
            y = y * sc_ref[...]
            o_ref[bi, r0:r0 + n, :] = (y * _silu(g_ref[bi, r0:r0 + n, :])).astype(o_ref.dtype)
        bufo_ref[bi] = x_scr[bi, lead + n_tokens + 1:lead + n_tokens + 1 + POOL_BUF, :]


def _pool(zp3, buf16, cnt, w_bd, bias, scale, *, n_tokens, rows, bb):
    b, length, _ = zp3.shape
    const = lambda bi: (0, 0)
    n_rows = SUBLANES + POOL_BUF + 1 + length
    return pl.pallas_call(
        functools.partial(_pool_kernel, n_tokens=n_tokens, rows=rows),
        grid=(b // bb,),
        in_specs=[
            pl.BlockSpec((bb, length, POOL_W), lambda bi: (bi, 0, 0)),
            pl.BlockSpec((bb, length, POOL_W), lambda bi: (bi, 0, 1)),
            pl.BlockSpec((bb, POOL_BUF + 1, POOL_W), lambda bi: (bi, 0, 0)),
            pl.BlockSpec((length, POOL_W), const),
            pl.BlockSpec((POOL_W, POOL_W), const),
            pl.BlockSpec((1, POOL_W), const),
            pl.BlockSpec((1, POOL_W), const),
        ],
        out_specs=[
            pl.BlockSpec((bb, length, POOL_W), lambda bi: (bi, 0, 0)),
            pl.BlockSpec((bb, POOL_BUF, POOL_W), lambda bi: (bi, 0, 0)),
        ],
        out_shape=[
            jax.ShapeDtypeStruct((b, length, POOL_W), BF16),
            jax.ShapeDtypeStruct((b, POOL_BUF, POOL_W), F32),
        ],
        scratch_shapes=[pltpu.VMEM((bb, n_rows, POOL_W), F32), pltpu.VMEM((bb, 3, n_rows, POOL_W), F32)],
        compiler_params=_cparams(("arbitrary",)),
        name="pool",
    )(zp3, zp3, buf16, cnt, w_bd, bias, scale)


def _out_proj_kernel(mr_ref, mm_ref, mp_ref, x_ref, w_ref, y_ref, mix_scr):
    mix_scr[:, :RET_W] = mr_ref[...]
    mix_scr[:, RET_W:RET_W + MOBA_W] = mm_ref[...]
    mix_scr[:, RET_W + MOBA_W:] = mp_ref[...]
    y_ref[...] = x_ref[...] + jnp.dot(mix_scr[...], w_ref[...], preferred_element_type=F32)


def _out_proj(mr, mm, mp, x2d, w_bf, *, tm, layer):
    m, d = x2d.shape
    d_mix = RET_W + MOBA_W + POOL_W
    row = lambda i: (i, 0)
    return pl.pallas_call(
        _out_proj_kernel,
        grid=(m // tm,),
        in_specs=[
            pl.BlockSpec((tm, RET_W), row),
            pl.BlockSpec((tm, MOBA_W), row),
            pl.BlockSpec((tm, POOL_W), row),
            pl.BlockSpec((tm, d), row),
            pl.BlockSpec((None, d_mix, d), lambda i: (layer, 0, 0)),
        ],
        out_specs=pl.BlockSpec((tm, d), row),
        out_shape=jax.ShapeDtypeStruct((m, d), F32),
        scratch_shapes=[pltpu.VMEM((tm, d_mix), BF16)],
        compiler_params=_cparams(("arbitrary",)),
        name="out_proj",
    )(mr, mm, mp, x2d, w_bf)


def _rope_tables(pos):
    inv = 1.0 / (ROPE_THETA ** (np.arange(HALF, dtype=np.float64) / HALF))
    ang = np.asarray(pos, np.float64)[:, None] * inv[None, :]
    c, s = np.cos(ang), np.sin(ang)
    return (jnp.asarray(np.concatenate([c, c, c, c], axis=-1), F32),
            jnp.asarray(np.concatenate([-s, s, -s, s], axis=-1), F32))


def _block_diag(blocks):
    g, n, _ = blocks.shape
    eye = jnp.eye(g, dtype=blocks.dtype)
    return (eye[:, None, :, None] * blocks[:, :, None, :]).reshape(g * n, g * n)


def _pool_counts(pos0, length):
    pos = pos0 + np.arange(length)
    w = np.repeat(np.asarray(POOL_WINDOWS), POOL_GC)
    return jnp.asarray(np.minimum(pos[:, None] + 1, w[None, :]), F32)


def _layer(x3, pos_tabs, ret_state, ret_prev, ret_tabs, ret_step, pool_buf16, pool_cnt, n_tokens, moba_fn, params, *,
           tm, layer):
    b, length, d = x3.shape
    nw, w_in, w_out, gnw, qnw, knw, bd, pw_bd, pbias, pscale = params
    x2d = x3.reshape(b * length, d)
    zr, mq, mk, mv, mg, zp = _in_proj(x2d, nw, w_in, pos_tabs[0], pos_tabs[1], qnw, knw, bd, tm=tm, layer=layer)
    three = lambda a: a.reshape(b, length, a.shape[-1])
    mix_r, states = _retention(three(zr), ret_state, ret_prev, ret_tabs, gnw, chunk=RET_CHUNK, bb=ret_step[0],
                               pairs=ret_step[1], layer=layer)
    mix_m, k_new, v_new = moba_fn(three(mq), three(mk), three(mv), three(mg))
    mix_p, buf_new = _pool(three(zp), pool_buf16, pool_cnt, pw_bd, pbias, pscale,
                           n_tokens=n_tokens, rows=min(length, 128), bb=ret_step[0])
    two = lambda a: a.reshape(b * length, a.shape[-1])
    tm_out = 2 * tm if (b * length) % (2 * tm) == 0 else tm
    y = _out_proj(two(mix_r), two(mix_m), two(mix_p), x2d, w_out, tm=tm_out, layer=layer)
    return y.reshape(b, length, d), states, k_new, v_new, buf_new


def _moba_prompt_wrap(mq3, mk3, mv3, mg3, *, prev_kv):
    return _moba_prompt(mq3, mk3, mv3, mg3, prev_kv)


def _moba_sample_wrap(mq3, mk3, mv3, mg3, *, page_table, cache_k4, cache_v4, layer, n_tokens):
    mix = _moba_sample(page_table, mq3, mk3, mv3, mg3, cache_k4, cache_v4, layer)
    rows = lambda t: t[:, :n_tokens].reshape(t.shape[0], n_tokens, MOBA_HEADS, HEAD_DIM)
    return mix, rows(mk3), rows(mv3)


def kernel(x_prompt, x_sample, cache_k, cache_v, state_ret, state_pool, page_table, norm_w, w_in, w_out, ret_gn_w, q_norm_w, k_norm_w, pool_w, pool_b, pool_scale):
    depth = w_in.shape[0]
    b, seq, d = x_prompt.shape
    db, dec_seq, _ = x_sample.shape
    n_pool, page = cache_k.shape[1], cache_k.shape[2]
    past_len = page_table.shape[1] * page
    assert dec_seq <= SUBLANES and seq % MOBA_BLOCK == 0 and seq % RET_CHUNK == 0 and past_len % MOBA_BLOCK == 0

    p_tabs = _rope_tables(np.arange(seq))
    tm_s = db * SUBLANES
    s_tabs = _rope_tables(np.tile(past_len + np.arange(SUBLANES), db))
    p_ret_tabs = _retention_tables(RET_CHUNK, RET_CHUNK)
    s_ret_tabs = _retention_tables(RET_CHUNK, dec_seq)
    p_cnt = _pool_counts(0, seq)
    s_cnt = _pool_counts(past_len, SUBLANES)
    head_of = np.arange(MOBA_W) // HEAD_DIM
    bd = jnp.asarray((head_of[:, None] == head_of[None, :]) / HEAD_DIM, BF16)
    zero_buf = jnp.zeros((b, POOL_BUF + 1, POOL_W), F32)

    cache_k4 = cache_k.transpose(0, 1, 3, 4, 2).reshape(depth, n_pool, MOBA_W, page)
    cache_v4 = cache_v.transpose(0, 1, 3, 4, 2).reshape(depth, n_pool, MOBA_W, page)
    xp = x_prompt
    xs = jnp.pad(x_sample, ((0, 0), (0, SUBLANES - dec_seq), (0, 0)))

    w_in_bf = w_in.astype(BF16)
    w_out_bf = w_out.astype(BF16)
    outs = [[] for _ in range(4)]
    prev_kv = s_p = s_s = None
    for l in range(depth):
        params = (norm_w[l].reshape(1, d), w_in_bf, w_out_bf,
                  ret_gn_w[l].reshape(RET_W // LANES, 1, LANES),
                  jnp.tile(q_norm_w[l], MOBA_HEADS).reshape(1, MOBA_W),
                  jnp.tile(k_norm_w[l], MOBA_HEADS).reshape(1, MOBA_W),
                  bd, _block_diag(pool_w[l]).astype(BF16),
                  pool_b[l].reshape(1, POOL_W), pool_scale[l].reshape(1, POOL_W))
        xp, s_p, kt, vt, b_p = _layer(xp, p_tabs, None, s_p, p_ret_tabs, (1, 1), zero_buf, p_cnt, seq,
                                      functools.partial(_moba_prompt_wrap, prev_kv=prev_kv), params,
                                      tm=512, layer=l)
        prev_kv = (kt, vt)
        moba_s = functools.partial(_moba_sample_wrap, page_table=page_table, cache_k4=cache_k4,
                                   cache_v4=cache_v4, layer=l, n_tokens=dec_seq)
        xs, s_s, k_s, v_s, b_s = _layer(xs, s_tabs, state_ret, s_s, s_ret_tabs,
                                        (math.gcd(db, 8), RET_W // LANES),
                                        jnp.pad(state_pool[l], ((0, 0), (1, 0), (0, 0))), s_cnt, dec_seq,
                                        moba_s, params, tm=tm_s, layer=l)
        for lst, val in zip(outs, (k_s, v_s, b_p, b_s)):
            lst.append(val)
    rows = lambda t: t.reshape(depth, b, MOBA_HEADS, HEAD_DIM, seq).transpose(0, 1, 4, 2, 3)
    k_s, v_s, b_p, b_s = (jnp.stack(o) for o in outs)
    return xp, xs[:, :dec_seq], rows(prev_kv[0]), rows(prev_kv[1]), k_s, v_s, s_p, s_s, b_p, b_s
```

```python
import functools
import math

import jax
import jax.numpy as jnp
import numpy as np
from jax import lax
from jax.experimental import pallas as pl
from jax.experimental.pallas import tpu as pltpu

F32 = jnp.float32
BF16 = jnp.bfloat16

HEAD_DIM = 64
HALF = HEAD_DIM // 2
LANES = 128
RET_HEADS = 6
MOBA_HEADS = 6
RET_W = RET_HEADS * HEAD_DIM
MOBA_W = MOBA_HEADS * HEAD_DIM
POOL_W = 256
POOL_GC = 64
POOL_WINDOWS = (2, 4, 8, 16)
POOL_BUF = 15
RET_CHUNK = 128
MOBA_BLOCK = 256
MOBA_TOPK = 3
ROPE_THETA = 10000.0
EPS = 1e-6
SUBLANES = 8
NEG_INF = float("-inf")
LOG2_E = 1.4426950408889634

OFF_RQ, OFF_RV, OFF_MQ, OFF_MV, OFF_PU = 0, 2 * RET_W, 4 * RET_W, 4 * RET_W + 2 * MOBA_W, 4 * RET_W + 4 * MOBA_W
D_IN = OFF_PU + 2 * POOL_W

VMEM_LIMIT = 56 * 1024 * 1024


def _silu(x):
    return x / (1.0 + jnp.exp(-x))


def _cparams(sem):
    return pltpu.CompilerParams(dimension_semantics=sem, vmem_limit_bytes=VMEM_LIMIT)


def _in_proj_kernel(x_ref, nw_ref, w_ref, cos_ref, sin_ref, qnw_ref, knw_ref, bd_ref,
                    zr_ref, mq_ref, mk_ref, mv_ref, mg_ref, zp_ref, h_scr, z_scr):
    tm = x_ref.shape[0]
    x = x_ref[...]
    ms = jnp.mean(x * x, axis=-1, keepdims=True)
    h_scr[...] = (x * lax.rsqrt(ms + EPS) * nw_ref[...]).astype(BF16)
    cos = cos_ref[...]
    sin = sin_ref[...]
    lane = lax.broadcasted_iota(jnp.int32, (tm, LANES), 1)
    first_half = (lane & HALF) == 0

    def rope(z):
        partner = jnp.where(first_half, pltpu.roll(z, LANES - HALF, 1), pltpu.roll(z, HALF, 1))
        return z * cos + partner * sin

    def proj(off, width):
        return jnp.dot(h_scr[...], w_ref[:, off:off + width], preferred_element_type=F32)

    z_scr[...] = proj(OFF_RQ, 2 * RET_W)
    for g in range(2 * RET_W // LANES):
        out = rope(z_scr[:, g * LANES:(g + 1) * LANES])
        if g >= RET_W // LANES:
            out = out * (HEAD_DIM ** -0.5)
        zr_ref[:, g * LANES:(g + 1) * LANES] = out
    zr_ref[:, OFF_RV:OFF_MQ] = proj(OFF_RV, 2 * RET_W)
    z_scr[...] = proj(OFF_MQ, 2 * MOBA_W)
    for t, (nw, dst) in enumerate(((qnw_ref, mq_ref), (knw_ref, mk_ref))):
        z = z_scr[:, t * MOBA_W:(t + 1) * MOBA_W]
        msq = jnp.dot((z * z).astype(BF16), bd_ref[...], preferred_element_type=F32)
        z_scr[:, t * MOBA_W:(t + 1) * MOBA_W] = z * lax.rsqrt(msq + EPS) * nw[...]
        for g in range(MOBA_W // LANES):
            c0 = t * MOBA_W + g * LANES
            dst[:, g * LANES:(g + 1) * LANES] = rope(z_scr[:, c0:c0 + LANES])
    z_scr[...] = proj(OFF_MV, 2 * MOBA_W)
    mv_ref[...] = z_scr[:, :MOBA_W]
    mg_ref[...] = z_scr[:, MOBA_W:]
    zp_ref[...] = proj(OFF_PU, 2 * POOL_W)


def _in_proj(x2d, nw, w_bf, cos, sin, qnw, knw, bd, *, tm, layer):
    m, d = x2d.shape
    n_pos = cos.shape[0] // tm
    row = lambda i: (i, 0)
    const = lambda i: (0, 0)
    pos = lambda i: (i % n_pos, 0)
    outs = [(2 * RET_W + 2 * RET_W), MOBA_W, MOBA_W, MOBA_W, MOBA_W, 2 * POOL_W]
    return pl.pallas_call(
        _in_proj_kernel,
        grid=(m // tm,),
        in_specs=[
            pl.BlockSpec((tm, d), row),
            pl.BlockSpec((1, d), const),
            pl.BlockSpec((None, d, D_IN), lambda i: (layer, 0, 0)),
            pl.BlockSpec((tm, LANES), pos),
            pl.BlockSpec((tm, LANES), pos),
            pl.BlockSpec((1, MOBA_W), const),
            pl.BlockSpec((1, MOBA_W), const),
            pl.BlockSpec((MOBA_W, MOBA_W), const),
        ],
        out_specs=[pl.BlockSpec((tm, w), row) for w in outs],
        out_shape=[jax.ShapeDtypeStruct((m, w), F32) for w in outs],
        scratch_shapes=[pltpu.VMEM((tm, d), BF16), pltpu.VMEM((tm, 2 * RET_W), F32)],
        compiler_params=_cparams(("arbitrary",)),
        name="in_proj",
    )(x2d, nw, w_bf, cos, sin, qnw, knw, bd)


def _retention_kernel(*refs, chunk, has_state, n_prev):
    q_ref, k_ref, v_ref, g_ref = refs[:4]
    s0_ref = refs[4] if has_state else None
    prev_ref = refs[4 + has_state] if n_prev else None
    dm_ref, rs_ref, kd_ref, gc_ref, gnw_ref, o_ref, sout_ref, oi_scr, kv_scr, sb_scr = refs[4 + has_state + (n_prev > 0):]
    if n_prev:
        sout_ref[0:n_prev] = prev_ref[...]
    bb, length, width = q_ref.shape
    pairs = width // LANES
    rows_in = min(length, chunk)
    n_chunks = max(1, length // chunk)
    lane = lax.broadcasted_iota(jnp.int32, (chunk, LANES), 1)
    head0 = lane < HEAD_DIM
    r = lax.broadcasted_iota(jnp.int32, (LANES, LANES), 0)
    c = lax.broadcasted_iota(jnp.int32, (LANES, LANES), 1)
    same_head = (r < HEAD_DIM) == (c < HEAD_DIM)
    contract_lanes = (((1,), (1,)), ((), ()))

    def head_mean(x):
        m0 = jnp.sum(jnp.where(head0, x, 0.0), axis=1, keepdims=True)
        m1 = jnp.sum(jnp.where(head0, 0.0, x), axis=1, keepdims=True)
        return jnp.where(head0, m0, m1) * (1.0 / HEAD_DIM)

    def load(ref, bi, p, ci):
        a = ref[bi, ci * chunk:ci * chunk + rows_in, p * LANES:(p + 1) * LANES]
        if rows_in < chunk:
            a = jnp.concatenate([a, jnp.zeros((chunk - rows_in, LANES), F32)], axis=0)
        return a

    items = [(bi, p, ci) for bi in range(bb) for p in range(pairs) for ci in range(n_chunks)]

    def a1(bi, p, ci):
        q = load(q_ref, bi, p, ci)
        k = load(k_ref, bi, p, ci)
        kb = k.astype(BF16)
        vb = load(v_ref, bi, p, ci).astype(BF16)
        q0 = jnp.where(head0, q, 0.0).astype(BF16)
        q1 = jnp.where(head0, 0.0, q).astype(BF16)
        in0 = (lax.dot_general(q0, kb, contract_lanes, preferred_element_type=F32) * dm_ref[2 * p]).astype(BF16)
        in1 = (lax.dot_general(q1, kb, contract_lanes, preferred_element_type=F32) * dm_ref[2 * p + 1]).astype(BF16)
        return bi, p, ci, k, vb, in0, in1

    def a2(st):
        bi, p, ci, k, vb, in0, in1 = st
        slot = bi * pairs + p
        oi_scr[slot, ci * chunk:(ci + 1) * chunk, :] = jnp.where(
            head0, jnp.dot(in0, vb, preferred_element_type=F32), jnp.dot(in1, vb, preferred_element_type=F32))
        kdt = (k * kd_ref[p]).T.astype(BF16)
        kv_scr[slot, ci] = jnp.where(same_head, jnp.dot(kdt, vb, preferred_element_type=F32), 0.0)

    st = None
    for item in items + [None]:
        nxt = a1(*item) if item is not None else None
        if st is not None:
            a2(st)
        st = nxt

    zero = jnp.zeros((HEAD_DIM, HEAD_DIM), F32)
    for bi in range(bb):
        for p in range(pairs):
            slot = bi * pairs + p
            if s0_ref is None:
                s = jnp.zeros((LANES, LANES), F32)
            else:
                s = jnp.concatenate([jnp.concatenate([s0_ref[bi, 2 * p], zero], axis=1),
                                     jnp.concatenate([zero, s0_ref[bi, 2 * p + 1]], axis=1)], axis=0)
            for ci in range(n_chunks):
                sb_scr[slot, ci] = s.astype(BF16)
                s = s * gc_ref[p] + kv_scr[slot, ci]
            sout_ref[n_prev, bi, 2 * p] = s[:HEAD_DIM, :HEAD_DIM]
            sout_ref[n_prev, bi, 2 * p + 1] = pltpu.roll(s[HEAD_DIM:], HEAD_DIM, 1)[:, :HEAD_DIM]

    def c1(bi, p, ci):
        slot = bi * pairs + p
        qb = load(q_ref, bi, p, ci).astype(BF16)
        o = (oi_scr[slot, ci * chunk:(ci + 1) * chunk, :]
             + jnp.dot(qb, sb_scr[slot, ci], preferred_element_type=F32) * rs_ref[p])
        return bi, p, ci, o, head_mean(o)

    def c2(st):
        bi, p, ci, o, mu = st
        oc = o - mu
        return bi, p, ci, oc, head_mean(oc * oc)

    def c3(st):
        bi, p, ci, oc, var = st
        on = oc * lax.rsqrt(var + EPS) * gnw_ref[p]
        out = on * _silu(load(g_ref, bi, p, ci))
        o_ref[bi, ci * chunk:ci * chunk + rows_in, p * LANES:(p + 1) * LANES] = out[:rows_in].astype(o_ref.dtype)

    s1 = s2 = None
    for item in items + [None, None]:
        new1 = c1(*item) if item is not None else None
        new2 = c2(s1) if s1 is not None else None
        if s2 is not None:
            c3(s2)
        s1, s2 = new1, new2


def _retention(zr3, state, prev_states, tabs, gnw, *, chunk, bb, pairs, layer):
    b, length, _ = zr3.shape
    n_pairs = RET_W // LANES
    dm, rs, kd, gc = tabs
    w = pairs * LANES
    n_chunks = max(1, length // chunk)
    n_prev = 0 if prev_states is None else prev_states.shape[0]
    col = lambda off: (lambda bi, p: (bi, 0, off + p))
    tab = lambda bi, p: (p, 0, 0)
    heads_blk = (bb, 2 * pairs, HEAD_DIM, HEAD_DIM)
    stacked = lambda n: pl.BlockSpec((n,) + heads_blk, lambda bi, p: (0, bi, p, 0, 0))
    extra_specs, extra_args = [], []
    if state is not None:
        extra_specs.append(pl.BlockSpec((None,) + heads_blk, lambda bi, p: (layer, bi, p, 0, 0)))
        extra_args.append(state)
    if n_prev:
        extra_specs.append(stacked(n_prev))
        extra_args.append(prev_states)
    return pl.pallas_call(
        functools.partial(_retention_kernel, chunk=chunk, has_state=state is not None, n_prev=n_prev),
        grid=(b // bb, n_pairs // pairs),
        in_specs=[
            pl.BlockSpec((bb, length, w), col(0)),
            pl.BlockSpec((bb, length, w), col(n_pairs // pairs)),
            pl.BlockSpec((bb, length, w), col(2 * n_pairs // pairs)),
            pl.BlockSpec((bb, length, w), col(3 * n_pairs // pairs)),
        ] + extra_specs + [
            pl.BlockSpec((2 * pairs, chunk, chunk), tab),
            pl.BlockSpec((pairs, chunk, LANES), tab),
            pl.BlockSpec((pairs, chunk, LANES), tab),
            pl.BlockSpec((pairs, 1, LANES), tab),
            pl.BlockSpec((pairs, 1, LANES), tab),
        ],
        out_specs=[pl.BlockSpec((bb, length, w), lambda bi, p: (bi, 0, p)), stacked(n_prev + 1)],
        out_shape=[
            jax.ShapeDtypeStruct((b, length, RET_W), BF16),
            jax.ShapeDtypeStruct((n_prev + 1, b, RET_HEADS, HEAD_DIM, HEAD_DIM), F32),
        ],
        scratch_shapes=[
            pltpu.VMEM((bb * pairs, n_chunks * chunk, LANES), F32),
            pltpu.VMEM((bb * pairs, n_chunks, LANES, LANES), F32),
            pltpu.VMEM((bb * pairs, n_chunks, LANES, LANES), BF16),
        ],
        compiler_params=_cparams(("arbitrary", "arbitrary")),
        name="retention",
    )(zr3, zr3, zr3, zr3, *extra_args, dm, rs, kd, gc, gnw)


def _retention_tables(chunk, n_tokens):
    lg = np.log(1.0 - 2.0 ** (-5.0 - np.arange(RET_HEADS, dtype=np.float64)))
    i = np.arange(chunk, dtype=np.float64)
    rel = i[:, None] - i[None, :]
    dm = np.where(rel[None] >= 0, np.exp(rel[None] * lg[:, None, None]), 0.0)
    lg_lanes = np.repeat(lg, HEAD_DIM).reshape(RET_W // LANES, 1, LANES)
    rs = np.exp((i + 1.0)[None, :, None] * lg_lanes)
    kd = np.where((i < n_tokens)[None, :, None], np.exp((n_tokens - 1.0 - i)[None, :, None] * lg_lanes), 0.0)
    gc = np.exp(float(n_tokens) * lg_lanes)
    return tuple(jnp.asarray(t, F32) for t in (dm, rs, kd, gc))


def _moba_prompt_kernel(*refs, n_blocks, n_prev):
    q_ref, k_ref, v_ref, g_ref = refs[:4]
    prev = refs[4:6] if n_prev else ()
    o_ref, kt_ref, vt_ref, kh_scr, vt_scr, km_scr, s_scr, p_scr = refs[4 + len(prev):]
    blk = MOBA_BLOCK
    sub = blk
    contract_lanes = (((1,), (1,)), ((), ()))
    lane = lax.broadcasted_iota(jnp.int32, (blk, LANES), 1)
    head0 = lane < HEAD_DIM
    for j in range(n_blocks):
        kj = k_ref[0, j * blk:(j + 1) * blk, :]
        kh_scr[0, j * blk:(j + 1) * blk, :] = jnp.where(head0, kj, 0.0).astype(BF16)
        kh_scr[1, j * blk:(j + 1) * blk, :] = jnp.where(head0, 0.0, kj).astype(BF16)
        vt = v_ref[0, j * blk:(j + 1) * blk, :].T
        vt_scr[:, j * blk:(j + 1) * blk] = vt.astype(BF16)
        km_scr[j:j + 1, :] = jnp.sum(kj, axis=0, keepdims=True) * (1.0 / blk)
        kt_ref[n_prev, 0, :, j * blk:(j + 1) * blk] = kj.T
        vt_ref[n_prev, 0, :, j * blk:(j + 1) * blk] = vt
    if n_prev:
        kt_ref[0:n_prev] = prev[0][...]
        vt_ref[0:n_prev] = prev[1][...]

    km = km_scr[...]
    lane8 = lax.broadcasted_iota(jnp.int32, (n_blocks, LANES), 1)
    row8 = lax.broadcasted_iota(jnp.int32, (n_blocks, blk), 0)
    krow = lax.broadcasted_iota(jnp.int32, (blk, blk), 0)
    qcol = lax.broadcasted_iota(jnp.int32, (blk, blk), 1)
    causal = jnp.where(krow <= qcol, 0.0, NEG_INF)
    fold = lambda t: t.reshape(t.shape[0] // SUBLANES, SUBLANES, blk)
    km_parts = []
    for h in range(2):
        kmh = jnp.where((lane8 < HEAD_DIM) == (h == 0), km, 0.0)
        kmh_hi = kmh.astype(BF16)
        km_parts.append((kmh_hi, (kmh - kmh_hi.astype(F32)).astype(BF16)))

    heads = range(2)
    tiles = lambda i: [(j, u, h) for j in range(i + 1) for u in range(blk // sub) for h in heads]

    def begin(i):
        q = q_ref[0, i * blk:(i + 1) * blk, :] * (HEAD_DIM ** -0.5 * LOG2_E)
        qh = q.astype(BF16)
        selmask = [None, None]
        if i > MOBA_TOPK:
            ql = (q - qh.astype(F32)).astype(BF16)
            for h in heads:
                kmh_hi, kmh_lo = km_parts[h]
                gate = (lax.dot_general(kmh_hi, qh, contract_lanes, preferred_element_type=F32)
                        + lax.dot_general(kmh_lo, qh, contract_lanes, preferred_element_type=F32)
                        + lax.dot_general(kmh_hi, ql, contract_lanes, preferred_element_type=F32))
                rank = jnp.zeros((n_blocks, blk), jnp.int32)
                for jp in range(i):
                    gj = gate[jp:jp + 1, :]
                    rank = rank + jnp.where((gj > gate) | ((gj == gate) & (jp < row8)), 1, 0)
                selmask[h] = jnp.where((rank < MOBA_TOPK) & (row8 < i), 0.0, NEG_INF)
        return dict(i=i, buf=i % 2, qh=qh, selmask=selmask, m8=[None, None], m=None,
                    l8=[jnp.zeros((SUBLANES, blk), F32) for _ in heads])

    def score_tile(st, j, u, h):
        i = st["i"]
        rows = slice(j * blk + u * sub, j * blk + (u + 1) * sub)
        s = lax.dot_general(kh_scr[h, rows, :], st["qh"], contract_lanes, preferred_element_type=F32)
        if j == i:
            s = s + causal[u * sub:(u + 1) * sub]
        s_scr[st["buf"], h, rows, :] = s
        tmax = jnp.max(fold(s), axis=0)
        if j < i and st["selmask"][h] is not None:
            tmax = tmax + st["selmask"][h][j:j + 1, :]
        st["m8"][h] = tmax if st["m8"][h] is None else jnp.maximum(st["m8"][h], tmax)

    def prob_tile(st, j, u, h):
        i = st["i"]
        if st["m"] is None:
            st["m"] = [jnp.max(st["m8"][hh], axis=0, keepdims=True) for hh in heads]
        rows = slice(j * blk + u * sub, j * blk + (u + 1) * sub)
        sm = st["selmask"][h]
        shift = -st["m"][h] if (j == i or sm is None) else sm[j:j + 1, :] - st["m"][h]
        p = jnp.exp2(s_scr[st["buf"], h, rows, :] + shift)
        st["l8"][h] = st["l8"][h] + jnp.sum(fold(p), axis=0)
        p_scr[st["buf"], h, rows, :] = p.astype(BF16)

    def finish(st):
        i = st["i"]
        nk = (i + 1) * blk
        halves = []
        for h in heads:
            l = jnp.sum(st["l8"][h], axis=0, keepdims=True)
            ot = jnp.dot(vt_scr[h * HEAD_DIM:(h + 1) * HEAD_DIM, 0:nk], p_scr[st["buf"], h, 0:nk, :],
                         preferred_element_type=F32)
            halves.append(ot / l)
        ot = jnp.concatenate(halves, axis=0)
        o_ref[0, i * blk:(i + 1) * blk, :] = (ot.T * _silu(g_ref[0, i * blk:(i + 1) * blk, :])).astype(o_ref.dtype)

    cur = begin(0)
    for t in tiles(0):
        score_tile(cur, *t)
    prev = None
    for i in range(n_blocks):
        nxt = begin(i + 1) if i + 1 < n_blocks else None
        a, b = tiles(i), (tiles(i + 1) if nxt is not None else [])
        for k in range(max(len(a), len(b))):
            if k == 2 and prev is not None:
                finish(prev)
                prev = None
            if k < len(b):
                score_tile(nxt, *b[k])
            if k < len(a):
                prob_tile(cur, *a[k])
        if prev is not None:
            finish(prev)
        prev, cur = cur, nxt
    finish(prev)


def _moba_prompt(mq3, mk3, mv3, mg3, prev_kv):
    b, length, _ = mq3.shape
    n_pairs = MOBA_W // LANES
    n_blocks = length // MOBA_BLOCK
    n_prev = 0 if prev_kv is None else prev_kv[0].shape[0]
    rows = pl.BlockSpec((1, length, LANES), lambda bi, p: (bi, 0, p))
    cols = lambda n: pl.BlockSpec((n, 1, LANES, length), lambda bi, p: (0, bi, p, 0))
    stacked = jax.ShapeDtypeStruct((n_prev + 1, b, MOBA_W, length), F32)
    return pl.pallas_call(
        functools.partial(_moba_prompt_kernel, n_blocks=n_blocks, n_prev=n_prev),
        grid=(b, n_pairs),
        in_specs=[rows, rows, rows, rows] + ([cols(n_prev), cols(n_prev)] if n_prev else []),
        out_specs=[rows, cols(n_prev + 1), cols(n_prev + 1)],
        out_shape=[jax.ShapeDtypeStruct((b, length, MOBA_W), BF16), stacked, stacked],
        scratch_shapes=[
            pltpu.VMEM((2, length, LANES), BF16),
            pltpu.VMEM((LANES, length), BF16),
            pltpu.VMEM((n_blocks, LANES), F32),
            pltpu.VMEM((2, 2, length, MOBA_BLOCK), F32),
            pltpu.VMEM((2, 2, length, MOBA_BLOCK), BF16),
        ],
        compiler_params=_cparams(("arbitrary", "arbitrary")),
        name="moba_prompt",
    )(mq3, mk3, mv3, mg3, *(prev_kv or ()))


def _moba_sample_kernel(pt_ref, q_ref, kn_ref, vn_ref, g_ref, ck_ref, cv_ref, o_ref,
                        kring, vring, ksem, vsem, qs_scr, s_scr, p_scr, *, layer, n_pages, page):
    b = pl.program_id(0)
    last = pl.num_programs(0) - 1
    n_blocks = n_pages * page // MOBA_BLOCK
    group = math.gcd(n_pages, 16)
    rows = MOBA_HEADS * SUBLANES
    contract_lanes = (((1,), (1,)), ((), ()))
    lane = lax.broadcasted_iota(jnp.int32, (SUBLANES, MOBA_W), 1)

    def k_copy(bi, pg):
        return pltpu.make_async_copy(ck_ref.at[layer, pt_ref[bi, pg]], kring.at[pg], ksem.at[pg])

    def v_copy(bi, pg):
        return pltpu.make_async_copy(cv_ref.at[layer, pt_ref[bi, pg]], vring.at[pg], vsem.at[pg])

    @pl.when(b == 0)
    def _():
        for pg in range(n_pages):
            k_copy(0, pg).start()

    q = q_ref[0] * (HEAD_DIM ** -0.5 * LOG2_E)
    for h in range(MOBA_HEADS):
        qs_scr[h * SUBLANES:(h + 1) * SUBLANES, :] = jnp.where(lane // HEAD_DIM == h, q, 0.0)
    qb = qs_scr[...].astype(BF16)

    for g0 in range(0, n_pages, group):
        for pg in range(g0, g0 + group):
            k_copy(b, pg).wait()
        for pg in range(g0, g0 + group):
            s_scr[:, pg * page:(pg + 1) * page] = jnp.dot(qb, kring[pg].astype(BF16), preferred_element_type=F32)
        for pg in range(g0, g0 + group):
            v_copy(b, pg).start()

    col = lax.broadcasted_iota(jnp.int32, (rows, LANES), 1)
    gate = jnp.zeros((rows, LANES), F32)
    for n in range(n_blocks):
        c0 = n * MOBA_BLOCK
        bsum = s_scr[:, c0:c0 + page]
        for u in range(1, MOBA_BLOCK // page):
            bsum = bsum + s_scr[:, c0 + u * page:c0 + (u + 1) * page]
        gate = jnp.where(col == n, jnp.sum(bsum, axis=1, keepdims=True), gate)
    rank = jnp.zeros(gate.shape, jnp.int32)
    for jp in range(n_blocks):
        gj = gate[:, jp:jp + 1]
        rank = rank + jnp.where((gj > gate) | ((gj == gate) & (jp < col)), 1, 0)
    sel = jnp.where(rank < min(MOBA_TOPK, n_blocks), 0.0, NEG_INF)
    own = jnp.concatenate([kn_ref[0], jnp.zeros((page - SUBLANES, MOBA_W), F32)], axis=0).astype(BF16)
    t_idx = lax.broadcasted_iota(jnp.int32, (rows, page), 0) % SUBLANES
    key_idx = lax.broadcasted_iota(jnp.int32, (rows, page), 1)
    s_own = lax.dot_general(qb, own, contract_lanes, preferred_element_type=F32)
    s_own = s_own + jnp.where(key_idx <= t_idx, 0.0, NEG_INF)
    s_scr[:, n_pages * page:] = s_own
    m_part = s_own
    for n in range(n_blocks):
        add = sel[:, n:n + 1]
        for u in range(MOBA_BLOCK // page):
            c0 = n * MOBA_BLOCK + u * page
            sm = s_scr[:, c0:c0 + page] + add
            s_scr[:, c0:c0 + page] = sm
            m_part = jnp.maximum(m_part, sm)
    m = jnp.max(m_part, axis=1, keepdims=True)
    l_part = jnp.zeros((rows, page), F32)
    for n in range(n_pages + 1):
        c0 = n * page
        p = jnp.exp2(s_scr[:, c0:c0 + page] - m)
        l_part = l_part + p
        p_scr[:, c0:c0 + page] = p.astype(BF16)
    l = jnp.sum(l_part, axis=1, keepdims=True)
    vown = jnp.concatenate([vn_ref[0], jnp.zeros((page - SUBLANES, MOBA_W), F32)], axis=0).astype(BF16)
    acc = jnp.dot(p_scr[:, n_pages * page:], vown, preferred_element_type=F32)

    b_next = jnp.minimum(b + 1, last)
    for g0 in range(0, n_pages, group):
        for pg in range(g0, g0 + group):
            v_copy(b, pg).wait()
        for pg in range(g0, g0 + group):
            acc = acc + lax.dot_general(p_scr[:, pg * page:(pg + 1) * page], vring[pg].astype(BF16),
                                        contract_lanes, preferred_element_type=F32)
        for pg in range(g0, g0 + group):
            k_copy(b_next, pg).start()

    @pl.when(b == last)
    def _():
        for pg in range(n_pages):
            k_copy(b_next, pg).wait()

    o = acc / l
    out = jnp.zeros((SUBLANES, MOBA_W), F32)
    for h in range(MOBA_HEADS):
        out = out + jnp.where(lane // HEAD_DIM == h, o[h * SUBLANES:(h + 1) * SUBLANES, :], 0.0)
    o_ref[0] = (out * _silu(g_ref[0])).astype(o_ref.dtype)


def _moba_sample(page_table, mq3, mk3, mv3, mg3, cache_k4, cache_v4, layer):
    db = mq3.shape[0]
    n_pages = page_table.shape[1]
    page = cache_k4.shape[3]
    rows = MOBA_HEADS * SUBLANES
    assert n_pages * page // MOBA_BLOCK <= LANES and page == LANES
    small = pl.BlockSpec((1, SUBLANES, MOBA_W), lambda bi, pt: (bi, 0, 0))
    hbm = pl.BlockSpec(memory_space=pl.ANY)
    width = (n_pages + 1) * page
    return pl.pallas_call(
        functools.partial(_moba_sample_kernel, layer=layer, n_pages=n_pages, page=page),
        grid_spec=pltpu.PrefetchScalarGridSpec(
            num_scalar_prefetch=1,
            grid=(db,),
            in_specs=[small, small, small, small, hbm, hbm],
            out_specs=small,
            scratch_shapes=[
                pltpu.VMEM((n_pages, MOBA_W, page), F32),
                pltpu.VMEM((n_pages, MOBA_W, page), F32),
                pltpu.SemaphoreType.DMA((n_pages,)),
                pltpu.SemaphoreType.DMA((n_pages,)),
                pltpu.VMEM((rows, MOBA_W), F32),
                pltpu.VMEM((rows, width), F32),
                pltpu.VMEM((rows, width), BF16),
            ],
        ),
        out_shape=jax.ShapeDtypeStruct((db, SUBLANES, MOBA_W), BF16),
        compiler_params=_cparams(("arbitrary",)),
        name="moba_sample",
    )(page_table, mq3, mk3, mv3, mg3, cache_k4, cache_v4)


def _moba_sparse_kernel(pt_ref, q_ref, kn_ref, vn_ref, g_ref, ck_ref, cv_ref, o_ref,
                        kring, ksem, vsel, vsem, idx_v, idx_s, isem, qs_scr, s_scr, p_scr, pc_scr, tail_scr, *,
                        layer, n_pages, page, n_tokens):
    b = pl.program_id(0)
    n_batches = pl.num_programs(0) - 1
    last = n_batches - 1
    n_blocks = n_pages * page // MOBA_BLOCK
    ppb = MOBA_BLOCK // page
    slots = n_tokens * min(MOBA_TOPK, n_blocks)
    group = math.gcd(n_pages, 16)
    rows = MOBA_HEADS * SUBLANES
    contract_lanes = (((1,), (1,)), ((), ()))
    lane = lax.broadcasted_iota(jnp.int32, (SUBLANES, MOBA_W), 1)

    def k_copy(bi, pg):
        buf = bi % 2
        return pltpu.make_async_copy(ck_ref.at[layer, pt_ref[bi, pg]], kring.at[buf, pg], ksem.at[buf, pg])

    def v_copy(bi, h, slot, u, blk):
        src = cv_ref.at[layer, pt_ref[bi, blk * ppb + u], pl.ds(h * HEAD_DIM, HEAD_DIM), :]
        return pltpu.make_async_copy(src, vsel.at[h, slot * ppb + u], vsem.at[0])

    @pl.when(b == 0)
    def _():
        for pg in range(n_pages):
            k_copy(0, pg).start()

    @pl.when((b == 0) & (last >= 1))
    def _():
        for pg in range(n_pages):
            k_copy(1, pg).start()

    @pl.when(b <= last)
    def _():
        q = q_ref[0] * (HEAD_DIM ** -0.5 * LOG2_E)
        for h in range(MOBA_HEADS):
            qs_scr[h * SUBLANES:(h + 1) * SUBLANES, :] = jnp.where(lane // HEAD_DIM == h, q, 0.0)
        qb = qs_scr[...].astype(BF16)
        for g0 in range(0, n_pages, group):
            for pg in range(g0, g0 + group):
                k_copy(b, pg).wait()
            for pg in range(g0, g0 + group):
                s_scr[:, pg * page:(pg + 1) * page] = jnp.dot(qb, kring[b % 2, pg].astype(BF16),
                                                              preferred_element_type=F32)

    @pl.when(b + 2 <= last)
    def _():
        for pg in range(n_pages):
            k_copy(b + 2, pg).start()

    @pl.when(b >= 1)
    def _():
        for h in range(MOBA_HEADS):
            for j in range(slots):
                for u in range(ppb):
                    v_copy(0, h, j, u, 0).wait()
        o_sel = []
        for h in range(MOBA_HEADS):
            parts = [jnp.zeros((SUBLANES, HEAD_DIM), F32) for _ in range(4)]
            for piece in range(slots * ppb):
                parts[piece % 4] = parts[piece % 4] + lax.dot_general(
                    pc_scr[h, piece].astype(BF16), vsel[h, piece].astype(BF16), contract_lanes,
                    preferred_element_type=F32)
            o_sel.append((parts[0] + parts[1]) + (parts[2] + parts[3]))
        o_ref[0] = ((jnp.concatenate(o_sel, axis=1) + tail_scr[0]) / tail_scr[1] * tail_scr[2]).astype(o_ref.dtype)

    @pl.when(b <= last)
    def _():
        qb = qs_scr[...].astype(BF16)
        col = lax.broadcasted_iota(jnp.int32, (rows, LANES), 1)
        gate = jnp.zeros((rows, LANES), F32)
        for n in range(n_blocks):
            c0 = n * MOBA_BLOCK
            bsum = s_scr[:, c0:c0 + page]
            for u in range(1, ppb):
                bsum = bsum + s_scr[:, c0 + u * page:c0 + (u + 1) * page]
            gate = jnp.where(col == n, jnp.sum(bsum, axis=1, keepdims=True), gate)
        rank = jnp.zeros(gate.shape, jnp.int32)
        for jp in range(n_blocks):
            gj = gate[:, jp:jp + 1]
            rank = rank + jnp.where((gj > gate) | ((gj == gate) & (jp < col)), 1, 0)
        sel = jnp.where(rank < min(MOBA_TOPK, n_blocks), 0.0, NEG_INF)
        own = jnp.concatenate([kn_ref[0], jnp.zeros((page - SUBLANES, MOBA_W), F32)], axis=0).astype(BF16)
        t_idx = lax.broadcasted_iota(jnp.int32, (rows, page), 0) % SUBLANES
        key_idx = lax.broadcasted_iota(jnp.int32, (rows, page), 1)
        s_own = lax.dot_general(qb, own, contract_lanes, preferred_element_type=F32)
        s_own = s_own + jnp.where(key_idx <= t_idx, 0.0, NEG_INF)
        s_scr[:, n_pages * page:] = s_own
        m_part = s_own
        for n in range(n_blocks):
            add = sel[:, n:n + 1]
            for u in range(ppb):
                c0 = n * MOBA_BLOCK + u * page
                sm = s_scr[:, c0:c0 + page] + add
                s_scr[:, c0:c0 + page] = sm
                m_part = jnp.maximum(m_part, sm)
        m = jnp.max(m_part, axis=1, keepdims=True)
        l_part = jnp.zeros((rows, page), F32)
        for n in range(n_pages + 1):
            c0 = n * page
            p = jnp.exp2(s_scr[:, c0:c0 + page] - m)
            l_part = l_part + p
            p_scr[:, c0:c0 + page] = p
        l = jnp.sum(l_part, axis=1, keepdims=True)

        row8 = lax.broadcasted_iota(jnp.int32, (SUBLANES, LANES), 0)
        col8 = lax.broadcasted_iota(jnp.int32, (SUBLANES, LANES), 1)
        sel_real = jnp.where(t_idx < n_tokens, sel, NEG_INF)
        chosen = jnp.full((SUBLANES, LANES), NEG_INF, F32)
        for h in range(MOBA_HEADS):
            any_row = jnp.max(sel_real[h * SUBLANES:(h + 1) * SUBLANES], axis=0, keepdims=True)
            chosen = jnp.where(row8 == h, any_row, chosen)
        chosen = chosen == 0.0
        before = (lax.broadcasted_iota(jnp.int32, (LANES, LANES), 0)
                  < lax.broadcasted_iota(jnp.int32, (LANES, LANES), 1))
        pos = jnp.dot(jnp.where(chosen, 1.0, 0.0).astype(BF16), jnp.where(before, 1.0, 0.0).astype(BF16),
                      preferred_element_type=F32)
        count = jnp.sum(jnp.where(chosen, 1.0, 0.0), axis=1, keepdims=True)
        blk_id = col8.astype(F32)
        pick = lambda j: jnp.sum(jnp.where(chosen & (pos == j), blk_id, 0.0), axis=1, keepdims=True)
        first = pick(0)
        table = jnp.where(col8 == slots, count, 0.0)
        for j in range(slots):
            table = jnp.where(col8 == j, jnp.where(count > j, pick(j), first), table)
        idx_v[...] = table.astype(jnp.int32)
        to_smem = pltpu.make_async_copy(idx_v, idx_s, isem.at[0])
        to_smem.start()
        to_smem.wait()

        for h in range(MOBA_HEADS):
            for j in range(slots):
                for u in range(ppb):
                    v_copy(b, h, j, u, idx_s[h, j]).start(priority=1)
        for h in range(MOBA_HEADS):
            for j in range(slots):
                weight = jnp.where(j < idx_s[h, slots], 1.0, 0.0)
                for u in range(ppb):
                    c0 = pl.multiple_of((idx_s[h, j] * ppb + u) * page, page)
                    pc_scr[h, j * ppb + u] = p_scr[h * SUBLANES:(h + 1) * SUBLANES, pl.ds(c0, page)] * weight
        vown = jnp.concatenate([vn_ref[0], jnp.zeros((page - SUBLANES, MOBA_W), F32)], axis=0).astype(BF16)
        o_own = jnp.dot(p_scr[:, n_pages * page:].astype(BF16), vown, preferred_element_type=F32)
        own_rows = jnp.zeros((SUBLANES, MOBA_W), F32)
        l_rows = jnp.zeros((SUBLANES, MOBA_W), F32)
        for h in range(MOBA_HEADS):
            mine = lane // HEAD_DIM == h
            own_rows = jnp.where(mine, o_own[h * SUBLANES:(h + 1) * SUBLANES, :], own_rows)
            l_rows = jnp.where(mine, l[h * SUBLANES:(h + 1) * SUBLANES, :], l_rows)
        tail_scr[0] = own_rows
        tail_scr[1] = l_rows
        tail_scr[2] = _silu(g_ref[0])


def _moba_sparse(page_table, mq3, mk3, mv3, mg3, cache_k4, cache_v4, layer, n_tokens):
    db = mq3.shape[0]
    n_pages = page_table.shape[1]
    page = cache_k4.shape[3]
    rows = MOBA_HEADS * SUBLANES
    n_blocks = n_pages * page // MOBA_BLOCK
    slots = n_tokens * min(MOBA_TOPK, n_blocks)
    assert n_blocks <= LANES and page == LANES and slots < LANES and MOBA_HEADS <= SUBLANES
    blk3 = (1, SUBLANES, MOBA_W)
    this = pl.BlockSpec(blk3, lambda bi, pt: (jnp.minimum(bi, db - 1), 0, 0))
    before = pl.BlockSpec(blk3, lambda bi, pt: (jnp.maximum(bi - 1, 0), 0, 0))
    hbm = pl.BlockSpec(memory_space=pl.ANY)
    width = (n_pages + 1) * page
    pieces = slots * MOBA_BLOCK // page
    return pl.pallas_call(
        functools.partial(_moba_sparse_kernel, layer=layer, n_pages=n_pages, page=page, n_tokens=n_tokens),
        grid_spec=pltpu.PrefetchScalarGridSpec(
            num_scalar_prefetch=1,
            grid=(db + 1,),
            in_specs=[this, this, this, this, hbm, hbm],
            out_specs=before,
            scratch_shapes=[
                pltpu.VMEM((2, n_pages, MOBA_W, page), F32),
                pltpu.SemaphoreType.DMA((2, n_pages)),
                pltpu.VMEM((MOBA_HEADS, pieces, HEAD_DIM, page), F32),
                pltpu.SemaphoreType.DMA((1,)),
                pltpu.VMEM((SUBLANES, LANES), jnp.int32),
                pltpu.SMEM((SUBLANES, LANES), jnp.int32),
                pltpu.SemaphoreType.DMA((1,)),
                pltpu.VMEM((rows, MOBA_W), F32),
                pltpu.VMEM((rows, width), F32),
                pltpu.VMEM((rows, width), F32),
                pltpu.VMEM((MOBA_HEADS, pieces, SUBLANES, page), F32),
                pltpu.VMEM((3, SUBLANES, MOBA_W), F32),
            ],
        ),
        out_shape=jax.ShapeDtypeStruct((db, SUBLANES, MOBA_W), BF16),
        compiler_params=_cparams(("arbitrary",)),
        name="moba_sample",
    )(page_table, mq3, mk3, mv3, mg3, cache_k4, cache_v4)


def _pool_kernel(u_ref, g_ref, buf_ref, cnt_ref, w_ref, b_ref, sc_ref, o_ref, bufo_ref, x_scr, w_scr, *,
                 n_tokens, rows):
    bb, length, _ = u_ref.shape
    lead = SUBLANES
    halo = lead + POOL_BUF + 1
    assert POOL_WINDOWS == (2, 4, 8, 16) and halo % SUBLANES == 0
    tiles = [(lead, halo - lead)] + [(halo + r0, rows) for r0 in range(0, length, rows)]
    for bi in range(bb):
        x_scr[bi, 0:lead, :] = jnp.zeros((lead, POOL_W), F32)
        x_scr[bi, lead:halo, :] = buf_ref[bi]
        x_scr[bi, halo:, :] = u_ref[bi]
        w_scr[bi, :, 0:lead, :] = jnp.zeros((3, lead, POOL_W), F32)
        for i0, n in tiles:
            group = lax.broadcasted_iota(jnp.int32, (n, POOL_W), 1) // POOL_GC
            u = x_scr[bi, i0:i0 + n, :]
            w2 = u + x_scr[bi, i0 - 1:i0 - 1 + n, :]
            w_scr[bi, 0, i0:i0 + n, :] = w2
            w4 = w2 + w_scr[bi, 0, i0 - 2:i0 - 2 + n, :]
            w_scr[bi, 1, i0:i0 + n, :] = w4
            w8 = w4 + w_scr[bi, 1, i0 - 4:i0 - 4 + n, :]
            w_scr[bi, 2, i0:i0 + n, :] = w8
            if i0 < halo:
                continue
            w16 = w8 + w_scr[bi, 2, i0 - 8:i0 - 8 + n, :]
            r0 = i0 - halo
            win = jnp.where(group == 0, w2, jnp.where(group == 1, w4, jnp.where(group == 2, w8, w16)))
            pooled = win / cnt_ref[r0:r0 + n, :] - u
            y = jnp.dot(pooled.astype(BF16), w_ref[...], preferred_element_type=F32) + b_ref[...]
            y = y * sc_ref[...]
            o_ref[bi, r0:r0 + n, :] = (y * _silu(g_ref[bi, r0:r0 + n, :])).astype(o_ref.dtype)
        bufo_ref[bi] = x_scr[bi, lead + n_tokens + 1:lead + n_tokens + 1 + POOL_BUF, :]


def _pool(zp3, buf16, cnt, w_bd, bias, scale, *, n_tokens, rows, bb):
    b, length, _ = zp3.shape
    const = lambda bi: (0, 0)
    n_rows = SUBLANES + POOL_BUF + 1 + length
    return pl.pallas_call(
        functools.partial(_pool_kernel, n_tokens=n_tokens, rows=rows),
        grid=(b // bb,),
        in_specs=[
            pl.BlockSpec((bb, length, POOL_W), lambda bi: (bi, 0, 0)),
            pl.BlockSpec((bb, length, POOL_W), lambda bi: (bi, 0, 1)),
            pl.BlockSpec((bb, POOL_BUF + 1, POOL_W), lambda bi: (bi, 0, 0)),
            pl.BlockSpec((length, POOL_W), const),
            pl.BlockSpec((POOL_W, POOL_W), const),
            pl.BlockSpec((1, POOL_W), const),
            pl.BlockSpec((1, POOL_W), const),
        ],
        out_specs=[
            pl.BlockSpec((bb, length, POOL_W), lambda bi: (bi, 0, 0)),
            pl.BlockSpec((bb, POOL_BUF, POOL_W), lambda bi: (bi, 0, 0)),
        ],
        out_shape=[
            jax.ShapeDtypeStruct((b, length, POOL_W), BF16),
            jax.ShapeDtypeStruct((b, POOL_BUF, POOL_W), F32),
        ],
        scratch_shapes=[pltpu.VMEM((bb, n_rows, POOL_W), F32), pltpu.VMEM((bb, 3, n_rows, POOL_W), F32)],
        compiler_params=_cparams(("arbitrary",)),
        name="pool",
    )(zp3, zp3, buf16, cnt, w_bd, bias, scale)


def _out_proj_kernel(mr_ref, mm_ref, mp_ref, x_ref, w_ref, y_ref, mix_scr):
    mix_scr[:, :RET_W] = mr_ref[...]
    mix_scr[:, RET_W:RET_W + MOBA_W] = mm_ref[...]
    mix_scr[:, RET_W + MOBA_W:] = mp_ref[...]
    y_ref[...] = x_ref[...] + jnp.dot(mix_scr[...], w_ref[...], preferred_element_type=F32)


def _out_proj(mr, mm, mp, x2d, w_bf, *, tm, layer):
    m, d = x2d.shape
    d_mix = RET_W + MOBA_W + POOL_W
    row = lambda i: (i, 0)
    return pl.pallas_call(
        _out_proj_kernel,
        grid=(m // tm,),
        in_specs=[
            pl.BlockSpec((tm, RET_W), row),
            pl.BlockSpec((tm, MOBA_W), row),
            pl.BlockSpec((tm, POOL_W), row),
            pl.BlockSpec((tm, d), row),
            pl.BlockSpec((None, d_mix, d), lambda i: (layer, 0, 0)),
        ],
        out_specs=pl.BlockSpec((tm, d), row),
        out_shape=jax.ShapeDtypeStruct((m, d), F32),
        scratch_shapes=[pltpu.VMEM((tm, d_mix), BF16)],
        compiler_params=_cparams(("arbitrary",)),
        name="out_proj",
    )(mr, mm, mp, x2d, w_bf)


def _rope_tables(pos):
    inv = 1.0 / (ROPE_THETA ** (np.arange(HALF, dtype=np.float64) / HALF))
    ang = np.asarray(pos, np.float64)[:, None] * inv[None, :]
    c, s = np.cos(ang), np.sin(ang)
    return (jnp.asarray(np.concatenate([c, c, c, c], axis=-1), F32),
            jnp.asarray(np.concatenate([-s, s, -s, s], axis=-1), F32))


def _block_diag(blocks):
    g, n, _ = blocks.shape
    eye = jnp.eye(g, dtype=blocks.dtype)
    return (eye[:, None, :, None] * blocks[:, :, None, :]).reshape(g * n, g * n)


def _pool_counts(pos0, length):
    pos = pos0 + np.arange(length)
    w = np.repeat(np.asarray(POOL_WINDOWS), POOL_GC)
    return jnp.asarray(np.minimum(pos[:, None] + 1, w[None, :]), F32)


def _layer(x3, pos_tabs, ret_state, ret_prev, ret_tabs, ret_step, pool_buf16, pool_cnt, n_tokens, moba_fn, params, *,
           tm, layer):
    b, length, d = x3.shape
    nw, w_in, w_out, gnw, qnw, knw, bd, pw_bd, pbias, pscale = params
    x2d = x3.reshape(b * length, d)
    zr, mq, mk, mv, mg, zp = _in_proj(x2d, nw, w_in, pos_tabs[0], pos_tabs[1], qnw, knw, bd, tm=tm, layer=layer)
    three = lambda a: a.reshape(b, length, a.shape[-1])
    mix_r, states = _retention(three(zr), ret_state, ret_prev, ret_tabs, gnw, chunk=RET_CHUNK, bb=ret_step[0],
                               pairs=ret_step[1], layer=layer)
    mix_m, k_new, v_new = moba_fn(three(mq), three(mk), three(mv), three(mg))
    mix_p, buf_new = _pool(three(zp), pool_buf16, pool_cnt, pw_bd, pbias, pscale,
                           n_tokens=n_tokens, rows=min(length, 128), bb=ret_step[0])
    two = lambda a: a.reshape(b * length, a.shape[-1])
    tm_out = 2 * tm if (b * length) % (2 * tm) == 0 else tm
    y = _out_proj(two(mix_r), two(mix_m), two(mix_p), x2d, w_out, tm=tm_out, layer=layer)
    return y.reshape(b, length, d), states, k_new, v_new, buf_new


def _moba_prompt_wrap(mq3, mk3, mv3, mg3, *, prev_kv):
    return _moba_prompt(mq3, mk3, mv3, mg3, prev_kv)


def _moba_sample_wrap(mq3, mk3, mv3, mg3, *, page_table, cache_k4, cache_v4, layer, n_tokens):
    mix = _moba_sparse(page_table, mq3, mk3, mv3, mg3, cache_k4, cache_v4, layer, n_tokens)
    rows = lambda t: t[:, :n_tokens].reshape(t.shape[0], n_tokens, MOBA_HEADS, HEAD_DIM)
    return mix, rows(mk3), rows(mv3)


def kernel(x_prompt, x_sample, cache_k, cache_v, state_ret, state_pool, page_table, norm_w, w_in, w_out, ret_gn_w, q_norm_w, k_norm_w, pool_w, pool_b, pool_scale):
    depth = w_in.shape[0]
    b, seq, d = x_prompt.shape
    db, dec_seq, _ = x_sample.shape
    n_pool, page = cache_k.shape[1], cache_k.shape[2]
    past_len = page_table.shape[1] * page
    assert dec_seq <= SUBLANES and seq % MOBA_BLOCK == 0 and seq % RET_CHUNK == 0 and past_len % MOBA_BLOCK == 0

    p_tabs = _rope_tables(np.arange(seq))
    tm_s = db * SUBLANES
    s_tabs = _rope_tables(np.tile(past_len + np.arange(SUBLANES), db))
    p_ret_tabs = _retention_tables(RET_CHUNK, RET_CHUNK)
    s_ret_tabs = _retention_tables(RET_CHUNK, dec_seq)
    p_cnt = _pool_counts(0, seq)
    s_cnt = _pool_counts(past_len, SUBLANES)
    head_of = np.arange(MOBA_W) // HEAD_DIM
    bd = jnp.asarray((head_of[:, None] == head_of[None, :]) / HEAD_DIM, BF16)
    zero_buf = jnp.zeros((b, POOL_BUF + 1, POOL_W), F32)

    cache_k4 = cache_k.transpose(0, 1, 3, 4, 2).reshape(depth, n_pool, MOBA_W, page)
    cache_v4 = cache_v.transpose(0, 1, 3, 4, 2).reshape(depth, n_pool, MOBA_W, page)
    xp = x_prompt
    xs = jnp.pad(x_sample, ((0, 0), (0, SUBLANES - dec_seq), (0, 0)))

    w_in_bf = w_in.astype(BF16)
    w_out_bf = w_out.astype(BF16)
    outs = [[] for _ in range(4)]
    prev_kv = s_p = s_s = None
    for l in range(depth):
        params = (norm_w[l].reshape(1, d), w_in_bf, w_out_bf,
                  ret_gn_w[l].reshape(RET_W // LANES, 1, LANES),
                  jnp.tile(q_norm_w[l], MOBA_HEADS).reshape(1, MOBA_W),
                  jnp.tile(k_norm_w[l], MOBA_HEADS).reshape(1, MOBA_W),
                  bd, _block_diag(pool_w[l]).astype(BF16),
                  pool_b[l].reshape(1, POOL_W), pool_scale[l].reshape(1, POOL_W))
        xp, s_p, kt, vt, b_p = _layer(xp, p_tabs, None, s_p, p_ret_tabs, (1, 1), zero_buf, p_cnt, seq,
                                      functools.partial(_moba_prompt_wrap, prev_kv=prev_kv), params,
                                      tm=512, layer=l)
        prev_kv = (kt, vt)
        moba_s = functools.partial(_moba_sample_wrap, page_table=page_table, cache_k4=cache_k4,
                                   cache_v4=cache_v4, layer=l, n_tokens=dec_seq)
        xs, s_s, k_s, v_s, b_s = _layer(xs, s_tabs, state_ret, s_s, s_ret_tabs,
                                        (math.gcd(db, 8), RET_W // LANES),
                                        jnp.pad(state_pool[l], ((0, 0), (1, 0), (0, 0))), s_cnt, dec_seq,
                                        moba_s, params, tm=tm_s, layer=l)
        for lst, val in zip(outs, (k_s, v_s, b_p, b_s)):
            lst.append(val)
    rows = lambda t: t.reshape(depth, b, MOBA_HEADS, HEAD_DIM, seq).transpose(0, 1, 4, 2, 3)
    k_s, v_s, b_p, b_s = (jnp.stack(o) for o in outs)
    return xp, xs[:, :dec_seq], rows(prev_kv[0]), rows(prev_kv[1]), k_s, v_s, s_p, s_s, b_p, b_s
```

```python
import functools
import math

import jax
import jax.numpy as jnp
import numpy as np
from jax import lax
from jax.experimental import pallas as pl
from jax.experimental.pallas import tpu as pltpu

F32 = jnp.float32
BF16 = jnp.bfloat16

HEAD_DIM = 64
HALF = HEAD_DIM // 2
LANES = 128
RET_HEADS = 6
MOBA_HEADS = 6
RET_W = RET_HEADS * HEAD_DIM
MOBA_W = MOBA_HEADS * HEAD_DIM
POOL_W = 256
POOL_GC = 64
POOL_WINDOWS = (2, 4, 8, 16)
POOL_BUF = 15
RET_CHUNK = 128
MOBA_BLOCK = 256
MOBA_TOPK = 3
ROPE_THETA = 10000.0
EPS = 1e-6
SUBLANES = 8
NEG_INF = float("-inf")
LOG2_E = 1.4426950408889634

OFF_RQ, OFF_RV, OFF_MQ, OFF_MV, OFF_PU = 0, 2 * RET_W, 4 * RET_W, 4 * RET_W + 2 * MOBA_W, 4 * RET_W + 4 * MOBA_W
D_IN = OFF_PU + 2 * POOL_W

VMEM_LIMIT = 56 * 1024 * 1024


def _silu(x):
    return x / (1.0 + jnp.exp(-x))


def _cparams(sem):
    return pltpu.CompilerParams(dimension_semantics=sem, vmem_limit_bytes=VMEM_LIMIT)


def _in_proj_kernel(x_ref, nw_ref, w_ref, cos_ref, sin_ref, qnw_ref, knw_ref, bd_ref,
                    zr_ref, mq_ref, mk_ref, mv_ref, mg_ref, zp_ref, h_scr, z_scr):
    tm = x_ref.shape[0]
    x = x_ref[...]
    ms = jnp.mean(x * x, axis=-1, keepdims=True)
    h_scr[...] = (x * lax.rsqrt(ms + EPS) * nw_ref[...]).astype(BF16)
    cos = cos_ref[...]
    sin = sin_ref[...]
    lane = lax.broadcasted_iota(jnp.int32, (tm, LANES), 1)
    first_half = (lane & HALF) == 0

    def rope(z):
        partner = jnp.where(first_half, pltpu.roll(z, LANES - HALF, 1), pltpu.roll(z, HALF, 1))
        return z * cos + partner * sin

    def proj(off, width):
        return jnp.dot(h_scr[...], w_ref[:, off:off + width], preferred_element_type=F32)

    z_scr[...] = proj(OFF_RQ, 2 * RET_W)
    for g in range(2 * RET_W // LANES):
        out = rope(z_scr[:, g * LANES:(g + 1) * LANES])
        if g >= RET_W // LANES:
            out = out * (HEAD_DIM ** -0.5)
        zr_ref[:, g * LANES:(g + 1) * LANES] = out
    zr_ref[:, OFF_RV:OFF_MQ] = proj(OFF_RV, 2 * RET_W)
    z_scr[...] = proj(OFF_MQ, 2 * MOBA_W)
    for t, (nw, dst) in enumerate(((qnw_ref, mq_ref), (knw_ref, mk_ref))):
        z = z_scr[:, t * MOBA_W:(t + 1) * MOBA_W]
        msq = jnp.dot((z * z).astype(BF16), bd_ref[...], preferred_element_type=F32)
        z_scr[:, t * MOBA_W:(t + 1) * MOBA_W] = z * lax.rsqrt(msq + EPS) * nw[...]
        for g in range(MOBA_W // LANES):
            c0 = t * MOBA_W + g * LANES
            dst[:, g * LANES:(g + 1) * LANES] = rope(z_scr[:, c0:c0 + LANES])
    z_scr[...] = proj(OFF_MV, 2 * MOBA_W)
    mv_ref[...] = z_scr[:, :MOBA_W]
    mg_ref[...] = z_scr[:, MOBA_W:]
    zp_ref[...] = proj(OFF_PU, 2 * POOL_W)


def _in_proj(x2d, nw, w_bf, cos, sin, qnw, knw, bd, *, tm, layer):
    m, d = x2d.shape
    n_pos = cos.shape[0] // tm
    row = lambda i: (i, 0)
    const = lambda i: (0, 0)
    pos = lambda i: (i % n_pos, 0)
    outs = [(2 * RET_W + 2 * RET_W), MOBA_W, MOBA_W, MOBA_W, MOBA_W, 2 * POOL_W]
    return pl.pallas_call(
        _in_proj_kernel,
        grid=(m // tm,),
        in_specs=[
            pl.BlockSpec((tm, d), row),
            pl.BlockSpec((1, d), const),
            pl.BlockSpec((None, d, D_IN), lambda i: (layer, 0, 0)),
            pl.BlockSpec((tm, LANES), pos),
            pl.BlockSpec((tm, LANES), pos),
            pl.BlockSpec((1, MOBA_W), const),
            pl.BlockSpec((1, MOBA_W), const),
            pl.BlockSpec((MOBA_W, MOBA_W), const),
        ],
        out_specs=[pl.BlockSpec((tm, w), row) for w in outs],
        out_shape=[jax.ShapeDtypeStruct((m, w), F32) for w in outs],
        scratch_shapes=[pltpu.VMEM((tm, d), BF16), pltpu.VMEM((tm, 2 * RET_W), F32)],
        compiler_params=_cparams(("arbitrary",)),
        name="in_proj",
    )(x2d, nw, w_bf, cos, sin, qnw, knw, bd)


def _retention_kernel(*refs, chunk, has_state, n_prev):
    q_ref, k_ref, v_ref, g_ref = refs[:4]
    s0_ref = refs[4] if has_state else None
    prev_ref = refs[4 + has_state] if n_prev else None
    dm_ref, rs_ref, kd_ref, gc_ref, gnw_ref, o_ref, sout_ref, oi_scr, kv_scr, sb_scr = refs[4 + has_state + (n_prev > 0):]
    if n_prev:
        sout_ref[0:n_prev] = prev_ref[...]
    bb, length, width = q_ref.shape
    pairs = width // LANES
    rows_in = min(length, chunk)
    n_chunks = max(1, length // chunk)
    lane = lax.broadcasted_iota(jnp.int32, (chunk, LANES), 1)
    head0 = lane < HEAD_DIM
    r = lax.broadcasted_iota(jnp.int32, (LANES, LANES), 0)
    c = lax.broadcasted_iota(jnp.int32, (LANES, LANES), 1)
    same_head = (r < HEAD_DIM) == (c < HEAD_DIM)
    contract_lanes = (((1,), (1,)), ((), ()))

    def head_mean(x):
        m0 = jnp.sum(jnp.where(head0, x, 0.0), axis=1, keepdims=True)
        m1 = jnp.sum(jnp.where(head0, 0.0, x), axis=1, keepdims=True)
        return jnp.where(head0, m0, m1) * (1.0 / HEAD_DIM)

    def load(ref, bi, p, ci):
        a = ref[bi, ci * chunk:ci * chunk + rows_in, p * LANES:(p + 1) * LANES]
        if rows_in < chunk:
            a = jnp.concatenate([a, jnp.zeros((chunk - rows_in, LANES), F32)], axis=0)
        return a

    items = [(bi, p, ci) for bi in range(bb) for p in range(pairs) for ci in range(n_chunks)]

    def a1(bi, p, ci):
        q = load(q_ref, bi, p, ci)
        k = load(k_ref, bi, p, ci)
        kb = k.astype(BF16)
        vb = load(v_ref, bi, p, ci).astype(BF16)
        q0 = jnp.where(head0, q, 0.0).astype(BF16)
        q1 = jnp.where(head0, 0.0, q).astype(BF16)
        in0 = (lax.dot_general(q0, kb, contract_lanes, preferred_element_type=F32) * dm_ref[2 * p]).astype(BF16)
        in1 = (lax.dot_general(q1, kb, contract_lanes, preferred_element_type=F32) * dm_ref[2 * p + 1]).astype(BF16)
        return bi, p, ci, k, vb, in0, in1

    def a2(st):
        bi, p, ci, k, vb, in0, in1 = st
        slot = bi * pairs + p
        oi_scr[slot, ci * chunk:(ci + 1) * chunk, :] = jnp.where(
            head0, jnp.dot(in0, vb, preferred_element_type=F32), jnp.dot(in1, vb, preferred_element_type=F32))
        kdt = (k * kd_ref[p]).T.astype(BF16)
        kv_scr[slot, ci] = jnp.where(same_head, jnp.dot(kdt, vb, preferred_element_type=F32), 0.0)

    st = None
    for item in items + [None]:
        nxt = a1(*item) if item is not None else None
        if st is not None:
            a2(st)
        st = nxt

    zero = jnp.zeros((HEAD_DIM, HEAD_DIM), F32)
    for bi in range(bb):
        for p in range(pairs):
            slot = bi * pairs + p
            if s0_ref is None:
                s = jnp.zeros((LANES, LANES), F32)
            else:
                s = jnp.concatenate([jnp.concatenate([s0_ref[bi, 2 * p], zero], axis=1),
                                     jnp.concatenate([zero, s0_ref[bi, 2 * p + 1]], axis=1)], axis=0)
            for ci in range(n_chunks):
                sb_scr[slot, ci] = s.astype(BF16)
                s = s * gc_ref[p] + kv_scr[slot, ci]
            sout_ref[n_prev, bi, 2 * p] = s[:HEAD_DIM, :HEAD_DIM]
            sout_ref[n_prev, bi, 2 * p + 1] = pltpu.roll(s[HEAD_DIM:], HEAD_DIM, 1)[:, :HEAD_DIM]

    def c1(bi, p, ci):
        slot = bi * pairs + p
        qb = load(q_ref, bi, p, ci).astype(BF16)
        o = (oi_scr[slot, ci * chunk:(ci + 1) * chunk, :]
             + jnp.dot(qb, sb_scr[slot, ci], preferred_element_type=F32) * rs_ref[p])
        return bi, p, ci, o, head_mean(o)

    def c2(st):
        bi, p, ci, o, mu = st
        oc = o - mu
        return bi, p, ci, oc, head_mean(oc * oc)

    def c3(st):
        bi, p, ci, oc, var = st
        on = oc * lax.rsqrt(var + EPS) * gnw_ref[p]
        out = on * _silu(load(g_ref, bi, p, ci))
        o_ref[bi, ci * chunk:ci * chunk + rows_in, p * LANES:(p + 1) * LANES] = out[:rows_in].astype(o_ref.dtype)

    s1 = s2 = None
    for item in items + [None, None]:
        new1 = c1(*item) if item is not None else None
        new2 = c2(s1) if s1 is not None else None
        if s2 is not None:
            c3(s2)
        s1, s2 = new1, new2


def _retention(zr3, state, prev_states, tabs, gnw, *, chunk, bb, pairs, layer):
    b, length, _ = zr3.shape
    n_pairs = RET_W // LANES
    dm, rs, kd, gc = tabs
    w = pairs * LANES
    n_chunks = max(1, length // chunk)
    n_prev = 0 if prev_states is None else prev_states.shape[0]
    col = lambda off: (lambda bi, p: (bi, 0, off + p))
    tab = lambda bi, p: (p, 0, 0)
    heads_blk = (bb, 2 * pairs, HEAD_DIM, HEAD_DIM)
    stacked = lambda n: pl.BlockSpec((n,) + heads_blk, lambda bi, p: (0, bi, p, 0, 0))
    extra_specs, extra_args = [], []
    if state is not None:
        extra_specs.append(pl.BlockSpec((None,) + heads_blk, lambda bi, p: (layer, bi, p, 0, 0)))
        extra_args.append(state)
    if n_prev:
        extra_specs.append(stacked(n_prev))
        extra_args.append(prev_states)
    return pl.pallas_call(
        functools.partial(_retention_kernel, chunk=chunk, has_state=state is not None, n_prev=n_prev),
        grid=(b // bb, n_pairs // pairs),
        in_specs=[
            pl.BlockSpec((bb, length, w), col(0)),
            pl.BlockSpec((bb, length, w), col(n_pairs // pairs)),
            pl.BlockSpec((bb, length, w), col(2 * n_pairs // pairs)),
            pl.BlockSpec((bb, length, w), col(3 * n_pairs // pairs)),
        ] + extra_specs + [
            pl.BlockSpec((2 * pairs, chunk, chunk), tab),
            pl.BlockSpec((pairs, chunk, LANES), tab),
            pl.BlockSpec((pairs, chunk, LANES), tab),
            pl.BlockSpec((pairs, 1, LANES), tab),
            pl.BlockSpec((pairs, 1, LANES), tab),
        ],
        out_specs=[pl.BlockSpec((bb, length, w), lambda bi, p: (bi, 0, p)), stacked(n_prev + 1)],
        out_shape=[
            jax.ShapeDtypeStruct((b, length, RET_W), BF16),
            jax.ShapeDtypeStruct((n_prev + 1, b, RET_HEADS, HEAD_DIM, HEAD_DIM), F32),
        ],
        scratch_shapes=[
            pltpu.VMEM((bb * pairs, n_chunks * chunk, LANES), F32),
            pltpu.VMEM((bb * pairs, n_chunks, LANES, LANES), F32),
            pltpu.VMEM((bb * pairs, n_chunks, LANES, LANES), BF16),
        ],
        compiler_params=_cparams(("arbitrary", "arbitrary")),
        name="retention",
    )(zr3, zr3, zr3, zr3, *extra_args, dm, rs, kd, gc, gnw)


def _retention_tables(chunk, n_tokens):
    lg = np.log(1.0 - 2.0 ** (-5.0 - np.arange(RET_HEADS, dtype=np.float64)))
    i = np.arange(chunk, dtype=np.float64)
    rel = i[:, None] - i[None, :]
    dm = np.where(rel[None] >= 0, np.exp(rel[None] * lg[:, None, None]), 0.0)
    lg_lanes = np.repeat(lg, HEAD_DIM).reshape(RET_W // LANES, 1, LANES)
    rs = np.exp((i + 1.0)[None, :, None] * lg_lanes)
    kd = np.where((i < n_tokens)[None, :, None], np.exp((n_tokens - 1.0 - i)[None, :, None] * lg_lanes), 0.0)
    gc = np.exp(float(n_tokens) * lg_lanes)
    return tuple(jnp.asarray(t, F32) for t in (dm, rs, kd, gc))


def _moba_prompt_kernel(*refs, n_blocks, n_prev):
    q_ref, k_ref, v_ref, g_ref = refs[:4]
    prev = refs[4:6] if n_prev else ()
    o_ref, kt_ref, vt_ref, kh_scr, vt_scr, km_scr, s_scr, p_scr = refs[4 + len(prev):]
    blk = MOBA_BLOCK
    sub = blk
    contract_lanes = (((1,), (1,)), ((), ()))
    lane = lax.broadcasted_iota(jnp.int32, (blk, LANES), 1)
    head0 = lane < HEAD_DIM
    for j in range(n_blocks):
        kj = k_ref[0, j * blk:(j + 1) * blk, :]
        kh_scr[0, j * blk:(j + 1) * blk, :] = jnp.where(head0, kj, 0.0).astype(BF16)
        kh_scr[1, j * blk:(j + 1) * blk, :] = jnp.where(head0, 0.0, kj).astype(BF16)
        vt = v_ref[0, j * blk:(j + 1) * blk, :].T
        vt_scr[:, j * blk:(j + 1) * blk] = vt.astype(BF16)
        km_scr[j:j + 1, :] = jnp.sum(kj, axis=0, keepdims=True) * (1.0 / blk)
        kt_ref[n_prev, 0, :, j * blk:(j + 1) * blk] = kj.T
        vt_ref[n_prev, 0, :, j * blk:(j + 1) * blk] = vt
    if n_prev:
        kt_ref[0:n_prev] = prev[0][...]
        vt_ref[0:n_prev] = prev[1][...]

    km = km_scr[...]
    lane8 = lax.broadcasted_iota(jnp.int32, (n_blocks, LANES), 1)
    row8 = lax.broadcasted_iota(jnp.int32, (n_blocks, blk), 0)
    krow = lax.broadcasted_iota(jnp.int32, (blk, blk), 0)
    qcol = lax.broadcasted_iota(jnp.int32, (blk, blk), 1)
    causal = jnp.where(krow <= qcol, 0.0, NEG_INF)
    fold = lambda t: t.reshape(t.shape[0] // SUBLANES, SUBLANES, blk)
    km_parts = []
    for h in range(2):
        kmh = jnp.where((lane8 < HEAD_DIM) == (h == 0), km, 0.0)
        kmh_hi = kmh.astype(BF16)
        km_parts.append((kmh_hi, (kmh - kmh_hi.astype(F32)).astype(BF16)))

    heads = range(2)
    tiles = lambda i: [(j, u, h) for j in range(i + 1) for u in range(blk // sub) for h in heads]

    def begin(i):
        q = q_ref[0, i * blk:(i + 1) * blk, :] * (HEAD_DIM ** -0.5 * LOG2_E)
        qh = q.astype(BF16)
        selmask = [None, None]
        if i > MOBA_TOPK:
            ql = (q - qh.astype(F32)).astype(BF16)
            for h in heads:
                kmh_hi, kmh_lo = km_parts[h]
                gate = (lax.dot_general(kmh_hi, qh, contract_lanes, preferred_element_type=F32)
                        + lax.dot_general(kmh_lo, qh, contract_lanes, preferred_element_type=F32)
                        + lax.dot_general(kmh_hi, ql, contract_lanes, preferred_element_type=F32))
                rank = jnp.zeros((n_blocks, blk), jnp.int32)
                for jp in range(i):
                    gj = gate[jp:jp + 1, :]
                    rank = rank + jnp.where((gj > gate) | ((gj == gate) & (jp < row8)), 1, 0)
                selmask[h] = jnp.where((rank < MOBA_TOPK) & (row8 < i), 0.0, NEG_INF)
        return dict(i=i, buf=i % 2, qh=qh, selmask=selmask, m8=[None, None], m=None,
                    l8=[jnp.zeros((SUBLANES, blk), F32) for _ in heads])

    def score_tile(st, j, u, h):
        i = st["i"]
        rows = slice(j * blk + u * sub, j * blk + (u + 1) * sub)
        s = lax.dot_general(kh_scr[h, rows, :], st["qh"], contract_lanes, preferred_element_type=F32)
        if j == i:
            s = s + causal[u * sub:(u + 1) * sub]
        s_scr[st["buf"], h, rows, :] = s
        tmax = jnp.max(fold(s), axis=0)
        if j < i and st["selmask"][h] is not None:
            tmax = tmax + st["selmask"][h][j:j + 1, :]
        st["m8"][h] = tmax if st["m8"][h] is None else jnp.maximum(st["m8"][h], tmax)

    def prob_tile(st, j, u, h):
        i = st["i"]
        if st["m"] is None:
            st["m"] = [jnp.max(st["m8"][hh], axis=0, keepdims=True) for hh in heads]
        rows = slice(j * blk + u * sub, j * blk + (u + 1) * sub)
        sm = st["selmask"][h]
        shift = -st["m"][h] if (j == i or sm is None) else sm[j:j + 1, :] - st["m"][h]
        p = jnp.exp2(s_scr[st["buf"], h, rows, :] + shift)
        st["l8"][h] = st["l8"][h] + jnp.sum(fold(p), axis=0)
        p_scr[st["buf"], h, rows, :] = p.astype(BF16)

    def finish(st):
        i = st["i"]
        nk = (i + 1) * blk
        halves = []
        for h in heads:
            l = jnp.sum(st["l8"][h], axis=0, keepdims=True)
            ot = jnp.dot(vt_scr[h * HEAD_DIM:(h + 1) * HEAD_DIM, 0:nk], p_scr[st["buf"], h, 0:nk, :],
                         preferred_element_type=F32)
            halves.append(ot / l)
        ot = jnp.concatenate(halves, axis=0)
        o_ref[0, i * blk:(i + 1) * blk, :] = (ot.T * _silu(g_ref[0, i * blk:(i + 1) * blk, :])).astype(o_ref.dtype)

    cur = begin(0)
    for t in tiles(0):
        score_tile(cur, *t)
    prev = None
    for i in range(n_blocks):
        nxt = begin(i + 1) if i + 1 < n_blocks else None
        a, b = tiles(i), (tiles(i + 1) if nxt is not None else [])
        for k in range(max(len(a), len(b))):
            if k == 2 and prev is not None:
                finish(prev)
                prev = None
            if k < len(b):
                score_tile(nxt, *b[k])
            if k < len(a):
                prob_tile(cur, *a[k])
        if prev is not None:
            finish(prev)
        prev, cur = cur, nxt
    finish(prev)


def _moba_prompt(mq3, mk3, mv3, mg3, prev_kv):
    b, length, _ = mq3.shape
    n_pairs = MOBA_W // LANES
    n_blocks = length // MOBA_BLOCK
    n_prev = 0 if prev_kv is None else prev_kv[0].shape[0]
    rows = pl.BlockSpec((1, length, LANES), lambda bi, p: (bi, 0, p))
    cols = lambda n: pl.BlockSpec((n, 1, LANES, length), lambda bi, p: (0, bi, p, 0))
    stacked = jax.ShapeDtypeStruct((n_prev + 1, b, MOBA_W, length), F32)
    return pl.pallas_call(
        functools.partial(_moba_prompt_kernel, n_blocks=n_blocks, n_prev=n_prev),
        grid=(b, n_pairs),
        in_specs=[rows, rows, rows, rows] + ([cols(n_prev), cols(n_prev)] if n_prev else []),
        out_specs=[rows, cols(n_prev + 1), cols(n_prev + 1)],
        out_shape=[jax.ShapeDtypeStruct((b, length, MOBA_W), BF16), stacked, stacked],
        scratch_shapes=[
            pltpu.VMEM((2, length, LANES), BF16),
            pltpu.VMEM((LANES, length), BF16),
            pltpu.VMEM((n_blocks, LANES), F32),
            pltpu.VMEM((2, 2, length, MOBA_BLOCK), F32),
            pltpu.VMEM((2, 2, length, MOBA_BLOCK), BF16),
        ],
        compiler_params=_cparams(("arbitrary", "arbitrary")),
        name="moba_prompt",
    )(mq3, mk3, mv3, mg3, *(prev_kv or ()))


def _moba_sample_kernel(pt_ref, q_ref, kn_ref, vn_ref, g_ref, ck_ref, cv_ref, o_ref,
                        kring, vring, ksem, vsem, qs_scr, s_scr, p_scr, *, layer, n_pages, page):
    b = pl.program_id(0)
    last = pl.num_programs(0) - 1
    n_blocks = n_pages * page // MOBA_BLOCK
    group = math.gcd(n_pages, 16)
    rows = MOBA_HEADS * SUBLANES
    contract_lanes = (((1,), (1,)), ((), ()))
    lane = lax.broadcasted_iota(jnp.int32, (SUBLANES, MOBA_W), 1)

    def k_copy(bi, pg):
        return pltpu.make_async_copy(ck_ref.at[layer, pt_ref[bi, pg]], kring.at[pg], ksem.at[pg])

    def v_copy(bi, pg):
        return pltpu.make_async_copy(cv_ref.at[layer, pt_ref[bi, pg]], vring.at[pg], vsem.at[pg])

    @pl.when(b == 0)
    def _():
        for pg in range(n_pages):
            k_copy(0, pg).start()

    q = q_ref[0] * (HEAD_DIM ** -0.5 * LOG2_E)
    for h in range(MOBA_HEADS):
        qs_scr[h * SUBLANES:(h + 1) * SUBLANES, :] = jnp.where(lane // HEAD_DIM == h, q, 0.0)
    qb = qs_scr[...].astype(BF16)

    for g0 in range(0, n_pages, group):
        for pg in range(g0, g0 + group):
            k_copy(b, pg).wait()
        for pg in range(g0, g0 + group):
            s_scr[:, pg * page:(pg + 1) * page] = jnp.dot(qb, kring[pg].astype(BF16), preferred_element_type=F32)
        for pg in range(g0, g0 + group):
            v_copy(b, pg).start()

    col = lax.broadcasted_iota(jnp.int32, (rows, LANES), 1)
    gate = jnp.zeros((rows, LANES), F32)
    for n in range(n_blocks):
        c0 = n * MOBA_BLOCK
        bsum = s_scr[:, c0:c0 + page]
        for u in range(1, MOBA_BLOCK // page):
            bsum = bsum + s_scr[:, c0 + u * page:c0 + (u + 1) * page]
        gate = jnp.where(col == n, jnp.sum(bsum, axis=1, keepdims=True), gate)
    rank = jnp.zeros(gate.shape, jnp.int32)
    for jp in range(n_blocks):
        gj = gate[:, jp:jp + 1]
        rank = rank + jnp.where((gj > gate) | ((gj == gate) & (jp < col)), 1, 0)
    sel = jnp.where(rank < min(MOBA_TOPK, n_blocks), 0.0, NEG_INF)
    own = jnp.concatenate([kn_ref[0], jnp.zeros((page - SUBLANES, MOBA_W), F32)], axis=0).astype(BF16)
    t_idx = lax.broadcasted_iota(jnp.int32, (rows, page), 0) % SUBLANES
    key_idx = lax.broadcasted_iota(jnp.int32, (rows, page), 1)
    s_own = lax.dot_general(qb, own, contract_lanes, preferred_element_type=F32)
    s_own = s_own + jnp.where(key_idx <= t_idx, 0.0, NEG_INF)
    s_scr[:, n_pages * page:] = s_own
    m_part = s_own
    for n in range(n_blocks):
        add = sel[:, n:n + 1]
        for u in range(MOBA_BLOCK // page):
            c0 = n * MOBA_BLOCK + u * page
            sm = s_scr[:, c0:c0 + page] + add
            s_scr[:, c0:c0 + page] = sm
            m_part = jnp.maximum(m_part, sm)
    m = jnp.max(m_part, axis=1, keepdims=True)
    l_part = jnp.zeros((rows, page), F32)
    for n in range(n_pages + 1):
        c0 = n * page
        p = jnp.exp2(s_scr[:, c0:c0 + page] - m)
        l_part = l_part + p
        p_scr[:, c0:c0 + page] = p.astype(BF16)
    l = jnp.sum(l_part, axis=1, keepdims=True)
    vown = jnp.concatenate([vn_ref[0], jnp.zeros((page - SUBLANES, MOBA_W), F32)], axis=0).astype(BF16)
    acc = jnp.dot(p_scr[:, n_pages * page:], vown, preferred_element_type=F32)

    b_next = jnp.minimum(b + 1, last)
    for g0 in range(0, n_pages, group):
        for pg in range(g0, g0 + group):
            v_copy(b, pg).wait()
        for pg in range(g0, g0 + group):
            acc = acc + lax.dot_general(p_scr[:, pg * page:(pg + 1) * page], vring[pg].astype(BF16),
                                        contract_lanes, preferred_element_type=F32)
        for pg in range(g0, g0 + group):
            k_copy(b_next, pg).start()

    @pl.when(b == last)
    def _():
        for pg in range(n_pages):
            k_copy(b_next, pg).wait()

    o = acc / l
    out = jnp.zeros((SUBLANES, MOBA_W), F32)
    for h in range(MOBA_HEADS):
        out = out + jnp.where(lane // HEAD_DIM == h, o[h * SUBLANES:(h + 1) * SUBLANES, :], 0.0)
    o_ref[0] = (out * _silu(g_ref[0])).astype(o_ref.dtype)


def _moba_sample(page_table, mq3, mk3, mv3, mg3, cache_k4, cache_v4, layer):
    db = mq3.shape[0]
    n_pages = page_table.shape[1]
    page = cache_k4.shape[3]
    rows = MOBA_HEADS * SUBLANES
    assert n_pages * page // MOBA_BLOCK <= LANES and page == LANES
    small = pl.BlockSpec((1, SUBLANES, MOBA_W), lambda bi, pt: (bi, 0, 0))
    hbm = pl.BlockSpec(memory_space=pl.ANY)
    width = (n_pages + 1) * page
    return pl.pallas_call(
        functools.partial(_moba_sample_kernel, layer=layer, n_pages=n_pages, page=page),
        grid_spec=pltpu.PrefetchScalarGridSpec(
            num_scalar_prefetch=1,
            grid=(db,),
            in_specs=[small, small, small, small, hbm, hbm],
            out_specs=small,
            scratch_shapes=[
                pltpu.VMEM((n_pages, MOBA_W, page), F32),
                pltpu.VMEM((n_pages, MOBA_W, page), F32),
                pltpu.SemaphoreType.DMA((n_pages,)),
                pltpu.SemaphoreType.DMA((n_pages,)),
                pltpu.VMEM((rows, MOBA_W), F32),
                pltpu.VMEM((rows, width), F32),
                pltpu.VMEM((rows, width), BF16),
            ],
        ),
        out_shape=jax.ShapeDtypeStruct((db, SUBLANES, MOBA_W), BF16),
        compiler_params=_cparams(("arbitrary",)),
        name="moba_sample",
    )(page_table, mq3, mk3, mv3, mg3, cache_k4, cache_v4)


def _moba_sparse_kernel(pt_ref, q_ref, kn_ref, vn_ref, g_ref, ck_ref, cv_ref, o_ref,
                        kring, ksem, vsel, vsem, idx_v, idx_s, isem, qs_scr, s_scr, p_scr, pc_scr, tail_scr, *,
                        layer, n_pages, page, n_tokens):
    b = pl.program_id(0)
    n_batches = pl.num_programs(0) - 1
    last = n_batches - 1
    n_blocks = n_pages * page // MOBA_BLOCK
    ppb = MOBA_BLOCK // page
    slots = n_tokens * min(MOBA_TOPK, n_blocks)
    group = math.gcd(n_pages, 16)
    rows = MOBA_HEADS * SUBLANES
    contract_lanes = (((1,), (1,)), ((), ()))
    lane = lax.broadcasted_iota(jnp.int32, (SUBLANES, MOBA_W), 1)

    def k_copy(bi, pg):
        buf = bi % 2
        return pltpu.make_async_copy(ck_ref.at[layer, pt_ref[bi, pg]], kring.at[buf, pg], ksem.at[buf, pg])

    def v_copy(bi, h, slot, u, blk):
        src = cv_ref.at[layer, pt_ref[bi, blk * ppb + u], pl.ds(h * HEAD_DIM, HEAD_DIM), :]
        return pltpu.make_async_copy(src, vsel.at[h, slot * ppb + u], vsem.at[0])

    @pl.when(b == 0)
    def _():
        for pg in range(n_pages):
            k_copy(0, pg).start()

    @pl.when((b == 0) & (last >= 1))
    def _():
        for pg in range(n_pages):
            k_copy(1, pg).start()

    @pl.when(b <= last)
    def _():
        q = q_ref[0] * (HEAD_DIM ** -0.5 * LOG2_E)
        for h in range(MOBA_HEADS):
            qs_scr[h * SUBLANES:(h + 1) * SUBLANES, :] = jnp.where(lane // HEAD_DIM == h, q, 0.0)
        qb = qs_scr[...].astype(BF16)
        for g0 in range(0, n_pages, group):
            for pg in range(g0, g0 + group):
                k_copy(b, pg).wait()
            for pg in range(g0, g0 + group, ppb):
                keys = jnp.concatenate([kring[b % 2, pg + u].astype(BF16) for u in range(ppb)], axis=1)
                s_scr[:, pg * page:(pg + ppb) * page] = jnp.dot(qb, keys, preferred_element_type=F32)

    @pl.when(b + 2 <= last)
    def _():
        for pg in range(n_pages):
            k_copy(b + 2, pg).start()

    @pl.when(b >= 1)
    def _():
        for h in range(MOBA_HEADS):
            for j in range(slots):
                for u in range(ppb):
                    v_copy(0, h, j, u, 0).wait()
        parts = [[jnp.zeros((SUBLANES, HEAD_DIM), F32) for _ in range(4)] for _ in range(MOBA_HEADS)]
        for piece in range(slots * ppb):
            for h in range(MOBA_HEADS):
                parts[h][piece % 4] = parts[h][piece % 4] + lax.dot_general(
                    pc_scr[h, piece].astype(BF16), vsel[h, piece].astype(BF16), contract_lanes,
                    preferred_element_type=F32)
        o_sel = [(p4[0] + p4[1]) + (p4[2] + p4[3]) for p4 in parts]
        o_ref[0] = ((jnp.concatenate(o_sel, axis=1) + tail_scr[0]) / tail_scr[1] * tail_scr[2]).astype(o_ref.dtype)

    @pl.when(b <= last)
    def _():
        qb = qs_scr[...].astype(BF16)
        col = lax.broadcasted_iota(jnp.int32, (rows, LANES), 1)
        gate = jnp.zeros((rows, LANES), F32)
        for n in range(n_blocks):
            c0 = n * MOBA_BLOCK
            bsum = s_scr[:, c0:c0 + page]
            for u in range(1, ppb):
                bsum = bsum + s_scr[:, c0 + u * page:c0 + (u + 1) * page]
            gate = jnp.where(col == n, jnp.sum(bsum, axis=1, keepdims=True), gate)
        rank = jnp.zeros(gate.shape, jnp.int32)
        for jp in range(n_blocks):
            gj = gate[:, jp:jp + 1]
            rank = rank + jnp.where((gj > gate) | ((gj == gate) & (jp < col)), 1, 0)
        sel = jnp.where(rank < min(MOBA_TOPK, n_blocks), 0.0, NEG_INF)
        own = jnp.concatenate([kn_ref[0], jnp.zeros((page - SUBLANES, MOBA_W), F32)], axis=0).astype(BF16)
        t_idx = lax.broadcasted_iota(jnp.int32, (rows, page), 0) % SUBLANES
        key_idx = lax.broadcasted_iota(jnp.int32, (rows, page), 1)
        s_own = lax.dot_general(qb, own, contract_lanes, preferred_element_type=F32)
        s_own = s_own + jnp.where(key_idx <= t_idx, 0.0, NEG_INF)
        s_scr[:, n_pages * page:] = s_own
        m_part = s_own
        for n in range(n_blocks):
            add = sel[:, n:n + 1]
            for u in range(ppb):
                c0 = n * MOBA_BLOCK + u * page
                sm = s_scr[:, c0:c0 + page] + add
                s_scr[:, c0:c0 + page] = sm
                m_part = jnp.maximum(m_part, sm)
        m = jnp.max(m_part, axis=1, keepdims=True)
        l_part = jnp.zeros((rows, page), F32)
        for n in range(n_pages + 1):
            c0 = n * page
            p = jnp.exp2(s_scr[:, c0:c0 + page] - m)
            l_part = l_part + p
            p_scr[:, c0:c0 + page] = p
        l = jnp.sum(l_part, axis=1, keepdims=True)

        row8 = lax.broadcasted_iota(jnp.int32, (SUBLANES, LANES), 0)
        col8 = lax.broadcasted_iota(jnp.int32, (SUBLANES, LANES), 1)
        sel_real = jnp.where(t_idx < n_tokens, sel, NEG_INF)
        chosen = jnp.full((SUBLANES, LANES), NEG_INF, F32)
        for h in range(MOBA_HEADS):
            any_row = jnp.max(sel_real[h * SUBLANES:(h + 1) * SUBLANES], axis=0, keepdims=True)
            chosen = jnp.where(row8 == h, any_row, chosen)
        chosen = chosen == 0.0
        before = (lax.broadcasted_iota(jnp.int32, (LANES, LANES), 0)
                  < lax.broadcasted_iota(jnp.int32, (LANES, LANES), 1))
        pos = jnp.dot(jnp.where(chosen, 1.0, 0.0).astype(BF16), jnp.where(before, 1.0, 0.0).astype(BF16),
                      preferred_element_type=F32)
        count = jnp.sum(jnp.where(chosen, 1.0, 0.0), axis=1, keepdims=True)
        blk_id = col8.astype(F32)
        pick = lambda j: jnp.sum(jnp.where(chosen & (pos == j), blk_id, 0.0), axis=1, keepdims=True)
        first = pick(0)
        table = jnp.where(col8 == slots, count, 0.0)
        for j in range(slots):
            table = jnp.where(col8 == j, jnp.where(count > j, pick(j), first), table)
        idx_v[...] = table.astype(jnp.int32)
        to_smem = pltpu.make_async_copy(idx_v, idx_s, isem.at[0])
        to_smem.start()
        to_smem.wait()

        for h in range(MOBA_HEADS):
            for j in range(slots):
                for u in range(ppb):
                    v_copy(b, h, j, u, idx_s[h, j]).start(priority=1)
        for h in range(MOBA_HEADS):
            for j in range(slots):
                weight = jnp.where(j < idx_s[h, slots], 1.0, 0.0)
                for u in range(ppb):
                    c0 = pl.multiple_of((idx_s[h, j] * ppb + u) * page, page)
                    pc_scr[h, j * ppb + u] = p_scr[h * SUBLANES:(h + 1) * SUBLANES, pl.ds(c0, page)] * weight
        vown = jnp.concatenate([vn_ref[0], jnp.zeros((page - SUBLANES, MOBA_W), F32)], axis=0).astype(BF16)
        o_own = jnp.dot(p_scr[:, n_pages * page:].astype(BF16), vown, preferred_element_type=F32)
        own_rows = jnp.zeros((SUBLANES, MOBA_W), F32)
        l_rows = jnp.zeros((SUBLANES, MOBA_W), F32)
        for h in range(MOBA_HEADS):
            mine = lane // HEAD_DIM == h
            own_rows = jnp.where(mine, o_own[h * SUBLANES:(h + 1) * SUBLANES, :], own_rows)
            l_rows = jnp.where(mine, l[h * SUBLANES:(h + 1) * SUBLANES, :], l_rows)
        tail_scr[0] = own_rows
        tail_scr[1] = l_rows
        tail_scr[2] = _silu(g_ref[0])


def _moba_sparse(page_table, mq3, mk3, mv3, mg3, cache_k4, cache_v4, layer, n_tokens):
    db = mq3.shape[0]
    n_pages = page_table.shape[1]
    page = cache_k4.shape[3]
    rows = MOBA_HEADS * SUBLANES
    n_blocks = n_pages * page // MOBA_BLOCK
    slots = n_tokens * min(MOBA_TOPK, n_blocks)
    assert n_blocks <= LANES and page == LANES and slots < LANES and MOBA_HEADS <= SUBLANES
    blk3 = (1, SUBLANES, MOBA_W)
    this = pl.BlockSpec(blk3, lambda bi, pt: (jnp.minimum(bi, db - 1), 0, 0))
    before = pl.BlockSpec(blk3, lambda bi, pt: (jnp.maximum(bi - 1, 0), 0, 0))
    hbm = pl.BlockSpec(memory_space=pl.ANY)
    width = (n_pages + 1) * page
    pieces = slots * MOBA_BLOCK // page
    return pl.pallas_call(
        functools.partial(_moba_sparse_kernel, layer=layer, n_pages=n_pages, page=page, n_tokens=n_tokens),
        grid_spec=pltpu.PrefetchScalarGridSpec(
            num_scalar_prefetch=1,
            grid=(db + 1,),
            in_specs=[this, this, this, this, hbm, hbm],
            out_specs=before,
            scratch_shapes=[
                pltpu.VMEM((2, n_pages, MOBA_W, page), F32),
                pltpu.SemaphoreType.DMA((2, n_pages)),
                pltpu.VMEM((MOBA_HEADS, pieces, HEAD_DIM, page), F32),
                pltpu.SemaphoreType.DMA((1,)),
                pltpu.VMEM((SUBLANES, LANES), jnp.int32),
                pltpu.SMEM((SUBLANES, LANES), jnp.int32),
                pltpu.SemaphoreType.DMA((1,)),
                pltpu.VMEM((rows, MOBA_W), F32),
                pltpu.VMEM((rows, width), F32),
                pltpu.VMEM((rows, width), F32),
                pltpu.VMEM((MOBA_HEADS, pieces, SUBLANES, page), F32),
                pltpu.VMEM((3, SUBLANES, MOBA_W), F32),
            ],
        ),
        out_shape=jax.ShapeDtypeStruct((db, SUBLANES, MOBA_W), BF16),
        compiler_params=_cparams(("arbitrary",)),
        name="moba_sample",
    )(page_table, mq3, mk3, mv3, mg3, cache_k4, cache_v4)


def _pool_kernel(u_ref, g_ref, buf_ref, cnt_ref, w_ref, b_ref, sc_ref, o_ref, bufo_ref, x_scr, w_scr, *,
                 n_tokens, rows):
    bb, length, _ = u_ref.shape
    lead = SUBLANES
    halo = lead + POOL_BUF + 1
    assert POOL_WINDOWS == (2, 4, 8, 16) and halo % SUBLANES == 0
    tiles = [(lead, halo - lead)] + [(halo + r0, rows) for r0 in range(0, length, rows)]
    for bi in range(bb):
        x_scr[bi, 0:lead, :] = jnp.zeros((lead, POOL_W), F32)
        x_scr[bi, lead:halo, :] = buf_ref[bi]
        x_scr[bi, halo:, :] = u_ref[bi]
        w_scr[bi, :, 0:lead, :] = jnp.zeros((3, lead, POOL_W), F32)
        for i0, n in tiles:
            group = lax.broadcasted_iota(jnp.int32, (n, POOL_W), 1) // POOL_GC
            u = x_scr[bi, i0:i0 + n, :]
            w2 = u + x_scr[bi, i0 - 1:i0 - 1 + n, :]
            w_scr[bi, 0, i0:i0 + n, :] = w2
            w4 = w2 + w_scr[bi, 0, i0 - 2:i0 - 2 + n, :]
            w_scr[bi, 1, i0:i0 + n, :] = w4
            w8 = w4 + w_scr[bi, 1, i0 - 4:i0 - 4 + n, :]
            w_scr[bi, 2, i0:i0 + n, :] = w8
            if i0 < halo:
                continue
            w16 = w8 + w_scr[bi, 2, i0 - 8:i0 - 8 + n, :]
            r0 = i0 - halo
            win = jnp.where(group == 0, w2, jnp.where(group == 1, w4, jnp.where(group == 2, w8, w16)))
            pooled = win / cnt_ref[r0:r0 + n, :] - u
            y = jnp.dot(pooled.astype(BF16), w_ref[...], preferred_element_type=F32) + b_ref[...]
            y = y * sc_ref[...]
            o_ref[bi, r0:r0 + n, :] = (y * _silu(g_ref[bi, r0:r0 + n, :])).astype(o_ref.dtype)
        bufo_ref[bi] = x_scr[bi, lead + n_tokens + 1:lead + n_tokens + 1 + POOL_BUF, :]


def _pool(zp3, buf16, cnt, w_bd, bias, scale, *, n_tokens, rows, bb):
    b, length, _ = zp3.shape
    const = lambda bi: (0, 0)
    n_rows = SUBLANES + POOL_BUF + 1 + length
    return pl.pallas_call(
        functools.partial(_pool_kernel, n_tokens=n_tokens, rows=rows),
        grid=(b // bb,),
        in_specs=[
            pl.BlockSpec((bb, length, POOL_W), lambda bi: (bi, 0, 0)),
            pl.BlockSpec((bb, length, POOL_W), lambda bi: (bi, 0, 1)),
            pl.BlockSpec((bb, POOL_BUF + 1, POOL_W), lambda bi: (bi, 0, 0)),
            pl.BlockSpec((length, POOL_W), const),
            pl.BlockSpec((POOL_W, POOL_W), const),
            pl.BlockSpec((1, POOL_W), const),
            pl.BlockSpec((1, POOL_W), const),
        ],
        out_specs=[
            pl.BlockSpec((bb, length, POOL_W), lambda bi: (bi, 0, 0)),
            pl.BlockSpec((bb, POOL_BUF, POOL_W), lambda bi: (bi, 0, 0)),
        ],
        out_shape=[
            jax.ShapeDtypeStruct((b, length, POOL_W), BF16),
            jax.ShapeDtypeStruct((b, POOL_BUF, POOL_W), F32),
        ],
        scratch_shapes=[pltpu.VMEM((bb, n_rows, POOL_W), F32), pltpu.VMEM((bb, 3, n_rows, POOL_W), F32)],
        compiler_params=_cparams(("arbitrary",)),
        name="pool",
    )(zp3, zp3, buf16, cnt, w_bd, bias, scale)


def _out_proj_kernel(mr_ref, mm_ref, mp_ref, x_ref, w_ref, y_ref, mix_scr):
    mix_scr[:, :RET_W] = mr_ref[...]
    mix_scr[:, RET_W:RET_W + MOBA_W] = mm_ref[...]
    mix_scr[:, RET_W + MOBA_W:] = mp_ref[...]
    y_ref[...] = x_ref[...] + jnp.dot(mix_scr[...], w_ref[...], preferred_element_type=F32)


def _out_proj(mr, mm, mp, x2d, w_bf, *, tm, layer):
    m, d = x2d.shape
    d_mix = RET_W + MOBA_W + POOL_W
    row = lambda i: (i, 0)
    return pl.pallas_call(
        _out_proj_kernel,
        grid=(m // tm,),
        in_specs=[
            pl.BlockSpec((tm, RET_W), row),
            pl.BlockSpec((tm, MOBA_W), row),
            pl.BlockSpec((tm, POOL_W), row),
            pl.BlockSpec((tm, d), row),
            pl.BlockSpec((None, d_mix, d), lambda i: (layer, 0, 0)),
        ],
        out_specs=pl.BlockSpec((tm, d), row),
        out_shape=jax.ShapeDtypeStruct((m, d), F32),
        scratch_shapes=[pltpu.VMEM((tm, d_mix), BF16)],
        compiler_params=_cparams(("arbitrary",)),
        name="out_proj",
    )(mr, mm, mp, x2d, w_bf)


def _rope_tables(pos):
    inv = 1.0 / (ROPE_THETA ** (np.arange(HALF, dtype=np.float64) / HALF))
    ang = np.asarray(pos, np.float64)[:, None] * inv[None, :]
    c, s = np.cos(ang), np.sin(ang)
    return (jnp.asarray(np.concatenate([c, c, c, c], axis=-1), F32),
            jnp.asarray(np.concatenate([-s, s, -s, s], axis=-1), F32))


def _block_diag(blocks):
    g, n, _ = blocks.shape
    eye = jnp.eye(g, dtype=blocks.dtype)
    return (eye[:, None, :, None] * blocks[:, :, None, :]).reshape(g * n, g * n)


def _pool_counts(pos0, length):
    pos = pos0 + np.arange(length)
    w = np.repeat(np.asarray(POOL_WINDOWS), POOL_GC)
    return jnp.asarray(np.minimum(pos[:, None] + 1, w[None, :]), F32)


def _layer(x3, pos_tabs, ret_state, ret_prev, ret_tabs, ret_step, pool_buf16, pool_cnt, n_tokens, moba_fn, params, *,
           tm, layer):
    b, length, d = x3.shape
    nw, w_in, w_out, gnw, qnw, knw, bd, pw_bd, pbias, pscale = params
    x2d = x3.reshape(b * length, d)
    zr, mq, mk, mv, mg, zp = _in_proj(x2d, nw, w_in, pos_tabs[0], pos_tabs[1], qnw, knw, bd, tm=tm, layer=layer)
    three = lambda a: a.reshape(b, length, a.shape[-1])
    mix_r, states = _retention(three(zr), ret_state, ret_prev, ret_tabs, gnw, chunk=RET_CHUNK, bb=ret_step[0],
                               pairs=ret_step[1], layer=layer)
    mix_m, k_new, v_new = moba_fn(three(mq), three(mk), three(mv), three(mg))
    mix_p, buf_new = _pool(three(zp), pool_buf16, pool_cnt, pw_bd, pbias, pscale,
                           n_tokens=n_tokens, rows=min(length, 128), bb=ret_step[0])
    two = lambda a: a.reshape(b * length, a.shape[-1])
    tm_out = max(t for t in (tm, 2 * tm, 4 * tm) if (b * length) % t == 0)
    y = _out_proj(two(mix_r), two(mix_m), two(mix_p), x2d, w_out, tm=tm_out, layer=layer)
    return y.reshape(b, length, d), states, k_new, v_new, buf_new


def _moba_prompt_wrap(mq3, mk3, mv3, mg3, *, prev_kv):
    return _moba_prompt(mq3, mk3, mv3, mg3, prev_kv)


def _moba_sample_wrap(mq3, mk3, mv3, mg3, *, page_table, cache_k4, cache_v4, layer, n_tokens):
    mix = _moba_sparse(page_table, mq3, mk3, mv3, mg3, cache_k4, cache_v4, layer, n_tokens)
    rows = lambda t: t[:, :n_tokens].reshape(t.shape[0], n_tokens, MOBA_HEADS, HEAD_DIM)
    return mix, rows(mk3), rows(mv3)


def kernel(x_prompt, x_sample, cache_k, cache_v, state_ret, state_pool, page_table, norm_w, w_in, w_out, ret_gn_w, q_norm_w, k_norm_w, pool_w, pool_b, pool_scale):
    depth = w_in.shape[0]
    b, seq, d = x_prompt.shape
    db, dec_seq, _ = x_sample.shape
    n_pool, page = cache_k.shape[1], cache_k.shape[2]
    past_len = page_table.shape[1] * page
    assert dec_seq <= SUBLANES and seq % MOBA_BLOCK == 0 and seq % RET_CHUNK == 0 and past_len % MOBA_BLOCK == 0

    p_tabs = _rope_tables(np.arange(seq))
    tm_s = db * SUBLANES
    s_tabs = _rope_tables(np.tile(past_len + np.arange(SUBLANES), db))
    p_ret_tabs = _retention_tables(RET_CHUNK, RET_CHUNK)
    s_ret_tabs = _retention_tables(RET_CHUNK, dec_seq)
    p_cnt = _pool_counts(0, seq)
    s_cnt = _pool_counts(past_len, SUBLANES)
    head_of = np.arange(MOBA_W) // HEAD_DIM
    bd = jnp.asarray((head_of[:, None] == head_of[None, :]) / HEAD_DIM, BF16)
    zero_buf = jnp.zeros((b, POOL_BUF + 1, POOL_W), F32)

    cache_k4 = cache_k.transpose(0, 1, 3, 4, 2).reshape(depth, n_pool, MOBA_W, page)
    cache_v4 = cache_v.transpose(0, 1, 3, 4, 2).reshape(depth, n_pool, MOBA_W, page)
    xp = x_prompt
    xs = jnp.pad(x_sample, ((0, 0), (0, SUBLANES - dec_seq), (0, 0)))

    w_in_bf = w_in.astype(BF16)
    w_out_bf = w_out.astype(BF16)
    outs = [[] for _ in range(4)]
    prev_kv = s_p = s_s = None
    for l in range(depth):
        params = (norm_w[l].reshape(1, d), w_in_bf, w_out_bf,
                  ret_gn_w[l].reshape(RET_W // LANES, 1, LANES),
                  jnp.tile(q_norm_w[l], MOBA_HEADS).reshape(1, MOBA_W),
                  jnp.tile(k_norm_w[l], MOBA_HEADS).reshape(1, MOBA_W),
                  bd, _block_diag(pool_w[l]).astype(BF16),
                  pool_b[l].reshape(1, POOL_W), pool_scale[l].reshape(1, POOL_W))
        xp, s_p, kt, vt, b_p = _layer(xp, p_tabs, None, s_p, p_ret_tabs, (1, 1), zero_buf, p_cnt, seq,
                                      functools.partial(_moba_prompt_wrap, prev_kv=prev_kv), params,
                                      tm=512, layer=l)
        prev_kv = (kt, vt)
        moba_s = functools.partial(_moba_sample_wrap, page_table=page_table, cache_k4=cache_k4,
                                   cache_v4=cache_v4, layer=l, n_tokens=dec_seq)
        xs, s_s, k_s, v_s, b_s = _layer(xs, s_tabs, state_ret, s_s, s_ret_tabs,
                                        (math.gcd(db, 8), RET_W // LANES),
                                        jnp.pad(state_pool[l], ((0, 0), (1, 0), (0, 0))), s_cnt, dec_seq,
                                        moba_s, params, tm=tm_s, layer=l)
        for lst, val in zip(outs, (k_s, v_s, b_p, b_s)):
            lst.append(val)
    rows = lambda t: t.reshape(depth, b, MOBA_HEADS, HEAD_DIM, seq).transpose(0, 1, 4, 2, 3)
    k_s, v_s, b_p, b_s = (jnp.stack(o) for o in outs)
    return xp, xs[:, :dec_seq], rows(prev_kv[0]), rows(prev_kv[1]), k_s, v_s, s_p, s_s, b_p, b_s
```

```python
import functools
import math

import jax
import jax.numpy as jnp
import numpy as np
from jax import lax
from jax.experimental import pallas as pl
from jax.experimental.pallas import tpu as pltpu

F32 = jnp.float32
BF16 = jnp.bfloat16

HEAD_DIM = 64
HALF = HEAD_DIM // 2
LANES = 128
RET_HEADS = 6
MOBA_HEADS = 6
RET_W = RET_HEADS * HEAD_DIM
MOBA_W = MOBA_HEADS * HEAD_DIM
POOL_W = 256
POOL_GC = 64
POOL_WINDOWS = (2, 4, 8, 16)
POOL_BUF = 15
RET_CHUNK = 128
MOBA_BLOCK = 256
MOBA_TOPK = 3
ROPE_THETA = 10000.0
EPS = 1e-6
SUBLANES = 8
NEG_INF = float("-inf")
LOG2_E = 1.4426950408889634

OFF_RQ, OFF_RV, OFF_MQ, OFF_MV, OFF_PU = 0, 2 * RET_W, 4 * RET_W, 4 * RET_W + 2 * MOBA_W, 4 * RET_W + 4 * MOBA_W
D_IN = OFF_PU + 2 * POOL_W

VMEM_LIMIT = 56 * 1024 * 1024


def _silu(x):
    return x / (1.0 + jnp.exp(-x))


def _cparams(sem):
    return pltpu.CompilerParams(dimension_semantics=sem, vmem_limit_bytes=VMEM_LIMIT)


def _in_proj_kernel(x_ref, nw_ref, w_ref, cos_ref, sin_ref, qnw_ref, knw_ref, bd_ref,
                    zr_ref, mq_ref, mk_ref, mv_ref, mg_ref, zp_ref, h_scr, z_scr):
    tm = x_ref.shape[0]
    x = x_ref[...]
    ms = jnp.mean(x * x, axis=-1, keepdims=True)
    h_scr[...] = (x * lax.rsqrt(ms + EPS) * nw_ref[...]).astype(BF16)
    cos = cos_ref[...]
    sin = sin_ref[...]
    lane = lax.broadcasted_iota(jnp.int32, (tm, LANES), 1)
    first_half = (lane & HALF) == 0

    def rope(z):
        partner = jnp.where(first_half, pltpu.roll(z, LANES - HALF, 1), pltpu.roll(z, HALF, 1))
        return z * cos + partner * sin

    def proj(off, width):
        return jnp.dot(h_scr[...], w_ref[:, off:off + width], preferred_element_type=F32)

    z_scr[...] = proj(OFF_RQ, 2 * RET_W)
    for g in range(2 * RET_W // LANES):
        out = rope(z_scr[:, g * LANES:(g + 1) * LANES])
        if g >= RET_W // LANES:
            out = out * (HEAD_DIM ** -0.5)
        zr_ref[:, g * LANES:(g + 1) * LANES] = out
    zr_ref[:, OFF_RV:OFF_MQ] = proj(OFF_RV, 2 * RET_W)
    z_scr[...] = proj(OFF_MQ, 2 * MOBA_W)
    for t, (nw, dst) in enumerate(((qnw_ref, mq_ref), (knw_ref, mk_ref))):
        z = z_scr[:, t * MOBA_W:(t + 1) * MOBA_W]
        msq = jnp.dot((z * z).astype(BF16), bd_ref[...], preferred_element_type=F32)
        z_scr[:, t * MOBA_W:(t + 1) * MOBA_W] = z * lax.rsqrt(msq + EPS) * nw[...]
        for g in range(MOBA_W // LANES):
            c0 = t * MOBA_W + g * LANES
            dst[:, g * LANES:(g + 1) * LANES] = rope(z_scr[:, c0:c0 + LANES])
    z_scr[...] = proj(OFF_MV, 2 * MOBA_W)
    mv_ref[...] = z_scr[:, :MOBA_W]
    mg_ref[...] = z_scr[:, MOBA_W:]
    zp_ref[...] = proj(OFF_PU, 2 * POOL_W)


def _in_proj(x2d, nw, w_bf, cos, sin, qnw, knw, bd, *, tm, layer):
    m, d = x2d.shape
    n_pos = cos.shape[0] // tm
    row = lambda i: (i, 0)
    const = lambda i: (0, 0)
    pos = lambda i: (i % n_pos, 0)
    outs = [(2 * RET_W + 2 * RET_W), MOBA_W, MOBA_W, MOBA_W, MOBA_W, 2 * POOL_W]
    return pl.pallas_call(
        _in_proj_kernel,
        grid=(m // tm,),
        in_specs=[
            pl.BlockSpec((tm, d), row),
            pl.BlockSpec((1, d), const),
            pl.BlockSpec((None, d, D_IN), lambda i: (layer, 0, 0)),
            pl.BlockSpec((tm, LANES), pos),
            pl.BlockSpec((tm, LANES), pos),
            pl.BlockSpec((1, MOBA_W), const),
            pl.BlockSpec((1, MOBA_W), const),
            pl.BlockSpec((MOBA_W, MOBA_W), const),
        ],
        out_specs=[pl.BlockSpec((tm, w), row) for w in outs],
        out_shape=[jax.ShapeDtypeStruct((m, w), F32) for w in outs],
        scratch_shapes=[pltpu.VMEM((tm, d), BF16), pltpu.VMEM((tm, 2 * RET_W), F32)],
        compiler_params=_cparams(("arbitrary",)),
        name="in_proj",
    )(x2d, nw, w_bf, cos, sin, qnw, knw, bd)


def _retention_kernel(*refs, chunk, has_state, n_prev):
    q_ref, k_ref, v_ref, g_ref = refs[:4]
    s0_ref = refs[4] if has_state else None
    prev_ref = refs[4 + has_state] if n_prev else None
    dm_ref, rs_ref, kd_ref, gc_ref, gnw_ref, o_ref, sout_ref, oi_scr, kv_scr, sb_scr = refs[4 + has_state + (n_prev > 0):]
    if n_prev:
        sout_ref[0:n_prev] = prev_ref[...]
    bb, length, width = q_ref.shape
    pairs = width // LANES
    rows_in = min(length, chunk)
    n_chunks = max(1, length // chunk)
    lane = lax.broadcasted_iota(jnp.int32, (chunk, LANES), 1)
    head0 = lane < HEAD_DIM
    r = lax.broadcasted_iota(jnp.int32, (LANES, LANES), 0)
    c = lax.broadcasted_iota(jnp.int32, (LANES, LANES), 1)
    same_head = (r < HEAD_DIM) == (c < HEAD_DIM)
    contract_lanes = (((1,), (1,)), ((), ()))

    def head_mean(x):
        m0 = jnp.sum(jnp.where(head0, x, 0.0), axis=1, keepdims=True)
        m1 = jnp.sum(jnp.where(head0, 0.0, x), axis=1, keepdims=True)
        return jnp.where(head0, m0, m1) * (1.0 / HEAD_DIM)

    def load(ref, bi, p, ci):
        a = ref[bi, ci * chunk:ci * chunk + rows_in, p * LANES:(p + 1) * LANES]
        if rows_in < chunk:
            a = jnp.concatenate([a, jnp.zeros((chunk - rows_in, LANES), F32)], axis=0)
        return a

    items = [(bi, p, ci) for bi in range(bb) for p in range(pairs) for ci in range(n_chunks)]

    def a1(bi, p, ci):
        q = load(q_ref, bi, p, ci)
        k = load(k_ref, bi, p, ci)
        kb = k.astype(BF16)
        vb = load(v_ref, bi, p, ci).astype(BF16)
        q0 = jnp.where(head0, q, 0.0).astype(BF16)
        q1 = jnp.where(head0, 0.0, q).astype(BF16)
        in0 = (lax.dot_general(q0, kb, contract_lanes, preferred_element_type=F32) * dm_ref[2 * p]).astype(BF16)
        in1 = (lax.dot_general(q1, kb, contract_lanes, preferred_element_type=F32) * dm_ref[2 * p + 1]).astype(BF16)
        return bi, p, ci, k, vb, in0, in1

    def a2(st):
        bi, p, ci, k, vb, in0, in1 = st
        slot = bi * pairs + p
        oi_scr[slot, ci * chunk:(ci + 1) * chunk, :] = jnp.where(
            head0, jnp.dot(in0, vb, preferred_element_type=F32), jnp.dot(in1, vb, preferred_element_type=F32))
        kdt = (k * kd_ref[p]).T.astype(BF16)
        kv_scr[slot, ci] = jnp.where(same_head, jnp.dot(kdt, vb, preferred_element_type=F32), 0.0)

    st = None
    for item in items + [None]:
        nxt = a1(*item) if item is not None else None
        if st is not None:
            a2(st)
        st = nxt

    zero = jnp.zeros((HEAD_DIM, HEAD_DIM), F32)
    for bi in range(bb):
        for p in range(pairs):
            slot = bi * pairs + p
            if s0_ref is None:
                s = jnp.zeros((LANES, LANES), F32)
            else:
                s = jnp.concatenate([jnp.concatenate([s0_ref[bi, 2 * p], zero], axis=1),
                                     jnp.concatenate([zero, s0_ref[bi, 2 * p + 1]], axis=1)], axis=0)
            for ci in range(n_chunks):
                sb_scr[slot, ci] = s.astype(BF16)
                s = s * gc_ref[p] + kv_scr[slot, ci]
            sout_ref[n_prev, bi, 2 * p] = s[:HEAD_DIM, :HEAD_DIM]
            sout_ref[n_prev, bi, 2 * p + 1] = pltpu.roll(s[HEAD_DIM:], HEAD_DIM, 1)[:, :HEAD_DIM]

    def c1(bi, p, ci):
        slot = bi * pairs + p
        qb = load(q_ref, bi, p, ci).astype(BF16)
        o = (oi_scr[slot, ci * chunk:(ci + 1) * chunk, :]
             + jnp.dot(qb, sb_scr[slot, ci], preferred_element_type=F32) * rs_ref[p])
        return bi, p, ci, o, head_mean(o)

    def c2(st):
        bi, p, ci, o, mu = st
        oc = o - mu
        return bi, p, ci, oc, head_mean(oc * oc)

    def c3(st):
        bi, p, ci, oc, var = st
        on = oc * lax.rsqrt(var + EPS) * gnw_ref[p]
        out = on * _silu(load(g_ref, bi, p, ci))
        o_ref[bi, ci * chunk:ci * chunk + rows_in, p * LANES:(p + 1) * LANES] = out[:rows_in].astype(o_ref.dtype)

    s1 = s2 = None
    for item in items + [None, None]:
        new1 = c1(*item) if item is not None else None
        new2 = c2(s1) if s1 is not None else None
        if s2 is not None:
            c3(s2)
        s1, s2 = new1, new2


def _retention(zr3, state, prev_states, tabs, gnw, *, chunk, bb, pairs, layer):
    b, length, _ = zr3.shape
    n_pairs = RET_W // LANES
    dm, rs, kd, gc = tabs
    w = pairs * LANES
    n_chunks = max(1, length // chunk)
    n_prev = 0 if prev_states is None else prev_states.shape[0]
    col = lambda off: (lambda bi, p: (bi, 0, off + p))
    tab = lambda bi, p: (p, 0, 0)
    heads_blk = (bb, 2 * pairs, HEAD_DIM, HEAD_DIM)
    stacked = lambda n: pl.BlockSpec((n,) + heads_blk, lambda bi, p: (0, bi, p, 0, 0))
    extra_specs, extra_args = [], []
    if state is not None:
        extra_specs.append(pl.BlockSpec((None,) + heads_blk, lambda bi, p: (layer, bi, p, 0, 0)))
        extra_args.append(state)
    if n_prev:
        extra_specs.append(stacked(n_prev))
        extra_args.append(prev_states)
    return pl.pallas_call(
        functools.partial(_retention_kernel, chunk=chunk, has_state=state is not None, n_prev=n_prev),
        grid=(b // bb, n_pairs // pairs),
        in_specs=[
            pl.BlockSpec((bb, length, w), col(0)),
            pl.BlockSpec((bb, length, w), col(n_pairs // pairs)),
            pl.BlockSpec((bb, length, w), col(2 * n_pairs // pairs)),
            pl.BlockSpec((bb, length, w), col(3 * n_pairs // pairs)),
        ] + extra_specs + [
            pl.BlockSpec((2 * pairs, chunk, chunk), tab),
            pl.BlockSpec((pairs, chunk, LANES), tab),
            pl.BlockSpec((pairs, chunk, LANES), tab),
            pl.BlockSpec((pairs, 1, LANES), tab),
            pl.BlockSpec((pairs, 1, LANES), tab),
        ],
        out_specs=[pl.BlockSpec((bb, length, w), lambda bi, p: (bi, 0, p)), stacked(n_prev + 1)],
        out_shape=[
            jax.ShapeDtypeStruct((b, length, RET_W), BF16),
            jax.ShapeDtypeStruct((n_prev + 1, b, RET_HEADS, HEAD_DIM, HEAD_DIM), F32),
        ],
        scratch_shapes=[
            pltpu.VMEM((bb * pairs, n_chunks * chunk, LANES), F32),
            pltpu.VMEM((bb * pairs, n_chunks, LANES, LANES), F32),
            pltpu.VMEM((bb * pairs, n_chunks, LANES, LANES), BF16),
        ],
        compiler_params=_cparams(("arbitrary", "arbitrary")),
        name="retention",
    )(zr3, zr3, zr3, zr3, *extra_args, dm, rs, kd, gc, gnw)


def _retention_tables(chunk, n_tokens):
    lg = np.log(1.0 - 2.0 ** (-5.0 - np.arange(RET_HEADS, dtype=np.float64)))
    i = np.arange(chunk, dtype=np.float64)
    rel = i[:, None] - i[None, :]
    dm = np.where(rel[None] >= 0, np.exp(rel[None] * lg[:, None, None]), 0.0)
    lg_lanes = np.repeat(lg, HEAD_DIM).reshape(RET_W // LANES, 1, LANES)
    rs = np.exp((i + 1.0)[None, :, None] * lg_lanes)
    kd = np.where((i < n_tokens)[None, :, None], np.exp((n_tokens - 1.0 - i)[None, :, None] * lg_lanes), 0.0)
    gc = np.exp(float(n_tokens) * lg_lanes)
    return tuple(jnp.asarray(t, F32) for t in (dm, rs, kd, gc))


def _moba_prompt_kernel(*refs, n_blocks, n_prev):
    q_ref, k_ref, v_ref, g_ref = refs[:4]
    prev = refs[4:6] if n_prev else ()
    o_ref, kt_ref, vt_ref, kh_scr, vt_scr, km_scr, s_scr, p_scr = refs[4 + len(prev):]
    blk = MOBA_BLOCK
    sub = blk
    contract_lanes = (((1,), (1,)), ((), ()))
    lane = lax.broadcasted_iota(jnp.int32, (blk, LANES), 1)
    head0 = lane < HEAD_DIM
    for j in range(n_blocks):
        kj = k_ref[0, j * blk:(j + 1) * blk, :]
        kh_scr[0, j * blk:(j + 1) * blk, :] = jnp.where(head0, kj, 0.0).astype(BF16)
        kh_scr[1, j * blk:(j + 1) * blk, :] = jnp.where(head0, 0.0, kj).astype(BF16)
        vt = v_ref[0, j * blk:(j + 1) * blk, :].T
        vt_scr[:, j * blk:(j + 1) * blk] = vt.astype(BF16)
        km_scr[j:j + 1, :] = jnp.sum(kj, axis=0, keepdims=True) * (1.0 / blk)
        kt_ref[n_prev, 0, :, j * blk:(j + 1) * blk] = kj.T
        vt_ref[n_prev, 0, :, j * blk:(j + 1) * blk] = vt
    if n_prev:
        kt_ref[0:n_prev] = prev[0][...]
        vt_ref[0:n_prev] = prev[1][...]

    km = km_scr[...]
    lane8 = lax.broadcasted_iota(jnp.int32, (n_blocks, LANES), 1)
    row8 = lax.broadcasted_iota(jnp.int32, (n_blocks, blk), 0)
    krow = lax.broadcasted_iota(jnp.int32, (blk, blk), 0)
    qcol = lax.broadcasted_iota(jnp.int32, (blk, blk), 1)
    causal = jnp.where(krow <= qcol, 0.0, NEG_INF)
    fold = lambda t: t.reshape(t.shape[0] // SUBLANES, SUBLANES, blk)
    km_parts = []
    for h in range(2):
        kmh = jnp.where((lane8 < HEAD_DIM) == (h == 0), km, 0.0)
        kmh_hi = kmh.astype(BF16)
        km_parts.append((kmh_hi, (kmh - kmh_hi.astype(F32)).astype(BF16)))

    heads = range(2)
    tiles = lambda i: [(j, u, h) for j in range(i + 1) for u in range(blk // sub) for h in heads]

    def begin(i):
        q = q_ref[0, i * blk:(i + 1) * blk, :] * (HEAD_DIM ** -0.5 * LOG2_E)
        qh = q.astype(BF16)
        selmask = [None, None]
        if i > MOBA_TOPK:
            ql = (q - qh.astype(F32)).astype(BF16)
            for h in heads:
                kmh_hi, kmh_lo = km_parts[h]
                gate = (lax.dot_general(kmh_hi, qh, contract_lanes, preferred_element_type=F32)
                        + lax.dot_general(kmh_lo, qh, contract_lanes, preferred_element_type=F32)
                        + lax.dot_general(kmh_hi, ql, contract_lanes, preferred_element_type=F32))
                rank = jnp.zeros((n_blocks, blk), jnp.int32)
                for jp in range(i):
                    gj = gate[jp:jp + 1, :]
                    rank = rank + jnp.where((gj > gate) | ((gj == gate) & (jp < row8)), 1, 0)
                selmask[h] = jnp.where((rank < MOBA_TOPK) & (row8 < i), 0.0, NEG_INF)
        return dict(i=i, buf=i % 2, qh=qh, selmask=selmask, m8=[None, None], m=None,
                    l8=[jnp.zeros((SUBLANES, blk), F32) for _ in heads])

    def score_tile(st, j, u, h):
        i = st["i"]
        rows = slice(j * blk + u * sub, j * blk + (u + 1) * sub)
        s = lax.dot_general(kh_scr[h, rows, :], st["qh"], contract_lanes, preferred_element_type=F32)
        if j == i:
            s = s + causal[u * sub:(u + 1) * sub]
        s_scr[st["buf"], h, rows, :] = s
        tmax = jnp.max(fold(s), axis=0)
        if j < i and st["selmask"][h] is not None:
            tmax = tmax + st["selmask"][h][j:j + 1, :]
        st["m8"][h] = tmax if st["m8"][h] is None else jnp.maximum(st["m8"][h], tmax)

    def prob_tile(st, j, u, h):
        i = st["i"]
        if st["m"] is None:
            st["m"] = [jnp.max(st["m8"][hh], axis=0, keepdims=True) for hh in heads]
        rows = slice(j * blk + u * sub, j * blk + (u + 1) * sub)
        sm = st["selmask"][h]
        shift = -st["m"][h] if (j == i or sm is None) else sm[j:j + 1, :] - st["m"][h]
        p = jnp.exp2(s_scr[st["buf"], h, rows, :] + shift)
        st["l8"][h] = st["l8"][h] + jnp.sum(fold(p), axis=0)
        p_scr[st["buf"], h, rows, :] = p.astype(BF16)

    def finish(st):
        i = st["i"]
        nk = (i + 1) * blk
        halves = []
        for h in heads:
            l = jnp.sum(st["l8"][h], axis=0, keepdims=True)
            ot = jnp.dot(vt_scr[h * HEAD_DIM:(h + 1) * HEAD_DIM, 0:nk], p_scr[st["buf"], h, 0:nk, :],
                         preferred_element_type=F32)
            halves.append(ot / l)
        ot = jnp.concatenate(halves, axis=0)
        o_ref[0, i * blk:(i + 1) * blk, :] = (ot.T * _silu(g_ref[0, i * blk:(i + 1) * blk, :])).astype(o_ref.dtype)

    cur = begin(0)
    for t in tiles(0):
        score_tile(cur, *t)
    prev = None
    for i in range(n_blocks):
        nxt = begin(i + 1) if i + 1 < n_blocks else None
        a, b = tiles(i), (tiles(i + 1) if nxt is not None else [])
        for k in range(max(len(a), len(b))):
            if k == 1 and prev is not None:
                finish(prev)
                prev = None
            if k < len(b):
                score_tile(nxt, *b[k])
            if k < len(a):
                prob_tile(cur, *a[k])
        if prev is not None:
            finish(prev)
        prev, cur = cur, nxt
    finish(prev)


def _moba_prompt(mq3, mk3, mv3, mg3, prev_kv):
    b, length, _ = mq3.shape
    n_pairs = MOBA_W // LANES
    n_blocks = length // MOBA_BLOCK
    n_prev = 0 if prev_kv is None else prev_kv[0].shape[0]
    rows = pl.BlockSpec((1, length, LANES), lambda bi, p: (bi, 0, p))
    cols = lambda n: pl.BlockSpec((n, 1, LANES, length), lambda bi, p: (0, bi, p, 0))
    stacked = jax.ShapeDtypeStruct((n_prev + 1, b, MOBA_W, length), F32)
    return pl.pallas_call(
        functools.partial(_moba_prompt_kernel, n_blocks=n_blocks, n_prev=n_prev),
        grid=(b, n_pairs),
        in_specs=[rows, rows, rows, rows] + ([cols(n_prev), cols(n_prev)] if n_prev else []),
        out_specs=[rows, cols(n_prev + 1), cols(n_prev + 1)],
        out_shape=[jax.ShapeDtypeStruct((b, length, MOBA_W), BF16), stacked, stacked],
        scratch_shapes=[
            pltpu.VMEM((2, length, LANES), BF16),
            pltpu.VMEM((LANES, length), BF16),
            pltpu.VMEM((n_blocks, LANES), F32),
            pltpu.VMEM((2, 2, length, MOBA_BLOCK), F32),
            pltpu.VMEM((2, 2, length, MOBA_BLOCK), BF16),
        ],
        compiler_params=_cparams(("arbitrary", "arbitrary")),
        name="moba_prompt",
    )(mq3, mk3, mv3, mg3, *(prev_kv or ()))


def _moba_sample_kernel(pt_ref, q_ref, kn_ref, vn_ref, g_ref, ck_ref, cv_ref, o_ref,
                        kring, ksem, vsel, vsem, idx_v, idx_s, isem, qs_scr, s_scr, p_scr, pc_scr, tail_scr, *,
                        layer, n_pages, page, n_tokens):
    b = pl.program_id(0)
    n_batches = pl.num_programs(0) - 1
    last = n_batches - 1
    n_blocks = n_pages * page // MOBA_BLOCK
    ppb = MOBA_BLOCK // page
    slots = n_tokens * min(MOBA_TOPK, n_blocks)
    group = math.gcd(n_pages, 16)
    rows = MOBA_HEADS * SUBLANES
    contract_lanes = (((1,), (1,)), ((), ()))
    lane = lax.broadcasted_iota(jnp.int32, (SUBLANES, MOBA_W), 1)

    def k_copy(bi, pg):
        buf = bi % 2
        return pltpu.make_async_copy(ck_ref.at[layer, pt_ref[bi, pg]], kring.at[buf, pg], ksem.at[buf, pg])

    def v_copy(bi, h, slot, u, blk):
        src = cv_ref.at[layer, pt_ref[bi, blk * ppb + u], pl.ds(h * HEAD_DIM, HEAD_DIM), :]
        return pltpu.make_async_copy(src, vsel.at[h, slot * ppb + u], vsem.at[0])

    @pl.when(b == 0)
    def _():
        for pg in range(n_pages):
            k_copy(0, pg).start()

    @pl.when((b == 0) & (last >= 1))
    def _():
        for pg in range(n_pages):
            k_copy(1, pg).start()

    @pl.when(b <= last)
    def _():
        q = q_ref[0] * (HEAD_DIM ** -0.5 * LOG2_E)
        for h in range(MOBA_HEADS):
            qs_scr[h * SUBLANES:(h + 1) * SUBLANES, :] = jnp.where(lane // HEAD_DIM == h, q, 0.0)
        qb = qs_scr[...].astype(BF16)
        for g0 in range(0, n_pages, group):
            for pg in range(g0, g0 + group):
                k_copy(b, pg).wait()
            for pg in range(g0, g0 + group, ppb):
                keys = jnp.concatenate([kring[b % 2, pg + u].astype(BF16) for u in range(ppb)], axis=1)
                s_scr[:, pg * page:(pg + ppb) * page] = jnp.dot(qb, keys, preferred_element_type=F32)

    @pl.when(b + 2 <= last)
    def _():
        for pg in range(n_pages):
            k_copy(b + 2, pg).start()

    @pl.when(b >= 1)
    def _():
        for h in range(MOBA_HEADS):
            for j in range(slots):
                for u in range(ppb):
                    v_copy(0, h, j, u, 0).wait()
        parts = [[jnp.zeros((SUBLANES, HEAD_DIM), F32) for _ in range(4)] for _ in range(MOBA_HEADS)]
        for piece in range(slots * ppb):
            for h in range(MOBA_HEADS):
                parts[h][piece % 4] = parts[h][piece % 4] + lax.dot_general(
                    pc_scr[h, piece].astype(BF16), vsel[h, piece].astype(BF16), contract_lanes,
                    preferred_element_type=F32)
        o_sel = [(p4[0] + p4[1]) + (p4[2] + p4[3]) for p4 in parts]
        o_ref[0] = ((jnp.concatenate(o_sel, axis=1) + tail_scr[0]) / tail_scr[1] * tail_scr[2]).astype(o_ref.dtype)

    @pl.when(b <= last)
    def _():
        qb = qs_scr[...].astype(BF16)
        col = lax.broadcasted_iota(jnp.int32, (rows, LANES), 1)
        gate = jnp.zeros((rows, LANES), F32)
        for n in range(n_blocks):
            c0 = n * MOBA_BLOCK
            bsum = s_scr[:, c0:c0 + page]
            for u in range(1, ppb):
                bsum = bsum + s_scr[:, c0 + u * page:c0 + (u + 1) * page]
            gate = jnp.where(col == n, jnp.sum(bsum, axis=1, keepdims=True), gate)
        rank = jnp.zeros(gate.shape, jnp.int32)
        for jp in range(n_blocks):
            gj = gate[:, jp:jp + 1]
            rank = rank + jnp.where((gj > gate) | ((gj == gate) & (jp < col)), 1, 0)
        sel = jnp.where(rank < min(MOBA_TOPK, n_blocks), 0.0, NEG_INF)
        own = jnp.concatenate([kn_ref[0], jnp.zeros((page - SUBLANES, MOBA_W), F32)], axis=0).astype(BF16)
        t_idx = lax.broadcasted_iota(jnp.int32, (rows, page), 0) % SUBLANES
        key_idx = lax.broadcasted_iota(jnp.int32, (rows, page), 1)
        s_own = lax.dot_general(qb, own, contract_lanes, preferred_element_type=F32)
        s_own = s_own + jnp.where(key_idx <= t_idx, 0.0, NEG_INF)
        s_scr[:, n_pages * page:] = s_own
        m_part = s_own
        for n in range(n_blocks):
            add = sel[:, n:n + 1]
            for u in range(ppb):
                c0 = n * MOBA_BLOCK + u * page
                sm = s_scr[:, c0:c0 + page] + add
                s_scr[:, c0:c0 + page] = sm
                m_part = jnp.maximum(m_part, sm)
        m = jnp.max(m_part, axis=1, keepdims=True)
        l_part = jnp.zeros((rows, page), F32)
        for n in range(n_pages + 1):
            c0 = n * page
            p = jnp.exp2(s_scr[:, c0:c0 + page] - m)
            l_part = l_part + p
            p_scr[:, c0:c0 + page] = p
        l = jnp.sum(l_part, axis=1, keepdims=True)

        row8 = lax.broadcasted_iota(jnp.int32, (SUBLANES, LANES), 0)
        col8 = lax.broadcasted_iota(jnp.int32, (SUBLANES, LANES), 1)
        sel_real = jnp.where(t_idx < n_tokens, sel, NEG_INF)
        chosen = jnp.full((SUBLANES, LANES), NEG_INF, F32)
        for h in range(MOBA_HEADS):
            any_row = jnp.max(sel_real[h * SUBLANES:(h + 1) * SUBLANES], axis=0, keepdims=True)
            chosen = jnp.where(row8 == h, any_row, chosen)
        chosen = chosen == 0.0
        before = (lax.broadcasted_iota(jnp.int32, (LANES, LANES), 0)
                  < lax.broadcasted_iota(jnp.int32, (LANES, LANES), 1))
        pos = jnp.dot(jnp.where(chosen, 1.0, 0.0).astype(BF16), jnp.where(before, 1.0, 0.0).astype(BF16),
                      preferred_element_type=F32)
        count = jnp.sum(jnp.where(chosen, 1.0, 0.0), axis=1, keepdims=True)
        blk_id = col8.astype(F32)
        pick = lambda j: jnp.sum(jnp.where(chosen & (pos == j), blk_id, 0.0), axis=1, keepdims=True)
        first = pick(0)
        table = jnp.where(col8 == slots, count, 0.0)
        for j in range(slots):
            table = jnp.where(col8 == j, jnp.where(count > j, pick(j), first), table)
        idx_v[...] = table.astype(jnp.int32)
        to_smem = pltpu.make_async_copy(idx_v, idx_s, isem.at[0])
        to_smem.start()
        to_smem.wait()

        for h in range(MOBA_HEADS):
            for j in range(slots):
                for u in range(ppb):
                    v_copy(b, h, j, u, idx_s[h, j]).start(priority=1)
        for h in range(MOBA_HEADS):
            for j in range(slots):
                weight = jnp.where(j < idx_s[h, slots], 1.0, 0.0)
                for u in range(ppb):
                    c0 = pl.multiple_of((idx_s[h, j] * ppb + u) * page, page)
                    pc_scr[h, j * ppb + u] = p_scr[h * SUBLANES:(h + 1) * SUBLANES, pl.ds(c0, page)] * weight
        vown = jnp.concatenate([vn_ref[0], jnp.zeros((page - SUBLANES, MOBA_W), F32)], axis=0).astype(BF16)
        o_own = jnp.dot(p_scr[:, n_pages * page:].astype(BF16), vown, preferred_element_type=F32)
        own_rows = jnp.zeros((SUBLANES, MOBA_W), F32)
        l_rows = jnp.zeros((SUBLANES, MOBA_W), F32)
        for h in range(MOBA_HEADS):
            mine = lane // HEAD_DIM == h
            own_rows = jnp.where(mine, o_own[h * SUBLANES:(h + 1) * SUBLANES, :], own_rows)
            l_rows = jnp.where(mine, l[h * SUBLANES:(h + 1) * SUBLANES, :], l_rows)
        tail_scr[0] = own_rows
        tail_scr[1] = l_rows
        tail_scr[2] = _silu(g_ref[0])


def _moba_sample(page_table, mq3, mk3, mv3, mg3, cache_k4, cache_v4, layer, n_tokens):
    db = mq3.shape[0]
    n_pages = page_table.shape[1]
    page = cache_k4.shape[3]
    rows = MOBA_HEADS * SUBLANES
    n_blocks = n_pages * page // MOBA_BLOCK
    slots = n_tokens * min(MOBA_TOPK, n_blocks)
    assert n_blocks <= LANES and page == LANES and slots < LANES and MOBA_HEADS <= SUBLANES
    blk3 = (1, SUBLANES, MOBA_W)
    this = pl.BlockSpec(blk3, lambda bi, pt: (jnp.minimum(bi, db - 1), 0, 0))
    before = pl.BlockSpec(blk3, lambda bi, pt: (jnp.maximum(bi - 1, 0), 0, 0))
    hbm = pl.BlockSpec(memory_space=pl.ANY)
    width = (n_pages + 1) * page
    pieces = slots * MOBA_BLOCK // page
    return pl.pallas_call(
        functools.partial(_moba_sample_kernel, layer=layer, n_pages=n_pages, page=page, n_tokens=n_tokens),
        grid_spec=pltpu.PrefetchScalarGridSpec(
            num_scalar_prefetch=1,
            grid=(db + 1,),
            in_specs=[this, this, this, this, hbm, hbm],
            out_specs=before,
            scratch_shapes=[
                pltpu.VMEM((2, n_pages, MOBA_W, page), F32),
                pltpu.SemaphoreType.DMA((2, n_pages)),
                pltpu.VMEM((MOBA_HEADS, pieces, HEAD_DIM, page), F32),
                pltpu.SemaphoreType.DMA((1,)),
                pltpu.VMEM((SUBLANES, LANES), jnp.int32),
                pltpu.SMEM((SUBLANES, LANES), jnp.int32),
                pltpu.SemaphoreType.DMA((1,)),
                pltpu.VMEM((rows, MOBA_W), F32),
                pltpu.VMEM((rows, width), F32),
                pltpu.VMEM((rows, width), F32),
                pltpu.VMEM((MOBA_HEADS, pieces, SUBLANES, page), F32),
                pltpu.VMEM((3, SUBLANES, MOBA_W), F32),
            ],
        ),
        out_shape=jax.ShapeDtypeStruct((db, SUBLANES, MOBA_W), BF16),
        compiler_params=_cparams(("arbitrary",)),
        name="moba_sample",
    )(page_table, mq3, mk3, mv3, mg3, cache_k4, cache_v4)


def _pool_kernel(u_ref, g_ref, buf_ref, inv_cnt_ref, w_ref, b_ref, sc_ref, o_ref, bufo_ref, x_scr, w_scr, *,
                 n_tokens, rows):
    bb, length, _ = u_ref.shape
    lead = SUBLANES
    halo = lead + POOL_BUF + 1
    assert POOL_WINDOWS == (2, 4, 8, 16) and halo % SUBLANES == 0
    tiles = [(lead, halo - lead)] + [(halo + r0, rows) for r0 in range(0, length, rows)]
    for bi in range(bb):
        x_scr[bi, 0:lead, :] = jnp.zeros((lead, POOL_W), F32)
        x_scr[bi, lead:halo, :] = buf_ref[bi]
        x_scr[bi, halo:, :] = u_ref[bi]
        w_scr[bi, :, 0:lead, :] = jnp.zeros((3, lead, POOL_W), F32)
        for i0, n in tiles:
            group = lax.broadcasted_iota(jnp.int32, (n, POOL_W), 1) // POOL_GC
            u = x_scr[bi, i0:i0 + n, :]
            w2 = u + x_scr[bi, i0 - 1:i0 - 1 + n, :]
            w_scr[bi, 0, i0:i0 + n, :] = w2
            w4 = w2 + w_scr[bi, 0, i0 - 2:i0 - 2 + n, :]
            w_scr[bi, 1, i0:i0 + n, :] = w4
            w8 = w4 + w_scr[bi, 1, i0 - 4:i0 - 4 + n, :]
            w_scr[bi, 2, i0:i0 + n, :] = w8
            if i0 < halo:
                continue
            w16 = w8 + w_scr[bi, 2, i0 - 8:i0 - 8 + n, :]
            r0 = i0 - halo
            win = jnp.where(group == 0, w2, jnp.where(group == 1, w4, jnp.where(group == 2, w8, w16)))
            pooled = win * inv_cnt_ref[r0:r0 + n, :] - u
            y = jnp.dot(pooled.astype(BF16), w_ref[...], preferred_element_type=F32) + b_ref[...]
            y = y * sc_ref[...]
            o_ref[bi, r0:r0 + n, :] = (y * _silu(g_ref[bi, r0:r0 + n, :])).astype(o_ref.dtype)
        bufo_ref[bi] = x_scr[bi, lead + n_tokens + 1:lead + n_tokens + 1 + POOL_BUF, :]


def _pool(zp3, buf16, cnt, w_bd, bias, scale, *, n_tokens, rows, bb):
    b, length, _ = zp3.shape
    const = lambda bi: (0, 0)
    n_rows = SUBLANES + POOL_BUF + 1 + length
    return pl.pallas_call(
        functools.partial(_pool_kernel, n_tokens=n_tokens, rows=rows),
        grid=(b // bb,),
        in_specs=[
            pl.BlockSpec((bb, length, POOL_W), lambda bi: (bi, 0, 0)),
            pl.BlockSpec((bb, length, POOL_W), lambda bi: (bi, 0, 1)),
            pl.BlockSpec((bb, POOL_BUF + 1, POOL_W), lambda bi: (bi, 0, 0)),
            pl.BlockSpec((length, POOL_W), const),
            pl.BlockSpec((POOL_W, POOL_W), const),
            pl.BlockSpec((1, POOL_W), const),
            pl.BlockSpec((1, POOL_W), const),
        ],
        out_specs=[
            pl.BlockSpec((bb, length, POOL_W), lambda bi: (bi, 0, 0)),
            pl.BlockSpec((bb, POOL_BUF, POOL_W), lambda bi: (bi, 0, 0)),
        ],
        out_shape=[
            jax.ShapeDtypeStruct((b, length, POOL_W), BF16),
            jax.ShapeDtypeStruct((b, POOL_BUF, POOL_W), F32),
        ],
        scratch_shapes=[pltpu.VMEM((bb, n_rows, POOL_W), F32), pltpu.VMEM((bb, 3, n_rows, POOL_W), F32)],
        compiler_params=_cparams(("arbitrary",)),
        name="pool",
    )(zp3, zp3, buf16, cnt, w_bd, bias, scale)


def _out_proj_kernel(mr_ref, mm_ref, mp_ref, x_ref, w_ref, y_ref, mix_scr):
    mix_scr[:, :RET_W] = mr_ref[...]
    mix_scr[:, RET_W:RET_W + MOBA_W] = mm_ref[...]
    mix_scr[:, RET_W + MOBA_W:] = mp_ref[...]
    y_ref[...] = x_ref[...] + jnp.dot(mix_scr[...], w_ref[...], preferred_element_type=F32)


def _out_proj(mr, mm, mp, x2d, w_bf, *, tm, layer):
    m, d = x2d.shape
    d_mix = RET_W + MOBA_W + POOL_W
    row = lambda i: (i, 0)
    return pl.pallas_call(
        _out_proj_kernel,
        grid=(m // tm,),
        in_specs=[
            pl.BlockSpec((tm, RET_W), row),
            pl.BlockSpec((tm, MOBA_W), row),
            pl.BlockSpec((tm, POOL_W), row),
            pl.BlockSpec((tm, d), row),
            pl.BlockSpec((None, d_mix, d), lambda i: (layer, 0, 0)),
        ],
        out_specs=pl.BlockSpec((tm, d), row),
        out_shape=jax.ShapeDtypeStruct((m, d), F32),
        scratch_shapes=[pltpu.VMEM((tm, d_mix), BF16)],
        compiler_params=_cparams(("arbitrary",)),
        name="out_proj",
    )(mr, mm, mp, x2d, w_bf)


def _rope_tables(pos):
    inv = 1.0 / (ROPE_THETA ** (np.arange(HALF, dtype=np.float64) / HALF))
    ang = np.asarray(pos, np.float64)[:, None] * inv[None, :]
    c, s = np.cos(ang), np.sin(ang)
    return (jnp.asarray(np.concatenate([c, c, c, c], axis=-1), F32),
            jnp.asarray(np.concatenate([-s, s, -s, s], axis=-1), F32))


def _block_diag(blocks):
    g, n, _ = blocks.shape
    eye = jnp.eye(g, dtype=blocks.dtype)
    return (eye[:, None, :, None] * blocks[:, :, None, :]).reshape(g * n, g * n)


def _pool_inv_counts(pos0, length):
    pos = pos0 + np.arange(length)
    w = np.repeat(np.asarray(POOL_WINDOWS), POOL_GC)
    return jnp.asarray(1.0 / np.minimum(pos[:, None] + 1, w[None, :]), F32)


def _layer(x3, pos_tabs, ret_state, ret_prev, ret_tabs, ret_step, pool_buf16, pool_cnt, n_tokens, moba_fn, params, *,
           tm, layer):
    b, length, d = x3.shape
    nw, w_in, w_out, gnw, qnw, knw, bd, pw_bd, pbias, pscale = params
    x2d = x3.reshape(b * length, d)
    zr, mq, mk, mv, mg, zp = _in_proj(x2d, nw, w_in, pos_tabs[0], pos_tabs[1], qnw, knw, bd, tm=tm, layer=layer)
    three = lambda a: a.reshape(b, length, a.shape[-1])
    mix_r, states = _retention(three(zr), ret_state, ret_prev, ret_tabs, gnw, chunk=RET_CHUNK, bb=ret_step[0],
                               pairs=ret_step[1], layer=layer)
    mix_m, k_new, v_new = moba_fn(three(mq), three(mk), three(mv), three(mg))
    mix_p, buf_new = _pool(three(zp), pool_buf16, pool_cnt, pw_bd, pbias, pscale,
                           n_tokens=n_tokens, rows=min(length, 128), bb=ret_step[0])
    two = lambda a: a.reshape(b * length, a.shape[-1])
    tm_out = max(t for t in (tm, 2 * tm, 4 * tm) if (b * length) % t == 0)
    y = _out_proj(two(mix_r), two(mix_m), two(mix_p), x2d, w_out, tm=tm_out, layer=layer)
    return y.reshape(b, length, d), states, k_new, v_new, buf_new


def _moba_prompt_wrap(mq3, mk3, mv3, mg3, *, prev_kv):
    return _moba_prompt(mq3, mk3, mv3, mg3, prev_kv)


def _moba_sample_wrap(mq3, mk3, mv3, mg3, *, page_table, cache_k4, cache_v4, layer, n_tokens):
    mix = _moba_sample(page_table, mq3, mk3, mv3, mg3, cache_k4, cache_v4, layer, n_tokens)
    rows = lambda t: t[:, :n_tokens].reshape(t.shape[0], n_tokens, MOBA_HEADS, HEAD_DIM)
    return mix, rows(mk3), rows(mv3)


def kernel(x_prompt, x_sample, cache_k, cache_v, state_ret, state_pool, page_table, norm_w, w_in, w_out, ret_gn_w, q_norm_w, k_norm_w, pool_w, pool_b, pool_scale):
    depth = w_in.shape[0]
    b, seq, d = x_prompt.shape
    db, dec_seq, _ = x_sample.shape
    n_pool, page = cache_k.shape[1], cache_k.shape[2]
    past_len = page_table.shape[1] * page
    assert dec_seq <= SUBLANES and seq % MOBA_BLOCK == 0 and seq % RET_CHUNK == 0 and past_len % MOBA_BLOCK == 0

    p_tabs = _rope_tables(np.arange(seq))
    tm_s = db * SUBLANES
    s_tabs = _rope_tables(np.tile(past_len + np.arange(SUBLANES), db))
    p_ret_tabs = _retention_tables(RET_CHUNK, RET_CHUNK)
    s_ret_tabs = _retention_tables(RET_CHUNK, dec_seq)
    p_cnt = _pool_inv_counts(0, seq)
    s_cnt = _pool_inv_counts(past_len, SUBLANES)
    head_of = np.arange(MOBA_W) // HEAD_DIM
    bd = jnp.asarray((head_of[:, None] == head_of[None, :]) / HEAD_DIM, BF16)
    zero_buf = jnp.zeros((b, POOL_BUF + 1, POOL_W), F32)

    cache_k4 = cache_k.transpose(0, 1, 3, 4, 2).reshape(depth, n_pool, MOBA_W, page)
    cache_v4 = cache_v.transpose(0, 1, 3, 4, 2).reshape(depth, n_pool, MOBA_W, page)
    xp = x_prompt
    xs = jnp.pad(x_sample, ((0, 0), (0, SUBLANES - dec_seq), (0, 0)))

    w_in_bf = w_in.astype(BF16)
    w_out_bf = w_out.astype(BF16)
    outs = [[] for _ in range(4)]
    prev_kv = s_p = s_s = None
    for l in range(depth):
        params = (norm_w[l].reshape(1, d), w_in_bf, w_out_bf,
                  ret_gn_w[l].reshape(RET_W // LANES, 1, LANES),
                  jnp.tile(q_norm_w[l], MOBA_HEADS).reshape(1, MOBA_W),
                  jnp.tile(k_norm_w[l], MOBA_HEADS).reshape(1, MOBA_W),
                  bd, _block_diag(pool_w[l]).astype(BF16),
                  pool_b[l].reshape(1, POOL_W), pool_scale[l].reshape(1, POOL_W))
        xp, s_p, kt, vt, b_p = _layer(xp, p_tabs, None, s_p, p_ret_tabs, (1, 1), zero_buf, p_cnt, seq,
                                      functools.partial(_moba_prompt_wrap, prev_kv=prev_kv), params,
                                      tm=512, layer=l)
        prev_kv = (kt, vt)
        moba_s = functools.partial(_moba_sample_wrap, page_table=page_table, cache_k4=cache_k4,
                                   cache_v4=cache_v4, layer=l, n_tokens=dec_seq)
        xs, s_s, k_s, v_s, b_s = _layer(xs, s_tabs, state_ret, s_s, s_ret_tabs,
                                        (math.gcd(db, 8), RET_W // LANES),
                                        jnp.pad(state_pool[l], ((0, 0), (1, 0), (0, 0))), s_cnt, dec_seq,
                                        moba_s, params, tm=tm_s, layer=l)
        for lst, val in zip(outs, (k_s, v_s, b_p, b_s)):
            lst.append(val)
    rows = lambda t: t.reshape(depth, b, MOBA_HEADS, HEAD_DIM, seq).transpose(0, 1, 4, 2, 3)
    k_s, v_s, b_p, b_s = (jnp.stack(o) for o in outs)
    return xp, xs[:, :dec_seq], rows(prev_kv[0]), rows(prev_kv[1]), k_s, v_s, s_p, s_s, b_p, b_s
```

```python
import functools
import math

import jax
import jax.numpy as jnp
import numpy as np
from jax import lax
from jax.experimental import pallas as pl
from jax.experimental.pallas import tpu as pltpu

F32 = jnp.float32
BF16 = jnp.bfloat16

HEAD_DIM = 64
HALF = HEAD_DIM // 2
LANES = 128
RET_HEADS = 6
MOBA_HEADS = 6
RET_W = RET_HEADS * HEAD_DIM
MOBA_W = MOBA_HEADS * HEAD_DIM
POOL_W = 256
POOL_GC = 64
POOL_WINDOWS = (2, 4, 8, 16)
POOL_BUF = 15
RET_CHUNK = 128
MOBA_BLOCK = 256
MOBA_TOPK = 3
ROPE_THETA = 10000.0
EPS = 1e-6
SUBLANES = 8
NEG_INF = float("-inf")
LOG2_E = 1.4426950408889634

OFF_RQ, OFF_RV, OFF_MQ, OFF_MV, OFF_PU = 0, 2 * RET_W, 4 * RET_W, 4 * RET_W + 2 * MOBA_W, 4 * RET_W + 4 * MOBA_W
D_IN = OFF_PU + 2 * POOL_W

VMEM_LIMIT = 56 * 1024 * 1024


def _silu(x):
    return x / (1.0 + jnp.exp(-x))


def _cparams(sem):
    return pltpu.CompilerParams(dimension_semantics=sem, vmem_limit_bytes=VMEM_LIMIT)


def _in_proj_kernel(x_ref, nw_ref, w_ref, cos_ref, sin_ref, qnw_ref, knw_ref,
                    zr_ref, mq_ref, mk_ref, mv_ref, mg_ref, zp_ref, h_scr, z_scr, zm_scr, zv_scr):
    tm = x_ref.shape[0]
    x = x_ref[...]
    ms = jnp.mean(x * x, axis=-1, keepdims=True)
    h_scr[...] = (x * lax.rsqrt(ms + EPS) * nw_ref[...]).astype(BF16)
    cos = cos_ref[...]
    sin = sin_ref[...]
    lane = lax.broadcasted_iota(jnp.int32, (tm, LANES), 1)
    first_half = (lane & HALF) == 0

    def rope(z):
        partner = jnp.where(first_half, pltpu.roll(z, LANES - HALF, 1), pltpu.roll(z, HALF, 1))
        return z * cos + partner * sin

    def proj(off, width):
        return jnp.dot(h_scr[...], w_ref[:, off:off + width], preferred_element_type=F32)

    head0 = lane < HEAD_DIM

    def ret_epilogue(g):
        out = rope(z_scr[:, g * LANES:(g + 1) * LANES])
        if g >= RET_W // LANES:
            out = out * (HEAD_DIM ** -0.5)
        zr_ref[:, g * LANES:(g + 1) * LANES] = out

    def moba_epilogue(t, g):
        nw, dst = ((qnw_ref, mq_ref), (knw_ref, mk_ref))[t]
        c0 = t * MOBA_W + g * LANES
        z = zm_scr[:, c0:c0 + LANES]
        zz = z * z
        ms0 = jnp.sum(jnp.where(head0, zz, 0.0), axis=1, keepdims=True)
        ms1 = jnp.sum(jnp.where(head0, 0.0, zz), axis=1, keepdims=True)
        msq = jnp.where(head0, ms0, ms1) * (1.0 / HEAD_DIM)
        zn = z * lax.rsqrt(msq + EPS) * nw[:, g * LANES:(g + 1) * LANES]
        dst[:, g * LANES:(g + 1) * LANES] = rope(zn)

    groups = RET_W // LANES
    piece = 2 * LANES
    z_scr[...] = proj(OFF_RQ, 2 * RET_W)
    zm_scr[...] = proj(OFF_MQ, 2 * MOBA_W)
    for g in range(2 * groups):
        if g % 2 == 0:
            c0 = OFF_RV + (g // 2) * piece
            zr_ref[:, c0:c0 + piece] = proj(c0, piece)
        ret_epilogue(g)
    for g in range(groups):
        c0 = g * piece
        zv_scr[:, c0:c0 + piece] = proj(OFF_MV + c0, piece)
        moba_epilogue(0, g)
    for g in range(groups):
        if g < 2 * POOL_W // piece:
            zp_ref[:, g * piece:(g + 1) * piece] = proj(OFF_PU + g * piece, piece)
        moba_epilogue(1, g)
    mv_ref[...] = zv_scr[:, :MOBA_W]
    mg_ref[...] = zv_scr[:, MOBA_W:]


def _in_proj(x2d, nw, w_bf, cos, sin, qnw, knw, *, tm, layer):
    m, d = x2d.shape
    n_pos = cos.shape[0] // tm
    row = lambda i: (i, 0)
    const = lambda i: (0, 0)
    pos = lambda i: (i % n_pos, 0)
    outs = [(2 * RET_W + 2 * RET_W), MOBA_W, MOBA_W, MOBA_W, MOBA_W, 2 * POOL_W]
    return pl.pallas_call(
        _in_proj_kernel,
        grid=(m // tm,),
        in_specs=[
            pl.BlockSpec((tm, d), row),
            pl.BlockSpec((1, d), const),
            pl.BlockSpec((None, d, D_IN), lambda i: (layer, 0, 0)),
            pl.BlockSpec((tm, LANES), pos),
            pl.BlockSpec((tm, LANES), pos),
            pl.BlockSpec((1, MOBA_W), const),
            pl.BlockSpec((1, MOBA_W), const),
        ],
        out_specs=[pl.BlockSpec((tm, w), row) for w in outs],
        out_shape=[jax.ShapeDtypeStruct((m, w), F32) for w in outs],
        scratch_shapes=[pltpu.VMEM((tm, d), BF16), pltpu.VMEM((tm, 2 * RET_W), F32),
                        pltpu.VMEM((tm, 2 * MOBA_W), F32), pltpu.VMEM((tm, 2 * MOBA_W), F32)],
        compiler_params=_cparams(("arbitrary",)),
        name="in_proj",
    )(x2d, nw, w_bf, cos, sin, qnw, knw)


def _retention_kernel(*refs, chunk, has_state, n_prev):
    q_ref, k_ref, v_ref, g_ref = refs[:4]
    s0_ref = refs[4] if has_state else None
    prev_ref = refs[4 + has_state] if n_prev else None
    dm_ref, rs_ref, kd_ref, gc_ref, gnw_ref, o_ref, sout_ref, oi_scr, kv_scr, sb_scr = refs[4 + has_state + (n_prev > 0):]
    if n_prev:
        sout_ref[0:n_prev] = prev_ref[...]
    bb, length, width = q_ref.shape
    pairs = width // LANES
    rows_in = min(length, chunk)
    n_chunks = max(1, length // chunk)
    lane = lax.broadcasted_iota(jnp.int32, (chunk, LANES), 1)
    head0 = lane < HEAD_DIM
    r = lax.broadcasted_iota(jnp.int32, (LANES, LANES), 0)
    c = lax.broadcasted_iota(jnp.int32, (LANES, LANES), 1)
    same_head = (r < HEAD_DIM) == (c < HEAD_DIM)
    contract_lanes = (((1,), (1,)), ((), ()))

    def head_mean(x):
        m0 = jnp.sum(jnp.where(head0, x, 0.0), axis=1, keepdims=True)
        m1 = jnp.sum(jnp.where(head0, 0.0, x), axis=1, keepdims=True)
        return jnp.where(head0, m0, m1) * (1.0 / HEAD_DIM)

    def load(ref, bi, p, ci):
        a = ref[bi, ci * chunk:ci * chunk + rows_in, p * LANES:(p + 1) * LANES]
        if rows_in < chunk:
            a = jnp.concatenate([a, jnp.zeros((chunk - rows_in, LANES), F32)], axis=0)
        return a

    items = [(bi, p, ci) for bi in range(bb) for p in range(pairs) for ci in range(n_chunks)]

    def a1(bi, p, ci):
        q = load(q_ref, bi, p, ci)
        k = load(k_ref, bi, p, ci)
        kb = k.astype(BF16)
        vb = load(v_ref, bi, p, ci).astype(BF16)
        q0 = jnp.where(head0, q, 0.0).astype(BF16)
        q1 = jnp.where(head0, 0.0, q).astype(BF16)
        in0 = (lax.dot_general(q0, kb, contract_lanes, preferred_element_type=F32) * dm_ref[2 * p]).astype(BF16)
        in1 = (lax.dot_general(q1, kb, contract_lanes, preferred_element_type=F32) * dm_ref[2 * p + 1]).astype(BF16)
        return bi, p, ci, k, vb, in0, in1

    def a2(st):
        bi, p, ci, k, vb, in0, in1 = st
        slot = bi * pairs + p
        oi_scr[slot, ci * chunk:(ci + 1) * chunk, :] = jnp.where(
            head0, jnp.dot(in0, vb, preferred_element_type=F32), jnp.dot(in1, vb, preferred_element_type=F32))
        kdt = (k * kd_ref[p]).T.astype(BF16)
        kv_scr[slot, ci] = jnp.where(same_head, jnp.dot(kdt, vb, preferred_element_type=F32), 0.0)

    st = None
    for item in items + [None]:
        nxt = a1(*item) if item is not None else None
        if st is not None:
            a2(st)
        st = nxt

    zero = jnp.zeros((HEAD_DIM, HEAD_DIM), F32)
    for bi in range(bb):
        for p in range(pairs):
            slot = bi * pairs + p
            if s0_ref is None:
                s = jnp.zeros((LANES, LANES), F32)
            else:
                s = jnp.concatenate([jnp.concatenate([s0_ref[bi, 2 * p], zero], axis=1),
                                     jnp.concatenate([zero, s0_ref[bi, 2 * p + 1]], axis=1)], axis=0)
            for ci in range(n_chunks):
                sb_scr[slot, ci] = s.astype(BF16)
                s = s * gc_ref[p] + kv_scr[slot, ci]
            sout_ref[n_prev, bi, 2 * p] = s[:HEAD_DIM, :HEAD_DIM]
            sout_ref[n_prev, bi, 2 * p + 1] = pltpu.roll(s[HEAD_DIM:], HEAD_DIM, 1)[:, :HEAD_DIM]

    def c1(bi, p, ci):
        slot = bi * pairs + p
        qb = load(q_ref, bi, p, ci).astype(BF16)
        o = (oi_scr[slot, ci * chunk:(ci + 1) * chunk, :]
             + jnp.dot(qb, sb_scr[slot, ci], preferred_element_type=F32) * rs_ref[p])
        return bi, p, ci, o, head_mean(o)

    def c2(st):
        bi, p, ci, o, mu = st
        oc = o - mu
        return bi, p, ci, oc, head_mean(oc * oc)

    def c3(st):
        bi, p, ci, oc, var = st
        on = oc * lax.rsqrt(var + EPS) * gnw_ref[p]
        out = on * _silu(load(g_ref, bi, p, ci))
        o_ref[bi, ci * chunk:ci * chunk + rows_in, p * LANES:(p + 1) * LANES] = out[:rows_in].astype(o_ref.dtype)

    s1 = s2 = None
    for item in items + [None, None]:
        new1 = c1(*item) if item is not None else None
        new2 = c2(s1) if s1 is not None else None
        if s2 is not None:
            c3(s2)
        s1, s2 = new1, new2


def _retention(zr3, state, prev_states, tabs, gnw, *, chunk, bb, pairs, layer):
    b, length, _ = zr3.shape
    n_pairs = RET_W // LANES
    dm, rs, kd, gc = tabs
    w = pairs * LANES
    n_chunks = max(1, length // chunk)
    n_prev = 0 if prev_states is None else prev_states.shape[0]
    col = lambda off: (lambda bi, p: (bi, 0, off + p))
    tab = lambda bi, p: (p, 0, 0)
    heads_blk = (bb, 2 * pairs, HEAD_DIM, HEAD_DIM)
    stacked = lambda n: pl.BlockSpec((n,) + heads_blk, lambda bi, p: (0, bi, p, 0, 0))
    extra_specs, extra_args = [], []
    if state is not None:
        extra_specs.append(pl.BlockSpec((None,) + heads_blk, lambda bi, p: (layer, bi, p, 0, 0)))
        extra_args.append(state)
    if n_prev:
        extra_specs.append(stacked(n_prev))
        extra_args.append(prev_states)
    return pl.pallas_call(
        functools.partial(_retention_kernel, chunk=chunk, has_state=state is not None, n_prev=n_prev),
        grid=(b // bb, n_pairs // pairs),
        in_specs=[
            pl.BlockSpec((bb, length, w), col(0)),
            pl.BlockSpec((bb, length, w), col(n_pairs // pairs)),
            pl.BlockSpec((bb, length, w), col(2 * n_pairs // pairs)),
            pl.BlockSpec((bb, length, w), col(3 * n_pairs // pairs)),
        ] + extra_specs + [
            pl.BlockSpec((2 * pairs, chunk, chunk), tab),
            pl.BlockSpec((pairs, chunk, LANES), tab),
            pl.BlockSpec((pairs, chunk, LANES), tab),
            pl.BlockSpec((pairs, 1, LANES), tab),
            pl.BlockSpec((pairs, 1, LANES), tab),
        ],
        out_specs=[pl.BlockSpec((bb, length, w), lambda bi, p: (bi, 0, p)), stacked(n_prev + 1)],
        out_shape=[
            jax.ShapeDtypeStruct((b, length, RET_W), BF16),
            jax.ShapeDtypeStruct((n_prev + 1, b, RET_HEADS, HEAD_DIM, HEAD_DIM), F32),
        ],
        scratch_shapes=[
            pltpu.VMEM((bb * pairs, n_chunks * chunk, LANES), F32),
            pltpu.VMEM((bb * pairs, n_chunks, LANES, LANES), F32),
            pltpu.VMEM((bb * pairs, n_chunks, LANES, LANES), BF16),
        ],
        compiler_params=_cparams(("arbitrary", "arbitrary")),
        name="retention",
    )(zr3, zr3, zr3, zr3, *extra_args, dm, rs, kd, gc, gnw)


def _retention_tables(chunk, n_tokens):
    lg = np.log(1.0 - 2.0 ** (-5.0 - np.arange(RET_HEADS, dtype=np.float64)))
    i = np.arange(chunk, dtype=np.float64)
    rel = i[:, None] - i[None, :]
    dm = np.where(rel[None] >= 0, np.exp(rel[None] * lg[:, None, None]), 0.0)
    lg_lanes = np.repeat(lg, HEAD_DIM).reshape(RET_W // LANES, 1, LANES)
    rs = np.exp((i + 1.0)[None, :, None] * lg_lanes)
    kd = np.where((i < n_tokens)[None, :, None], np.exp((n_tokens - 1.0 - i)[None, :, None] * lg_lanes), 0.0)
    gc = np.exp(float(n_tokens) * lg_lanes)
    return tuple(jnp.asarray(t, F32) for t in (dm, rs, kd, gc))


def _moba_prompt_kernel(*refs, n_blocks, n_prev):
    q_ref, k_ref, v_ref, g_ref = refs[:4]
    prev = refs[4:6] if n_prev else ()
    o_ref, kt_ref, vt_ref, kh_scr, vt_scr, km_scr, s_scr, p_scr = refs[4 + len(prev):]
    blk = MOBA_BLOCK
    sub = blk
    contract_lanes = (((1,), (1,)), ((), ()))
    lane = lax.broadcasted_iota(jnp.int32, (blk, LANES), 1)
    head0 = lane < HEAD_DIM
    for j in range(n_blocks):
        kj = k_ref[0, j * blk:(j + 1) * blk, :]
        kh_scr[0, j * blk:(j + 1) * blk, :] = jnp.where(head0, kj, 0.0).astype(BF16)
        kh_scr[1, j * blk:(j + 1) * blk, :] = jnp.where(head0, 0.0, kj).astype(BF16)
        vt = v_ref[0, j * blk:(j + 1) * blk, :].T
        vt_scr[:, j * blk:(j + 1) * blk] = vt.astype(BF16)
        km_scr[j:j + 1, :] = jnp.sum(kj, axis=0, keepdims=True) * (1.0 / blk)
        kt_ref[n_prev, 0, :, j * blk:(j + 1) * blk] = kj.T
        vt_ref[n_prev, 0, :, j * blk:(j + 1) * blk] = vt
    if n_prev:
        kt_ref[0:n_prev] = prev[0][...]
        vt_ref[0:n_prev] = prev[1][...]

    km = km_scr[...]
    lane8 = lax.broadcasted_iota(jnp.int32, (n_blocks, LANES), 1)
    row8 = lax.broadcasted_iota(jnp.int32, (n_blocks, blk), 0)
    krow = lax.broadcasted_iota(jnp.int32, (blk, blk), 0)
    qcol = lax.broadcasted_iota(jnp.int32, (blk, blk), 1)
    causal = jnp.where(krow <= qcol, 0.0, NEG_INF)
    fold = lambda t: t.reshape(t.shape[0] // SUBLANES, SUBLANES, blk)
    km_parts = []
    for h in range(2):
        kmh = jnp.where((lane8 < HEAD_DIM) == (h == 0), km, 0.0)
        kmh_hi = kmh.astype(BF16)
        km_parts.append((kmh_hi, (kmh - kmh_hi.astype(F32)).astype(BF16)))

    heads = range(2)
    tiles = lambda i: [(j, u, h) for j in range(i + 1) for u in range(blk // sub) for h in heads]

    def begin(i):
        q = q_ref[0, i * blk:(i + 1) * blk, :] * (HEAD_DIM ** -0.5 * LOG2_E)
        qh = q.astype(BF16)
        selmask = [None, None]
        if i > MOBA_TOPK:
            ql = (q - qh.astype(F32)).astype(BF16)
            for h in heads:
                kmh_hi, kmh_lo = km_parts[h]
                gate = (lax.dot_general(kmh_hi, qh, contract_lanes, preferred_element_type=F32)
                        + lax.dot_general(kmh_lo, qh, contract_lanes, preferred_element_type=F32)
                        + lax.dot_general(kmh_hi, ql, contract_lanes, preferred_element_type=F32))
                rank = jnp.zeros((n_blocks, blk), jnp.int32)
                for jp in range(i):
                    gj = gate[jp:jp + 1, :]
                    rank = rank + jnp.where((gj > gate) | ((gj == gate) & (jp < row8)), 1, 0)
                selmask[h] = jnp.where((rank < MOBA_TOPK) & (row8 < i), 0.0, NEG_INF)
        return dict(i=i, buf=i % 2, qh=qh, selmask=selmask, m8=[None, None], m=None,
                    l8=[jnp.zeros((SUBLANES, blk), F32) for _ in heads])

    def score_tile(st, j, u, h):
        i = st["i"]
        rows = slice(j * blk + u * sub, j * blk + (u + 1) * sub)
        s = lax.dot_general(kh_scr[h, rows, :], st["qh"], contract_lanes, preferred_element_type=F32)
        if j == i:
            s = s + causal[u * sub:(u + 1) * sub]
        s_scr[st["buf"], h, rows, :] = s
        tmax = jnp.max(fold(s), axis=0)
        if j < i and st["selmask"][h] is not None:
            tmax = tmax + st["selmask"][h][j:j + 1, :]
        st["m8"][h] = tmax if st["m8"][h] is None else jnp.maximum(st["m8"][h], tmax)

    def prob_tile(st, j, u, h):
        i = st["i"]
        if st["m"] is None:
            st["m"] = [jnp.max(st["m8"][hh], axis=0, keepdims=True) for hh in heads]
        rows = slice(j * blk + u * sub, j * blk + (u + 1) * sub)
        sm = st["selmask"][h]
        shift = -st["m"][h] if (j == i or sm is None) else sm[j:j + 1, :] - st["m"][h]
        p = jnp.exp2(s_scr[st["buf"], h, rows, :] + shift)
        st["l8"][h] = st["l8"][h] + jnp.sum(fold(p), axis=0)
        p_scr[st["buf"], h, rows, :] = p.astype(BF16)

    def finish(st):
        i = st["i"]
        nk = (i + 1) * blk
        halves = []
        for h in heads:
            l = jnp.sum(st["l8"][h], axis=0, keepdims=True)
            ot = jnp.dot(vt_scr[h * HEAD_DIM:(h + 1) * HEAD_DIM, 0:nk], p_scr[st["buf"], h, 0:nk, :],
                         preferred_element_type=F32)
            halves.append(ot / l)
        ot = jnp.concatenate(halves, axis=0)
        o_ref[0, i * blk:(i + 1) * blk, :] = (ot.T * _silu(g_ref[0, i * blk:(i + 1) * blk, :])).astype(o_ref.dtype)

    cur = begin(0)
    for t in tiles(0):
        score_tile(cur, *t)
    prev = None
    for i in range(n_blocks):
        nxt = begin(i + 1) if i + 1 < n_blocks else None
        a, b = tiles(i), (tiles(i + 1) if nxt is not None else [])
        for k in range(max(len(a), len(b))):
            if k == 1 and prev is not None:
                finish(prev)
                prev = None
            if k < len(b):
                score_tile(nxt, *b[k])
            if k < len(a):
                prob_tile(cur, *a[k])
        if prev is not None:
            finish(prev)
        prev, cur = cur, nxt
    finish(prev)


def _moba_prompt(mq3, mk3, mv3, mg3, prev_kv):
    b, length, _ = mq3.shape
    n_pairs = MOBA_W // LANES
    n_blocks = length // MOBA_BLOCK
    n_prev = 0 if prev_kv is None else prev_kv[0].shape[0]
    rows = pl.BlockSpec((1, length, LANES), lambda bi, p: (bi, 0, p))
    cols = lambda n: pl.BlockSpec((n, 1, LANES, length), lambda bi, p: (0, bi, p, 0))
    stacked = jax.ShapeDtypeStruct((n_prev + 1, b, MOBA_W, length), F32)
    return pl.pallas_call(
        functools.partial(_moba_prompt_kernel, n_blocks=n_blocks, n_prev=n_prev),
        grid=(b, n_pairs),
        in_specs=[rows, rows, rows, rows] + ([cols(n_prev), cols(n_prev)] if n_prev else []),
        out_specs=[rows, cols(n_prev + 1), cols(n_prev + 1)],
        out_shape=[jax.ShapeDtypeStruct((b, length, MOBA_W), BF16), stacked, stacked],
        scratch_shapes=[
            pltpu.VMEM((2, length, LANES), BF16),
            pltpu.VMEM((LANES, length), BF16),
            pltpu.VMEM((n_blocks, LANES), F32),
            pltpu.VMEM((2, 2, length, MOBA_BLOCK), F32),
            pltpu.VMEM((2, 2, length, MOBA_BLOCK), BF16),
        ],
        compiler_params=_cparams(("arbitrary", "arbitrary")),
        name="moba_prompt",
    )(mq3, mk3, mv3, mg3, *(prev_kv or ()))


def _moba_sample_kernel(pt_ref, q_ref, kn_ref, vn_ref, g_ref, ck_ref, cv_ref, o_ref,
                        kring, ksem, vsel, vsem, idx_v, idx_s, isem, qs_scr, s_scr, p_scr, pc_scr, tail_scr, *,
                        layer, n_pages, page, n_tokens):
    b = pl.program_id(0)
    n_batches = pl.num_programs(0) - 1
    last = n_batches - 1
    n_blocks = n_pages * page // MOBA_BLOCK
    ppb = MOBA_BLOCK // page
    slots = n_tokens * min(MOBA_TOPK, n_blocks)
    group = math.gcd(n_pages, 16)
    rows = MOBA_HEADS * SUBLANES
    contract_lanes = (((1,), (1,)), ((), ()))
    lane = lax.broadcasted_iota(jnp.int32, (SUBLANES, MOBA_W), 1)

    def k_copy(bi, pg):
        buf = bi % 2
        return pltpu.make_async_copy(ck_ref.at[layer, pt_ref[bi, pg]], kring.at[buf, pg], ksem.at[buf, pg])

    def v_copy(bi, h, slot, u, blk):
        src = cv_ref.at[layer, pt_ref[bi, blk * ppb + u], pl.ds(h * HEAD_DIM, HEAD_DIM), :]
        return pltpu.make_async_copy(src, vsel.at[h, slot * ppb + u], vsem.at[0])

    @pl.when(b == 0)
    def _():
        for pg in range(n_pages):
            k_copy(0, pg).start()

    @pl.when((b == 0) & (last >= 1))
    def _():
        for pg in range(n_pages):
            k_copy(1, pg).start()

    @pl.when(b <= last)
    def _():
        q = q_ref[0] * (HEAD_DIM ** -0.5 * LOG2_E)
        for h in range(MOBA_HEADS):
            qs_scr[h * SUBLANES:(h + 1) * SUBLANES, :] = jnp.where(lane // HEAD_DIM == h, q, 0.0)
        qb = qs_scr[...].astype(BF16)
        for g0 in range(0, n_pages, group):
            for pg in range(g0, g0 + group):
                k_copy(b, pg).wait()
            for pg in range(g0, g0 + group, ppb):
                keys = jnp.concatenate([kring[b % 2, pg + u].astype(BF16) for u in range(ppb)], axis=1)
                s_scr[:, pg * page:(pg + ppb) * page] = jnp.dot(qb, keys, preferred_element_type=F32)

    @pl.when(b + 2 <= last)
    def _():
        for pg in range(n_pages):
            k_copy(b + 2, pg).start()

    @pl.when(b >= 1)
    def _():
        for h in range(MOBA_HEADS):
            for j in range(slots):
                for u in range(ppb):
                    v_copy(0, h, j, u, 0).wait()
        parts = [[jnp.zeros((SUBLANES, HEAD_DIM), F32) for _ in range(4)] for _ in range(MOBA_HEADS)]
        for piece in range(slots * ppb):
            for h in range(MOBA_HEADS):
                parts[h][piece % 4] = parts[h][piece % 4] + lax.dot_general(
                    pc_scr[h, piece].astype(BF16), vsel[h, piece].astype(BF16), contract_lanes,
                    preferred_element_type=F32)
        o_sel = [(p4[0] + p4[1]) + (p4[2] + p4[3]) for p4 in parts]
        o_ref[0] = ((jnp.concatenate(o_sel, axis=1) + tail_scr[0]) / tail_scr[1] * tail_scr[2]).astype(o_ref.dtype)

    @pl.when(b <= last)
    def _():
        qb = qs_scr[...].astype(BF16)
        col = lax.broadcasted_iota(jnp.int32, (rows, LANES), 1)
        gate = jnp.zeros((rows, LANES), F32)
        for n in range(n_blocks):
            c0 = n * MOBA_BLOCK
            bsum = s_scr[:, c0:c0 + page]
            for u in range(1, ppb):
                bsum = bsum + s_scr[:, c0 + u * page:c0 + (u + 1) * page]
            gate = jnp.where(col == n, jnp.sum(bsum, axis=1, keepdims=True), gate)
        rank = jnp.zeros(gate.shape, jnp.int32)
        for jp in range(n_blocks):
            gj = gate[:, jp:jp + 1]
            rank = rank + jnp.where((gj > gate) | ((gj == gate) & (jp < col)), 1, 0)
        sel = jnp.where(rank < min(MOBA_TOPK, n_blocks), 0.0, NEG_INF)
        own = jnp.concatenate([kn_ref[0], jnp.zeros((page - SUBLANES, MOBA_W), F32)], axis=0).astype(BF16)
        t_idx = lax.broadcasted_iota(jnp.int32, (rows, page), 0) % SUBLANES
        key_idx = lax.broadcasted_iota(jnp.int32, (rows, page), 1)
        s_own = lax.dot_general(qb, own, contract_lanes, preferred_element_type=F32)
        s_own = s_own + jnp.where(key_idx <= t_idx, 0.0, NEG_INF)
        s_scr[:, n_pages * page:] = s_own
        m_part = s_own
        for n in range(n_blocks):
            add = sel[:, n:n + 1]
            for u in range(ppb):
                c0 = n * MOBA_BLOCK + u * page
                sm = s_scr[:, c0:c0 + page] + add
                s_scr[:, c0:c0 + page] = sm
                m_part = jnp.maximum(m_part, sm)
        m = jnp.max(m_part, axis=1, keepdims=True)
        l_part = jnp.zeros((rows, page), F32)
        for n in range(n_pages + 1):
            c0 = n * page
            p = jnp.exp2(s_scr[:, c0:c0 + page] - m)
            l_part = l_part + p
            p_scr[:, c0:c0 + page] = p
        l = jnp.sum(l_part, axis=1, keepdims=True)

        row8 = lax.broadcasted_iota(jnp.int32, (SUBLANES, LANES), 0)
        col8 = lax.broadcasted_iota(jnp.int32, (SUBLANES, LANES), 1)
        sel_real = jnp.where(t_idx < n_tokens, sel, NEG_INF)
        chosen = jnp.full((SUBLANES, LANES), NEG_INF, F32)
        for h in range(MOBA_HEADS):
            any_row = jnp.max(sel_real[h * SUBLANES:(h + 1) * SUBLANES], axis=0, keepdims=True)
            chosen = jnp.where(row8 == h, any_row, chosen)
        chosen = chosen == 0.0
        before = (lax.broadcasted_iota(jnp.int32, (LANES, LANES), 0)
                  < lax.broadcasted_iota(jnp.int32, (LANES, LANES), 1))
        pos = jnp.dot(jnp.where(chosen, 1.0, 0.0).astype(BF16), jnp.where(before, 1.0, 0.0).astype(BF16),
                      preferred_element_type=F32)
        count = jnp.sum(jnp.where(chosen, 1.0, 0.0), axis=1, keepdims=True)
        blk_id = col8.astype(F32)
        pick = lambda j: jnp.sum(jnp.where(chosen & (pos == j), blk_id, 0.0), axis=1, keepdims=True)
        first = pick(0)
        table = jnp.where(col8 == slots, count, 0.0)
        for j in range(slots):
            table = jnp.where(col8 == j, jnp.where(count > j, pick(j), first), table)
        idx_v[...] = table.astype(jnp.int32)
        to_smem = pltpu.make_async_copy(idx_v, idx_s, isem.at[0])
        to_smem.start()
        to_smem.wait()

        for h in range(MOBA_HEADS):
            for j in range(slots):
                for u in range(ppb):
                    v_copy(b, h, j, u, idx_s[h, j]).start(priority=1)
        for h in range(MOBA_HEADS):
            for j in range(slots):
                weight = jnp.where(j < idx_s[h, slots], 1.0, 0.0)
                for u in range(ppb):
                    c0 = pl.multiple_of((idx_s[h, j] * ppb + u) * page, page)
                    pc_scr[h, j * ppb + u] = p_scr[h * SUBLANES:(h + 1) * SUBLANES, pl.ds(c0, page)] * weight
        vown = jnp.concatenate([vn_ref[0], jnp.zeros((page - SUBLANES, MOBA_W), F32)], axis=0).astype(BF16)
        o_own = jnp.dot(p_scr[:, n_pages * page:].astype(BF16), vown, preferred_element_type=F32)
        own_rows = jnp.zeros((SUBLANES, MOBA_W), F32)
        l_rows = jnp.zeros((SUBLANES, MOBA_W), F32)
        for h in range(MOBA_HEADS):
            mine = lane // HEAD_DIM == h
            own_rows = jnp.where(mine, o_own[h * SUBLANES:(h + 1) * SUBLANES, :], own_rows)
            l_rows = jnp.where(mine, l[h * SUBLANES:(h + 1) * SUBLANES, :], l_rows)
        tail_scr[0] = own_rows
        tail_scr[1] = l_rows
        tail_scr[2] = _silu(g_ref[0])


def _moba_sample(page_table, mq3, mk3, mv3, mg3, cache_k4, cache_v4, layer, n_tokens):
    db = mq3.shape[0]
    n_pages = page_table.shape[1]
    page = cache_k4.shape[3]
    rows = MOBA_HEADS * SUBLANES
    n_blocks = n_pages * page // MOBA_BLOCK
    slots = n_tokens * min(MOBA_TOPK, n_blocks)
    assert n_blocks <= LANES and page == LANES and slots < LANES and MOBA_HEADS <= SUBLANES
    blk3 = (1, SUBLANES, MOBA_W)
    this = pl.BlockSpec(blk3, lambda bi, pt: (jnp.minimum(bi, db - 1), 0, 0))
    before = pl.BlockSpec(blk3, lambda bi, pt: (jnp.maximum(bi - 1, 0), 0, 0))
    hbm = pl.BlockSpec(memory_space=pl.ANY)
    width = (n_pages + 1) * page
    pieces = slots * MOBA_BLOCK // page
    return pl.pallas_call(
        functools.partial(_moba_sample_kernel, layer=layer, n_pages=n_pages, page=page, n_tokens=n_tokens),
        grid_spec=pltpu.PrefetchScalarGridSpec(
            num_scalar_prefetch=1,
            grid=(db + 1,),
            in_specs=[this, this, this, this, hbm, hbm],
            out_specs=before,
            scratch_shapes=[
                pltpu.VMEM((2, n_pages, MOBA_W, page), F32),
                pltpu.SemaphoreType.DMA((2, n_pages)),
                pltpu.VMEM((MOBA_HEADS, pieces, HEAD_DIM, page), F32),
                pltpu.SemaphoreType.DMA((1,)),
                pltpu.VMEM((SUBLANES, LANES), jnp.int32),
                pltpu.SMEM((SUBLANES, LANES), jnp.int32),
                pltpu.SemaphoreType.DMA((1,)),
                pltpu.VMEM((rows, MOBA_W), F32),
                pltpu.VMEM((rows, width), F32),
                pltpu.VMEM((rows, width), F32),
                pltpu.VMEM((MOBA_HEADS, pieces, SUBLANES, page), F32),
                pltpu.VMEM((3, SUBLANES, MOBA_W), F32),
            ],
        ),
        out_shape=jax.ShapeDtypeStruct((db, SUBLANES, MOBA_W), BF16),
        compiler_params=_cparams(("arbitrary",)),
        name="moba_sample",
    )(page_table, mq3, mk3, mv3, mg3, cache_k4, cache_v4)


def _pool_kernel(u_ref, g_ref, buf_ref, inv_cnt_ref, w_ref, b_ref, sc_ref, o_ref, bufo_ref, x_scr, w_scr, *,
                 n_tokens, rows):
    bb, length, _ = u_ref.shape
    lead = SUBLANES
    halo = lead + POOL_BUF + 1
    assert POOL_WINDOWS == (2, 4, 8, 16) and halo % SUBLANES == 0
    tiles = [(lead, halo - lead)] + [(halo + r0, rows) for r0 in range(0, length, rows)]
    for bi in range(bb):
        x_scr[bi, 0:lead, :] = jnp.zeros((lead, POOL_W), F32)
        x_scr[bi, lead:halo, :] = buf_ref[bi]
        x_scr[bi, halo:, :] = u_ref[bi]
        w_scr[bi, :, 0:lead, :] = jnp.zeros((3, lead, POOL_W), F32)
        for i0, n in tiles:
            group = lax.broadcasted_iota(jnp.int32, (n, POOL_W), 1) // POOL_GC
            u = x_scr[bi, i0:i0 + n, :]
            w2 = u + x_scr[bi, i0 - 1:i0 - 1 + n, :]
            w_scr[bi, 0, i0:i0 + n, :] = w2
            w4 = w2 + w_scr[bi, 0, i0 - 2:i0 - 2 + n, :]
            w_scr[bi, 1, i0:i0 + n, :] = w4
            w8 = w4 + w_scr[bi, 1, i0 - 4:i0 - 4 + n, :]
            w_scr[bi, 2, i0:i0 + n, :] = w8
            if i0 < halo:
                continue
            w16 = w8 + w_scr[bi, 2, i0 - 8:i0 - 8 + n, :]
            r0 = i0 - halo
            win = jnp.where(group == 0, w2, jnp.where(group == 1, w4, jnp.where(group == 2, w8, w16)))
            pooled = win * inv_cnt_ref[r0:r0 + n, :] - u
            y = jnp.dot(pooled.astype(BF16), w_ref[...], preferred_element_type=F32) + b_ref[...]
            y = y * sc_ref[...]
            o_ref[bi, r0:r0 + n, :] = (y * _silu(g_ref[bi, r0:r0 + n, :])).astype(o_ref.dtype)
        bufo_ref[bi] = x_scr[bi, lead + n_tokens + 1:lead + n_tokens + 1 + POOL_BUF, :]


def _pool(zp3, buf16, cnt, w_bd, bias, scale, *, n_tokens, rows, bb):
    b, length, _ = zp3.shape
    const = lambda bi: (0, 0)
    n_rows = SUBLANES + POOL_BUF + 1 + length
    return pl.pallas_call(
        functools.partial(_pool_kernel, n_tokens=n_tokens, rows=rows),
        grid=(b // bb,),
        in_specs=[
            pl.BlockSpec((bb, length, POOL_W), lambda bi: (bi, 0, 0)),
            pl.BlockSpec((bb, length, POOL_W), lambda bi: (bi, 0, 1)),
            pl.BlockSpec((bb, POOL_BUF + 1, POOL_W), lambda bi: (bi, 0, 0)),
            pl.BlockSpec((length, POOL_W), const),
            pl.BlockSpec((POOL_W, POOL_W), const),
            pl.BlockSpec((1, POOL_W), const),
            pl.BlockSpec((1, POOL_W), const),
        ],
        out_specs=[
            pl.BlockSpec((bb, length, POOL_W), lambda bi: (bi, 0, 0)),
            pl.BlockSpec((bb, POOL_BUF, POOL_W), lambda bi: (bi, 0, 0)),
        ],
        out_shape=[
            jax.ShapeDtypeStruct((b, length, POOL_W), BF16),
            jax.ShapeDtypeStruct((b, POOL_BUF, POOL_W), F32),
        ],
        scratch_shapes=[pltpu.VMEM((bb, n_rows, POOL_W), F32), pltpu.VMEM((bb, 3, n_rows, POOL_W), F32)],
        compiler_params=_cparams(("arbitrary",)),
        name="pool",
    )(zp3, zp3, buf16, cnt, w_bd, bias, scale)


def _out_proj_kernel(mr_ref, mm_ref, mp_ref, x_ref, w_ref, y_ref, mix_scr):
    mix_scr[:, :RET_W] = mr_ref[...]
    mix_scr[:, RET_W:RET_W + MOBA_W] = mm_ref[...]
    mix_scr[:, RET_W + MOBA_W:] = mp_ref[...]
    y_ref[...] = x_ref[...] + jnp.dot(mix_scr[...], w_ref[...], preferred_element_type=F32)


def _out_proj(mr, mm, mp, x2d, w_bf, *, tm, layer):
    m, d = x2d.shape
    d_mix = RET_W + MOBA_W + POOL_W
    row = lambda i: (i, 0)
    return pl.pallas_call(
        _out_proj_kernel,
        grid=(m // tm,),
        in_specs=[
            pl.BlockSpec((tm, RET_W), row),
            pl.BlockSpec((tm, MOBA_W), row),
            pl.BlockSpec((tm, POOL_W), row),
            pl.BlockSpec((tm, d), row),
            pl.BlockSpec((None, d_mix, d), lambda i: (layer, 0, 0)),
        ],
        out_specs=pl.BlockSpec((tm, d), row),
        out_shape=jax.ShapeDtypeStruct((m, d), F32),
        scratch_shapes=[pltpu.VMEM((tm, d_mix), BF16)],
        compiler_params=_cparams(("arbitrary",)),
        name="out_proj",
    )(mr, mm, mp, x2d, w_bf)


def _rope_tables(pos):
    inv = 1.0 / (ROPE_THETA ** (np.arange(HALF, dtype=np.float64) / HALF))
    ang = np.asarray(pos, np.float64)[:, None] * inv[None, :]
    c, s = np.cos(ang), np.sin(ang)
    return (jnp.asarray(np.concatenate([c, c, c, c], axis=-1), F32),
            jnp.asarray(np.concatenate([-s, s, -s, s], axis=-1), F32))


def _block_diag(blocks):
    g, n, _ = blocks.shape
    eye = jnp.eye(g, dtype=blocks.dtype)
    return (eye[:, None, :, None] * blocks[:, :, None, :]).reshape(g * n, g * n)


def _pool_inv_counts(pos0, length):
    pos = pos0 + np.arange(length)
    w = np.repeat(np.asarray(POOL_WINDOWS), POOL_GC)
    return jnp.asarray(1.0 / np.minimum(pos[:, None] + 1, w[None, :]), F32)


def _layer(x3, pos_tabs, ret_state, ret_prev, ret_tabs, ret_step, pool_buf16, pool_cnt, n_tokens, moba_fn, params, *,
           tm, layer):
    b, length, d = x3.shape
    nw, w_in, w_out, gnw, qnw, knw, pw_bd, pbias, pscale = params
    x2d = x3.reshape(b * length, d)
    zr, mq, mk, mv, mg, zp = _in_proj(x2d, nw, w_in, pos_tabs[0], pos_tabs[1], qnw, knw, tm=tm, layer=layer)
    three = lambda a: a.reshape(b, length, a.shape[-1])
    mix_r, states = _retention(three(zr), ret_state, ret_prev, ret_tabs, gnw, chunk=RET_CHUNK, bb=ret_step[0],
                               pairs=ret_step[1], layer=layer)
    mix_m, k_new, v_new = moba_fn(three(mq), three(mk), three(mv), three(mg))
    mix_p, buf_new = _pool(three(zp), pool_buf16, pool_cnt, pw_bd, pbias, pscale,
                           n_tokens=n_tokens, rows=min(length, 128), bb=ret_step[0])
    two = lambda a: a.reshape(b * length, a.shape[-1])
    tm_out = max(t for t in (tm, 2 * tm, 4 * tm) if (b * length) % t == 0)
    y = _out_proj(two(mix_r), two(mix_m), two(mix_p), x2d, w_out, tm=tm_out, layer=layer)
    return y.reshape(b, length, d), states, k_new, v_new, buf_new


def _moba_prompt_wrap(mq3, mk3, mv3, mg3, *, prev_kv):
    return _moba_prompt(mq3, mk3, mv3, mg3, prev_kv)


def _moba_sample_wrap(mq3, mk3, mv3, mg3, *, page_table, cache_k4, cache_v4, layer, n_tokens):
    mix = _moba_sample(page_table, mq3, mk3, mv3, mg3, cache_k4, cache_v4, layer, n_tokens)
    rows = lambda t: t[:, :n_tokens].reshape(t.shape[0], n_tokens, MOBA_HEADS, HEAD_DIM)
    return mix, rows(mk3), rows(mv3)


def kernel(x_prompt, x_sample, cache_k, cache_v, state_ret, state_pool, page_table, norm_w, w_in, w_out, ret_gn_w, q_norm_w, k_norm_w, pool_w, pool_b, pool_scale):
    depth = w_in.shape[0]
    b, seq, d = x_prompt.shape
    db, dec_seq, _ = x_sample.shape
    n_pool, page = cache_k.shape[1], cache_k.shape[2]
    past_len = page_table.shape[1] * page
    assert dec_seq <= SUBLANES and seq % MOBA_BLOCK == 0 and seq % RET_CHUNK == 0 and past_len % MOBA_BLOCK == 0

    p_tabs = _rope_tables(np.arange(seq))
    tm_s = db * SUBLANES
    s_tabs = _rope_tables(np.tile(past_len + np.arange(SUBLANES), db))
    p_ret_tabs = _retention_tables(RET_CHUNK, RET_CHUNK)
    s_ret_tabs = _retention_tables(RET_CHUNK, dec_seq)
    p_cnt = _pool_inv_counts(0, seq)
    s_cnt = _pool_inv_counts(past_len, SUBLANES)
    zero_buf = jnp.zeros((b, POOL_BUF + 1, POOL_W), F32)

    cache_k4 = cache_k.transpose(0, 1, 3, 4, 2).reshape(depth, n_pool, MOBA_W, page)
    cache_v4 = cache_v.transpose(0, 1, 3, 4, 2).reshape(depth, n_pool, MOBA_W, page)
    xp = x_prompt
    xs = jnp.pad(x_sample, ((0, 0), (0, SUBLANES - dec_seq), (0, 0)))

    w_in_bf = w_in.astype(BF16)
    w_out_bf = w_out.astype(BF16)
    outs = [[] for _ in range(4)]
    prev_kv = s_p = s_s = None
    for l in range(depth):
        params = (norm_w[l].reshape(1, d), w_in_bf, w_out_bf,
                  ret_gn_w[l].reshape(RET_W // LANES, 1, LANES),
                  jnp.tile(q_norm_w[l], MOBA_HEADS).reshape(1, MOBA_W),
                  jnp.tile(k_norm_w[l], MOBA_HEADS).reshape(1, MOBA_W),
                  _block_diag(pool_w[l]).astype(BF16),
                  pool_b[l].reshape(1, POOL_W), pool_scale[l].reshape(1, POOL_W))
        xp, s_p, kt, vt, b_p = _layer(xp, p_tabs, None, s_p, p_ret_tabs, (1, 1), zero_buf, p_cnt, seq,
                                      functools.partial(_moba_prompt_wrap, prev_kv=prev_kv), params,
                                      tm=512, layer=l)
        prev_kv = (kt, vt)
        moba_s = functools.partial(_moba_sample_wrap, page_table=page_table, cache_k4=cache_k4,
                                   cache_v4=cache_v4, layer=l, n_tokens=dec_seq)
        xs, s_s, k_s, v_s, b_s = _layer(xs, s_tabs, state_ret, s_s, s_ret_tabs,
                                        (math.gcd(db, 8), RET_W // LANES),
                                        jnp.pad(state_pool[l], ((0, 0), (1, 0), (0, 0))), s_cnt, dec_seq,
                                        moba_s, params, tm=tm_s, layer=l)
        for lst, val in zip(outs, (k_s, v_s, b_p, b_s)):
            lst.append(val)
    rows = lambda t: t.reshape(depth, b, MOBA_HEADS, HEAD_DIM, seq).transpose(0, 1, 4, 2, 3)
    k_s, v_s, b_p, b_s = (jnp.stack(o) for o in outs)
    return xp, xs[:, :dec_seq], rows(prev_kv[0]), rows(prev_kv[1]), k_s, v_s, s_p, s_s, b_p, b_s
```

```python
import functools
import math

import jax
import jax.numpy as jnp
import numpy as np
from jax import lax
from jax.experimental import pallas as pl
from jax.experimental.pallas import tpu as pltpu

F32 = jnp.float32
BF16 = jnp.bfloat16

HEAD_DIM = 64
HALF = HEAD_DIM // 2
LANES = 128
RET_HEADS = 6
MOBA_HEADS = 6
RET_W = RET_HEADS * HEAD_DIM
MOBA_W = MOBA_HEADS * HEAD_DIM
POOL_W = 256
POOL_GC = 64
POOL_WINDOWS = (2, 4, 8, 16)
POOL_BUF = 15
RET_CHUNK = 128
MOBA_BLOCK = 256
MOBA_TOPK = 3
ROPE_THETA = 10000.0
EPS = 1e-6
SUBLANES = 8
NEG_INF = float("-inf")
LOG2_E = 1.4426950408889634

OFF_RQ, OFF_RV, OFF_MQ, OFF_MV, OFF_PU = 0, 2 * RET_W, 4 * RET_W, 4 * RET_W + 2 * MOBA_W, 4 * RET_W + 4 * MOBA_W
D_IN = OFF_PU + 2 * POOL_W

VMEM_LIMIT = 56 * 1024 * 1024


def _silu(x):
    return x / (1.0 + jnp.exp(-x))


def _cparams(sem):
    return pltpu.CompilerParams(dimension_semantics=sem, vmem_limit_bytes=VMEM_LIMIT)


def _in_proj_kernel(x_ref, nw_ref, w_ref, cos_ref, sin_ref, qnw_ref, knw_ref,
                    zr_ref, mq_ref, mk_ref, mv_ref, mg_ref, zp_ref, h_scr, z_scr, zm_scr, zv_scr):
    tm = x_ref.shape[0]
    x = x_ref[...]
    ms = jnp.mean(x * x, axis=-1, keepdims=True)
    h_scr[...] = (x * lax.rsqrt(ms + EPS) * nw_ref[...]).astype(BF16)
    cos = cos_ref[...]
    sin = sin_ref[...]
    lane = lax.broadcasted_iota(jnp.int32, (tm, LANES), 1)
    first_half = (lane & HALF) == 0

    def rope(z):
        partner = jnp.where(first_half, pltpu.roll(z, LANES - HALF, 1), pltpu.roll(z, HALF, 1))
        return z * cos + partner * sin

    def proj(off, width):
        return jnp.dot(h_scr[...], w_ref[:, off:off + width], preferred_element_type=F32)

    head0 = lane < HEAD_DIM

    def ret_epilogue(g):
        out = rope(z_scr[:, g * LANES:(g + 1) * LANES])
        if g >= RET_W // LANES:
            out = out * (HEAD_DIM ** -0.5)
        zr_ref[:, g * LANES:(g + 1) * LANES] = out

    def moba_epilogue(t, g):
        nw, dst = ((qnw_ref, mq_ref), (knw_ref, mk_ref))[t]
        c0 = t * MOBA_W + g * LANES
        z = zm_scr[:, c0:c0 + LANES]
        zz = z * z
        ms0 = jnp.sum(jnp.where(head0, zz, 0.0), axis=1, keepdims=True)
        ms1 = jnp.sum(jnp.where(head0, 0.0, zz), axis=1, keepdims=True)
        msq = jnp.where(head0, ms0, ms1) * (1.0 / HEAD_DIM)
        zn = z * lax.rsqrt(msq + EPS) * nw[:, g * LANES:(g + 1) * LANES]
        dst[:, g * LANES:(g + 1) * LANES] = rope(zn)

    groups = RET_W // LANES
    piece = 2 * LANES
    z_scr[...] = proj(OFF_RQ, 2 * RET_W)
    zm_scr[...] = proj(OFF_MQ, 2 * MOBA_W)
    for g in range(2 * groups):
        if g % 2 == 0:
            c0 = OFF_RV + (g // 2) * piece
            zr_ref[:, c0:c0 + piece] = proj(c0, piece)
        ret_epilogue(g)
    for g in range(groups):
        c0 = g * piece
        zv_scr[:, c0:c0 + piece] = proj(OFF_MV + c0, piece)
        moba_epilogue(0, g)
    for g in range(groups):
        if g < 2 * POOL_W // piece:
            zp_ref[:, g * piece:(g + 1) * piece] = proj(OFF_PU + g * piece, piece)
        moba_epilogue(1, g)
    mv_ref[...] = zv_scr[:, :MOBA_W]
    mg_ref[...] = zv_scr[:, MOBA_W:]


def _in_proj(x2d, nw, w_bf, cos, sin, qnw, knw, *, tm, layer):
    m, d = x2d.shape
    n_pos = cos.shape[0] // tm
    row = lambda i: (i, 0)
    const = lambda i: (0, 0)
    pos = lambda i: (i % n_pos, 0)
    outs = [(2 * RET_W + 2 * RET_W), MOBA_W, MOBA_W, MOBA_W, MOBA_W, 2 * POOL_W]
    return pl.pallas_call(
        _in_proj_kernel,
        grid=(m // tm,),
        in_specs=[
            pl.BlockSpec((tm, d), row),
            pl.BlockSpec((1, d), const),
            pl.BlockSpec((None, d, D_IN), lambda i: (layer, 0, 0)),
            pl.BlockSpec((tm, LANES), pos),
            pl.BlockSpec((tm, LANES), pos),
            pl.BlockSpec((1, MOBA_W), const),
            pl.BlockSpec((1, MOBA_W), const),
        ],
        out_specs=[pl.BlockSpec((tm, w), row) for w in outs],
        out_shape=[jax.ShapeDtypeStruct((m, w), F32) for w in outs],
        scratch_shapes=[pltpu.VMEM((tm, d), BF16), pltpu.VMEM((tm, 2 * RET_W), F32),
                        pltpu.VMEM((tm, 2 * MOBA_W), F32), pltpu.VMEM((tm, 2 * MOBA_W), F32)],
        compiler_params=_cparams(("arbitrary",)),
        name="in_proj",
    )(x2d, nw, w_bf, cos, sin, qnw, knw)


def _retention_kernel(*refs, chunk, has_state, n_prev):
    q_ref, k_ref, v_ref, g_ref = refs[:4]
    s0_ref = refs[4] if has_state else None
    prev_ref = refs[4 + has_state] if n_prev else None
    dm_ref, rs_ref, kd_ref, gc_ref, gnw_ref, o_ref, sout_ref, oi_scr, kv_scr, sb_scr = refs[4 + has_state + (n_prev > 0):]
    if n_prev:
        sout_ref[0:n_prev] = prev_ref[...]
    bb, length, width = q_ref.shape
    pairs = width // LANES
    rows_in = min(length, chunk)
    n_chunks = max(1, length // chunk)
    lane = lax.broadcasted_iota(jnp.int32, (chunk, LANES), 1)
    head0 = lane < HEAD_DIM
    r = lax.broadcasted_iota(jnp.int32, (LANES, LANES), 0)
    c = lax.broadcasted_iota(jnp.int32, (LANES, LANES), 1)
    same_head = (r < HEAD_DIM) == (c < HEAD_DIM)
    contract_lanes = (((1,), (1,)), ((), ()))

    def head_mean(x):
        m0 = jnp.sum(jnp.where(head0, x, 0.0), axis=1, keepdims=True)
        m1 = jnp.sum(jnp.where(head0, 0.0, x), axis=1, keepdims=True)
        return jnp.where(head0, m0, m1) * (1.0 / HEAD_DIM)

    def load(ref, bi, p, ci):
        a = ref[bi, ci * chunk:ci * chunk + rows_in, p * LANES:(p + 1) * LANES]
        if rows_in < chunk:
            a = jnp.concatenate([a, jnp.zeros((chunk - rows_in, LANES), F32)], axis=0)
        return a

    items = [(bi, p, ci) for bi in range(bb) for p in range(pairs) for ci in range(n_chunks)]

    def a1(bi, p, ci):
        q = load(q_ref, bi, p, ci)
        k = load(k_ref, bi, p, ci)
        kb = k.astype(BF16)
        vb = load(v_ref, bi, p, ci).astype(BF16)
        q0 = jnp.where(head0, q, 0.0).astype(BF16)
        q1 = jnp.where(head0, 0.0, q).astype(BF16)
        in0 = (lax.dot_general(q0, kb, contract_lanes, preferred_element_type=F32) * dm_ref[2 * p]).astype(BF16)
        in1 = (lax.dot_general(q1, kb, contract_lanes, preferred_element_type=F32) * dm_ref[2 * p + 1]).astype(BF16)
        return bi, p, ci, k, vb, in0, in1

    def a2(st):
        bi, p, ci, k, vb, in0, in1 = st
        slot = bi * pairs + p
        oi_scr[slot, ci * chunk:(ci + 1) * chunk, :] = jnp.where(
            head0, jnp.dot(in0, vb, preferred_element_type=F32), jnp.dot(in1, vb, preferred_element_type=F32))
        kdt = (k * kd_ref[p]).T.astype(BF16)
        kv_scr[slot, ci] = jnp.where(same_head, jnp.dot(kdt, vb, preferred_element_type=F32), 0.0)

    st = None
    for item in items + [None]:
        nxt = a1(*item) if item is not None else None
        if st is not None:
            a2(st)
        st = nxt

    zero = jnp.zeros((HEAD_DIM, HEAD_DIM), F32)
    for bi in range(bb):
        for p in range(pairs):
            slot = bi * pairs + p
            if s0_ref is None:
                s = jnp.zeros((LANES, LANES), F32)
            else:
                s = jnp.concatenate([jnp.concatenate([s0_ref[bi, 2 * p], zero], axis=1),
                                     jnp.concatenate([zero, s0_ref[bi, 2 * p + 1]], axis=1)], axis=0)
            for ci in range(n_chunks):
                sb_scr[slot, ci] = s.astype(BF16)
                s = s * gc_ref[p] + kv_scr[slot, ci]
            sout_ref[n_prev, bi, 2 * p] = s[:HEAD_DIM, :HEAD_DIM]
            sout_ref[n_prev, bi, 2 * p + 1] = pltpu.roll(s[HEAD_DIM:], HEAD_DIM, 1)[:, :HEAD_DIM]

    def c1(bi, p, ci):
        slot = bi * pairs + p
        qb = load(q_ref, bi, p, ci).astype(BF16)
        o = (oi_scr[slot, ci * chunk:(ci + 1) * chunk, :]
             + jnp.dot(qb, sb_scr[slot, ci], preferred_element_type=F32) * rs_ref[p])
        return bi, p, ci, o, head_mean(o)

    def c2(st):
        bi, p, ci, o, mu = st
        oc = o - mu
        return bi, p, ci, oc, head_mean(oc * oc)

    def c3(st):
        bi, p, ci, oc, var = st
        on = oc * lax.rsqrt(var + EPS) * gnw_ref[p]
        out = on * _silu(load(g_ref, bi, p, ci))
        o_ref[bi, ci * chunk:ci * chunk + rows_in, p * LANES:(p + 1) * LANES] = out[:rows_in].astype(o_ref.dtype)

    s1 = s2 = None
    for item in items + [None, None]:
        new1 = c1(*item) if item is not None else None
        new2 = c2(s1) if s1 is not None else None
        if s2 is not None:
            c3(s2)
        s1, s2 = new1, new2


def _retention(zr3, state, prev_states, tabs, gnw, *, chunk, bb, pairs, layer):
    b, length, _ = zr3.shape
    n_pairs = RET_W // LANES
    dm, rs, kd, gc = tabs
    w = pairs * LANES
    n_chunks = max(1, length // chunk)
    n_prev = 0 if prev_states is None else prev_states.shape[0]
    col = lambda off: (lambda bi, p: (bi, 0, off + p))
    tab = lambda bi, p: (p, 0, 0)
    heads_blk = (bb, 2 * pairs, HEAD_DIM, HEAD_DIM)
    stacked = lambda n: pl.BlockSpec((n,) + heads_blk, lambda bi, p: (0, bi, p, 0, 0))
    extra_specs, extra_args = [], []
    if state is not None:
        extra_specs.append(pl.BlockSpec((None,) + heads_blk, lambda bi, p: (layer, bi, p, 0, 0)))
        extra_args.append(state)
    if n_prev:
        extra_specs.append(stacked(n_prev))
        extra_args.append(prev_states)
    return pl.pallas_call(
        functools.partial(_retention_kernel, chunk=chunk, has_state=state is not None, n_prev=n_prev),
        grid=(b // bb, n_pairs // pairs),
        in_specs=[
            pl.BlockSpec((bb, length, w), col(0)),
            pl.BlockSpec((bb, length, w), col(n_pairs // pairs)),
            pl.BlockSpec((bb, length, w), col(2 * n_pairs // pairs)),
            pl.BlockSpec((bb, length, w), col(3 * n_pairs // pairs)),
        ] + extra_specs + [
            pl.BlockSpec((2 * pairs, chunk, chunk), tab),
            pl.BlockSpec((pairs, chunk, LANES), tab),
            pl.BlockSpec((pairs, chunk, LANES), tab),
            pl.BlockSpec((pairs, 1, LANES), tab),
            pl.BlockSpec((pairs, 1, LANES), tab),
        ],
        out_specs=[pl.BlockSpec((bb, length, w), lambda bi, p: (bi, 0, p)), stacked(n_prev + 1)],
        out_shape=[
            jax.ShapeDtypeStruct((b, length, RET_W), BF16),
            jax.ShapeDtypeStruct((n_prev + 1, b, RET_HEADS, HEAD_DIM, HEAD_DIM), F32),
        ],
        scratch_shapes=[
            pltpu.VMEM((bb * pairs, n_chunks * chunk, LANES), F32),
            pltpu.VMEM((bb * pairs, n_chunks, LANES, LANES), F32),
            pltpu.VMEM((bb * pairs, n_chunks, LANES, LANES), BF16),
        ],
        compiler_params=_cparams(("arbitrary", "arbitrary")),
        name="retention",
    )(zr3, zr3, zr3, zr3, *extra_args, dm, rs, kd, gc, gnw)


def _retention_tables(chunk, n_tokens):
    lg = np.log(1.0 - 2.0 ** (-5.0 - np.arange(RET_HEADS, dtype=np.float64)))
    i = np.arange(chunk, dtype=np.float64)
    rel = i[:, None] - i[None, :]
    dm = np.where(rel[None] >= 0, np.exp(rel[None] * lg[:, None, None]), 0.0)
    lg_lanes = np.repeat(lg, HEAD_DIM).reshape(RET_W // LANES, 1, LANES)
    rs = np.exp((i + 1.0)[None, :, None] * lg_lanes)
    kd = np.where((i < n_tokens)[None, :, None], np.exp((n_tokens - 1.0 - i)[None, :, None] * lg_lanes), 0.0)
    gc = np.exp(float(n_tokens) * lg_lanes)
    return tuple(jnp.asarray(t, F32) for t in (dm, rs, kd, gc))


def _moba_prompt_kernel(*refs, n_blocks, n_prev):
    q_ref, k_ref, v_ref, g_ref = refs[:4]
    prev = refs[4:6] if n_prev else ()
    o_ref, kt_ref, vt_ref, kh_scr, vt_scr, km_scr, s_scr, p_scr = refs[4 + len(prev):]
    blk = MOBA_BLOCK
    sub = blk
    contract_lanes = (((1,), (1,)), ((), ()))
    lane = lax.broadcasted_iota(jnp.int32, (blk, LANES), 1)
    head0 = lane < HEAD_DIM
    for j in range(n_blocks):
        kj = k_ref[0, j * blk:(j + 1) * blk, :]
        kh_scr[0, j * blk:(j + 1) * blk, :] = jnp.where(head0, kj, 0.0).astype(BF16)
        kh_scr[1, j * blk:(j + 1) * blk, :] = jnp.where(head0, 0.0, kj).astype(BF16)
        vt = v_ref[0, j * blk:(j + 1) * blk, :].T
        vt_scr[:, j * blk:(j + 1) * blk] = vt.astype(BF16)
        km_scr[j:j + 1, :] = jnp.sum(kj, axis=0, keepdims=True) * (1.0 / blk)
        kt_ref[n_prev, 0, :, j * blk:(j + 1) * blk] = kj.T
        vt_ref[n_prev, 0, :, j * blk:(j + 1) * blk] = vt
    if n_prev:
        kt_ref[0:n_prev] = prev[0][...]
        vt_ref[0:n_prev] = prev[1][...]

    km = km_scr[...]
    lane8 = lax.broadcasted_iota(jnp.int32, (n_blocks, LANES), 1)
    row8 = lax.broadcasted_iota(jnp.int32, (n_blocks, blk), 0)
    krow = lax.broadcasted_iota(jnp.int32, (blk, blk), 0)
    qcol = lax.broadcasted_iota(jnp.int32, (blk, blk), 1)
    causal = jnp.where(krow <= qcol, 0.0, NEG_INF)
    fold = lambda t: t.reshape(t.shape[0] // SUBLANES, SUBLANES, blk)
    km_parts = []
    for h in range(2):
        kmh = jnp.where((lane8 < HEAD_DIM) == (h == 0), km, 0.0)
        kmh_hi = kmh.astype(BF16)
        km_parts.append((kmh_hi, (kmh - kmh_hi.astype(F32)).astype(BF16)))

    heads = range(2)
    tiles = lambda i: [(j, u, h) for j in range(i + 1) for u in range(blk // sub) for h in heads]

    def begin(i):
        q = q_ref[0, i * blk:(i + 1) * blk, :] * (HEAD_DIM ** -0.5 * LOG2_E)
        qh = q.astype(BF16)
        selmask = [None, None]
        if i > MOBA_TOPK:
            ql = (q - qh.astype(F32)).astype(BF16)
            for h in heads:
                kmh_hi, kmh_lo = km_parts[h]
                gate = (lax.dot_general(kmh_hi, qh, contract_lanes, preferred_element_type=F32)
                        + lax.dot_general(kmh_lo, qh, contract_lanes, preferred_element_type=F32)
                        + lax.dot_general(kmh_hi, ql, contract_lanes, preferred_element_type=F32))
                rank = jnp.zeros((n_blocks, blk), jnp.int32)
                for jp in range(i):
                    gj = gate[jp:jp + 1, :]
                    rank = rank + jnp.where((gj > gate) | ((gj == gate) & (jp < row8)), 1, 0)
                selmask[h] = jnp.where((rank < MOBA_TOPK) & (row8 < i), 0.0, NEG_INF)
        return dict(i=i, buf=i % 2, qh=qh, selmask=selmask, m8=[None, None], m=None,
                    l8=[jnp.zeros((SUBLANES, blk), F32) for _ in heads])

    def score_tile(st, j, u, h):
        i = st["i"]
        rows = slice(j * blk + u * sub, j * blk + (u + 1) * sub)
        s = lax.dot_general(kh_scr[h, rows, :], st["qh"], contract_lanes, preferred_element_type=F32)
        if j == i:
            s = s + causal[u * sub:(u + 1) * sub]
        s_scr[st["buf"], h, rows, :] = s
        tmax = jnp.max(fold(s), axis=0)
        if j < i and st["selmask"][h] is not None:
            tmax = tmax + st["selmask"][h][j:j + 1, :]
        st["m8"][h] = tmax if st["m8"][h] is None else jnp.maximum(st["m8"][h], tmax)

    def prob_tile(st, j, u, h):
        i = st["i"]
        if st["m"] is None:
            st["m"] = [jnp.max(st["m8"][hh], axis=0, keepdims=True) for hh in heads]
        rows = slice(j * blk + u * sub, j * blk + (u + 1) * sub)
        sm = st["selmask"][h]
        shift = -st["m"][h] if (j == i or sm is None) else sm[j:j + 1, :] - st["m"][h]
        p = jnp.exp2(s_scr[st["buf"], h, rows, :] + shift)
        st["l8"][h] = st["l8"][h] + jnp.sum(fold(p), axis=0)
        p_scr[st["buf"], h, rows, :] = p.astype(BF16)

    def finish(st):
        i = st["i"]
        nk = (i + 1) * blk
        halves = []
        for h in heads:
            l = jnp.sum(st["l8"][h], axis=0, keepdims=True)
            ot = jnp.dot(vt_scr[h * HEAD_DIM:(h + 1) * HEAD_DIM, 0:nk], p_scr[st["buf"], h, 0:nk, :],
                         preferred_element_type=F32)
            halves.append(ot / l)
        ot = jnp.concatenate(halves, axis=0)
        o_ref[0, i * blk:(i + 1) * blk, :] = (ot.T * _silu(g_ref[0, i * blk:(i + 1) * blk, :])).astype(o_ref.dtype)

    cur = begin(0)
    for t in tiles(0):
        score_tile(cur, *t)
    prev = None
    for i in range(n_blocks):
        nxt = begin(i + 1) if i + 1 < n_blocks else None
        a, b = tiles(i), (tiles(i + 1) if nxt is not None else [])
        for k in range(max(len(a), len(b))):
            if k == 1 and prev is not None:
                finish(prev)
                prev = None
            if k < len(b):
                score_tile(nxt, *b[k])
            if k < len(a):
                prob_tile(cur, *a[k])
        if prev is not None:
            finish(prev)
        prev, cur = cur, nxt
    finish(prev)


def _moba_prompt(mq3, mk3, mv3, mg3, prev_kv):
    b, length, _ = mq3.shape
    n_pairs = MOBA_W // LANES
    n_blocks = length // MOBA_BLOCK
    n_prev = 0 if prev_kv is None else prev_kv[0].shape[0]
    rows = pl.BlockSpec((1, length, LANES), lambda bi, p: (bi, 0, p))
    cols = lambda n: pl.BlockSpec((n, 1, LANES, length), lambda bi, p: (0, bi, p, 0))
    stacked = jax.ShapeDtypeStruct((n_prev + 1, b, MOBA_W, length), F32)
    return pl.pallas_call(
        functools.partial(_moba_prompt_kernel, n_blocks=n_blocks, n_prev=n_prev),
        grid=(b, n_pairs),
        in_specs=[rows, rows, rows, rows] + ([cols(n_prev), cols(n_prev)] if n_prev else []),
        out_specs=[rows, cols(n_prev + 1), cols(n_prev + 1)],
        out_shape=[jax.ShapeDtypeStruct((b, length, MOBA_W), BF16), stacked, stacked],
        scratch_shapes=[
            pltpu.VMEM((2, length, LANES), BF16),
            pltpu.VMEM((LANES, length), BF16),
            pltpu.VMEM((n_blocks, LANES), F32),
            pltpu.VMEM((2, 2, length, MOBA_BLOCK), F32),
            pltpu.VMEM((2, 2, length, MOBA_BLOCK), BF16),
        ],
        compiler_params=_cparams(("arbitrary", "arbitrary")),
        name="moba_prompt",
    )(mq3, mk3, mv3, mg3, *(prev_kv or ()))


def _moba_sample_kernel(pt_ref, q_ref, kn_ref, vn_ref, g_ref, ck_ref, cv_ref, o_ref,
                        kring, ksem, vsel, vsem, idx_v, idx_s, isem, qs_scr, s_scr, p_scr, pc_scr, tail_scr, *,
                        layer, n_pages, page, n_tokens):
    b = pl.program_id(0)
    n_batches = pl.num_programs(0) - 1
    last = n_batches - 1
    n_blocks = n_pages * page // MOBA_BLOCK
    ppb = MOBA_BLOCK // page
    slots = n_tokens * min(MOBA_TOPK, n_blocks)
    group = math.gcd(n_pages, 16)
    rows = MOBA_HEADS * SUBLANES
    contract_lanes = (((1,), (1,)), ((), ()))
    lane = lax.broadcasted_iota(jnp.int32, (SUBLANES, MOBA_W), 1)

    def k_copy(bi, pg):
        buf = bi % 2
        return pltpu.make_async_copy(ck_ref.at[layer, pt_ref[bi, pg]], kring.at[buf, pg], ksem.at[buf, pg])

    def v_copy(bi, h, slot, u, blk):
        src = cv_ref.at[layer, pt_ref[bi, blk * ppb + u], pl.ds(h * HEAD_DIM, HEAD_DIM), :]
        return pltpu.make_async_copy(src, vsel.at[h, slot * ppb + u], vsem.at[0])

    @pl.when(b == 0)
    def _():
        for pg in range(n_pages):
            k_copy(0, pg).start()

    @pl.when((b == 0) & (last >= 1))
    def _():
        for pg in range(n_pages):
            k_copy(1, pg).start()

    @pl.when(b <= last)
    def _():
        q = q_ref[0] * (HEAD_DIM ** -0.5 * LOG2_E)
        for h in range(MOBA_HEADS):
            qs_scr[h * SUBLANES:(h + 1) * SUBLANES, :] = jnp.where(lane // HEAD_DIM == h, q, 0.0)
        qb = qs_scr[...].astype(BF16)
        for g0 in range(0, n_pages, group):
            for pg in range(g0, g0 + group):
                k_copy(b, pg).wait()
            for pg in range(g0, g0 + group, ppb):
                keys = jnp.concatenate([kring[b % 2, pg + u].astype(BF16) for u in range(ppb)], axis=1)
                s_scr[:, pg * page:(pg + ppb) * page] = jnp.dot(qb, keys, preferred_element_type=F32)

    @pl.when(b + 2 <= last)
    def _():
        for pg in range(n_pages):
            k_copy(b + 2, pg).start()

    @pl.when(b >= 1)
    def _():
        for h in range(MOBA_HEADS):
            for j in range(slots):
                for u in range(ppb):
                    v_copy(0, h, j, u, 0).wait()
        parts = [[jnp.zeros((SUBLANES, HEAD_DIM), F32) for _ in range(4)] for _ in range(MOBA_HEADS)]
        for piece in range(slots * ppb):
            for h in range(MOBA_HEADS):
                parts[h][piece % 4] = parts[h][piece % 4] + lax.dot_general(
                    pc_scr[h, piece].astype(BF16), vsel[h, piece].astype(BF16), contract_lanes,
                    preferred_element_type=F32)
        o_sel = [(p4[0] + p4[1]) + (p4[2] + p4[3]) for p4 in parts]
        o_ref[0] = ((jnp.concatenate(o_sel, axis=1) + tail_scr[0]) / tail_scr[1] * tail_scr[2]).astype(o_ref.dtype)

    @pl.when(b <= last)
    def _():
        qb = qs_scr[...].astype(BF16)
        col = lax.broadcasted_iota(jnp.int32, (rows, LANES), 1)
        gate = jnp.zeros((rows, LANES), F32)
        for n in range(n_blocks):
            c0 = n * MOBA_BLOCK
            bsum = s_scr[:, c0:c0 + page]
            for u in range(1, ppb):
                bsum = bsum + s_scr[:, c0 + u * page:c0 + (u + 1) * page]
            gate = jnp.where(col == n, jnp.sum(bsum, axis=1, keepdims=True), gate)
        rank = jnp.zeros(gate.shape, jnp.int32)
        for jp in range(n_blocks):
            gj = gate[:, jp:jp + 1]
            rank = rank + jnp.where((gj > gate) | ((gj == gate) & (jp < col)), 1, 0)
        sel = jnp.where(rank < min(MOBA_TOPK, n_blocks), 0.0, NEG_INF)
        own = jnp.concatenate([kn_ref[0], jnp.zeros((page - SUBLANES, MOBA_W), F32)], axis=0).astype(BF16)
        t_idx = lax.broadcasted_iota(jnp.int32, (rows, page), 0) % SUBLANES
        key_idx = lax.broadcasted_iota(jnp.int32, (rows, page), 1)
        s_own = lax.dot_general(qb, own, contract_lanes, preferred_element_type=F32)
        s_own = s_own + jnp.where(key_idx <= t_idx, 0.0, NEG_INF)
        s_scr[:, n_pages * page:] = s_own
        m_part = s_own
        for n in range(n_blocks):
            add = sel[:, n:n + 1]
            for u in range(ppb):
                c0 = n * MOBA_BLOCK + u * page
                sm = s_scr[:, c0:c0 + page] + add
                s_scr[:, c0:c0 + page] = sm
                m_part = jnp.maximum(m_part, sm)
        m = jnp.max(m_part, axis=1, keepdims=True)
        l_part = jnp.zeros((rows, page), F32)
        for n in range(n_pages + 1):
            c0 = n * page
            p = jnp.exp2(s_scr[:, c0:c0 + page] - m)
            l_part = l_part + p
            p_scr[:, c0:c0 + page] = p
        l = jnp.sum(l_part, axis=1, keepdims=True)

        row8 = lax.broadcasted_iota(jnp.int32, (SUBLANES, LANES), 0)
        col8 = lax.broadcasted_iota(jnp.int32, (SUBLANES, LANES), 1)
        sel_real = jnp.where(t_idx < n_tokens, sel, NEG_INF)
        chosen = jnp.full((SUBLANES, LANES), NEG_INF, F32)
        for h in range(MOBA_HEADS):
            any_row = jnp.max(sel_real[h * SUBLANES:(h + 1) * SUBLANES], axis=0, keepdims=True)
            chosen = jnp.where(row8 == h, any_row, chosen)
        chosen = chosen == 0.0
        before = (lax.broadcasted_iota(jnp.int32, (LANES, LANES), 0)
                  < lax.broadcasted_iota(jnp.int32, (LANES, LANES), 1))
        pos = jnp.dot(jnp.where(chosen, 1.0, 0.0).astype(BF16), jnp.where(before, 1.0, 0.0).astype(BF16),
                      preferred_element_type=F32)
        count = jnp.sum(jnp.where(chosen, 1.0, 0.0), axis=1, keepdims=True)
        blk_id = col8.astype(F32)
        pick = lambda j: jnp.sum(jnp.where(chosen & (pos == j), blk_id, 0.0), axis=1, keepdims=True)
        first = pick(0)
        table = jnp.where(col8 == slots, count, 0.0)
        for j in range(slots):
            table = jnp.where(col8 == j, jnp.where(count > j, pick(j), first), table)
        idx_v[...] = table.astype(jnp.int32)
        to_smem = pltpu.make_async_copy(idx_v, idx_s, isem.at[0])
        to_smem.start()
        to_smem.wait()

        for h in range(MOBA_HEADS):
            for j in range(slots):
                for u in range(ppb):
                    v_copy(b, h, j, u, idx_s[h, j]).start(priority=u % 2)
        for h in range(MOBA_HEADS):
            for j in range(slots):
                weight = jnp.where(j < idx_s[h, slots], 1.0, 0.0)
                for u in range(ppb):
                    c0 = pl.multiple_of((idx_s[h, j] * ppb + u) * page, page)
                    pc_scr[h, j * ppb + u] = p_scr[h * SUBLANES:(h + 1) * SUBLANES, pl.ds(c0, page)] * weight
        vown = jnp.concatenate([vn_ref[0], jnp.zeros((page - SUBLANES, MOBA_W), F32)], axis=0).astype(BF16)
        o_own = jnp.dot(p_scr[:, n_pages * page:].astype(BF16), vown, preferred_element_type=F32)
        own_rows = jnp.zeros((SUBLANES, MOBA_W), F32)
        l_rows = jnp.zeros((SUBLANES, MOBA_W), F32)
        for h in range(MOBA_HEADS):
            mine = lane // HEAD_DIM == h
            own_rows = jnp.where(mine, o_own[h * SUBLANES:(h + 1) * SUBLANES, :], own_rows)
            l_rows = jnp.where(mine, l[h * SUBLANES:(h + 1) * SUBLANES, :], l_rows)
        tail_scr[0] = own_rows
        tail_scr[1] = l_rows
        tail_scr[2] = _silu(g_ref[0])


def _moba_sample(page_table, mq3, mk3, mv3, mg3, cache_k4, cache_v4, layer, n_tokens):
    db = mq3.shape[0]
    n_pages = page_table.shape[1]
    page = cache_k4.shape[3]
    rows = MOBA_HEADS * SUBLANES
    n_blocks = n_pages * page // MOBA_BLOCK
    slots = n_tokens * min(MOBA_TOPK, n_blocks)
    assert n_blocks <= LANES and page == LANES and slots < LANES and MOBA_HEADS <= SUBLANES
    blk3 = (1, SUBLANES, MOBA_W)
    this = pl.BlockSpec(blk3, lambda bi, pt: (jnp.minimum(bi, db - 1), 0, 0))
    before = pl.BlockSpec(blk3, lambda bi, pt: (jnp.maximum(bi - 1, 0), 0, 0))
    hbm = pl.BlockSpec(memory_space=pl.ANY)
    width = (n_pages + 1) * page
    pieces = slots * MOBA_BLOCK // page
    return pl.pallas_call(
        functools.partial(_moba_sample_kernel, layer=layer, n_pages=n_pages, page=page, n_tokens=n_tokens),
        grid_spec=pltpu.PrefetchScalarGridSpec(
            num_scalar_prefetch=1,
            grid=(db + 1,),
            in_specs=[this, this, this, this, hbm, hbm],
            out_specs=before,
            scratch_shapes=[
                pltpu.VMEM((2, n_pages, MOBA_W, page), F32),
                pltpu.SemaphoreType.DMA((2, n_pages)),
                pltpu.VMEM((MOBA_HEADS, pieces, HEAD_DIM, page), F32),
                pltpu.SemaphoreType.DMA((1,)),
                pltpu.VMEM((SUBLANES, LANES), jnp.int32),
                pltpu.SMEM((SUBLANES, LANES), jnp.int32),
                pltpu.SemaphoreType.DMA((1,)),
                pltpu.VMEM((rows, MOBA_W), F32),
                pltpu.VMEM((rows, width), F32),
                pltpu.VMEM((rows, width), F32),
                pltpu.VMEM((MOBA_HEADS, pieces, SUBLANES, page), F32),
                pltpu.VMEM((3, SUBLANES, MOBA_W), F32),
            ],
        ),
        out_shape=jax.ShapeDtypeStruct((db, SUBLANES, MOBA_W), BF16),
        compiler_params=_cparams(("arbitrary",)),
        name="moba_sample",
    )(page_table, mq3, mk3, mv3, mg3, cache_k4, cache_v4)


def _pool_kernel(u_ref, g_ref, buf_ref, inv_cnt_ref, w_ref, b_ref, sc_ref, o_ref, bufo_ref, x_scr, w_scr, *,
                 n_tokens, rows):
    bb, length, _ = u_ref.shape
    lead = SUBLANES
    halo = lead + POOL_BUF + 1
    assert POOL_WINDOWS == (2, 4, 8, 16) and halo % SUBLANES == 0
    tiles = [(lead, halo - lead)] + [(halo + r0, rows) for r0 in range(0, length, rows)]
    for bi in range(bb):
        x_scr[bi, 0:lead, :] = jnp.zeros((lead, POOL_W), F32)
        x_scr[bi, lead:halo, :] = buf_ref[bi]
        x_scr[bi, halo:, :] = u_ref[bi]
        w_scr[bi, :, 0:lead, :] = jnp.zeros((3, lead, POOL_W), F32)
        for i0, n in tiles:
            group = lax.broadcasted_iota(jnp.int32, (n, POOL_W), 1) // POOL_GC
            u = x_scr[bi, i0:i0 + n, :]
            w2 = u + x_scr[bi, i0 - 1:i0 - 1 + n, :]
            w_scr[bi, 0, i0:i0 + n, :] = w2
            w4 = w2 + w_scr[bi, 0, i0 - 2:i0 - 2 + n, :]
            w_scr[bi, 1, i0:i0 + n, :] = w4
            w8 = w4 + w_scr[bi, 1, i0 - 4:i0 - 4 + n, :]
            w_scr[bi, 2, i0:i0 + n, :] = w8
            if i0 < halo:
                continue
            w16 = w8 + w_scr[bi, 2, i0 - 8:i0 - 8 + n, :]
            r0 = i0 - halo
            win = jnp.where(group == 0, w2, jnp.where(group == 1, w4, jnp.where(group == 2, w8, w16)))
            pooled = win * inv_cnt_ref[r0:r0 + n, :] - u
            y = jnp.dot(pooled.astype(BF16), w_ref[...], preferred_element_type=F32) + b_ref[...]
            y = y * sc_ref[...]
            o_ref[bi, r0:r0 + n, :] = (y * _silu(g_ref[bi, r0:r0 + n, :])).astype(o_ref.dtype)
        bufo_ref[bi] = x_scr[bi, lead + n_tokens + 1:lead + n_tokens + 1 + POOL_BUF, :]


def _pool(zp3, buf16, cnt, w_bd, bias, scale, *, n_tokens, rows, bb):
    b, length, _ = zp3.shape
    const = lambda bi: (0, 0)
    n_rows = SUBLANES + POOL_BUF + 1 + length
    return pl.pallas_call(
        functools.partial(_pool_kernel, n_tokens=n_tokens, rows=rows),
        grid=(b // bb,),
        in_specs=[
            pl.BlockSpec((bb, length, POOL_W), lambda bi: (bi, 0, 0)),
            pl.BlockSpec((bb, length, POOL_W), lambda bi: (bi, 0, 1)),
            pl.BlockSpec((bb, POOL_BUF + 1, POOL_W), lambda bi: (bi, 0, 0)),
            pl.BlockSpec((length, POOL_W), const),
            pl.BlockSpec((POOL_W, POOL_W), const),
            pl.BlockSpec((1, POOL_W), const),
            pl.BlockSpec((1, POOL_W), const),
        ],
        out_specs=[
            pl.BlockSpec((bb, length, POOL_W), lambda bi: (bi, 0, 0)),
            pl.BlockSpec((bb, POOL_BUF, POOL_W), lambda bi: (bi, 0, 0)),
        ],
        out_shape=[
            jax.ShapeDtypeStruct((b, length, POOL_W), BF16),
            jax.ShapeDtypeStruct((b, POOL_BUF, POOL_W), F32),
        ],
        scratch_shapes=[pltpu.VMEM((bb, n_rows, POOL_W), F32), pltpu.VMEM((bb, 3, n_rows, POOL_W), F32)],
        compiler_params=_cparams(("arbitrary",)),
        name="pool",
    )(zp3, zp3, buf16, cnt, w_bd, bias, scale)


def _out_proj_kernel(mr_ref, mm_ref, mp_ref, x_ref, w_ref, y_ref, mix_scr):
    mix_scr[:, :RET_W] = mr_ref[...]
    mix_scr[:, RET_W:RET_W + MOBA_W] = mm_ref[...]
    mix_scr[:, RET_W + MOBA_W:] = mp_ref[...]
    y_ref[...] = x_ref[...] + jnp.dot(mix_scr[...], w_ref[...], preferred_element_type=F32)


def _out_proj(mr, mm, mp, x2d, w_bf, *, tm, layer):
    m, d = x2d.shape
    d_mix = RET_W + MOBA_W + POOL_W
    row = lambda i: (i, 0)
    return pl.pallas_call(
        _out_proj_kernel,
        grid=(m // tm,),
        in_specs=[
            pl.BlockSpec((tm, RET_W), row),
            pl.BlockSpec((tm, MOBA_W), row),
            pl.BlockSpec((tm, POOL_W), row),
            pl.BlockSpec((tm, d), row),
            pl.BlockSpec((None, d_mix, d), lambda i: (layer, 0, 0)),
        ],
        out_specs=pl.BlockSpec((tm, d), row),
        out_shape=jax.ShapeDtypeStruct((m, d), F32),
        scratch_shapes=[pltpu.VMEM((tm, d_mix), BF16)],
        compiler_params=_cparams(("arbitrary",)),
        name="out_proj",
    )(mr, mm, mp, x2d, w_bf)


def _rope_tables(pos):
    inv = 1.0 / (ROPE_THETA ** (np.arange(HALF, dtype=np.float64) / HALF))
    ang = np.asarray(pos, np.float64)[:, None] * inv[None, :]
    c, s = np.cos(ang), np.sin(ang)
    return (jnp.asarray(np.concatenate([c, c, c, c], axis=-1), F32),
            jnp.asarray(np.concatenate([-s, s, -s, s], axis=-1), F32))


def _block_diag(blocks):
    g, n, _ = blocks.shape
    eye = jnp.eye(g, dtype=blocks.dtype)
    return (eye[:, None, :, None] * blocks[:, :, None, :]).reshape(g * n, g * n)


def _pool_inv_counts(pos0, length):
    pos = pos0 + np.arange(length)
    w = np.repeat(np.asarray(POOL_WINDOWS), POOL_GC)
    return jnp.asarray(1.0 / np.minimum(pos[:, None] + 1, w[None, :]), F32)


def _layer(x3, pos_tabs, ret_state, ret_prev, ret_tabs, ret_step, pool_buf16, pool_cnt, n_tokens, moba_fn, params, *,
           tm, layer):
    b, length, d = x3.shape
    nw, w_in, w_out, gnw, qnw, knw, pw_bd, pbias, pscale = params
    x2d = x3.reshape(b * length, d)
    zr, mq, mk, mv, mg, zp = _in_proj(x2d, nw, w_in, pos_tabs[0], pos_tabs[1], qnw, knw, tm=tm, layer=layer)
    three = lambda a: a.reshape(b, length, a.shape[-1])
    mix_r, states = _retention(three(zr), ret_state, ret_prev, ret_tabs, gnw, chunk=RET_CHUNK, bb=ret_step[0],
                               pairs=ret_step[1], layer=layer)
    mix_m, k_new, v_new = moba_fn(three(mq), three(mk), three(mv), three(mg))
    mix_p, buf_new = _pool(three(zp), pool_buf16, pool_cnt, pw_bd, pbias, pscale,
                           n_tokens=n_tokens, rows=min(length, 128), bb=ret_step[0])
    two = lambda a: a.reshape(b * length, a.shape[-1])
    tm_out = max(t for t in (tm, 2 * tm, 4 * tm) if (b * length) % t == 0)
    y = _out_proj(two(mix_r), two(mix_m), two(mix_p), x2d, w_out, tm=tm_out, layer=layer)
    return y.reshape(b, length, d), states, k_new, v_new, buf_new


def _moba_prompt_wrap(mq3, mk3, mv3, mg3, *, prev_kv):
    return _moba_prompt(mq3, mk3, mv3, mg3, prev_kv)


def _moba_sample_wrap(mq3, mk3, mv3, mg3, *, page_table, cache_k4, cache_v4, layer, n_tokens):
    mix = _moba_sample(page_table, mq3, mk3, mv3, mg3, cache_k4, cache_v4, layer, n_tokens)
    rows = lambda t: t[:, :n_tokens].reshape(t.shape[0], n_tokens, MOBA_HEADS, HEAD_DIM)
    return mix, rows(mk3), rows(mv3)


def kernel(x_prompt, x_sample, cache_k, cache_v, state_ret, state_pool, page_table, norm_w, w_in, w_out, ret_gn_w, q_norm_w, k_norm_w, pool_w, pool_b, pool_scale):
    depth = w_in.shape[0]
    b, seq, d = x_prompt.shape
    db, dec_seq, _ = x_sample.shape
    n_pool, page = cache_k.shape[1], cache_k.shape[2]
    past_len = page_table.shape[1] * page
    assert dec_seq <= SUBLANES and seq % MOBA_BLOCK == 0 and seq % RET_CHUNK == 0 and past_len % MOBA_BLOCK == 0

    p_tabs = _rope_tables(np.arange(seq))
    tm_s = db * SUBLANES
    s_tabs = _rope_tables(np.tile(past_len + np.arange(SUBLANES), db))
    p_ret_tabs = _retention_tables(RET_CHUNK, RET_CHUNK)
    s_ret_tabs = _retention_tables(RET_CHUNK, dec_seq)
    p_cnt = _pool_inv_counts(0, seq)
    s_cnt = _pool_inv_counts(past_len, SUBLANES)
    zero_buf = jnp.zeros((b, POOL_BUF + 1, POOL_W), F32)

    cache_k4 = cache_k.transpose(0, 1, 3, 4, 2).reshape(depth, n_pool, MOBA_W, page)
    cache_v4 = cache_v.transpose(0, 1, 3, 4, 2).reshape(depth, n_pool, MOBA_W, page)
    xp = x_prompt
    xs = jnp.pad(x_sample, ((0, 0), (0, SUBLANES - dec_seq), (0, 0)))

    w_in_bf = w_in.astype(BF16)
    w_out_bf = w_out.astype(BF16)
    outs = [[] for _ in range(4)]
    prev_kv = s_p = s_s = None
    for l in range(depth):
        params = (norm_w[l].reshape(1, d), w_in_bf, w_out_bf,
                  ret_gn_w[l].reshape(RET_W // LANES, 1, LANES),
                  jnp.tile(q_norm_w[l], MOBA_HEADS).reshape(1, MOBA_W),
                  jnp.tile(k_norm_w[l], MOBA_HEADS).reshape(1, MOBA_W),
                  _block_diag(pool_w[l]).astype(BF16),
                  pool_b[l].reshape(1, POOL_W), pool_scale[l].reshape(1, POOL_W))
        xp, s_p, kt, vt, b_p = _layer(xp, p_tabs, None, s_p, p_ret_tabs, (1, 1), zero_buf, p_cnt, seq,
                                      functools.partial(_moba_prompt_wrap, prev_kv=prev_kv), params,
                                      tm=512, layer=l)
        prev_kv = (kt, vt)
        moba_s = functools.partial(_moba_sample_wrap, page_table=page_table, cache_k4=cache_k4,
                                   cache_v4=cache_v4, layer=l, n_tokens=dec_seq)
        xs, s_s, k_s, v_s, b_s = _layer(xs, s_tabs, state_ret, s_s, s_ret_tabs,
                                        (math.gcd(db, 8), RET_W // LANES),
                                        jnp.pad(state_pool[l], ((0, 0), (1, 0), (0, 0))), s_cnt, dec_seq,
                                        moba_s, params, tm=tm_s, layer=l)
        for lst, val in zip(outs, (k_s, v_s, b_p, b_s)):
            lst.append(val)
    rows = lambda t: t.reshape(depth, b, MOBA_HEADS, HEAD_DIM, seq).transpose(0, 1, 4, 2, 3)
    k_s, v_s, b_p, b_s = (jnp.stack(o) for o in outs)
    return xp, xs[:, :dec_seq], rows(prev_kv[0]), rows(prev_kv[1]), k_s, v_s, s_p, s_s, b_p, b_s
```

```python
import functools
import math

import jax
import jax.numpy as jnp
import numpy as np
from jax import lax
from jax.experimental import pallas as pl
from jax.experimental.pallas import tpu as pltpu

F32 = jnp.float32
BF16 = jnp.bfloat16

HEAD_DIM = 64
HALF = HEAD_DIM // 2
LANES = 128
RET_HEADS = 6
MOBA_HEADS = 6
RET_W = RET_HEADS * HEAD_DIM
MOBA_W = MOBA_HEADS * HEAD_DIM
POOL_W = 256
POOL_GC = 64
POOL_WINDOWS = (2, 4, 8, 16)
POOL_BUF = 15
RET_CHUNK = 128
MOBA_BLOCK = 256
MOBA_TOPK = 3
ROPE_THETA = 10000.0
EPS = 1e-6
SUBLANES = 8
NEG_INF = float("-inf")
LOG2_E = 1.4426950408889634

OFF_RQ, OFF_RV, OFF_MQ, OFF_MV, OFF_PU = 0, 2 * RET_W, 4 * RET_W, 4 * RET_W + 2 * MOBA_W, 4 * RET_W + 4 * MOBA_W
D_IN = OFF_PU + 2 * POOL_W

VMEM_LIMIT = 56 * 1024 * 1024


def _silu(x):
    return x / (1.0 + jnp.exp(-x))


def _cparams(sem):
    return pltpu.CompilerParams(dimension_semantics=sem, vmem_limit_bytes=VMEM_LIMIT)


def _in_proj_kernel(x_ref, nw_ref, w_ref, cos_ref, sin_ref, qnw_ref, knw_ref,
                    zr_ref, mq_ref, mk_ref, mv_ref, mg_ref, zp_ref, h_scr, z_scr, zm_scr, zv_scr):
    tm = x_ref.shape[0]
    x = x_ref[...]
    ms = jnp.mean(x * x, axis=-1, keepdims=True)
    h_scr[...] = (x * lax.rsqrt(ms + EPS) * nw_ref[...]).astype(BF16)
    cos = cos_ref[...]
    sin = sin_ref[...]
    lane = lax.broadcasted_iota(jnp.int32, (tm, LANES), 1)
    first_half = (lane & HALF) == 0

    def rope(z):
        partner = jnp.where(first_half, pltpu.roll(z, LANES - HALF, 1), pltpu.roll(z, HALF, 1))
        return z * cos + partner * sin

    def proj(off, width):
        return jnp.dot(h_scr[...], w_ref[:, off:off + width], preferred_element_type=F32)

    head0 = lane < HEAD_DIM

    def ret_epilogue(g):
        out = rope(z_scr[:, g * LANES:(g + 1) * LANES])
        if g >= RET_W // LANES:
            out = out * (HEAD_DIM ** -0.5)
        zr_ref[:, g * LANES:(g + 1) * LANES] = out

    def moba_epilogue(t, g):
        nw, dst = ((qnw_ref, mq_ref), (knw_ref, mk_ref))[t]
        c0 = t * MOBA_W + g * LANES
        z = zm_scr[:, c0:c0 + LANES]
        zz = z * z
        ms0 = jnp.sum(jnp.where(head0, zz, 0.0), axis=1, keepdims=True)
        ms1 = jnp.sum(jnp.where(head0, 0.0, zz), axis=1, keepdims=True)
        msq = jnp.where(head0, ms0, ms1) * (1.0 / HEAD_DIM)
        zn = z * lax.rsqrt(msq + EPS) * nw[:, g * LANES:(g + 1) * LANES]
        dst[:, g * LANES:(g + 1) * LANES] = rope(zn)

    groups = RET_W // LANES
    piece = 2 * LANES
    z_scr[...] = proj(OFF_RQ, 2 * RET_W)
    zm_scr[...] = proj(OFF_MQ, 2 * MOBA_W)
    for g in range(2 * groups):
        if g % 2 == 0:
            c0 = OFF_RV + (g // 2) * piece
            zr_ref[:, c0:c0 + piece] = proj(c0, piece)
        ret_epilogue(g)
    for g in range(groups):
        c0 = g * piece
        zv_scr[:, c0:c0 + piece] = proj(OFF_MV + c0, piece)
        moba_epilogue(0, g)
    for g in range(groups):
        if g < 2 * POOL_W // piece:
            zp_ref[:, g * piece:(g + 1) * piece] = proj(OFF_PU + g * piece, piece)
        moba_epilogue(1, g)
    mv_ref[...] = zv_scr[:, :MOBA_W]
    mg_ref[...] = zv_scr[:, MOBA_W:]


def _in_proj(x2d, nw, w_bf, cos, sin, qnw, knw, *, tm, layer):
    m, d = x2d.shape
    n_pos = cos.shape[0] // tm
    row = lambda i: (i, 0)
    const = lambda i: (0, 0)
    pos = lambda i: (i % n_pos, 0)
    outs = [(2 * RET_W + 2 * RET_W), MOBA_W, MOBA_W, MOBA_W, MOBA_W, 2 * POOL_W]
    return pl.pallas_call(
        _in_proj_kernel,
        grid=(m // tm,),
        in_specs=[
            pl.BlockSpec((tm, d), row),
            pl.BlockSpec((1, d), const),
            pl.BlockSpec((None, d, D_IN), lambda i: (layer, 0, 0)),
            pl.BlockSpec((tm, LANES), pos),
            pl.BlockSpec((tm, LANES), pos),
            pl.BlockSpec((1, MOBA_W), const),
            pl.BlockSpec((1, MOBA_W), const),
        ],
        out_specs=[pl.BlockSpec((tm, w), row) for w in outs],
        out_shape=[jax.ShapeDtypeStruct((m, w), F32) for w in outs],
        scratch_shapes=[pltpu.VMEM((tm, d), BF16), pltpu.VMEM((tm, 2 * RET_W), F32),
                        pltpu.VMEM((tm, 2 * MOBA_W), F32), pltpu.VMEM((tm, 2 * MOBA_W), F32)],
        compiler_params=_cparams(("arbitrary",)),
        name="in_proj",
    )(x2d, nw, w_bf, cos, sin, qnw, knw)


def _retention_kernel(*refs, chunk, has_state, n_prev):
    q_ref, k_ref, v_ref, g_ref = refs[:4]
    s0_ref = refs[4] if has_state else None
    prev_ref = refs[4 + has_state] if n_prev else None
    dm_ref, rs_ref, kd_ref, gc_ref, gnw_ref, o_ref, sout_ref, oi_scr, kv_scr, sb_scr = refs[4 + has_state + (n_prev > 0):]
    if n_prev:
        sout_ref[0:n_prev] = prev_ref[...]
    bb, length, width = q_ref.shape
    pairs = width // LANES
    rows_in = min(length, chunk)
    n_chunks = max(1, length // chunk)
    lane = lax.broadcasted_iota(jnp.int32, (chunk, LANES), 1)
    head0 = lane < HEAD_DIM
    r = lax.broadcasted_iota(jnp.int32, (LANES, LANES), 0)
    c = lax.broadcasted_iota(jnp.int32, (LANES, LANES), 1)
    same_head = (r < HEAD_DIM) == (c < HEAD_DIM)
    contract_lanes = (((1,), (1,)), ((), ()))

    def head_mean(x):
        m0 = jnp.sum(jnp.where(head0, x, 0.0), axis=1, keepdims=True)
        m1 = jnp.sum(jnp.where(head0, 0.0, x), axis=1, keepdims=True)
        return jnp.where(head0, m0, m1) * (1.0 / HEAD_DIM)

    def load(ref, bi, p, ci):
        a = ref[bi, ci * chunk:ci * chunk + rows_in, p * LANES:(p + 1) * LANES]
        if rows_in < chunk:
            a = jnp.concatenate([a, jnp.zeros((chunk - rows_in, LANES), F32)], axis=0)
        return a

    items = [(bi, p, ci) for bi in range(bb) for p in range(pairs) for ci in range(n_chunks)]

    def a1(bi, p, ci):
        q = load(q_ref, bi, p, ci)
        k = load(k_ref, bi, p, ci)
        kb = k.astype(BF16)
        vb = load(v_ref, bi, p, ci).astype(BF16)
        q0 = jnp.where(head0, q, 0.0).astype(BF16)
        q1 = jnp.where(head0, 0.0, q).astype(BF16)
        in0 = (lax.dot_general(q0, kb, contract_lanes, preferred_element_type=F32) * dm_ref[2 * p]).astype(BF16)
        in1 = (lax.dot_general(q1, kb, contract_lanes, preferred_element_type=F32) * dm_ref[2 * p + 1]).astype(BF16)
        return bi, p, ci, k, vb, in0, in1

    def a2(st):
        bi, p, ci, k, vb, in0, in1 = st
        slot = bi * pairs + p
        oi_scr[slot, ci * chunk:(ci + 1) * chunk, :] = jnp.where(
            head0, jnp.dot(in0, vb, preferred_element_type=F32), jnp.dot(in1, vb, preferred_element_type=F32))
        kdt = (k * kd_ref[p]).T.astype(BF16)
        kv_scr[slot, ci] = jnp.where(same_head, jnp.dot(kdt, vb, preferred_element_type=F32), 0.0)

    pending = []
    for item in items + [None, None]:
        if item is not None:
            pending.append(a1(*item))
        if len(pending) > 2 or (item is None and pending):
            a2(pending.pop(0))

    zero = jnp.zeros((HEAD_DIM, HEAD_DIM), F32)
    for bi in range(bb):
        for p in range(pairs):
            slot = bi * pairs + p
            if s0_ref is None:
                s = jnp.zeros((LANES, LANES), F32)
            else:
                s = jnp.concatenate([jnp.concatenate([s0_ref[bi, 2 * p], zero], axis=1),
                                     jnp.concatenate([zero, s0_ref[bi, 2 * p + 1]], axis=1)], axis=0)
            for ci in range(n_chunks):
                sb_scr[slot, ci] = s.astype(BF16)
                s = s * gc_ref[p] + kv_scr[slot, ci]
            sout_ref[n_prev, bi, 2 * p] = s[:HEAD_DIM, :HEAD_DIM]
            sout_ref[n_prev, bi, 2 * p + 1] = pltpu.roll(s[HEAD_DIM:], HEAD_DIM, 1)[:, :HEAD_DIM]

    def c1(bi, p, ci):
        slot = bi * pairs + p
        qb = load(q_ref, bi, p, ci).astype(BF16)
        o = (oi_scr[slot, ci * chunk:(ci + 1) * chunk, :]
             + jnp.dot(qb, sb_scr[slot, ci], preferred_element_type=F32) * rs_ref[p])
        return bi, p, ci, o, head_mean(o)

    def c2(st):
        bi, p, ci, o, mu = st
        oc = o - mu
        return bi, p, ci, oc, head_mean(oc * oc)

    def c3(st):
        bi, p, ci, oc, var = st
        on = oc * lax.rsqrt(var + EPS) * gnw_ref[p]
        out = on * _silu(load(g_ref, bi, p, ci))
        o_ref[bi, ci * chunk:ci * chunk + rows_in, p * LANES:(p + 1) * LANES] = out[:rows_in].astype(o_ref.dtype)

    s1 = s2 = None
    for item in items + [None, None]:
        new1 = c1(*item) if item is not None else None
        new2 = c2(s1) if s1 is not None else None
        if s2 is not None:
            c3(s2)
        s1, s2 = new1, new2


def _retention(zr3, state, prev_states, tabs, gnw, *, chunk, bb, pairs, layer):
    b, length, _ = zr3.shape
    n_pairs = RET_W // LANES
    dm, rs, kd, gc = tabs
    w = pairs * LANES
    n_chunks = max(1, length // chunk)
    n_prev = 0 if prev_states is None else prev_states.shape[0]
    col = lambda off: (lambda bi, p: (bi, 0, off + p))
    tab = lambda bi, p: (p, 0, 0)
    heads_blk = (bb, 2 * pairs, HEAD_DIM, HEAD_DIM)
    stacked = lambda n: pl.BlockSpec((n,) + heads_blk, lambda bi, p: (0, bi, p, 0, 0))
    extra_specs, extra_args = [], []
    if state is not None:
        extra_specs.append(pl.BlockSpec((None,) + heads_blk, lambda bi, p: (layer, bi, p, 0, 0)))
        extra_args.append(state)
    if n_prev:
        extra_specs.append(stacked(n_prev))
        extra_args.append(prev_states)
    return pl.pallas_call(
        functools.partial(_retention_kernel, chunk=chunk, has_state=state is not None, n_prev=n_prev),
        grid=(b // bb, n_pairs // pairs),
        in_specs=[
            pl.BlockSpec((bb, length, w), col(0)),
            pl.BlockSpec((bb, length, w), col(n_pairs // pairs)),
            pl.BlockSpec((bb, length, w), col(2 * n_pairs // pairs)),
            pl.BlockSpec((bb, length, w), col(3 * n_pairs // pairs)),
        ] + extra_specs + [
            pl.BlockSpec((2 * pairs, chunk, chunk), tab),
            pl.BlockSpec((pairs, chunk, LANES), tab),
            pl.BlockSpec((pairs, chunk, LANES), tab),
            pl.BlockSpec((pairs, 1, LANES), tab),
            pl.BlockSpec((pairs, 1, LANES), tab),
        ],
        out_specs=[pl.BlockSpec((bb, length, w), lambda bi, p: (bi, 0, p)), stacked(n_prev + 1)],
        out_shape=[
            jax.ShapeDtypeStruct((b, length, RET_W), BF16),
            jax.ShapeDtypeStruct((n_prev + 1, b, RET_HEADS, HEAD_DIM, HEAD_DIM), F32),
        ],
        scratch_shapes=[
            pltpu.VMEM((bb * pairs, n_chunks * chunk, LANES), F32),
            pltpu.VMEM((bb * pairs, n_chunks, LANES, LANES), F32),
            pltpu.VMEM((bb * pairs, n_chunks, LANES, LANES), BF16),
        ],
        compiler_params=_cparams(("arbitrary", "arbitrary")),
        name="retention",
    )(zr3, zr3, zr3, zr3, *extra_args, dm, rs, kd, gc, gnw)


def _retention_tables(chunk, n_tokens):
    lg = np.log(1.0 - 2.0 ** (-5.0 - np.arange(RET_HEADS, dtype=np.float64)))
    i = np.arange(chunk, dtype=np.float64)
    rel = i[:, None] - i[None, :]
    dm = np.where(rel[None] >= 0, np.exp(rel[None] * lg[:, None, None]), 0.0)
    lg_lanes = np.repeat(lg, HEAD_DIM).reshape(RET_W // LANES, 1, LANES)
    rs = np.exp((i + 1.0)[None, :, None] * lg_lanes)
    kd = np.where((i < n_tokens)[None, :, None], np.exp((n_tokens - 1.0 - i)[None, :, None] * lg_lanes), 0.0)
    gc = np.exp(float(n_tokens) * lg_lanes)
    return tuple(jnp.asarray(t, F32) for t in (dm, rs, kd, gc))


def _moba_prompt_kernel(*refs, n_blocks, n_prev):
    q_ref, k_ref, v_ref, g_ref = refs[:4]
    prev = refs[4:6] if n_prev else ()
    o_ref, kt_ref, vt_ref, kh_scr, vt_scr, km_scr, s_scr, p_scr = refs[4 + len(prev):]
    blk = MOBA_BLOCK
    sub = blk
    contract_lanes = (((1,), (1,)), ((), ()))
    lane = lax.broadcasted_iota(jnp.int32, (blk, LANES), 1)
    head0 = lane < HEAD_DIM
    for j in range(n_blocks):
        kj = k_ref[0, j * blk:(j + 1) * blk, :]
        kh_scr[0, j * blk:(j + 1) * blk, :] = jnp.where(head0, kj, 0.0).astype(BF16)
        kh_scr[1, j * blk:(j + 1) * blk, :] = jnp.where(head0, 0.0, kj).astype(BF16)
        vt = v_ref[0, j * blk:(j + 1) * blk, :].T
        vt_scr[:, j * blk:(j + 1) * blk] = vt.astype(BF16)
        km_scr[j:j + 1, :] = jnp.sum(kj, axis=0, keepdims=True) * (1.0 / blk)
        kt_ref[n_prev, 0, :, j * blk:(j + 1) * blk] = kj.T
        vt_ref[n_prev, 0, :, j * blk:(j + 1) * blk] = vt
    if n_prev:
        kt_ref[0:n_prev] = prev[0][...]
        vt_ref[0:n_prev] = prev[1][...]

    km = km_scr[...]
    lane8 = lax.broadcasted_iota(jnp.int32, (n_blocks, LANES), 1)
    row8 = lax.broadcasted_iota(jnp.int32, (n_blocks, blk), 0)
    krow = lax.broadcasted_iota(jnp.int32, (blk, blk), 0)
    qcol = lax.broadcasted_iota(jnp.int32, (blk, blk), 1)
    causal = jnp.where(krow <= qcol, 0.0, NEG_INF)
    fold = lambda t: t.reshape(t.shape[0] // SUBLANES, SUBLANES, blk)
    km_parts = []
    for h in range(2):
        kmh = jnp.where((lane8 < HEAD_DIM) == (h == 0), km, 0.0)
        kmh_hi = kmh.astype(BF16)
        km_parts.append((kmh_hi, (kmh - kmh_hi.astype(F32)).astype(BF16)))

    heads = range(2)
    tiles = lambda i: [(j, u, h) for j in range(i + 1) for u in range(blk // sub) for h in heads]

    def begin(i):
        q = q_ref[0, i * blk:(i + 1) * blk, :] * (HEAD_DIM ** -0.5 * LOG2_E)
        qh = q.astype(BF16)
        selmask = [None, None]
        if i > MOBA_TOPK:
            ql = (q - qh.astype(F32)).astype(BF16)
            for h in heads:
                kmh_hi, kmh_lo = km_parts[h]
                gate = (lax.dot_general(kmh_hi, qh, contract_lanes, preferred_element_type=F32)
                        + lax.dot_general(kmh_lo, qh, contract_lanes, preferred_element_type=F32)
                        + lax.dot_general(kmh_hi, ql, contract_lanes, preferred_element_type=F32))
                rank = jnp.zeros((n_blocks, blk), jnp.int32)
                for jp in range(i):
                    gj = gate[jp:jp + 1, :]
                    rank = rank + jnp.where((gj > gate) | ((gj == gate) & (jp < row8)), 1, 0)
                selmask[h] = jnp.where((rank < MOBA_TOPK) & (row8 < i), 0.0, NEG_INF)
        return dict(i=i, buf=i % 2, qh=qh, selmask=selmask, m8=[None, None], m=None,
                    l8=[jnp.zeros((SUBLANES, blk), F32) for _ in heads])

    def score_tile(st, j, u, h):
        i = st["i"]
        rows = slice(j * blk + u * sub, j * blk + (u + 1) * sub)
        s = lax.dot_general(kh_scr[h, rows, :], st["qh"], contract_lanes, preferred_element_type=F32)
        if j == i:
            s = s + causal[u * sub:(u + 1) * sub]
        s_scr[st["buf"], h, rows, :] = s
        tmax = jnp.max(fold(s), axis=0)
        if j < i and st["selmask"][h] is not None:
            tmax = tmax + st["selmask"][h][j:j + 1, :]
        st["m8"][h] = tmax if st["m8"][h] is None else jnp.maximum(st["m8"][h], tmax)

    def prob_tile(st, j, u, h):
        i = st["i"]
        if st["m"] is None:
            st["m"] = [jnp.max(st["m8"][hh], axis=0, keepdims=True) for hh in heads]
        rows = slice(j * blk + u * sub, j * blk + (u + 1) * sub)
        sm = st["selmask"][h]
        shift = -st["m"][h] if (j == i or sm is None) else sm[j:j + 1, :] - st["m"][h]
        p = jnp.exp2(s_scr[st["buf"], h, rows, :] + shift)
        st["l8"][h] = st["l8"][h] + jnp.sum(fold(p), axis=0)
        p_scr[st["buf"], h, rows, :] = p.astype(BF16)

    def finish(st):
        i = st["i"]
        nk = (i + 1) * blk
        halves = []
        for h in heads:
            l = jnp.sum(st["l8"][h], axis=0, keepdims=True)
            ot = jnp.dot(vt_scr[h * HEAD_DIM:(h + 1) * HEAD_DIM, 0:nk], p_scr[st["buf"], h, 0:nk, :],
                         preferred_element_type=F32)
            halves.append(ot / l)
        ot = jnp.concatenate(halves, axis=0)
        o_ref[0, i * blk:(i + 1) * blk, :] = (ot.T * _silu(g_ref[0, i * blk:(i + 1) * blk, :])).astype(o_ref.dtype)

    cur = begin(0)
    for t in tiles(0):
        score_tile(cur, *t)
    prev = None
    for i in range(n_blocks):
        nxt = begin(i + 1) if i + 1 < n_blocks else None
        a, b = tiles(i), (tiles(i + 1) if nxt is not None else [])
        for k in range(max(len(a), len(b))):
            if k == 1 and prev is not None:
                finish(prev)
                prev = None
            if k < len(b):
                score_tile(nxt, *b[k])
            if k < len(a):
                prob_tile(cur, *a[k])
        if prev is not None:
            finish(prev)
        prev, cur = cur, nxt
    finish(prev)


def _moba_prompt(mq3, mk3, mv3, mg3, prev_kv):
    b, length, _ = mq3.shape
    n_pairs = MOBA_W // LANES
    n_blocks = length // MOBA_BLOCK
    n_prev = 0 if prev_kv is None else prev_kv[0].shape[0]
    rows = pl.BlockSpec((1, length, LANES), lambda bi, p: (bi, 0, p))
    cols = lambda n: pl.BlockSpec((n, 1, LANES, length), lambda bi, p: (0, bi, p, 0))
    stacked = jax.ShapeDtypeStruct((n_prev + 1, b, MOBA_W, length), F32)
    return pl.pallas_call(
        functools.partial(_moba_prompt_kernel, n_blocks=n_blocks, n_prev=n_prev),
        grid=(b, n_pairs),
        in_specs=[rows, rows, rows, rows] + ([cols(n_prev), cols(n_prev)] if n_prev else []),
        out_specs=[rows, cols(n_prev + 1), cols(n_prev + 1)],
        out_shape=[jax.ShapeDtypeStruct((b, length, MOBA_W), BF16), stacked, stacked],
        scratch_shapes=[
            pltpu.VMEM((2, length, LANES), BF16),
            pltpu.VMEM((LANES, length), BF16),
            pltpu.VMEM((n_blocks, LANES), F32),
            pltpu.VMEM((2, 2, length, MOBA_BLOCK), F32),
            pltpu.VMEM((2, 2, length, MOBA_BLOCK), BF16),
        ],
        compiler_params=_cparams(("arbitrary", "arbitrary")),
        name="moba_prompt",
    )(mq3, mk3, mv3, mg3, *(prev_kv or ()))


def _moba_sample_kernel(pt_ref, q_ref, kn_ref, vn_ref, g_ref, ck_ref, cv_ref, o_ref,
                        kring, ksem, vsel, vsem, idx_v, idx_s, isem, qs_scr, s_scr, p_scr, pc_scr, tail_scr, *,
                        layer, n_pages, page, n_tokens):
    b = pl.program_id(0)
    n_batches = pl.num_programs(0) - 1
    last = n_batches - 1
    n_blocks = n_pages * page // MOBA_BLOCK
    ppb = MOBA_BLOCK // page
    slots = n_tokens * min(MOBA_TOPK, n_blocks)
    group = math.gcd(n_pages, 16)
    rows = MOBA_HEADS * SUBLANES
    contract_lanes = (((1,), (1,)), ((), ()))
    lane = lax.broadcasted_iota(jnp.int32, (SUBLANES, MOBA_W), 1)

    def k_copy(bi, pg):
        buf = bi % 2
        return pltpu.make_async_copy(ck_ref.at[layer, pt_ref[bi, pg]], kring.at[buf, pg], ksem.at[buf, pg])

    def v_copy(bi, h, slot, u, blk):
        src = cv_ref.at[layer, pt_ref[bi, blk * ppb + u], pl.ds(h * HEAD_DIM, HEAD_DIM), :]
        return pltpu.make_async_copy(src, vsel.at[h, slot * ppb + u], vsem.at[0])

    @pl.when(b == 0)
    def _():
        for pg in range(n_pages):
            k_copy(0, pg).start()

    @pl.when((b == 0) & (last >= 1))
    def _():
        for pg in range(n_pages):
            k_copy(1, pg).start()

    @pl.when(b <= last)
    def _():
        q = q_ref[0] * (HEAD_DIM ** -0.5 * LOG2_E)
        for h in range(MOBA_HEADS):
            qs_scr[h * SUBLANES:(h + 1) * SUBLANES, :] = jnp.where(lane // HEAD_DIM == h, q, 0.0)
        qb = qs_scr[...].astype(BF16)
        for g0 in range(0, n_pages, group):
            for pg in range(g0, g0 + group):
                k_copy(b, pg).wait()
            for pg in range(g0, g0 + group, ppb):
                keys = jnp.concatenate([kring[b % 2, pg + u].astype(BF16) for u in range(ppb)], axis=1)
                s_scr[:, pg * page:(pg + ppb) * page] = jnp.dot(qb, keys, preferred_element_type=F32)

    @pl.when(b + 2 <= last)
    def _():
        for pg in range(n_pages):
            k_copy(b + 2, pg).start()

    @pl.when(b >= 1)
    def _():
        for h in range(MOBA_HEADS):
            for j in range(slots):
                for u in range(ppb):
                    v_copy(0, h, j, u, 0).wait()
        parts = [[jnp.zeros((SUBLANES, HEAD_DIM), F32) for _ in range(4)] for _ in range(MOBA_HEADS)]
        for piece in range(slots * ppb):
            for h in range(MOBA_HEADS):
                parts[h][piece % 4] = parts[h][piece % 4] + lax.dot_general(
                    pc_scr[h, piece].astype(BF16), vsel[h, piece].astype(BF16), contract_lanes,
                    preferred_element_type=F32)
        o_sel = [(p4[0] + p4[1]) + (p4[2] + p4[3]) for p4 in parts]
        o_ref[0] = ((jnp.concatenate(o_sel, axis=1) + tail_scr[0]) / tail_scr[1] * tail_scr[2]).astype(o_ref.dtype)

    @pl.when(b <= last)
    def _():
        qb = qs_scr[...].astype(BF16)
        col = lax.broadcasted_iota(jnp.int32, (rows, LANES), 1)
        gate = jnp.zeros((rows, LANES), F32)
        for n in range(n_blocks):
            c0 = n * MOBA_BLOCK
            bsum = s_scr[:, c0:c0 + page]
            for u in range(1, ppb):
                bsum = bsum + s_scr[:, c0 + u * page:c0 + (u + 1) * page]
            gate = jnp.where(col == n, jnp.sum(bsum, axis=1, keepdims=True), gate)
        rank = jnp.zeros(gate.shape, jnp.int32)
        for jp in range(n_blocks):
            gj = gate[:, jp:jp + 1]
            rank = rank + jnp.where((gj > gate) | ((gj == gate) & (jp < col)), 1, 0)
        sel = jnp.where(rank < min(MOBA_TOPK, n_blocks), 0.0, NEG_INF)
        own = jnp.concatenate([kn_ref[0], jnp.zeros((page - SUBLANES, MOBA_W), F32)], axis=0).astype(BF16)
        t_idx = lax.broadcasted_iota(jnp.int32, (rows, page), 0) % SUBLANES
        key_idx = lax.broadcasted_iota(jnp.int32, (rows, page), 1)
        s_own = lax.dot_general(qb, own, contract_lanes, preferred_element_type=F32)
        s_own = s_own + jnp.where(key_idx <= t_idx, 0.0, NEG_INF)
        s_scr[:, n_pages * page:] = s_own
        m_part = s_own
        for n in range(n_blocks):
            add = sel[:, n:n + 1]
            for u in range(ppb):
                c0 = n * MOBA_BLOCK + u * page
                sm = s_scr[:, c0:c0 + page] + add
                s_scr[:, c0:c0 + page] = sm
                m_part = jnp.maximum(m_part, sm)
        m = jnp.max(m_part, axis=1, keepdims=True)
        l_part = jnp.zeros((rows, page), F32)
        for n in range(n_pages + 1):
            c0 = n * page
            p = jnp.exp2(s_scr[:, c0:c0 + page] - m)
            l_part = l_part + p
            p_scr[:, c0:c0 + page] = p
        l = jnp.sum(l_part, axis=1, keepdims=True)

        row8 = lax.broadcasted_iota(jnp.int32, (SUBLANES, LANES), 0)
        col8 = lax.broadcasted_iota(jnp.int32, (SUBLANES, LANES), 1)
        sel_real = jnp.where(t_idx < n_tokens, sel, NEG_INF)
        chosen = jnp.full((SUBLANES, LANES), NEG_INF, F32)
        for h in range(MOBA_HEADS):
            any_row = jnp.max(sel_real[h * SUBLANES:(h + 1) * SUBLANES], axis=0, keepdims=True)
            chosen = jnp.where(row8 == h, any_row, chosen)
        chosen = chosen == 0.0
        before = (lax.broadcasted_iota(jnp.int32, (LANES, LANES), 0)
                  < lax.broadcasted_iota(jnp.int32, (LANES, LANES), 1))
        pos = jnp.dot(jnp.where(chosen, 1.0, 0.0).astype(BF16), jnp.where(before, 1.0, 0.0).astype(BF16),
                      preferred_element_type=F32)
        count = jnp.sum(jnp.where(chosen, 1.0, 0.0), axis=1, keepdims=True)
        blk_id = col8.astype(F32)
        pick = lambda j: jnp.sum(jnp.where(chosen & (pos == j), blk_id, 0.0), axis=1, keepdims=True)
        first = pick(0)
        table = jnp.where(col8 == slots, count, 0.0)
        for j in range(slots):
            table = jnp.where(col8 == j, jnp.where(count > j, pick(j), first), table)
        idx_v[...] = table.astype(jnp.int32)
        to_smem = pltpu.make_async_copy(idx_v, idx_s, isem.at[0])
        to_smem.start()
        to_smem.wait()

        for h in range(MOBA_HEADS):
            for j in range(slots):
                for u in range(ppb):
                    v_copy(b, h, j, u, idx_s[h, j]).start(priority=1)
        for h in range(MOBA_HEADS):
            for j in range(slots):
                weight = jnp.where(j < idx_s[h, slots], 1.0, 0.0)
                for u in range(ppb):
                    c0 = pl.multiple_of((idx_s[h, j] * ppb + u) * page, page)
                    pc_scr[h, j * ppb + u] = p_scr[h * SUBLANES:(h + 1) * SUBLANES, pl.ds(c0, page)] * weight
        vown = jnp.concatenate([vn_ref[0], jnp.zeros((page - SUBLANES, MOBA_W), F32)], axis=0).astype(BF16)
        o_own = jnp.dot(p_scr[:, n_pages * page:].astype(BF16), vown, preferred_element_type=F32)
        own_rows = jnp.zeros((SUBLANES, MOBA_W), F32)
        l_rows = jnp.zeros((SUBLANES, MOBA_W), F32)
        for h in range(MOBA_HEADS):
            mine = lane // HEAD_DIM == h
            own_rows = jnp.where(mine, o_own[h * SUBLANES:(h + 1) * SUBLANES, :], own_rows)
            l_rows = jnp.where(mine, l[h * SUBLANES:(h + 1) * SUBLANES, :], l_rows)
        tail_scr[0] = own_rows
        tail_scr[1] = l_rows
        tail_scr[2] = _silu(g_ref[0])


def _moba_sample(page_table, mq3, mk3, mv3, mg3, cache_k4, cache_v4, layer, n_tokens):
    db = mq3.shape[0]
    n_pages = page_table.shape[1]
    page = cache_k4.shape[3]
    rows = MOBA_HEADS * SUBLANES
    n_blocks = n_pages * page // MOBA_BLOCK
    slots = n_tokens * min(MOBA_TOPK, n_blocks)
    assert n_blocks <= LANES and page == LANES and slots < LANES and MOBA_HEADS <= SUBLANES
    blk3 = (1, SUBLANES, MOBA_W)
    this = pl.BlockSpec(blk3, lambda bi, pt: (jnp.minimum(bi, db - 1), 0, 0))
    before = pl.BlockSpec(blk3, lambda bi, pt: (jnp.maximum(bi - 1, 0), 0, 0))
    hbm = pl.BlockSpec(memory_space=pl.ANY)
    width = (n_pages + 1) * page
    pieces = slots * MOBA_BLOCK // page
    return pl.pallas_call(
        functools.partial(_moba_sample_kernel, layer=layer, n_pages=n_pages, page=page, n_tokens=n_tokens),
        grid_spec=pltpu.PrefetchScalarGridSpec(
            num_scalar_prefetch=1,
            grid=(db + 1,),
            in_specs=[this, this, this, this, hbm, hbm],
            out_specs=before,
            scratch_shapes=[
                pltpu.VMEM((2, n_pages, MOBA_W, page), F32),
                pltpu.SemaphoreType.DMA((2, n_pages)),
                pltpu.VMEM((MOBA_HEADS, pieces, HEAD_DIM, page), F32),
                pltpu.SemaphoreType.DMA((1,)),
                pltpu.VMEM((SUBLANES, LANES), jnp.int32),
                pltpu.SMEM((SUBLANES, LANES), jnp.int32),
                pltpu.SemaphoreType.DMA((1,)),
                pltpu.VMEM((rows, MOBA_W), F32),
                pltpu.VMEM((rows, width), F32),
                pltpu.VMEM((rows, width), F32),
                pltpu.VMEM((MOBA_HEADS, pieces, SUBLANES, page), F32),
                pltpu.VMEM((3, SUBLANES, MOBA_W), F32),
            ],
        ),
        out_shape=jax.ShapeDtypeStruct((db, SUBLANES, MOBA_W), BF16),
        compiler_params=_cparams(("arbitrary",)),
        name="moba_sample",
    )(page_table, mq3, mk3, mv3, mg3, cache_k4, cache_v4)


def _pool_kernel(u_ref, g_ref, buf_ref, inv_cnt_ref, w_ref, b_ref, sc_ref, o_ref, bufo_ref, x_scr, w_scr, *,
                 n_tokens, rows):
    bb, length, _ = u_ref.shape
    lead = SUBLANES
    halo = lead + POOL_BUF + 1
    assert POOL_WINDOWS == (2, 4, 8, 16) and halo % SUBLANES == 0
    tiles = [(lead, halo - lead)] + [(halo + r0, rows) for r0 in range(0, length, rows)]
    for bi in range(bb):
        x_scr[bi, 0:lead, :] = jnp.zeros((lead, POOL_W), F32)
        x_scr[bi, lead:halo, :] = buf_ref[bi]
        x_scr[bi, halo:, :] = u_ref[bi]
        w_scr[bi, :, 0:lead, :] = jnp.zeros((3, lead, POOL_W), F32)
        for i0, n in tiles:
            group = lax.broadcasted_iota(jnp.int32, (n, POOL_W), 1) // POOL_GC
            u = x_scr[bi, i0:i0 + n, :]
            w2 = u + x_scr[bi, i0 - 1:i0 - 1 + n, :]
            w_scr[bi, 0, i0:i0 + n, :] = w2
            w4 = w2 + w_scr[bi, 0, i0 - 2:i0 - 2 + n, :]
            w_scr[bi, 1, i0:i0 + n, :] = w4
            w8 = w4 + w_scr[bi, 1, i0 - 4:i0 - 4 + n, :]
            w_scr[bi, 2, i0:i0 + n, :] = w8
            if i0 < halo:
                continue
            w16 = w8 + w_scr[bi, 2, i0 - 8:i0 - 8 + n, :]
            r0 = i0 - halo
            win = jnp.where(group == 0, w2, jnp.where(group == 1, w4, jnp.where(group == 2, w8, w16)))
            pooled = win * inv_cnt_ref[r0:r0 + n, :] - u
            y = jnp.dot(pooled.astype(BF16), w_ref[...], preferred_element_type=F32) + b_ref[...]
            y = y * sc_ref[...]
            o_ref[bi, r0:r0 + n, :] = (y * _silu(g_ref[bi, r0:r0 + n, :])).astype(o_ref.dtype)
        bufo_ref[bi] = x_scr[bi, lead + n_tokens + 1:lead + n_tokens + 1 + POOL_BUF, :]


def _pool(zp3, buf16, cnt, w_bd, bias, scale, *, n_tokens, rows, bb):
    b, length, _ = zp3.shape
    const = lambda bi: (0, 0)
    n_rows = SUBLANES + POOL_BUF + 1 + length
    return pl.pallas_call(
        functools.partial(_pool_kernel, n_tokens=n_tokens, rows=rows),
        grid=(b // bb,),
        in_specs=[
            pl.BlockSpec((bb, length, POOL_W), lambda bi: (bi, 0, 0)),
            pl.BlockSpec((bb, length, POOL_W), lambda bi: (bi, 0, 1)),
            pl.BlockSpec((bb, POOL_BUF + 1, POOL_W), lambda bi: (bi, 0, 0)),
            pl.BlockSpec((length, POOL_W), const),
            pl.BlockSpec((POOL_W, POOL_W), const),
            pl.BlockSpec((1, POOL_W), const),
            pl.BlockSpec((1, POOL_W), const),
        ],
        out_specs=[
            pl.BlockSpec((bb, length, POOL_W), lambda bi: (bi, 0, 0)),
            pl.BlockSpec((bb, POOL_BUF, POOL_W), lambda bi: (bi, 0, 0)),
        ],
        out_shape=[
            jax.ShapeDtypeStruct((b, length, POOL_W), BF16),
            jax.ShapeDtypeStruct((b, POOL_BUF, POOL_W), F32),
        ],
        scratch_shapes=[pltpu.VMEM((bb, n_rows, POOL_W), F32), pltpu.VMEM((bb, 3, n_rows, POOL_W), F32)],
        compiler_params=_cparams(("arbitrary",)),
        name="pool",
    )(zp3, zp3, buf16, cnt, w_bd, bias, scale)


def _out_proj_kernel(mr_ref, mm_ref, mp_ref, x_ref, w_ref, y_ref, mix_scr):
    mix_scr[:, :RET_W] = mr_ref[...]
    mix_scr[:, RET_W:RET_W + MOBA_W] = mm_ref[...]
    mix_scr[:, RET_W + MOBA_W:] = mp_ref[...]
    y_ref[...] = x_ref[...] + jnp.dot(mix_scr[...], w_ref[...], preferred_element_type=F32)


def _out_proj(mr, mm, mp, x2d, w_bf, *, tm, layer):
    m, d = x2d.shape
    d_mix = RET_W + MOBA_W + POOL_W
    row = lambda i: (i, 0)
    return pl.pallas_call(
        _out_proj_kernel,
        grid=(m // tm,),
        in_specs=[
            pl.BlockSpec((tm, RET_W), row),
            pl.BlockSpec((tm, MOBA_W), row),
            pl.BlockSpec((tm, POOL_W), row),
            pl.BlockSpec((tm, d), row),
            pl.BlockSpec((None, d_mix, d), lambda i: (layer, 0, 0)),
        ],
        out_specs=pl.BlockSpec((tm, d), row),
        out_shape=jax.ShapeDtypeStruct((m, d), F32),
        scratch_shapes=[pltpu.VMEM((tm, d_mix), BF16)],
        compiler_params=_cparams(("arbitrary",)),
        name="out_proj",
    )(mr, mm, mp, x2d, w_bf)


def _rope_tables(pos):
    inv = 1.0 / (ROPE_THETA ** (np.arange(HALF, dtype=np.float64) / HALF))
    ang = np.asarray(pos, np.float64)[:, None] * inv[None, :]
    c, s = np.cos(ang), np.sin(ang)
    return (jnp.asarray(np.concatenate([c, c, c, c], axis=-1), F32),
            jnp.asarray(np.concatenate([-s, s, -s, s], axis=-1), F32))


def _block_diag(blocks):
    g, n, _ = blocks.shape
    eye = jnp.eye(g, dtype=blocks.dtype)
    return (eye[:, None, :, None] * blocks[:, :, None, :]).reshape(g * n, g * n)


def _pool_inv_counts(pos0, length):
    pos = pos0 + np.arange(length)
    w = np.repeat(np.asarray(POOL_WINDOWS), POOL_GC)
    return jnp.asarray(1.0 / np.minimum(pos[:, None] + 1, w[None, :]), F32)


def _layer(x3, pos_tabs, ret_state, ret_prev, ret_tabs, ret_step, pool_buf16, pool_cnt, n_tokens, moba_fn, params, *,
           tm, layer):
    b, length, d = x3.shape
    nw, w_in, w_out, gnw, qnw, knw, pw_bd, pbias, pscale = params
    x2d = x3.reshape(b * length, d)
    zr, mq, mk, mv, mg, zp = _in_proj(x2d, nw, w_in, pos_tabs[0], pos_tabs[1], qnw, knw, tm=tm, layer=layer)
    three = lambda a: a.reshape(b, length, a.shape[-1])
    mix_r, states = _retention(three(zr), ret_state, ret_prev, ret_tabs, gnw, chunk=RET_CHUNK, bb=ret_step[0],
                               pairs=ret_step[1], layer=layer)
    mix_m, k_new, v_new = moba_fn(three(mq), three(mk), three(mv), three(mg))
    mix_p, buf_new = _pool(three(zp), pool_buf16, pool_cnt, pw_bd, pbias, pscale,
                           n_tokens=n_tokens, rows=min(length, 128), bb=ret_step[0])
    two = lambda a: a.reshape(b * length, a.shape[-1])
    tm_out = max(t for t in (tm, 2 * tm, 4 * tm) if (b * length) % t == 0)
    y = _out_proj(two(mix_r), two(mix_m), two(mix_p), x2d, w_out, tm=tm_out, layer=layer)
    return y.reshape(b, length, d), states, k_new, v_new, buf_new


def _moba_prompt_wrap(mq3, mk3, mv3, mg3, *, prev_kv):
    return _moba_prompt(mq3, mk3, mv3, mg3, prev_kv)


def _moba_sample_wrap(mq3, mk3, mv3, mg3, *, page_table, cache_k4, cache_v4, layer, n_tokens):
    mix = _moba_sample(page_table, mq3, mk3, mv3, mg3, cache_k4, cache_v4, layer, n_tokens)
    rows = lambda t: t[:, :n_tokens].reshape(t.shape[0], n_tokens, MOBA_HEADS, HEAD_DIM)
    return mix, rows(mk3), rows(mv3)


def kernel(x_prompt, x_sample, cache_k, cache_v, state_ret, state_pool, page_table, norm_w, w_in, w_out, ret_gn_w, q_norm_w, k_norm_w, pool_w, pool_b, pool_scale):
    depth = w_in.shape[0]
    b, seq, d = x_prompt.shape
    db, dec_seq, _ = x_sample.shape
    n_pool, page = cache_k.shape[1], cache_k.shape[2]
    past_len = page_table.shape[1] * page
    assert dec_seq <= SUBLANES and seq % MOBA_BLOCK == 0 and seq % RET_CHUNK == 0 and past_len % MOBA_BLOCK == 0

    p_tabs = _rope_tables(np.arange(seq))
    tm_s = db * SUBLANES
    s_tabs = _rope_tables(np.tile(past_len + np.arange(SUBLANES), db))
    p_ret_tabs = _retention_tables(RET_CHUNK, RET_CHUNK)
    s_ret_tabs = _retention_tables(RET_CHUNK, dec_seq)
    p_cnt = _pool_inv_counts(0, seq)
    s_cnt = _pool_inv_counts(past_len, SUBLANES)
    zero_buf = jnp.zeros((b, POOL_BUF + 1, POOL_W), F32)

    cache_k4 = cache_k.transpose(0, 1, 3, 4, 2).reshape(depth, n_pool, MOBA_W, page)
    cache_v4 = cache_v.transpose(0, 1, 3, 4, 2).reshape(depth, n_pool, MOBA_W, page)
    xp = x_prompt
    xs = jnp.pad(x_sample, ((0, 0), (0, SUBLANES - dec_seq), (0, 0)))

    w_in_bf = w_in.astype(BF16)
    w_out_bf = w_out.astype(BF16)
    outs = [[] for _ in range(4)]
    prev_kv = s_p = s_s = None
    for l in range(depth):
        params = (norm_w[l].reshape(1, d), w_in_bf, w_out_bf,
                  ret_gn_w[l].reshape(RET_W // LANES, 1, LANES),
                  jnp.tile(q_norm_w[l], MOBA_HEADS).reshape(1, MOBA_W),
                  jnp.tile(k_norm_w[l], MOBA_HEADS).reshape(1, MOBA_W),
                  _block_diag(pool_w[l]).astype(BF16),
                  pool_b[l].reshape(1, POOL_W), pool_scale[l].reshape(1, POOL_W))
        xp, s_p, kt, vt, b_p = _layer(xp, p_tabs, None, s_p, p_ret_tabs, (1, 1), zero_buf, p_cnt, seq,
                                      functools.partial(_moba_prompt_wrap, prev_kv=prev_kv), params,
                                      tm=512, layer=l)
        prev_kv = (kt, vt)
        moba_s = functools.partial(_moba_sample_wrap, page_table=page_table, cache_k4=cache_k4,
                                   cache_v4=cache_v4, layer=l, n_tokens=dec_seq)
        xs, s_s, k_s, v_s, b_s = _layer(xs, s_tabs, state_ret, s_s, s_ret_tabs,
                                        (math.gcd(db, 8), RET_W // LANES),
                                        jnp.pad(state_pool[l], ((0, 0), (1, 0), (0, 0))), s_cnt, dec_seq,
                                        moba_s, params, tm=tm_s, layer=l)
        for lst, val in zip(outs, (k_s, v_s, b_p, b_s)):
            lst.append(val)
    rows = lambda t: t.reshape(depth, b, MOBA_HEADS, HEAD_DIM, seq).transpose(0, 1, 4, 2, 3)
    k_s, v_s, b_p, b_s = (jnp.stack(o) for o in outs)
    return xp, xs[:, :dec_seq], rows(prev_kv[0]), rows(prev_kv[1]), k_s, v_s, s_p, s_s, b_p, b_s
```
